```python
import jax, jax.numpy as jnp
from jax import lax
import numpy as np

D_MODEL = 1024
BATCH = 4
SEQ = 8192
DEPTH = 2

D_RNN = 1408
RNN_BLOCKS = 16
RNN_BW = D_RNN // RNN_BLOCKS
CONV_WIDTH = 4
LRU_C = 8.0
N_HEADS = 16
HEAD_DIM = 64
N_KV_GROUPS = 4
HEADS_PER_GROUP = N_HEADS // N_KV_GROUPS
CMP_BLOCK = 32
CMP_STRIDE = 16
CMP_HIDDEN = 256
SEL_BLOCK = 64
N_SEL = 16
WINDOW = 512
Q_BLOCK = 128
SEL_Q_BLOCK = 32
FORCE_SCORE = 1.0e4
ROPE_THETA = 10000.0
NSA_IN = N_HEADS * HEAD_DIM + 6 * N_KV_GROUPS * HEAD_DIM + 3 * N_HEADS
D_FF = 3584
N_EXPERTS = 8
TOP_K = 2
RMS_EPS = 1e-6

kernel_name = "hybrid_rglru_nsa_moe_trunk"


def rmsnorm(x, g):
    xf = x.astype(jnp.float32)
    y = xf * lax.rsqrt(jnp.mean(xf * xf, axis=-1, keepdims=True) + RMS_EPS)
    return (y * g.astype(jnp.float32)).astype(x.dtype)


def rope(x, pos):
    half = HEAD_DIM // 2
    freqs = ROPE_THETA ** (-jnp.arange(half, dtype=jnp.float32) / half)
    ang = pos[:, None] * freqs[None, :]
    cos = jnp.cos(ang)[None, :, None, :].astype(x.dtype)
    sin = jnp.sin(ang)[None, :, None, :].astype(x.dtype)
    x1, x2 = x[..., :half], x[..., half:]
    return jnp.concatenate([x1 * cos - x2 * sin, x2 * cos + x1 * sin], axis=-1)


def masked_softmax(s, mask):
    s = jnp.where(mask, s.astype(jnp.float32), -jnp.inf)
    m = jnp.max(s, axis=-1, keepdims=True)
    m = jnp.where(jnp.isfinite(m), m, 0.0)
    e = jnp.where(mask, jnp.exp(s - m), 0.0)
    return e / jnp.maximum(jnp.sum(e, axis=-1, keepdims=True), 1e-30)


def rglru_mixer(x, w_in, b_in, conv_w, conv_b, w_a, b_a, w_i, b_i, lam, w_out, b_out):
    B, S, _ = x.shape
    proj = x @ w_in + b_in
    gate, xr = jnp.split(proj, 2, axis=-1)
    gate = jax.nn.gelu(gate)
    xp = jnp.pad(xr, ((0, 0), (CONV_WIDTH - 1, 0), (0, 0)))
    xc = conv_b + sum(xp[:, k:k + S] * conv_w[k] for k in range(CONV_WIDTH))
    xb = xc.reshape(B, S, RNN_BLOCKS, RNN_BW)
    r = jax.nn.sigmoid(jnp.einsum('bsnc,ncd->bsnd', xb, w_a).reshape(B, S, D_RNN) + b_a)
    i = jax.nn.sigmoid(jnp.einsum('bsnc,ncd->bsnd', xb, w_i).reshape(B, S, D_RNN) + b_i)
    log_a = (-LRU_C * r.astype(jnp.float32)) * jax.nn.softplus(-lam.astype(jnp.float32))
    a = jnp.exp(log_a)
    mult = jnp.sqrt(-jnp.expm1(2.0 * log_a))
    mult = mult.at[:, 0].set(1.0)
    u = mult * (i * xc).astype(jnp.float32)

    def combine(lhs, rhs):
        a1, b1 = lhs
        a2, b2 = rhs
        return a1 * a2, a2 * b1 + b2

    _, h = lax.associative_scan(combine, (a, u), axis=1)
    y = h.astype(x.dtype) * gate
    return y @ w_out + b_out


def compress(t, pe, w1, w2):
    B, S, G, D = t.shape
    r = CMP_BLOCK // CMP_STRIDE
    n_ch = S // CMP_STRIDE
    ch = t.reshape(B, n_ch, CMP_STRIDE, G, D)
    blk = jnp.concatenate([ch[:, j:n_ch - r + 1 + j] for j in range(r)], axis=2)
    blk = blk + pe[None, None, :, None, :]
    flat = blk.transpose(0, 1, 3, 2, 4).reshape(B, n_ch - r + 1, G, CMP_BLOCK * D)
    return jax.nn.gelu(flat @ w1) @ w2


def overlap_matrix(n_c, n_s):
    r = CMP_BLOCK // CMP_STRIDE
    qn = SEL_BLOCK // CMP_STRIDE
    chunks = np.arange(n_c)[:, None] + np.arange(r)[None, :]
    m = np.zeros((n_c, n_s), np.float32)
    np.add.at(m, (np.repeat(np.arange(n_c), r), (chunks // qn).ravel()), 1.0)
    return m


def nsa_mixer(x, w_in, cmp_pe, cmp_w1, cmp_w2, w_out):
    B, S, _ = x.shape
    H, G, HP, DH = N_HEADS, N_KV_GROUPS, HEADS_PER_GROUP, HEAD_DIM
    proj = x @ w_in
    cuts = list(np.cumsum([H * DH] + [G * DH] * 6))
    q, kc, vc, ks, vs, kw, vw, g = jnp.split(proj, cuts, axis=-1)
    q = q.reshape(B, S, H, DH)
    kc, vc, ks, vs, kw, vw = [t.reshape(B, S, G, DH) for t in (kc, vc, ks, vs, kw, vw)]
    gates = jax.nn.sigmoid(g.reshape(B, S, H, 3))
    pos = jnp.arange(S, dtype=jnp.float32)
    q = rope(q, pos) * (DH ** -0.5)
    kc, ks, kw = rope(kc, pos), rope(ks, pos), rope(kw, pos)
    qg = q.reshape(B, S, G, HP, DH)

    k_cmp = compress(kc, cmp_pe[0], cmp_w1[0], cmp_w2[0])
    v_cmp = compress(vc, cmp_pe[1], cmp_w1[1], cmp_w2[1])
    n_c = k_cmp.shape[1]
    n_s = S // SEL_BLOCK
    k_sel = min(N_SEL, n_s)
    cmp_end = jnp.arange(n_c) * CMP_STRIDE + CMP_BLOCK - 1
    blk_start = jnp.arange(n_s) * SEL_BLOCK
    overlap = jnp.asarray(overlap_matrix(n_c, n_s))

    def cmp_block(i):
        s0 = i * Q_BLOCK
        qb = lax.dynamic_slice_in_dim(qg, s0, Q_BLOCK, axis=1)
        t = s0 + jnp.arange(Q_BLOCK)
        sc = jnp.einsum('bqghd,bcgd->bghqc', qb, k_cmp)
        p = masked_softmax(sc, cmp_end[None, :] <= t[:, None])
        o = jnp.einsum('bghqc,bcgd->bqghd', p.astype(v_cmp.dtype), v_cmp)
        imp = jnp.einsum('bghqc,cn->bgqn', p, overlap)
        cur = t[:, None] // SEL_BLOCK
        j = jnp.arange(n_s)[None, :]
        forced = (j == 0) | (j == cur) | (j == cur - 1)
        valid = blk_start[None, :] <= t[:, None]
        score = jnp.where(forced, FORCE_SCORE, jnp.where(valid, imp, -1.0))
        _, idx = lax.top_k(score, k_sel)
        return o, idx

    o_cmp, sel_idx = lax.map(cmp_block, jnp.arange(S // Q_BLOCK))
    o_cmp = o_cmp.transpose(1, 0, 2, 3, 4, 5).reshape(B, S, H, DH)
    sel_idx = sel_idx.transpose(1, 2, 0, 3, 4).reshape(B, G, S, k_sel)

    kb = ks.reshape(B, n_s, SEL_BLOCK, G, DH).transpose(0, 3, 1, 2, 4)
    vb = vs.reshape(B, n_s, SEL_BLOCK, G, DH).transpose(0, 3, 1, 2, 4)
    bi = jnp.arange(B)[:, None, None, None]
    gi = jnp.arange(G)[None, :, None, None]

    def sel_block(i):
        s0 = i * SEL_Q_BLOCK
        qb = lax.dynamic_slice_in_dim(qg, s0, SEL_Q_BLOCK, axis=1)
        idx = lax.dynamic_slice_in_dim(sel_idx, s0, SEL_Q_BLOCK, axis=2)
        t = s0 + jnp.arange(SEL_Q_BLOCK)
        kg = kb[bi, gi, idx]
        vg = vb[bi, gi, idx]
        sc = jnp.einsum('bqghd,bgqkld->bghqkl', qb, kg)
        kpos = idx[..., None] * SEL_BLOCK + jnp.arange(SEL_BLOCK)
        mask = kpos <= t[None, None, :, None, None]
        n_k = k_sel * SEL_BLOCK
        p = masked_softmax(sc.reshape(B, G, HP, SEL_Q_BLOCK, n_k),
                           mask[:, :, None].reshape(B, G, 1, SEL_Q_BLOCK, n_k))
        return jnp.einsum('bghqn,bgqnd->bqghd', p.astype(vg.dtype),
                          vg.reshape(B, G, SEL_Q_BLOCK, n_k, DH))

    o_sel = lax.map(sel_block, jnp.arange(S // SEL_Q_BLOCK))
    o_sel = o_sel.transpose(1, 0, 2, 3, 4, 5).reshape(B, S, H, DH)

    kw_p = jnp.pad(kw, ((0, 0), (WINDOW, 0), (0, 0), (0, 0)))
    vw_p = jnp.pad(vw, ((0, 0), (WINDOW, 0), (0, 0), (0, 0)))
    span = WINDOW + Q_BLOCK

    def win_block(i):
        s0 = i * Q_BLOCK
        qb = lax.dynamic_slice_in_dim(qg, s0, Q_BLOCK, axis=1)
        kb_ = lax.dynamic_slice_in_dim(kw_p, s0, span, axis=1)
        vb_ = lax.dynamic_slice_in_dim(vw_p, s0, span, axis=1)
        t = s0 + jnp.arange(Q_BLOCK)
        kpos = s0 - WINDOW + jnp.arange(span)
        d = t[:, None] - kpos[None, :]
        mask = (kpos[None, :] >= 0) & (d >= 0) & (d < WINDOW)
        sc = jnp.einsum('bqghd,bkgd->bghqk', qb, kb_)
        p = masked_softmax(sc, mask)
        return jnp.einsum('bghqk,bkgd->bqghd', p.astype(vb_.dtype), vb_)

    o_win = lax.map(win_block, jnp.arange(S // Q_BLOCK))
    o_win = o_win.transpose(1, 0, 2, 3, 4, 5).reshape(B, S, H, DH)

    gates = gates.astype(x.dtype)
    o = gates[..., 0, None] * o_cmp + gates[..., 1, None] * o_sel + gates[..., 2, None] * o_win
    return o.reshape(B, S, H * DH) @ w_out


def swiglu(x, w_gate, w_up, w_down):
    return (jax.nn.silu(x @ w_gate) * (x @ w_up)) @ w_down


def moe_ffn(x, w_router, w_gate, w_up, w_down):
    B, S, D = x.shape
    xt = x.reshape(B * S, D)
    logits = (xt @ w_router).astype(jnp.float32)
    top_v, top_i = lax.top_k(logits, TOP_K)
    top_w = jax.nn.softmax(top_v, axis=-1)
    gates = jnp.sum(jax.nn.one_hot(top_i, N_EXPERTS, dtype=jnp.float32) * top_w[..., None], axis=1)
    gates = gates.astype(x.dtype)
    out = jnp.zeros_like(xt)
    for e in range(N_EXPERTS):
        out = out + gates[:, e:e + 1] * swiglu(xt, w_gate[e], w_up[e], w_down[e])
    return out.reshape(B, S, D)


def setup_inputs(seed: int = 0) -> dict:
    key = jax.random.key(seed)
    ks = iter(jax.random.split(key, 40))
    n_a = (DEPTH + 1) // 2
    n_b = DEPTH // 2
    f32 = jnp.float32

    def nrm(shape, fan_in):
        return jax.random.normal(next(ks), shape, f32) * (fan_in ** -0.5)

    def small(shape, s=0.02):
        return jax.random.normal(next(ks), shape, f32) * s

    u = jax.random.uniform(next(ks), (n_a, D_RNN), f32, minval=0.9, maxval=0.999)
    s = u ** (1.0 / LRU_C)
    lam = jnp.log(s) - jnp.log1p(-s)
    return {
        "x": jax.random.normal(next(ks), (BATCH, SEQ, D_MODEL), f32),
        "norm_mix": 1.0 + small((DEPTH, D_MODEL), 0.05),
        "norm_ffn": 1.0 + small((DEPTH, D_MODEL), 0.05),
        "norm_final": 1.0 + small((D_MODEL,), 0.05),
        "lru_w_in": nrm((n_a, D_MODEL, 2 * D_RNN), D_MODEL),
        "lru_b_in": small((n_a, 2 * D_RNN)),
        "lru_conv_w": nrm((n_a, CONV_WIDTH, D_RNN), CONV_WIDTH),
        "lru_conv_b": small((n_a, D_RNN)),
        "lru_w_a": nrm((n_a, RNN_BLOCKS, RNN_BW, RNN_BW), RNN_BW),
        "lru_b_a": small((n_a, D_RNN), 0.1),
        "lru_w_i": nrm((n_a, RNN_BLOCKS, RNN_BW, RNN_BW), RNN_BW),
        "lru_b_i": small((n_a, D_RNN), 0.1),
        "lru_lambda": lam,
        "lru_w_out": nrm((n_a, D_RNN, D_MODEL), D_RNN),
        "lru_b_out": small((n_a, D_MODEL)),
        "nsa_w_in": nrm((n_b, D_MODEL, NSA_IN), D_MODEL),
        "nsa_cmp_pe": small((n_b, 2, CMP_BLOCK, HEAD_DIM), 0.1),
        "nsa_cmp_w1": nrm((n_b, 2, CMP_BLOCK * HEAD_DIM, CMP_HIDDEN), CMP_BLOCK * HEAD_DIM),
        "nsa_cmp_w2": nrm((n_b, 2, CMP_HIDDEN, HEAD_DIM), CMP_HIDDEN),
        "nsa_w_out": nrm((n_b, N_HEADS * HEAD_DIM, D_MODEL), N_HEADS * HEAD_DIM),
        "ffn_w_gate": nrm((n_a, D_MODEL, D_FF), D_MODEL),
        "ffn_w_up": nrm((n_a, D_MODEL, D_FF), D_MODEL),
        "ffn_w_down": nrm((n_a, D_FF, D_MODEL), D_FF),
        "moe_w_router": nrm((n_b, D_MODEL, N_EXPERTS), D_MODEL),
        "moe_w_gate": nrm((n_b, N_EXPERTS, D_MODEL, D_FF), D_MODEL),
        "moe_w_up": nrm((n_b, N_EXPERTS, D_MODEL, D_FF), D_MODEL),
        "moe_w_down": nrm((n_b, N_EXPERTS, D_FF, D_MODEL), D_FF),
    }


def reference(x, norm_mix, norm_ffn, norm_final,
              lru_w_in, lru_b_in, lru_conv_w, lru_conv_b, lru_w_a, lru_b_a,
              lru_w_i, lru_b_i, lru_lambda, lru_w_out, lru_b_out,
              nsa_w_in, nsa_cmp_pe, nsa_cmp_w1, nsa_cmp_w2, nsa_w_out,
              ffn_w_gate, ffn_w_up, ffn_w_down,
              moe_w_router, moe_w_gate, moe_w_up, moe_w_down):
    for layer in range(DEPTH):
        j = layer // 2
        h = rmsnorm(x, norm_mix[layer])
        if layer % 2 == 0:
            x = x + rglru_mixer(h, lru_w_in[j], lru_b_in[j], lru_conv_w[j], lru_conv_b[j],
                                lru_w_a[j], lru_b_a[j], lru_w_i[j], lru_b_i[j],
                                lru_lambda[j], lru_w_out[j], lru_b_out[j])
            h = rmsnorm(x, norm_ffn[layer])
            x = x + swiglu(h, ffn_w_gate[j], ffn_w_up[j], ffn_w_down[j])
        else:
            x = x + nsa_mixer(h, nsa_w_in[j], nsa_cmp_pe[j], nsa_cmp_w1[j], nsa_cmp_w2[j],
                              nsa_w_out[j])
            h = rmsnorm(x, norm_ffn[layer])
            x = x + moe_ffn(h, moe_w_router[j], moe_w_gate[j], moe_w_up[j], moe_w_down[j])
    return rmsnorm(x, norm_final)
```

```python
import functools

import numpy as np
import jax
import jax.numpy as jnp
from jax import lax
from jax.experimental import pallas as pl
from jax.experimental.pallas import tpu as pltpu

F32 = jnp.float32
BF16 = jnp.bfloat16

RMS_EPS = 1e-6
LRU_C = 8.0
CONV_WIDTH = 4
N_HEADS = 16
HEAD_DIM = 64
N_KV_GROUPS = 4
HEADS_PER_GROUP = N_HEADS // N_KV_GROUPS
CMP_BLOCK = 32
CMP_STRIDE = 16
SEL_BLOCK = 64
N_SEL = 16
WINDOW = 512
FORCE_SCORE = 1.0e4
ROPE_THETA = 10000.0
TOP_K = 2

LANES = 128
MASK_VALUE = -1.0e30
VMEM_LIMIT_BYTES = 56 * 1024 * 1024

_NT = (((1,), (1,)), ((), ()))


def _params(*semantics):
    return pltpu.CompilerParams(dimension_semantics=semantics, vmem_limit_bytes=VMEM_LIMIT_BYTES)


def _resident(shape):
    zeros = (0,) * len(shape)
    return pl.BlockSpec(shape, lambda *_: zeros, pipeline_mode=pl.Buffered(1))


def _rms(x, g):
    return x * lax.rsqrt(jnp.mean(x * x, axis=-1, keepdims=True) + RMS_EPS) * g


def _dot(a, b):
    return jnp.dot(a, b, preferred_element_type=F32)


def _lane_col(vals, lane, idx):
    return jnp.sum(jnp.where(lane == idx, vals, 0.0), axis=1, keepdims=True)


def _lru_body(x_ref, gn_ref, win_ref, bin_ref, cw_ref, cb_ref, wg_ref, bg_ref, lam_ref,
              wout_ref, bout_ref, o_ref, xbuf, hcar, *, ts, dr):
    t = pl.program_id(1)

    @pl.when(t == 0)
    def _():
        xbuf[0:8, :] = jnp.zeros((8, dr), F32)
        hcar[...] = jnp.zeros_like(hcar)

    x = x_ref[0]
    hn = _rms(x, gn_ref[...]).astype(BF16)
    proj = _dot(hn, win_ref[...]) + bin_ref[...]
    gate = jax.nn.gelu(proj[:, :dr])
    xr = proj[:, dr:]

    xbuf[8:8 + ts, :] = xr
    xc = cb_ref[...] + xr * cw_ref[CONV_WIDTH - 1:CONV_WIDTH, :]
    for lag in range(1, CONV_WIDTH):
        k = CONV_WIDTH - 1 - lag
        xc = xc + xbuf[pl.ds(8 - lag, ts), :] * cw_ref[k:k + 1, :]
    xbuf[0:8, :] = xbuf[ts:ts + 8, :]

    gl = _dot(xc.astype(BF16), wg_ref[...]) + bg_ref[...]
    r = jax.nn.sigmoid(gl[:, :dr])
    i = jax.nn.sigmoid(gl[:, dr:])
    z = -lam_ref[...]
    softplus = jnp.maximum(z, 0.0) + jnp.log(1.0 + jnp.exp(-jnp.abs(z)))
    log_a = (-LRU_C * r) * softplus
    a = jnp.exp(log_a)
    mult = jnp.sqrt(1.0 - a * a)
    row = lax.broadcasted_iota(jnp.int32, (ts, 1), 0)
    mult = jnp.where((row == 0) & (t == 0), 1.0, mult)
    u = mult * (i * xc)

    shift = 1
    while shift < ts:
        keep = row >= shift
        a_prev = jnp.where(keep, pltpu.roll(a, shift, 0), 1.0)
        u_prev = jnp.where(keep, pltpu.roll(u, shift, 0), 0.0)
        u = a * u_prev + u
        a = a * a_prev
        shift *= 2
    h = u + a * hcar[...]
    hcar[...] = h[ts - 1:ts, :]

    y = (h * gate).astype(BF16)
    o_ref[0] = x + _dot(y, wout_ref[...]) + bout_ref[...]


def _lru_layer(x, gn, w_in, b_in, conv_w, conv_b, w_a, b_a, w_i, b_i, lam, w_out, b_out):
    B, S, D = x.shape
    dr = w_out.shape[0]
    ts = min(256, S)
    assert S % ts == 0 and ts % 8 == 0
    wg = jnp.concatenate([jax.scipy.linalg.block_diag(*w_a), jax.scipy.linalg.block_diag(*w_i)], axis=1)
    row = lambda v: v.reshape(1, -1)
    body = functools.partial(_lru_body, ts=ts, dr=dr)
    return pl.pallas_call(
        body,
        grid=(B, S // ts),
        in_specs=[
            pl.BlockSpec((1, ts, D), lambda b, t: (b, t, 0)),
            _resident((1, D)),
            _resident((D, 2 * dr)), _resident((1, 2 * dr)),
            _resident((CONV_WIDTH, dr)), _resident((1, dr)),
            _resident((dr, 2 * dr)), _resident((1, 2 * dr)),
            _resident((1, dr)),
            _resident((dr, D)), _resident((1, D)),
        ],
        out_specs=pl.BlockSpec((1, ts, D), lambda b, t: (b, t, 0)),
        out_shape=jax.ShapeDtypeStruct((B, S, D), F32),
        scratch_shapes=[pltpu.VMEM((ts + 8, dr), F32), pltpu.VMEM((1, dr), F32)],
        compiler_params=_params("arbitrary", "arbitrary"),
        name="lru_mixer",
    )(x, row(gn), w_in.astype(BF16), row(b_in), conv_w, row(conv_b), wg.astype(BF16),
      row(jnp.concatenate([b_a, b_i])), row(lam), w_out.astype(BF16), row(b_out))


def _ffn_body(*refs, n_e, n_f, use_gates, final_norm):
    x_ref, gn_ref = refs[0], refs[1]
    k = 2
    gates_ref = gfin_ref = None
    if use_gates:
        gates_ref = refs[k]; k += 1
    wg_ref, wu_ref, wd_ref = refs[k:k + 3]; k += 3
    if final_norm:
        gfin_ref = refs[k]; k += 1
    o_ref, hn_ref, acc_ref = refs[k:k + 3]
    e = pl.program_id(1)
    f = pl.program_id(2)

    @pl.when((e == 0) & (f == 0))
    def _():
        hn_ref[...] = _rms(x_ref[...], gn_ref[...]).astype(BF16)
        acc_ref[...] = jnp.zeros_like(acc_ref)

    hn = hn_ref[...]
    g = _dot(hn, wg_ref[0])
    u = _dot(hn, wu_ref[0])
    act = (g * jax.nn.sigmoid(g)) * u
    if use_gates:
        gates = gates_ref[...]
        lane = lax.broadcasted_iota(jnp.int32, gates.shape, 1)
        act = act * _lane_col(gates, lane, e)
    acc_ref[...] += _dot(act.astype(BF16), wd_ref[0])

    @pl.when((e == n_e - 1) & (f == n_f - 1))
    def _():
        out = x_ref[...] + acc_ref[...]
        if final_norm:
            out = _rms(out, gfin_ref[...])
        o_ref[...] = out


def _ffn_layer(x2, gn, w_gate, w_up, w_down, gates=None, g_final=None, tm=1024, tf=512):
    M, D = x2.shape
    n_e, _, F = w_gate.shape
    tm = min(tm, M)
    tf = min(tf, F)
    assert M % tm == 0 and F % tf == 0
    n_f = F // tf
    use_gates = gates is not None
    final_norm = g_final is not None
    in_specs = [pl.BlockSpec((tm, D), lambda i, e, f: (i, 0)), _resident((1, D))]
    args = [x2, gn.reshape(1, D)]
    if use_gates:
        in_specs.append(pl.BlockSpec((tm, LANES), lambda i, e, f: (i, 0)))
        args.append(gates)
    in_specs += [
        pl.BlockSpec((1, D, tf), lambda i, e, f: (e, 0, f)),
        pl.BlockSpec((1, D, tf), lambda i, e, f: (e, 0, f)),
        pl.BlockSpec((1, tf, D), lambda i, e, f: (e, f, 0)),
    ]
    args += [w_gate.astype(BF16), w_up.astype(BF16), w_down.astype(BF16)]
    if final_norm:
        in_specs.append(_resident((1, D)))
        args.append(g_final.reshape(1, D))
    body = functools.partial(_ffn_body, n_e=n_e, n_f=n_f, use_gates=use_gates, final_norm=final_norm)
    return pl.pallas_call(
        body,
        grid=(M // tm, n_e, n_f),
        in_specs=in_specs,
        out_specs=pl.BlockSpec((tm, D), lambda i, e, f: (i, 0)),
        out_shape=jax.ShapeDtypeStruct((M, D), F32),
        scratch_shapes=[pltpu.VMEM((tm, D), BF16), pltpu.VMEM((tm, D), F32)],
        compiler_params=_params("arbitrary", "arbitrary", "arbitrary"),
        name="moe_swiglu" if use_gates else "dense_swiglu",
    )(*args)


def _router_body(x_ref, gn_ref, wh_ref, wl_ref, o_ref, *, n_experts):
    hn = _rms(x_ref[...], gn_ref[...])
    hh = hn.astype(BF16)
    hl = (hn - hh.astype(F32)).astype(BF16)
    wh = wh_ref[...]
    logits = _dot(hh, wh) + _dot(hh, wl_ref[...]) + _dot(hl, wh)
    lane = lax.broadcasted_iota(jnp.int32, logits.shape, 1)
    lg = jnp.where(lane < n_experts, logits, -jnp.inf)
    m0 = jnp.max(lg, axis=1, keepdims=True)
    i0 = jnp.min(jnp.where(lg == m0, lane, LANES), axis=1, keepdims=True)
    lg = jnp.where(lane == i0, -jnp.inf, lg)
    m1 = jnp.max(lg, axis=1, keepdims=True)
    i1 = jnp.min(jnp.where(lg == m1, lane, LANES), axis=1, keepdims=True)
    e1 = jnp.exp(m1 - m0)
    w0 = 1.0 / (1.0 + e1)
    o_ref[...] = jnp.where(lane == i0, w0, 0.0) + jnp.where(lane == i1, e1 * w0, 0.0)


def _router(x2, gn, w_router, tm=512):
    M, D = x2.shape
    n_experts = w_router.shape[1]
    tm = min(tm, M)
    wpad = jnp.pad(w_router, ((0, 0), (0, LANES - n_experts)))
    wh = wpad.astype(BF16)
    wl = (wpad - wh.astype(F32)).astype(BF16)
    return pl.pallas_call(
        functools.partial(_router_body, n_experts=n_experts),
        grid=(M // tm,),
        in_specs=[pl.BlockSpec((tm, D), lambda i: (i, 0)), _resident((1, D)),
                  _resident((D, LANES)), _resident((D, LANES))],
        out_specs=pl.BlockSpec((tm, LANES), lambda i: (i, 0)),
        out_shape=jax.ShapeDtypeStruct((M, LANES), F32),
        compiler_params=_params("arbitrary"),
        name="moe_router",
    )(x2, gn.reshape(1, D), wh, wl)


def _nsa_proj_body(x_ref, gn_ref, w_ref, cos_ref, sin_ref,
                   q_ref, kc_ref, vc_ref, ks_ref, vs_ref, kw_ref, vw_ref, g_ref, *, tm, tiles_per_seq):
    i = pl.program_id(0)
    hn = _rms(x_ref[...], gn_ref[...]).astype(BF16)
    proj = _dot(hn, w_ref[...])
    lane = lax.broadcasted_iota(jnp.int32, (tm, LANES), 1)
    first_half = (lane & (HEAD_DIM - 1)) < HEAD_DIM // 2
    cos = cos_ref[...]
    sin = sin_ref[...]

    def chunk(c):
        return proj[:, c * LANES:(c + 1) * LANES]

    def rope(v):
        rot = jnp.where(first_half, pltpu.roll(v, LANES - HEAD_DIM // 2, 1), pltpu.roll(v, HEAD_DIM // 2, 1))
        return v * cos + rot * sin

    def heads(v):
        return v[:, :HEAD_DIM], pltpu.roll(v, HEAD_DIM, 1)[:, :HEAD_DIM]

    c = 0
    for cc in range(N_HEADS // 2):
        lo, hi = heads(rope(chunk(c)) * (HEAD_DIM ** -0.5)); c += 1
        q_ref[2 * cc] = lo.astype(BF16)
        q_ref[2 * cc + 1] = hi.astype(BF16)
    for cc in range(N_KV_GROUPS // 2):
        lo, hi = heads(rope(chunk(c))); c += 1
        kc_ref[2 * cc] = lo
        kc_ref[2 * cc + 1] = hi
    for cc in range(N_KV_GROUPS // 2):
        lo, hi = heads(chunk(c)); c += 1
        vc_ref[2 * cc] = lo
        vc_ref[2 * cc + 1] = hi
    pos = (i % tiles_per_seq) * tm + lax.broadcasted_iota(jnp.int32, (tm, 1), 0)
    onehot = (lane == pos // SEL_BLOCK).astype(BF16)
    for cc in range(N_KV_GROUPS // 2):
        v = rope(chunk(c)); c += 1
        for j, vv in enumerate((v, pltpu.roll(v, HEAD_DIM, 1))):
            ks_ref[2 * cc + j, :, 0:LANES] = jnp.where(lane < HEAD_DIM, vv, 0.0).astype(BF16)
            ks_ref[2 * cc + j, :, LANES:2 * LANES] = onehot
    for ref, roped in ((vs_ref, False), (kw_ref, True), (vw_ref, False)):
        for cc in range(N_KV_GROUPS // 2):
            v = chunk(c); c += 1
            lo, hi = heads(rope(v) if roped else v)
            ref[2 * cc] = lo.astype(BF16)
            ref[2 * cc + 1] = hi.astype(BF16)
    g_ref[...] = jax.nn.sigmoid(chunk(c))


def _nsa_proj(x2, gn, w_in, S, tm=256):
    M, D = x2.shape
    H, G, DH = N_HEADS, N_KV_GROUPS, HEAD_DIM
    tm = min(tm, S)
    assert S % tm == 0
    n_in = w_in.shape[1]
    n_pad = -(-n_in // LANES) * LANES
    wp = jnp.pad(w_in, ((0, 0), (0, n_pad - n_in))).astype(BF16)
    half = DH // 2
    freqs = ROPE_THETA ** (-jnp.arange(half, dtype=F32) / half)
    ang = jnp.arange(S, dtype=F32)[:, None] * freqs[None, :]
    cos = jnp.tile(jnp.cos(ang), (1, 2 * LANES // DH))
    sin = jnp.tile(jnp.concatenate([-jnp.sin(ang), jnp.sin(ang)], axis=1), (1, LANES // DH))
    tiles_per_seq = S // tm
    hd = lambda n, dt: jax.ShapeDtypeStruct((n, M, DH), dt)
    hspec = lambda n: pl.BlockSpec((n, tm, DH), lambda i: (0, i, 0))
    return pl.pallas_call(
        functools.partial(_nsa_proj_body, tm=tm, tiles_per_seq=tiles_per_seq),
        grid=(M // tm,),
        in_specs=[pl.BlockSpec((tm, D), lambda i: (i, 0)), _resident((1, D)), _resident((D, n_pad)),
                  pl.BlockSpec((tm, LANES), lambda i: (i % tiles_per_seq, 0)),
                  pl.BlockSpec((tm, LANES), lambda i: (i % tiles_per_seq, 0))],
        out_specs=[hspec(H), hspec(G), hspec(G),
                   pl.BlockSpec((G, tm, 2 * LANES), lambda i: (0, i, 0)),
                   hspec(G), hspec(G), hspec(G),
                   pl.BlockSpec((tm, LANES), lambda i: (i, 0))],
        out_shape=[hd(H, BF16), hd(G, F32), hd(G, F32),
                   jax.ShapeDtypeStruct((G, M, 2 * LANES), BF16),
                   hd(G, BF16), hd(G, BF16), hd(G, BF16),
                   jax.ShapeDtypeStruct((M, LANES), F32)],
        compiler_params=_params("arbitrary"),
        name="nsa_proj",
    )(x2, gn.reshape(1, D), wp, cos, sin)


def _compress_body(kc_ref, vc_ref, pe_ref, w1_ref, w2_ref, ko_ref, vo_ref, *, nc):
    half = CMP_BLOCK // 2
    for kv, (src, dst) in enumerate(((kc_ref, ko_ref), (vc_ref, vo_ref))):
        top = jnp.zeros((nc, w1_ref.shape[2]), F32)
        bot = jnp.zeros((nc, w1_ref.shape[2]), F32)
        for j in range(half):
            xj = src[0, pl.ds(j, nc, stride=CMP_STRIDE), :]
            top = top + _dot((xj + pe_ref[kv, j:j + 1, :]).astype(BF16),
                             w1_ref[kv, j * HEAD_DIM:(j + 1) * HEAD_DIM, :])
            bot = bot + _dot((xj + pe_ref[kv, half + j:half + j + 1, :]).astype(BF16),
                             w1_ref[kv, (half + j) * HEAD_DIM:(half + j + 1) * HEAD_DIM, :])
        hid = top + pltpu.roll(bot, nc - 1, 0)
        dst[0] = _dot(jax.nn.gelu(hid).astype(BF16), w2_ref[kv]).astype(BF16)


def _compress(kc, vc, pe, w1, w2, B, S):
    assert CMP_BLOCK == 2 * CMP_STRIDE
    G, M, DH = kc.shape
    nc = S // CMP_STRIDE
    spec_in = pl.BlockSpec((1, S, DH), lambda b, g: (g, b, 0))
    spec_out = pl.BlockSpec((1, nc, DH), lambda b, g: (g, b, 0))
    out = jax.ShapeDtypeStruct((G, B * nc, DH), BF16)
    return pl.pallas_call(
        functools.partial(_compress_body, nc=nc),
        grid=(B, G),
        in_specs=[spec_in, spec_in, _resident(pe.shape), _resident(w1.shape), _resident(w2.shape)],
        out_specs=[spec_out, spec_out],
        out_shape=[out, out],
        compiler_params=_params("arbitrary", "arbitrary"),
        name="nsa_compress",
    )(kc, vc, pe, w1.astype(BF16), w2.astype(BF16))


def _store_heads(o_ref, o4, gates, g, branch, tq):
    lane = lax.broadcasted_iota(jnp.int32, gates.shape, 1)
    for h in range(HEADS_PER_GROUP):
        col = 3 * (HEADS_PER_GROUP * g + h) + branch
        oh = o4[h * tq:(h + 1) * tq, :] * _lane_col(gates, lane, col)
        o_ref[:, h * HEAD_DIM:(h + 1) * HEAD_DIM] = oh.astype(o_ref.dtype)


def _cmp_body(q_ref, kc_ref, vc_ref, gates_ref, ov_ref, o_ref, bias_ref, *, tq, nc, n_s, k_sel):
    g = pl.program_id(1)
    i = pl.program_id(2)
    hp = HEADS_PER_GROUP
    q4 = q_ref[...].reshape(hp * tq, HEAD_DIM)
    s = lax.dot_general(q4, kc_ref[0], _NT, preferred_element_type=F32)
    t4 = i * tq + (lax.broadcasted_iota(jnp.int32, (hp * tq, 1), 0) & (tq - 1))
    cend = lax.broadcasted_iota(jnp.int32, (1, nc), 1) * CMP_STRIDE + (CMP_BLOCK - 1)
    mask = cend <= t4
    m = jnp.max(jnp.where(mask, s, MASK_VALUE), axis=1, keepdims=True)
    m = jnp.where(m > 0.5 * MASK_VALUE, m, 0.0)
    e = jnp.where(mask, jnp.exp(s - m), 0.0)
    p = e * (1.0 / jnp.maximum(jnp.sum(e, axis=1, keepdims=True), 1e-30))
    o4 = _dot(p.astype(BF16), vc_ref[0])
    _store_heads(o_ref, o4, gates_ref[...], g, 0, tq)

    ps = p[0:tq]
    for h in range(1, hp):
        ps = ps + p[h * tq:(h + 1) * tq]
    ph = ps.astype(BF16)
    pl_ = (ps - ph.astype(F32)).astype(BF16)
    imp = _dot(ph, ov_ref[...]) + _dot(pl_, ov_ref[...])
    t = i * tq + lax.broadcasted_iota(jnp.int32, (tq, 1), 0)
    j = lax.broadcasted_iota(jnp.int32, (tq, LANES), 1)
    cur = t // SEL_BLOCK
    forced = (j == 0) | (j == cur) | (j == cur - 1)
    valid = j * SEL_BLOCK <= t
    score = jnp.where(forced, FORCE_SCORE, jnp.where(valid, imp, -1.0))
    score = jnp.where(j < n_s, score, -jnp.inf)
    sel = jnp.zeros((tq, LANES), jnp.bool_)
    for _ in range(k_sel):
        mx = jnp.max(score, axis=1, keepdims=True)
        idx = jnp.min(jnp.where(score == mx, j, LANES), axis=1, keepdims=True)
        hit = j == idx
        sel = sel | hit
        score = jnp.where(hit, -jnp.inf, score)
    bias_ref[0] = jnp.where(sel, 0.0, MASK_VALUE).astype(BF16)


def _overlap_matrix(nc, n_s):
    r = CMP_BLOCK // CMP_STRIDE
    qn = SEL_BLOCK // CMP_STRIDE
    m = np.zeros((nc, LANES), np.float32)
    n_c = nc - r + 1
    chunks = np.arange(n_c)[:, None] + np.arange(r)[None, :]
    np.add.at(m, (np.repeat(np.arange(n_c), r), (chunks // qn).ravel()), 1.0)
    return m


def _cmp_attention(q, kcmp, vcmp, gates, B, S, tq=128):
    H, M, DH = q.shape
    G = N_KV_GROUPS
    tq = min(tq, S)
    nq = S // tq
    nc = S // CMP_STRIDE
    n_s = S // SEL_BLOCK
    assert n_s <= LANES and tq & (tq - 1) == 0
    k_sel = min(N_SEL, n_s)
    ov = jnp.asarray(_overlap_matrix(nc, n_s), BF16)
    return pl.pallas_call(
        functools.partial(_cmp_body, tq=tq, nc=nc, n_s=n_s, k_sel=k_sel),
        grid=(B, G, nq),
        in_specs=[pl.BlockSpec((HEADS_PER_GROUP, tq, DH), lambda b, g, i: (g, b * nq + i, 0)),
                  pl.BlockSpec((1, nc, DH), lambda b, g, i: (g, b, 0)),
                  pl.BlockSpec((1, nc, DH), lambda b, g, i: (g, b, 0)),
                  pl.BlockSpec((tq, LANES), lambda b, g, i: (b * nq + i, 0)),
                  _resident((nc, LANES))],
        out_specs=[pl.BlockSpec((tq, HEADS_PER_GROUP * DH), lambda b, g, i: (b * nq + i, g)),
                   pl.BlockSpec((1, tq, LANES), lambda b, g, i: (g, b * nq + i, 0))],
        out_shape=[jax.ShapeDtypeStruct((M, H * DH), BF16), jax.ShapeDtypeStruct((G, M, LANES), BF16)],
        compiler_params=_params("arbitrary", "arbitrary", "arbitrary"),
        name="nsa_cmp_select",
    )(q, kcmp, vcmp, gates, ov)


def _sel_body(q_ref, bias_ref, k_ref, v_ref, gates_ref, o_ref, qcat, *, tq):
    g = pl.program_id(1)
    i = pl.program_id(2)
    hp = HEADS_PER_GROUP
    rows = hp * tq
    qcat[:, 0:LANES] = jnp.zeros((rows, LANES), BF16)
    qcat[:, 0:HEAD_DIM] = q_ref[...].reshape(rows, HEAD_DIM)
    bias = bias_ref[0]
    for h in range(hp):
        qcat[h * tq:(h + 1) * tq, LANES:2 * LANES] = bias
    qc = qcat[...]

    def tile(j, carry, causal):
        m, l, acc = carry
        start = pl.multiple_of(j * tq, tq)
        s = lax.dot_general(qc, k_ref[0, pl.ds(start, tq), :], _NT, preferred_element_type=F32)
        if causal:
            r = lax.broadcasted_iota(jnp.int32, (rows, 1), 0) & (tq - 1)
            c = lax.broadcasted_iota(jnp.int32, (1, tq), 1)
            s = jnp.where(c <= r, s, MASK_VALUE)
        m_new = jnp.maximum(m, jnp.max(s, axis=1, keepdims=True))
        alpha = jnp.exp(m - m_new)
        p = jnp.exp(s - m_new)
        l = alpha * l + jnp.sum(p, axis=1, keepdims=True)
        acc = alpha * acc + _dot(p.astype(BF16), v_ref[0, pl.ds(start, tq), :])
        return m_new, l, acc

    init = (jnp.full((rows, 1), MASK_VALUE, F32), jnp.zeros((rows, 1), F32), jnp.zeros((rows, HEAD_DIM), F32))
    carry = lax.fori_loop(0, i, lambda j, c: tile(j, c, False), init)
    m, l, acc = tile(i, carry, True)
    _store_heads(o_ref, acc * (1.0 / l), gates_ref[...], g, 1, tq)


def _sel_attention(q, bias, kscat, vs, gates, B, S, tq=256):
    H, M, DH = q.shape
    G = N_KV_GROUPS
    tq = min(tq, S)
    nq = S // tq
    assert tq & (tq - 1) == 0 and tq % SEL_BLOCK == 0
    return pl.pallas_call(
        functools.partial(_sel_body, tq=tq),
        grid=(B, G, nq),
        in_specs=[pl.BlockSpec((HEADS_PER_GROUP, tq, DH), lambda b, g, i: (g, b * nq + i, 0)),
                  pl.BlockSpec((1, tq, LANES), lambda b, g, i: (g, b * nq + i, 0)),
                  pl.BlockSpec((1, S, 2 * LANES), lambda b, g, i: (g, b, 0)),
                  pl.BlockSpec((1, S, DH), lambda b, g, i: (g, b, 0)),
                  pl.BlockSpec((tq, LANES), lambda b, g, i: (b * nq + i, 0))],
        out_specs=pl.BlockSpec((tq, HEADS_PER_GROUP * DH), lambda b, g, i: (b * nq + i, g)),
        out_shape=jax.ShapeDtypeStruct((M, H * DH), BF16),
        scratch_shapes=[pltpu.VMEM((HEADS_PER_GROUP * tq, 2 * LANES), BF16)],
        compiler_params=_params("arbitrary", "arbitrary", "arbitrary"),
        name="nsa_selected",
    )(q, bias, kscat, vs, gates)


def _win_body(q_ref, k_ref, v_ref, gates_ref, o_ref, *, tq, n_back):
    g = pl.program_id(1)
    i = pl.program_id(2)
    hp = HEADS_PER_GROUP
    rows = hp * tq
    q4 = q_ref[...].reshape(rows, HEAD_DIM)
    r = lax.broadcasted_iota(jnp.int32, (rows, 1), 0) & (tq - 1)
    c = lax.broadcasted_iota(jnp.int32, (1, tq), 1)
    scores, starts = [], []
    for back in range(n_back, -1, -1):
        jt = i - back
        start = pl.multiple_of(jnp.maximum(jt, 0) * tq, tq)
        s = lax.dot_general(q4, k_ref[0, pl.ds(start, tq), :], _NT, preferred_element_type=F32)
        d = r - c + back * tq
        ok = (d >= 0) & (d < WINDOW) & (jt >= 0)
        scores.append(jnp.where(ok, s, MASK_VALUE))
        starts.append(start)
    m = scores[0].max(axis=1, keepdims=True)
    for s in scores[1:]:
        m = jnp.maximum(m, s.max(axis=1, keepdims=True))
    l = jnp.zeros((rows, 1), F32)
    acc = jnp.zeros((rows, HEAD_DIM), F32)
    for s, start in zip(scores, starts):
        p = jnp.exp(s - m)
        l = l + jnp.sum(p, axis=1, keepdims=True)
        acc = acc + _dot(p.astype(BF16), v_ref[0, pl.ds(start, tq), :])
    _store_heads(o_ref, acc * (1.0 / l), gates_ref[...], g, 2, tq)


def _win_attention(q, kw, vw, gates, B, S, tq=256):
    H, M, DH = q.shape
    G = N_KV_GROUPS
    tq = min(tq, S)
    nq = S // tq
    assert tq & (tq - 1) == 0
    n_back = -(-WINDOW // tq)
    return pl.pallas_call(
        functools.partial(_win_body, tq=tq, n_back=n_back),
        grid=(B, G, nq),
        in_specs=[pl.BlockSpec((HEADS_PER_GROUP, tq, DH), lambda b, g, i: (g, b * nq + i, 0)),
                  pl.BlockSpec((1, S, DH), lambda b, g, i: (g, b, 0)),
                  pl.BlockSpec((1, S, DH), lambda b, g, i: (g, b, 0)),
                  pl.BlockSpec((tq, LANES), lambda b, g, i: (b * nq + i, 0))],
        out_specs=pl.BlockSpec((tq, HEADS_PER_GROUP * DH), lambda b, g, i: (b * nq + i, g)),
        out_shape=jax.ShapeDtypeStruct((M, H * DH), BF16),
        compiler_params=_params("arbitrary", "arbitrary", "arbitrary"),
        name="nsa_window",
    )(q, kw, vw, gates)


def _outproj_body(x_ref, a_ref, b_ref, c_ref, w_ref, o_ref):
    o = a_ref[...].astype(F32) + b_ref[...].astype(F32) + c_ref[...].astype(F32)
    o_ref[...] = x_ref[...] + _dot(o.astype(BF16), w_ref[...])


def _outproj(x2, oc, os_, ow, w_out, tm=512):
    M, D = x2.shape
    K = w_out.shape[0]
    tm = min(tm, M)
    spec_o = pl.BlockSpec((tm, K), lambda i: (i, 0))
    return pl.pallas_call(
        _outproj_body,
        grid=(M // tm,),
        in_specs=[pl.BlockSpec((tm, D), lambda i: (i, 0)), spec_o, spec_o, spec_o, _resident((K, D))],
        out_specs=pl.BlockSpec((tm, D), lambda i: (i, 0)),
        out_shape=jax.ShapeDtypeStruct((M, D), F32),
        compiler_params=_params("arbitrary"),
        name="nsa_outproj",
    )(x2, oc, os_, ow, w_out.astype(BF16))


def _nsa_layer(x, gn, w_in, cmp_pe, cmp_w1, cmp_w2, w_out):
    B, S, D = x.shape
    x2 = x.reshape(B * S, D)
    q, kc, vc, kscat, vs, kw, vw, gates = _nsa_proj(x2, gn, w_in, S)
    kcmp, vcmp = _compress(kc, vc, cmp_pe, cmp_w1, cmp_w2, B, S)
    o_cmp, bias = _cmp_attention(q, kcmp, vcmp, gates, B, S)
    o_sel = _sel_attention(q, bias, kscat, vs, gates, B, S)
    o_win = _win_attention(q, kw, vw, gates, B, S)
    return _outproj(x2, o_cmp, o_sel, o_win, w_out).reshape(B, S, D)


def kernel(x, norm_mix, norm_ffn, norm_final, lru_w_in, lru_b_in, lru_conv_w, lru_conv_b, lru_w_a, lru_b_a, lru_w_i, lru_b_i, lru_lambda, lru_w_out, lru_b_out, nsa_w_in, nsa_cmp_pe, nsa_cmp_w1, nsa_cmp_w2, nsa_w_out, ffn_w_gate, ffn_w_up, ffn_w_down, moe_w_router, moe_w_gate, moe_w_up, moe_w_down):
    B, S, D = x.shape
    assert norm_mix.shape[0] == 2 and lru_w_in.shape[0] == 1 and nsa_w_in.shape[0] == 1
    x = _lru_layer(x, norm_mix[0], lru_w_in[0], lru_b_in[0], lru_conv_w[0], lru_conv_b[0], lru_w_a[0],
                   lru_b_a[0], lru_w_i[0], lru_b_i[0], lru_lambda[0], lru_w_out[0], lru_b_out[0])
    x2 = _ffn_layer(x.reshape(B * S, D), norm_ffn[0], ffn_w_gate, ffn_w_up, ffn_w_down)
    x2 = _nsa_layer(x2.reshape(B, S, D), norm_mix[1], nsa_w_in[0], nsa_cmp_pe[0], nsa_cmp_w1[0],
                    nsa_cmp_w2[0], nsa_w_out[0]).reshape(B * S, D)
    gates = _router(x2, norm_ffn[1], moe_w_router[0])
    out = _ffn_layer(x2, norm_ffn[1], moe_w_gate[0], moe_w_up[0], moe_w_down[0], gates=gates, g_final=norm_final)
    return out.reshape(B, S, D)
```

```python
import functools

import numpy as np
import jax
import jax.numpy as jnp
from jax import lax
from jax.experimental import pallas as pl
from jax.experimental.pallas import tpu as pltpu

F32 = jnp.float32
BF16 = jnp.bfloat16

RMS_EPS = 1e-6
LRU_C = 8.0
CONV_WIDTH = 4
N_HEADS = 16
HEAD_DIM = 64
N_KV_GROUPS = 4
HEADS_PER_GROUP = N_HEADS // N_KV_GROUPS
CMP_BLOCK = 32
CMP_STRIDE = 16
SEL_BLOCK = 64
N_SEL = 16
WINDOW = 512
FORCE_SCORE = 1.0e4
ROPE_THETA = 10000.0
TOP_K = 2

LANES = 128
MASK_VALUE = -1.0e30
VMEM_LIMIT_BYTES = 56 * 1024 * 1024

_NT = (((1,), (1,)), ((), ()))
Q_SCALE = HEAD_DIM ** -0.5 * 1.4426950408889634


def _params(*semantics):
    return pltpu.CompilerParams(dimension_semantics=semantics, vmem_limit_bytes=VMEM_LIMIT_BYTES)


def _resident(shape):
    zeros = (0,) * len(shape)
    return pl.BlockSpec(shape, lambda *_: zeros, pipeline_mode=pl.Buffered(1))


def _rms(x, g):
    return x * lax.rsqrt(jnp.mean(x * x, axis=-1, keepdims=True) + RMS_EPS) * g


def _dot(a, b):
    return jnp.dot(a, b, preferred_element_type=F32)


def _lane_col(vals, lane, idx):
    return jnp.sum(jnp.where(lane == idx, vals, 0.0), axis=1, keepdims=True)


def _lru_body(x_ref, gn_ref, win_ref, bin_ref, cw_ref, cb_ref, wg_ref, bg_ref, lam_ref,
              wout_ref, bout_ref, o_ref, xbuf, hcar, *, ts, dr):
    t = pl.program_id(1)

    @pl.when(t == 0)
    def _():
        xbuf[0:8, :] = jnp.zeros((8, dr), F32)
        hcar[...] = jnp.zeros_like(hcar)

    x = x_ref[0]
    hn = _rms(x, gn_ref[...]).astype(BF16)
    proj = _dot(hn, win_ref[...]) + bin_ref[...]
    gate = jax.nn.gelu(proj[:, :dr])
    xr = proj[:, dr:]

    xbuf[8:8 + ts, :] = xr
    xc = cb_ref[...] + xr * cw_ref[CONV_WIDTH - 1:CONV_WIDTH, :]
    for lag in range(1, CONV_WIDTH):
        k = CONV_WIDTH - 1 - lag
        xc = xc + xbuf[pl.ds(8 - lag, ts), :] * cw_ref[k:k + 1, :]
    xbuf[0:8, :] = xbuf[ts:ts + 8, :]

    gl = _dot(xc.astype(BF16), wg_ref[...]) + bg_ref[...]
    r = jax.nn.sigmoid(gl[:, :dr])
    i = jax.nn.sigmoid(gl[:, dr:])
    z = -lam_ref[...]
    softplus = jnp.maximum(z, 0.0) + jnp.log(1.0 + jnp.exp(-jnp.abs(z)))
    log_a = (-LRU_C * r) * softplus
    a = jnp.exp(log_a)
    mult = jnp.sqrt(1.0 - a * a)
    row = lax.broadcasted_iota(jnp.int32, (ts, 1), 0)
    mult = jnp.where((row == 0) & (t == 0), 1.0, mult)
    u = mult * (i * xc)

    shift = 1
    while shift < ts:
        keep = row >= shift
        a_prev = jnp.where(keep, pltpu.roll(a, shift, 0), 1.0)
        u_prev = jnp.where(keep, pltpu.roll(u, shift, 0), 0.0)
        u = a * u_prev + u
        a = a * a_prev
        shift *= 2
    h = u + a * hcar[...]
    hcar[...] = h[ts - 1:ts, :]

    y = (h * gate).astype(BF16)
    o_ref[0] = x + _dot(y, wout_ref[...]) + bout_ref[...]


def _lru_layer(x, gn, w_in, b_in, conv_w, conv_b, w_a, b_a, w_i, b_i, lam, w_out, b_out):
    B, S, D = x.shape
    dr = w_out.shape[0]
    ts = min(256, S)
    assert S % ts == 0 and ts % 8 == 0
    wg = jnp.concatenate([jax.scipy.linalg.block_diag(*w_a), jax.scipy.linalg.block_diag(*w_i)], axis=1)
    row = lambda v: v.reshape(1, -1)
    body = functools.partial(_lru_body, ts=ts, dr=dr)
    return pl.pallas_call(
        body,
        grid=(B, S // ts),
        in_specs=[
            pl.BlockSpec((1, ts, D), lambda b, t: (b, t, 0)),
            _resident((1, D)),
            _resident((D, 2 * dr)), _resident((1, 2 * dr)),
            _resident((CONV_WIDTH, dr)), _resident((1, dr)),
            _resident((dr, 2 * dr)), _resident((1, 2 * dr)),
            _resident((1, dr)),
            _resident((dr, D)), _resident((1, D)),
        ],
        out_specs=pl.BlockSpec((1, ts, D), lambda b, t: (b, t, 0)),
        out_shape=jax.ShapeDtypeStruct((B, S, D), F32),
        scratch_shapes=[pltpu.VMEM((ts + 8, dr), F32), pltpu.VMEM((1, dr), F32)],
        compiler_params=_params("arbitrary", "arbitrary"),
        name="lru_mixer",
    )(x, row(gn), w_in.astype(BF16), row(b_in), conv_w, row(conv_b), wg.astype(BF16),
      row(jnp.concatenate([b_a, b_i])), row(lam), w_out.astype(BF16), row(b_out))


def _ffn_body(*refs, n_e, n_f, use_gates, final_norm):
    x_ref, gn_ref = refs[0], refs[1]
    k = 2
    gates_ref = gfin_ref = None
    if use_gates:
        gates_ref = refs[k]; k += 1
    wg_ref, wu_ref, wd_ref = refs[k:k + 3]; k += 3
    if final_norm:
        gfin_ref = refs[k]; k += 1
    o_ref, hn_ref, acc_ref = refs[k:k + 3]
    e = pl.program_id(1)
    f = pl.program_id(2)

    @pl.when((e == 0) & (f == 0))
    def _():
        hn_ref[...] = _rms(x_ref[...], gn_ref[...]).astype(BF16)
        acc_ref[...] = jnp.zeros_like(acc_ref)

    hn = hn_ref[...]
    g = _dot(hn, wg_ref[0])
    u = _dot(hn, wu_ref[0])
    act = (g * jax.nn.sigmoid(g)) * u
    if use_gates:
        gates = gates_ref[...]
        lane = lax.broadcasted_iota(jnp.int32, gates.shape, 1)
        act = act * _lane_col(gates, lane, e)
    acc_ref[...] += _dot(act.astype(BF16), wd_ref[0])

    @pl.when((e == n_e - 1) & (f == n_f - 1))
    def _():
        out = x_ref[...] + acc_ref[...]
        if final_norm:
            out = _rms(out, gfin_ref[...])
        o_ref[...] = out


def _ffn_layer(x2, gn, w_gate, w_up, w_down, gates=None, g_final=None, tm=1024, tf=512):
    M, D = x2.shape
    n_e, _, F = w_gate.shape
    tm = min(tm, M)
    tf = min(tf, F)
    assert M % tm == 0 and F % tf == 0
    n_f = F // tf
    use_gates = gates is not None
    final_norm = g_final is not None
    in_specs = [pl.BlockSpec((tm, D), lambda i, e, f: (i, 0)), _resident((1, D))]
    args = [x2, gn.reshape(1, D)]
    if use_gates:
        in_specs.append(pl.BlockSpec((tm, LANES), lambda i, e, f: (i, 0)))
        args.append(gates)
    in_specs += [
        pl.BlockSpec((1, D, tf), lambda i, e, f: (e, 0, f)),
        pl.BlockSpec((1, D, tf), lambda i, e, f: (e, 0, f)),
        pl.BlockSpec((1, tf, D), lambda i, e, f: (e, f, 0)),
    ]
    args += [w_gate.astype(BF16), w_up.astype(BF16), w_down.astype(BF16)]
    if final_norm:
        in_specs.append(_resident((1, D)))
        args.append(g_final.reshape(1, D))
    body = functools.partial(_ffn_body, n_e=n_e, n_f=n_f, use_gates=use_gates, final_norm=final_norm)
    return pl.pallas_call(
        body,
        grid=(M // tm, n_e, n_f),
        in_specs=in_specs,
        out_specs=pl.BlockSpec((tm, D), lambda i, e, f: (i, 0)),
        out_shape=jax.ShapeDtypeStruct((M, D), F32),
        scratch_shapes=[pltpu.VMEM((tm, D), BF16), pltpu.VMEM((tm, D), F32)],
        compiler_params=_params("arbitrary", "arbitrary", "arbitrary"),
        name="moe_swiglu" if use_gates else "dense_swiglu",
    )(*args)


def _router_body(x_ref, gn_ref, wh_ref, wl_ref, o_ref, *, n_experts):
    hn = _rms(x_ref[...], gn_ref[...])
    hh = hn.astype(BF16)
    hl = (hn - hh.astype(F32)).astype(BF16)
    wh = wh_ref[...]
    logits = _dot(hh, wh) + _dot(hh, wl_ref[...]) + _dot(hl, wh)
    lane = lax.broadcasted_iota(jnp.int32, logits.shape, 1)
    lg = jnp.where(lane < n_experts, logits, -jnp.inf)
    m0 = jnp.max(lg, axis=1, keepdims=True)
    i0 = jnp.min(jnp.where(lg == m0, lane, LANES), axis=1, keepdims=True)
    lg = jnp.where(lane == i0, -jnp.inf, lg)
    m1 = jnp.max(lg, axis=1, keepdims=True)
    i1 = jnp.min(jnp.where(lg == m1, lane, LANES), axis=1, keepdims=True)
    e1 = jnp.exp(m1 - m0)
    w0 = 1.0 / (1.0 + e1)
    o_ref[...] = jnp.where(lane == i0, w0, 0.0) + jnp.where(lane == i1, e1 * w0, 0.0)


def _router(x2, gn, w_router, tm=512):
    M, D = x2.shape
    n_experts = w_router.shape[1]
    tm = min(tm, M)
    wpad = jnp.pad(w_router, ((0, 0), (0, LANES - n_experts)))
    wh = wpad.astype(BF16)
    wl = (wpad - wh.astype(F32)).astype(BF16)
    return pl.pallas_call(
        functools.partial(_router_body, n_experts=n_experts),
        grid=(M // tm,),
        in_specs=[pl.BlockSpec((tm, D), lambda i: (i, 0)), _resident((1, D)),
                  _resident((D, LANES)), _resident((D, LANES))],
        out_specs=pl.BlockSpec((tm, LANES), lambda i: (i, 0)),
        out_shape=jax.ShapeDtypeStruct((M, LANES), F32),
        compiler_params=_params("arbitrary"),
        name="moe_router",
    )(x2, gn.reshape(1, D), wh, wl)


def _nsa_proj_body(x_ref, gn_ref, w_ref, cos_ref, sin_ref,
                   q_ref, kc_ref, vc_ref, ks_ref, vs_ref, kw_ref, vw_ref, g_ref, *, tm, tiles_per_seq):
    i = pl.program_id(0)
    hn = _rms(x_ref[...], gn_ref[...]).astype(BF16)
    proj = _dot(hn, w_ref[...])
    lane = lax.broadcasted_iota(jnp.int32, (tm, LANES), 1)
    first_half = (lane & (HEAD_DIM - 1)) < HEAD_DIM // 2
    cos = cos_ref[...]
    sin = sin_ref[...]

    def chunk(c):
        return proj[:, c * LANES:(c + 1) * LANES]

    def rope(v):
        rot = jnp.where(first_half, pltpu.roll(v, LANES - HEAD_DIM // 2, 1), pltpu.roll(v, HEAD_DIM // 2, 1))
        return v * cos + rot * sin

    def heads(v):
        return v[:, :HEAD_DIM], pltpu.roll(v, HEAD_DIM, 1)[:, :HEAD_DIM]

    c = 0
    for cc in range(N_HEADS // 2):
        lo, hi = heads(rope(chunk(c)) * Q_SCALE); c += 1
        q_ref[2 * cc] = lo.astype(BF16)
        q_ref[2 * cc + 1] = hi.astype(BF16)
    for cc in range(N_KV_GROUPS // 2):
        lo, hi = heads(rope(chunk(c))); c += 1
        kc_ref[2 * cc] = lo
        kc_ref[2 * cc + 1] = hi
    for cc in range(N_KV_GROUPS // 2):
        lo, hi = heads(chunk(c)); c += 1
        vc_ref[2 * cc] = lo
        vc_ref[2 * cc + 1] = hi
    pos = (i % tiles_per_seq) * tm + lax.broadcasted_iota(jnp.int32, (tm, 1), 0)
    onehot = (lane == pos // SEL_BLOCK).astype(BF16)
    for cc in range(N_KV_GROUPS // 2):
        v = rope(chunk(c)); c += 1
        for j, vv in enumerate((v, pltpu.roll(v, HEAD_DIM, 1))):
            ks_ref[2 * cc + j, :, 0:LANES] = jnp.where(lane < HEAD_DIM, vv, 0.0).astype(BF16)
            ks_ref[2 * cc + j, :, LANES:2 * LANES] = onehot
    ones_col = jnp.where(lane == HEAD_DIM, 1.0, 0.0)

    def store_values(ref):
        nonlocal c
        for cc in range(N_KV_GROUPS // 2):
            v = chunk(c); c += 1
            for j, vv in enumerate((v, pltpu.roll(v, HEAD_DIM, 1))):
                ref[2 * cc + j] = jnp.where(lane < HEAD_DIM, vv, ones_col).astype(BF16)

    store_values(vs_ref)
    for cc in range(N_KV_GROUPS // 2):
        lo, hi = heads(rope(chunk(c))); c += 1
        kw_ref[2 * cc] = lo.astype(BF16)
        kw_ref[2 * cc + 1] = hi.astype(BF16)
    store_values(vw_ref)
    g_ref[...] = jax.nn.sigmoid(chunk(c))


def _nsa_proj(x2, gn, w_in, S, tm=256):
    M, D = x2.shape
    H, G, DH = N_HEADS, N_KV_GROUPS, HEAD_DIM
    tm = min(tm, S)
    assert S % tm == 0
    n_in = w_in.shape[1]
    n_pad = -(-n_in // LANES) * LANES
    wp = jnp.pad(w_in, ((0, 0), (0, n_pad - n_in))).astype(BF16)
    half = DH // 2
    freqs = ROPE_THETA ** (-jnp.arange(half, dtype=F32) / half)
    ang = jnp.arange(S, dtype=F32)[:, None] * freqs[None, :]
    cos = jnp.tile(jnp.cos(ang), (1, 2 * LANES // DH))
    sin = jnp.tile(jnp.concatenate([-jnp.sin(ang), jnp.sin(ang)], axis=1), (1, LANES // DH))
    tiles_per_seq = S // tm
    hd = lambda n, dt: jax.ShapeDtypeStruct((n, M, DH), dt)
    hspec = lambda n: pl.BlockSpec((n, tm, DH), lambda i: (0, i, 0))
    vd = jax.ShapeDtypeStruct((G, M, LANES), BF16)
    vspec = pl.BlockSpec((G, tm, LANES), lambda i: (0, i, 0))
    return pl.pallas_call(
        functools.partial(_nsa_proj_body, tm=tm, tiles_per_seq=tiles_per_seq),
        grid=(M // tm,),
        in_specs=[pl.BlockSpec((tm, D), lambda i: (i, 0)), _resident((1, D)), _resident((D, n_pad)),
                  pl.BlockSpec((tm, LANES), lambda i: (i % tiles_per_seq, 0)),
                  pl.BlockSpec((tm, LANES), lambda i: (i % tiles_per_seq, 0))],
        out_specs=[hspec(H), hspec(G), hspec(G),
                   pl.BlockSpec((G, tm, 2 * LANES), lambda i: (0, i, 0)),
                   vspec, hspec(G), vspec,
                   pl.BlockSpec((tm, LANES), lambda i: (i, 0))],
        out_shape=[hd(H, BF16), hd(G, F32), hd(G, F32),
                   jax.ShapeDtypeStruct((G, M, 2 * LANES), BF16),
                   vd, hd(G, BF16), vd,
                   jax.ShapeDtypeStruct((M, LANES), F32)],
        compiler_params=_params("arbitrary"),
        name="nsa_proj",
    )(x2, gn.reshape(1, D), wp, cos, sin)


def _compress_body(kc_ref, vc_ref, pe_ref, w1_ref, w2_ref, ko_ref, vo_ref, *, nc):
    half = CMP_BLOCK // 2
    for kv, (src, dst) in enumerate(((kc_ref, ko_ref), (vc_ref, vo_ref))):
        top = jnp.zeros((nc, w1_ref.shape[2]), F32)
        bot = jnp.zeros((nc, w1_ref.shape[2]), F32)
        for j in range(half):
            xj = src[0, pl.ds(j, nc, stride=CMP_STRIDE), :]
            top = top + _dot((xj + pe_ref[kv, j:j + 1, :]).astype(BF16),
                             w1_ref[kv, j * HEAD_DIM:(j + 1) * HEAD_DIM, :])
            bot = bot + _dot((xj + pe_ref[kv, half + j:half + j + 1, :]).astype(BF16),
                             w1_ref[kv, (half + j) * HEAD_DIM:(half + j + 1) * HEAD_DIM, :])
        hid = top + pltpu.roll(bot, nc - 1, 0)
        dst[0] = _dot(jax.nn.gelu(hid).astype(BF16), w2_ref[kv]).astype(BF16)


def _compress(kc, vc, pe, w1, w2, B, S):
    assert CMP_BLOCK == 2 * CMP_STRIDE
    G, M, DH = kc.shape
    nc = S // CMP_STRIDE
    spec_in = pl.BlockSpec((1, S, DH), lambda b, g: (g, b, 0))
    spec_out = pl.BlockSpec((1, nc, DH), lambda b, g: (g, b, 0))
    out = jax.ShapeDtypeStruct((G, B * nc, DH), BF16)
    return pl.pallas_call(
        functools.partial(_compress_body, nc=nc),
        grid=(B, G),
        in_specs=[spec_in, spec_in, _resident(pe.shape), _resident(w1.shape), _resident(w2.shape)],
        out_specs=[spec_out, spec_out],
        out_shape=[out, out],
        compiler_params=_params("arbitrary", "arbitrary"),
        name="nsa_compress",
    )(kc, vc, pe, w1.astype(BF16), w2.astype(BF16))


def _store_head(o_ref, oh, gates, g, h, branch):
    lane = lax.broadcasted_iota(jnp.int32, gates.shape, 1)
    col = 3 * (HEADS_PER_GROUP * g + h) + branch
    o_ref[:, h * HEAD_DIM:(h + 1) * HEAD_DIM] = (oh * _lane_col(gates, lane, col)).astype(o_ref.dtype)


def _normalized_head(acc):
    return acc[:, :HEAD_DIM] * (1.0 / acc[:, HEAD_DIM:HEAD_DIM + 1])


def _cmp_body(q_ref, kc_ref, vc_ref, gates_ref, ov_ref, o_ref, bias_ref, *, tq, nc, n_s, k_sel):
    g = pl.program_id(1)
    i = pl.program_id(2)
    gates = gates_ref[...]
    kc = kc_ref[0]
    vc = vc_ref[0]
    t = i * tq + lax.broadcasted_iota(jnp.int32, (tq, 1), 0)
    cend = lax.broadcasted_iota(jnp.int32, (1, nc), 1) * CMP_STRIDE + (CMP_BLOCK - 1)
    mask = cend <= t
    ps = None
    for h in range(HEADS_PER_GROUP):
        s = lax.dot_general(q_ref[h], kc, _NT, preferred_element_type=F32)
        s = jnp.where(mask, s, MASK_VALUE)
        m = jnp.max(s, axis=1, keepdims=True)
        m = jnp.where(m > 0.5 * MASK_VALUE, m, 0.0)
        e = jnp.exp2(s - m)
        p = e * (1.0 / jnp.maximum(jnp.sum(e, axis=1, keepdims=True), 1e-30))
        _store_head(o_ref, _dot(p.astype(BF16), vc), gates, g, h, 0)
        ps = p if ps is None else ps + p

    ph = ps.astype(BF16)
    pl_ = (ps - ph.astype(F32)).astype(BF16)
    imp = _dot(ph, ov_ref[...]) + _dot(pl_, ov_ref[...])
    j = lax.broadcasted_iota(jnp.int32, (tq, LANES), 1)
    cur = t // SEL_BLOCK
    forced = (j == 0) | (j == cur) | (j == cur - 1)
    valid = j * SEL_BLOCK <= t
    score = jnp.where(forced, FORCE_SCORE, jnp.where(valid, imp, -1.0))
    score = jnp.where(j < n_s, score, -jnp.inf)
    x = score.T
    blk = lax.broadcasted_iota(jnp.int32, (LANES, tq), 0).astype(F32)
    sel = jnp.zeros((LANES, tq), F32)
    for _ in range(k_sel):
        mx = jnp.max(x, axis=0, keepdims=True)
        idx = jnp.min(jnp.where(x == mx, blk, float(LANES)), axis=0, keepdims=True)
        hit = blk == idx
        sel = jnp.where(hit, 1.0, sel)
        x = jnp.where(hit, -jnp.inf, x)
    bias_ref[0] = jnp.where(sel.T > 0.5, 0.0, MASK_VALUE).astype(BF16)


def _overlap_matrix(nc, n_s):
    r = CMP_BLOCK // CMP_STRIDE
    qn = SEL_BLOCK // CMP_STRIDE
    m = np.zeros((nc, LANES), np.float32)
    n_c = nc - r + 1
    chunks = np.arange(n_c)[:, None] + np.arange(r)[None, :]
    np.add.at(m, (np.repeat(np.arange(n_c), r), (chunks // qn).ravel()), 1.0)
    return m


def _cmp_attention(q, kcmp, vcmp, gates, B, S, tq=256):
    H, M, DH = q.shape
    G = N_KV_GROUPS
    tq = min(tq, S)
    nq = S // tq
    nc = S // CMP_STRIDE
    n_s = S // SEL_BLOCK
    assert n_s <= LANES and tq & (tq - 1) == 0
    k_sel = min(N_SEL, n_s)
    ov = jnp.asarray(_overlap_matrix(nc, n_s), BF16)
    return pl.pallas_call(
        functools.partial(_cmp_body, tq=tq, nc=nc, n_s=n_s, k_sel=k_sel),
        grid=(B, G, nq),
        in_specs=[pl.BlockSpec((HEADS_PER_GROUP, tq, DH), lambda b, g, i: (g, b * nq + i, 0)),
                  pl.BlockSpec((1, nc, DH), lambda b, g, i: (g, b, 0)),
                  pl.BlockSpec((1, nc, DH), lambda b, g, i: (g, b, 0)),
                  pl.BlockSpec((tq, LANES), lambda b, g, i: (b * nq + i, 0)),
                  _resident((nc, LANES))],
        out_specs=[pl.BlockSpec((tq, HEADS_PER_GROUP * DH), lambda b, g, i: (b * nq + i, g)),
                   pl.BlockSpec((1, tq, LANES), lambda b, g, i: (g, b * nq + i, 0))],
        out_shape=[jax.ShapeDtypeStruct((M, H * DH), BF16), jax.ShapeDtypeStruct((G, M, LANES), BF16)],
        compiler_params=_params("arbitrary", "arbitrary", "arbitrary"),
        name="nsa_cmp_select",
    )(q, kcmp, vcmp, gates, ov)


def _sel_body(q_ref, bias_ref, k_ref, v_ref, gates_ref, o_ref, qcat, m_ref, acc_ref, *, tq):
    g = pl.program_id(1)
    i = pl.program_id(2)
    hp = HEADS_PER_GROUP
    rows = hp * tq
    qcat[:, 0:LANES] = jnp.zeros((rows, LANES), BF16)
    qcat[:, 0:HEAD_DIM] = q_ref[...].reshape(rows, HEAD_DIM)
    bias = bias_ref[0]
    for h in range(hp):
        qcat[h * tq:(h + 1) * tq, LANES:2 * LANES] = bias
    m_ref[...] = jnp.full((rows, LANES), MASK_VALUE, F32)
    acc_ref[...] = jnp.zeros((rows, LANES), F32)

    def tile(j, causal):
        start = pl.multiple_of(j * tq, tq)
        kt = k_ref[0, pl.ds(start, tq), :]
        vt = v_ref[0, pl.ds(start, tq), :]
        for h in range(hp):
            r0 = h * tq
            s = lax.dot_general(qcat[r0:r0 + tq, :], kt, _NT, preferred_element_type=F32)
            if causal:
                r = lax.broadcasted_iota(jnp.int32, (tq, 1), 0)
                c = lax.broadcasted_iota(jnp.int32, (1, tq), 1)
                s = jnp.where(c <= r, s, MASK_VALUE)
            m_prev = m_ref[r0:r0 + tq, :]
            m_new = jnp.maximum(m_prev, jnp.max(s, axis=1, keepdims=True))
            alpha = jnp.exp2(m_prev - m_new)
            p = jnp.exp2(s - jnp.concatenate([m_new] * (tq // LANES), axis=1))
            acc_ref[r0:r0 + tq, :] = alpha * acc_ref[r0:r0 + tq, :] + _dot(p.astype(BF16), vt)
            m_ref[r0:r0 + tq, :] = m_new

    def full_tile(j, carry):
        tile(j, False)
        return carry

    lax.fori_loop(0, i, full_tile, 0)
    tile(i, True)
    gates = gates_ref[...]
    for h in range(hp):
        _store_head(o_ref, _normalized_head(acc_ref[h * tq:(h + 1) * tq, :]), gates, g, h, 1)


def _sel_attention(q, bias, kscat, vs, gates, B, S, tq=512):
    H, M, DH = q.shape
    G = N_KV_GROUPS
    tq = min(tq, S)
    nq = S // tq
    assert tq % LANES == 0 and tq % SEL_BLOCK == 0 and S % tq == 0
    rows = HEADS_PER_GROUP * tq
    return pl.pallas_call(
        functools.partial(_sel_body, tq=tq),
        grid=(B, G, nq),
        in_specs=[pl.BlockSpec((HEADS_PER_GROUP, tq, DH), lambda b, g, i: (g, b * nq + i, 0)),
                  pl.BlockSpec((1, tq, LANES), lambda b, g, i: (g, b * nq + i, 0)),
                  pl.BlockSpec((1, S, 2 * LANES), lambda b, g, i: (g, b, 0)),
                  pl.BlockSpec((1, S, LANES), lambda b, g, i: (g, b, 0)),
                  pl.BlockSpec((tq, LANES), lambda b, g, i: (b * nq + i, 0))],
        out_specs=pl.BlockSpec((tq, HEADS_PER_GROUP * DH), lambda b, g, i: (b * nq + i, g)),
        out_shape=jax.ShapeDtypeStruct((M, H * DH), BF16),
        scratch_shapes=[pltpu.VMEM((rows, 2 * LANES), BF16), pltpu.VMEM((rows, LANES), F32),
                        pltpu.VMEM((rows, LANES), F32)],
        compiler_params=_params("arbitrary", "arbitrary", "arbitrary"),
        name="nsa_selected",
    )(q, bias, kscat, vs, gates)


def _win_body(q_ref, k_ref, v_ref, gates_ref, o_ref, *, tq, n_back):
    g = pl.program_id(1)
    i = pl.program_id(2)
    gates = gates_ref[...]
    r = lax.broadcasted_iota(jnp.int32, (tq, 1), 0)
    c = lax.broadcasted_iota(jnp.int32, (1, tq), 1)
    tiles = []
    for back in range(n_back, -1, -1):
        jt = i - back
        start = pl.multiple_of(jnp.maximum(jt, 0) * tq, tq)
        d = r - c + back * tq
        ok = (d >= 0) & (d < WINDOW) & (jt >= 0)
        tiles.append((k_ref[0, pl.ds(start, tq), :], v_ref[0, pl.ds(start, tq), :], ok))
    for h in range(HEADS_PER_GROUP):
        q = q_ref[h]
        scores = [jnp.where(ok, lax.dot_general(q, kt, _NT, preferred_element_type=F32), MASK_VALUE)
                  for kt, _, ok in tiles]
        m = scores[0].max(axis=1, keepdims=True)
        for s in scores[1:]:
            m = jnp.maximum(m, s.max(axis=1, keepdims=True))
        acc = jnp.zeros((tq, LANES), F32)
        for s, (_, vt, _) in zip(scores, tiles):
            acc = acc + _dot(jnp.exp2(s - m).astype(BF16), vt)
        _store_head(o_ref, _normalized_head(acc), gates, g, h, 2)


def _win_attention(q, kw, vw, gates, B, S, tq=256):
    H, M, DH = q.shape
    G = N_KV_GROUPS
    tq = min(tq, S)
    nq = S // tq
    assert tq & (tq - 1) == 0
    n_back = -(-WINDOW // tq)
    return pl.pallas_call(
        functools.partial(_win_body, tq=tq, n_back=n_back),
        grid=(B, G, nq),
        in_specs=[pl.BlockSpec((HEADS_PER_GROUP, tq, DH), lambda b, g, i: (g, b * nq + i, 0)),
                  pl.BlockSpec((1, S, DH), lambda b, g, i: (g, b, 0)),
                  pl.BlockSpec((1, S, LANES), lambda b, g, i: (g, b, 0)),
                  pl.BlockSpec((tq, LANES), lambda b, g, i: (b * nq + i, 0))],
        out_specs=pl.BlockSpec((tq, HEADS_PER_GROUP * DH), lambda b, g, i: (b * nq + i, g)),
        out_shape=jax.ShapeDtypeStruct((M, H * DH), BF16),
        compiler_params=_params("arbitrary", "arbitrary", "arbitrary"),
        name="nsa_window",
    )(q, kw, vw, gates)


def _outproj_body(x_ref, a_ref, b_ref, c_ref, w_ref, o_ref):
    o = a_ref[...].astype(F32) + b_ref[...].astype(F32) + c_ref[...].astype(F32)
    o_ref[...] = x_ref[...] + _dot(o.astype(BF16), w_ref[...])


def _outproj(x2, oc, os_, ow, w_out, tm=512):
    M, D = x2.shape
    K = w_out.shape[0]
    tm = min(tm, M)
    spec_o = pl.BlockSpec((tm, K), lambda i: (i, 0))
    return pl.pallas_call(
        _outproj_body,
        grid=(M // tm,),
        in_specs=[pl.BlockSpec((tm, D), lambda i: (i, 0)), spec_o, spec_o, spec_o, _resident((K, D))],
        out_specs=pl.BlockSpec((tm, D), lambda i: (i, 0)),
        out_shape=jax.ShapeDtypeStruct((M, D), F32),
        compiler_params=_params("arbitrary"),
        name="nsa_outproj",
    )(x2, oc, os_, ow, w_out.astype(BF16))


def _nsa_layer(x, gn, w_in, cmp_pe, cmp_w1, cmp_w2, w_out):
    B, S, D = x.shape
    x2 = x.reshape(B * S, D)
    q, kc, vc, kscat, vs, kw, vw, gates = _nsa_proj(x2, gn, w_in, S)
    kcmp, vcmp = _compress(kc, vc, cmp_pe, cmp_w1, cmp_w2, B, S)
    o_cmp, bias = _cmp_attention(q, kcmp, vcmp, gates, B, S)
    o_sel = _sel_attention(q, bias, kscat, vs, gates, B, S)
    o_win = _win_attention(q, kw, vw, gates, B, S)
    return _outproj(x2, o_cmp, o_sel, o_win, w_out).reshape(B, S, D)


def kernel(x, norm_mix, norm_ffn, norm_final, lru_w_in, lru_b_in, lru_conv_w, lru_conv_b, lru_w_a, lru_b_a, lru_w_i, lru_b_i, lru_lambda, lru_w_out, lru_b_out, nsa_w_in, nsa_cmp_pe, nsa_cmp_w1, nsa_cmp_w2, nsa_w_out, ffn_w_gate, ffn_w_up, ffn_w_down, moe_w_router, moe_w_gate, moe_w_up, moe_w_down):
    B, S, D = x.shape
    assert norm_mix.shape[0] == 2 and lru_w_in.shape[0] == 1 and nsa_w_in.shape[0] == 1
    x = _lru_layer(x, norm_mix[0], lru_w_in[0], lru_b_in[0], lru_conv_w[0], lru_conv_b[0], lru_w_a[0],
                   lru_b_a[0], lru_w_i[0], lru_b_i[0], lru_lambda[0], lru_w_out[0], lru_b_out[0])
    x2 = _ffn_layer(x.reshape(B * S, D), norm_ffn[0], ffn_w_gate, ffn_w_up, ffn_w_down)
    x2 = _nsa_layer(x2.reshape(B, S, D), norm_mix[1], nsa_w_in[0], nsa_cmp_pe[0], nsa_cmp_w1[0],
                    nsa_cmp_w2[0], nsa_w_out[0]).reshape(B * S, D)
    gates = _router(x2, norm_ffn[1], moe_w_router[0])
    out = _ffn_layer(x2, norm_ffn[1], moe_w_gate[0], moe_w_up[0], moe_w_down[0], gates=gates, g_final=norm_final)
    return out.reshape(B, S, D)
```

```python
import functools

import numpy as np
import jax
import jax.numpy as jnp
from jax import lax
from jax.experimental import pallas as pl
from jax.experimental.pallas import tpu as pltpu

F32 = jnp.float32
BF16 = jnp.bfloat16

RMS_EPS = 1e-6
LRU_C = 8.0
CONV_WIDTH = 4
N_HEADS = 16
HEAD_DIM = 64
N_KV_GROUPS = 4
HEADS_PER_GROUP = N_HEADS // N_KV_GROUPS
CMP_BLOCK = 32
CMP_STRIDE = 16
SEL_BLOCK = 64
N_SEL = 16
WINDOW = 512
FORCE_SCORE = 1.0e4
ROPE_THETA = 10000.0
TOP_K = 2

LANES = 128
MASK_VALUE = -1.0e30
VMEM_LIMIT_BYTES = 56 * 1024 * 1024

_NT = (((1,), (1,)), ((), ()))
Q_SCALE = HEAD_DIM ** -0.5 * 1.4426950408889634


def _params(*semantics):
    return pltpu.CompilerParams(dimension_semantics=semantics, vmem_limit_bytes=VMEM_LIMIT_BYTES)


def _resident(shape):
    zeros = (0,) * len(shape)
    return pl.BlockSpec(shape, lambda *_: zeros, pipeline_mode=pl.Buffered(1))


def _rms(x, g):
    return x * lax.rsqrt(jnp.mean(x * x, axis=-1, keepdims=True) + RMS_EPS) * g


def _dot(a, b):
    return jnp.dot(a, b, preferred_element_type=F32)


def _lane_col(vals, lane, idx):
    return jnp.sum(jnp.where(lane == idx, vals, 0.0), axis=1, keepdims=True)


def _lru_body(x_ref, gn_ref, win_ref, bin_ref, cw_ref, cb_ref, wg_ref, bg_ref, lam_ref,
              wout_ref, bout_ref, o_ref, xbuf, hcar, *, ts, dr):
    t = pl.program_id(1)

    @pl.when(t == 0)
    def _():
        xbuf[0:8, :] = jnp.zeros((8, dr), F32)
        hcar[...] = jnp.zeros_like(hcar)

    x = x_ref[0]
    hn = _rms(x, gn_ref[...]).astype(BF16)
    proj = _dot(hn, win_ref[...]) + bin_ref[...]
    gate = jax.nn.gelu(proj[:, :dr])
    xr = proj[:, dr:]

    xbuf[8:8 + ts, :] = xr
    xc = cb_ref[...] + xr * cw_ref[CONV_WIDTH - 1:CONV_WIDTH, :]
    for lag in range(1, CONV_WIDTH):
        k = CONV_WIDTH - 1 - lag
        xc = xc + xbuf[pl.ds(8 - lag, ts), :] * cw_ref[k:k + 1, :]
    xbuf[0:8, :] = xbuf[ts:ts + 8, :]

    gl = _dot(xc.astype(BF16), wg_ref[...]) + bg_ref[...]
    r = jax.nn.sigmoid(gl[:, :dr])
    i = jax.nn.sigmoid(gl[:, dr:])
    z = -lam_ref[...]
    softplus = jnp.maximum(z, 0.0) + jnp.log(1.0 + jnp.exp(-jnp.abs(z)))
    log_a = (-LRU_C * r) * softplus
    a = jnp.exp(log_a)
    mult = jnp.sqrt(1.0 - a * a)
    row = lax.broadcasted_iota(jnp.int32, (ts, 1), 0)
    mult = jnp.where((row == 0) & (t == 0), 1.0, mult)
    u = mult * (i * xc)

    shift = 1
    while shift < ts:
        keep = row >= shift
        a_prev = jnp.where(keep, pltpu.roll(a, shift, 0), 1.0)
        u_prev = jnp.where(keep, pltpu.roll(u, shift, 0), 0.0)
        u = a * u_prev + u
        a = a * a_prev
        shift *= 2
    h = u + a * hcar[...]
    hcar[...] = h[ts - 1:ts, :]

    y = (h * gate).astype(BF16)
    o_ref[0] = x + _dot(y, wout_ref[...]) + bout_ref[...]


def _lru_layer(x, gn, w_in, b_in, conv_w, conv_b, w_a, b_a, w_i, b_i, lam, w_out, b_out):
    B, S, D = x.shape
    dr = w_out.shape[0]
    ts = min(256, S)
    assert S % ts == 0 and ts % 8 == 0
    wg = jnp.concatenate([jax.scipy.linalg.block_diag(*w_a), jax.scipy.linalg.block_diag(*w_i)], axis=1)
    row = lambda v: v.reshape(1, -1)
    body = functools.partial(_lru_body, ts=ts, dr=dr)
    return pl.pallas_call(
        body,
        grid=(B, S // ts),
        in_specs=[
            pl.BlockSpec((1, ts, D), lambda b, t: (b, t, 0)),
            _resident((1, D)),
            _resident((D, 2 * dr)), _resident((1, 2 * dr)),
            _resident((CONV_WIDTH, dr)), _resident((1, dr)),
            _resident((dr, 2 * dr)), _resident((1, 2 * dr)),
            _resident((1, dr)),
            _resident((dr, D)), _resident((1, D)),
        ],
        out_specs=pl.BlockSpec((1, ts, D), lambda b, t: (b, t, 0)),
        out_shape=jax.ShapeDtypeStruct((B, S, D), F32),
        scratch_shapes=[pltpu.VMEM((ts + 8, dr), F32), pltpu.VMEM((1, dr), F32)],
        compiler_params=_params("arbitrary", "arbitrary"),
        name="lru_mixer",
    )(x, row(gn), w_in.astype(BF16), row(b_in), conv_w, row(conv_b), wg.astype(BF16),
      row(jnp.concatenate([b_a, b_i])), row(lam), w_out.astype(BF16), row(b_out))


def _ffn_body(*refs, n_e, n_f, use_gates, final_norm):
    x_ref, gn_ref = refs[0], refs[1]
    k = 2
    gates_ref = gfin_ref = None
    if use_gates:
        gates_ref = refs[k]; k += 1
    wg_ref, wu_ref, wd_ref = refs[k:k + 3]; k += 3
    if final_norm:
        gfin_ref = refs[k]; k += 1
    o_ref, hn_ref, acc_ref = refs[k:k + 3]
    e = pl.program_id(1)
    f = pl.program_id(2)

    @pl.when((e == 0) & (f == 0))
    def _():
        hn_ref[...] = _rms(x_ref[...], gn_ref[...]).astype(BF16)
        acc_ref[...] = jnp.zeros_like(acc_ref)

    hn = hn_ref[...]
    g = _dot(hn, wg_ref[0])
    u = _dot(hn, wu_ref[0])
    act = (g * jax.nn.sigmoid(g)) * u
    if use_gates:
        gates = gates_ref[...]
        lane = lax.broadcasted_iota(jnp.int32, gates.shape, 1)
        act = act * _lane_col(gates, lane, e)
    acc_ref[...] += _dot(act.astype(BF16), wd_ref[0])

    @pl.when((e == n_e - 1) & (f == n_f - 1))
    def _():
        out = x_ref[...] + acc_ref[...]
        if final_norm:
            out = _rms(out, gfin_ref[...])
        o_ref[...] = out


def _ffn_layer(x2, gn, w_gate, w_up, w_down, gates=None, g_final=None, tm=1024, tf=512):
    M, D = x2.shape
    n_e, _, F = w_gate.shape
    tm = min(tm, M)
    tf = min(tf, F)
    assert M % tm == 0 and F % tf == 0
    n_f = F // tf
    use_gates = gates is not None
    final_norm = g_final is not None
    in_specs = [pl.BlockSpec((tm, D), lambda i, e, f: (i, 0)), _resident((1, D))]
    args = [x2, gn.reshape(1, D)]
    if use_gates:
        in_specs.append(pl.BlockSpec((tm, LANES), lambda i, e, f: (i, 0)))
        args.append(gates)
    in_specs += [
        pl.BlockSpec((1, D, tf), lambda i, e, f: (e, 0, f)),
        pl.BlockSpec((1, D, tf), lambda i, e, f: (e, 0, f)),
        pl.BlockSpec((1, tf, D), lambda i, e, f: (e, f, 0)),
    ]
    args += [w_gate.astype(BF16), w_up.astype(BF16), w_down.astype(BF16)]
    if final_norm:
        in_specs.append(_resident((1, D)))
        args.append(g_final.reshape(1, D))
    body = functools.partial(_ffn_body, n_e=n_e, n_f=n_f, use_gates=use_gates, final_norm=final_norm)
    return pl.pallas_call(
        body,
        grid=(M // tm, n_e, n_f),
        in_specs=in_specs,
        out_specs=pl.BlockSpec((tm, D), lambda i, e, f: (i, 0)),
        out_shape=jax.ShapeDtypeStruct((M, D), F32),
        scratch_shapes=[pltpu.VMEM((tm, D), BF16), pltpu.VMEM((tm, D), F32)],
        compiler_params=_params("arbitrary", "arbitrary", "arbitrary"),
        name="moe_swiglu" if use_gates else "dense_swiglu",
    )(*args)


def _router_body(x_ref, gn_ref, wh_ref, wl_ref, tril_ref, o_ref, cnt_ref, *, n_experts):
    @pl.when(pl.program_id(0) == 0)
    def _():
        cnt_ref[...] = jnp.zeros_like(cnt_ref)

    hn = _rms(x_ref[...], gn_ref[...])
    hh = hn.astype(BF16)
    hl = (hn - hh.astype(F32)).astype(BF16)
    wh = wh_ref[...]
    logits = _dot(hh, wh) + _dot(hh, wl_ref[...]) + _dot(hl, wh)
    lane = lax.broadcasted_iota(jnp.int32, logits.shape, 1)
    lg = jnp.where(lane < n_experts, logits, -jnp.inf)
    m0 = jnp.max(lg, axis=1, keepdims=True)
    i0 = jnp.min(jnp.where(lg == m0, lane, LANES), axis=1, keepdims=True)
    lg = jnp.where(lane == i0, -jnp.inf, lg)
    m1 = jnp.max(lg, axis=1, keepdims=True)
    i1 = jnp.min(jnp.where(lg == m1, lane, LANES), axis=1, keepdims=True)
    e1 = jnp.exp(m1 - m0)
    w0 = 1.0 / (1.0 + e1)
    routed = ((lane == i0) | (lane == i1)).astype(F32)
    incl = _dot(tril_ref[...], routed.astype(BF16))
    excl = incl - routed + cnt_ref[0:1, :]

    def put(col, v):
        return jnp.where(lane == col, v, 0.0)

    o_ref[...] = (put(META_I0, i0.astype(F32)) + put(META_I1, i1.astype(F32)) + put(META_W0, w0)
                  + put(META_W1, e1 * w0) + put(META_R0, _lane_col(excl, lane, i0))
                  + put(META_R1, _lane_col(excl, lane, i1)))
    cnt_ref[...] = jnp.broadcast_to(cnt_ref[0:1, :] + incl[incl.shape[0] - 1:, :], cnt_ref.shape)


META_I0, META_I1, META_W0, META_W1, META_R0, META_R1 = range(6)


def _router(x2, gn, w_router, tm=512):
    M, D = x2.shape
    n_experts = w_router.shape[1]
    tm = min(tm, M)
    wpad = jnp.pad(w_router, ((0, 0), (0, LANES - n_experts)))
    wh = wpad.astype(BF16)
    wl = (wpad - wh.astype(F32)).astype(BF16)
    tril = jnp.tril(jnp.ones((tm, tm), BF16))
    return pl.pallas_call(
        functools.partial(_router_body, n_experts=n_experts),
        grid=(M // tm,),
        in_specs=[pl.BlockSpec((tm, D), lambda i: (i, 0)), _resident((1, D)),
                  _resident((D, LANES)), _resident((D, LANES)), _resident((tm, tm))],
        out_specs=[pl.BlockSpec((tm, LANES), lambda i: (i, 0)), pl.BlockSpec((8, LANES), lambda i: (0, 0))],
        out_shape=[jax.ShapeDtypeStruct((M, LANES), F32), jax.ShapeDtypeStruct((8, LANES), F32)],
        compiler_params=_params("arbitrary"),
        name="moe_router",
    )(x2, gn.reshape(1, D), wh, wl, tril)


def _gather_rows_body(idx_ref, table_ref, out_ref, sem, *, rt):
    base = pl.program_id(0) * rt

    def row_copy(r):
        return pltpu.make_async_copy(table_ref.at[pl.ds(idx_ref[0, 0, r], 1)], out_ref.at[pl.ds(base + r, 1)], sem)

    def start(r, c):
        row_copy(r).start()
        return c

    def wait(r, c):
        row_copy(r).wait()
        return c

    lax.fori_loop(0, rt, start, 0, unroll=8)
    lax.fori_loop(0, rt, wait, 0, unroll=8)


def _gather_rows(table, idx, rt=512):
    R = idx.shape[0]
    assert R % rt == 0
    return pl.pallas_call(
        functools.partial(_gather_rows_body, rt=rt),
        grid=(R // rt,),
        in_specs=[pl.BlockSpec((1, 1, rt), lambda i: (i, 0, 0), memory_space=pltpu.SMEM),
                  pl.BlockSpec(memory_space=pl.ANY)],
        out_specs=pl.BlockSpec(memory_space=pl.ANY),
        out_shape=jax.ShapeDtypeStruct((R, table.shape[1]), table.dtype),
        scratch_shapes=[pltpu.SemaphoreType.DMA(())],
        compiler_params=_params("arbitrary"),
        name="moe_dispatch",
    )(idx.reshape(R // rt, 1, rt), table)


def _combine_body(pos_ref, x_ref, meta_ref, y_ref, gfin_ref, o_ref, ybuf, sem, *, tm):
    def row_copy(k, r):
        return pltpu.make_async_copy(y_ref.at[pl.ds(pos_ref[0, 0, k * tm + r], 1)], ybuf.at[k, pl.ds(r, 1)], sem)

    def start(r, c):
        row_copy(0, r).start()
        row_copy(1, r).start()
        return c

    def wait(r, c):
        row_copy(0, r).wait()
        row_copy(1, r).wait()
        return c

    lax.fori_loop(0, tm, start, 0, unroll=8)
    lax.fori_loop(0, tm, wait, 0, unroll=8)
    meta = meta_ref[...]
    lane = lax.broadcasted_iota(jnp.int32, meta.shape, 1)
    out = (x_ref[...] + _lane_col(meta, lane, META_W0) * ybuf[0] + _lane_col(meta, lane, META_W1) * ybuf[1])
    o_ref[...] = _rms(out, gfin_ref[...])


def _combine(x2, meta, y, pos0, pos1, g_final, tm=512):
    M, D = x2.shape
    tm = min(tm, M)
    pos = jnp.concatenate([pos0.reshape(M // tm, 1, tm), pos1.reshape(M // tm, 1, tm)], axis=2)
    return pl.pallas_call(
        functools.partial(_combine_body, tm=tm),
        grid=(M // tm,),
        in_specs=[pl.BlockSpec((1, 1, 2 * tm), lambda i: (i, 0, 0), memory_space=pltpu.SMEM),
                  pl.BlockSpec((tm, D), lambda i: (i, 0)),
                  pl.BlockSpec((tm, LANES), lambda i: (i, 0)),
                  pl.BlockSpec(memory_space=pl.ANY),
                  _resident((1, D))],
        out_specs=pl.BlockSpec((tm, D), lambda i: (i, 0)),
        out_shape=jax.ShapeDtypeStruct((M, D), F32),
        scratch_shapes=[pltpu.VMEM((2, tm, D), F32), pltpu.SemaphoreType.DMA(())],
        compiler_params=_params("arbitrary"),
        name="moe_combine",
    )(pos, x2, meta, y, g_final.reshape(1, D))


def _grouped_ffn_body(te_ref, nused_ref, x_ref, gn_ref, wg_ref, wu_ref, wd_ref, o_ref, hn_ref, acc_ref, *, n_f):
    j = pl.program_id(0)
    f = pl.program_id(1)
    used = j < nused_ref[0]

    @pl.when(f == 0)
    def _():
        hn_ref[...] = _rms(x_ref[...], gn_ref[...]).astype(BF16)
        acc_ref[...] = jnp.zeros_like(acc_ref)

    @pl.when(used)
    def _():
        hn = hn_ref[...]
        g = _dot(hn, wg_ref[0])
        u = _dot(hn, wu_ref[0])
        acc_ref[...] += _dot(((g * jax.nn.sigmoid(g)) * u).astype(BF16), wd_ref[0])

    @pl.when(f == n_f - 1)
    def _():
        o_ref[...] = acc_ref[...]


def _grouped_ffn(xs, gn, w_gate, w_up, w_down, tile_expert, n_used, tm, tf=512):
    P, D = xs.shape
    F = w_gate.shape[2]
    tf = min(tf, F)
    n_f = F // tf
    grid_spec = pltpu.PrefetchScalarGridSpec(
        num_scalar_prefetch=2,
        grid=(P // tm, n_f),
        in_specs=[pl.BlockSpec((tm, D), lambda j, f, te, nu: (j, 0)),
                  pl.BlockSpec((1, D), lambda j, f, te, nu: (0, 0)),
                  pl.BlockSpec((1, D, tf), lambda j, f, te, nu: (te[j], 0, f)),
                  pl.BlockSpec((1, D, tf), lambda j, f, te, nu: (te[j], 0, f)),
                  pl.BlockSpec((1, tf, D), lambda j, f, te, nu: (te[j], f, 0))],
        out_specs=pl.BlockSpec((tm, D), lambda j, f, te, nu: (j, 0)),
        scratch_shapes=[pltpu.VMEM((tm, D), BF16), pltpu.VMEM((tm, D), F32)])
    return pl.pallas_call(
        functools.partial(_grouped_ffn_body, n_f=n_f),
        grid_spec=grid_spec,
        out_shape=jax.ShapeDtypeStruct((P, D), F32),
        compiler_params=_params("arbitrary", "arbitrary"),
        name="moe_grouped_swiglu",
    )(tile_expert, n_used, xs, gn.reshape(1, D), w_gate.astype(BF16), w_up.astype(BF16), w_down.astype(BF16))


def _moe_layer(x2, gn, w_router, w_gate, w_up, w_down, g_final, tm=512):
    M, D = x2.shape
    n_e = w_router.shape[1]
    tm = min(tm, M)
    meta, counts = _router(x2, gn, w_router)
    as_int = lambda col: meta[:, col].astype(jnp.int32)
    i0, i1, r0, r1 = as_int(META_I0), as_int(META_I1), as_int(META_R0), as_int(META_R1)
    padded = (counts[0, :n_e].astype(jnp.int32) + tm - 1) // tm * tm
    ends = jnp.cumsum(padded)
    offsets = ends - padded
    pos0 = offsets[i0] + r0
    pos1 = offsets[i1] + r1
    P = TOP_K * M + n_e * tm
    tok = jnp.arange(M, dtype=jnp.int32)
    src = jnp.zeros((P,), jnp.int32).at[pos0].set(tok).at[pos1].set(tok)
    tile_start = jnp.arange(P // tm, dtype=jnp.int32) * tm
    tile_expert = jnp.minimum(jnp.searchsorted(ends, tile_start, side="right"), n_e - 1).astype(jnp.int32)
    n_used = (ends[n_e - 1:] // tm).astype(jnp.int32)
    xs = _gather_rows(x2, src)
    y = _grouped_ffn(xs, gn, w_gate, w_up, w_down, tile_expert, n_used, tm)
    return _combine(x2, meta, y, pos0, pos1, g_final, tm)


def _nsa_proj_body(x_ref, gn_ref, w_ref, cos_ref, sin_ref,
                   q_ref, kc_ref, vc_ref, ks_ref, vs_ref, kw_ref, vw_ref, g_ref, *, tm, tiles_per_seq):
    i = pl.program_id(0)
    hn = _rms(x_ref[...], gn_ref[...]).astype(BF16)
    proj = _dot(hn, w_ref[...])
    lane = lax.broadcasted_iota(jnp.int32, (tm, LANES), 1)
    first_half = (lane & (HEAD_DIM - 1)) < HEAD_DIM // 2
    cos = cos_ref[...]
    sin = sin_ref[...]

    def chunk(c):
        return proj[:, c * LANES:(c + 1) * LANES]

    def rope(v):
        rot = jnp.where(first_half, pltpu.roll(v, LANES - HEAD_DIM // 2, 1), pltpu.roll(v, HEAD_DIM // 2, 1))
        return v * cos + rot * sin

    def heads(v):
        return v[:, :HEAD_DIM], pltpu.roll(v, HEAD_DIM, 1)[:, :HEAD_DIM]

    c = 0
    for cc in range(N_HEADS // 2):
        lo, hi = heads(rope(chunk(c)) * Q_SCALE); c += 1
        q_ref[2 * cc] = lo.astype(BF16)
        q_ref[2 * cc + 1] = hi.astype(BF16)
    for cc in range(N_KV_GROUPS // 2):
        lo, hi = heads(rope(chunk(c))); c += 1
        kc_ref[2 * cc] = lo
        kc_ref[2 * cc + 1] = hi
    for cc in range(N_KV_GROUPS // 2):
        lo, hi = heads(chunk(c)); c += 1
        vc_ref[2 * cc] = lo
        vc_ref[2 * cc + 1] = hi
    pos = (i % tiles_per_seq) * tm + lax.broadcasted_iota(jnp.int32, (tm, 1), 0)
    onehot = (lane == pos // SEL_BLOCK).astype(BF16)
    for cc in range(N_KV_GROUPS // 2):
        v = rope(chunk(c)); c += 1
        for j, vv in enumerate((v, pltpu.roll(v, HEAD_DIM, 1))):
            ks_ref[2 * cc + j, :, 0:LANES] = jnp.where(lane < HEAD_DIM, vv, 0.0).astype(BF16)
            ks_ref[2 * cc + j, :, LANES:2 * LANES] = onehot
    ones_col = jnp.where(lane == HEAD_DIM, 1.0, 0.0)

    def store_values(ref):
        nonlocal c
        for cc in range(N_KV_GROUPS // 2):
            v = chunk(c); c += 1
            for j, vv in enumerate((v, pltpu.roll(v, HEAD_DIM, 1))):
                ref[2 * cc + j] = jnp.where(lane < HEAD_DIM, vv, ones_col).astype(BF16)

    store_values(vs_ref)
    for cc in range(N_KV_GROUPS // 2):
        lo, hi = heads(rope(chunk(c))); c += 1
        kw_ref[2 * cc] = lo.astype(BF16)
        kw_ref[2 * cc + 1] = hi.astype(BF16)
    store_values(vw_ref)
    g_ref[...] = jax.nn.sigmoid(chunk(c))


def _nsa_proj(x2, gn, w_in, S, tm=256):
    M, D = x2.shape
    H, G, DH = N_HEADS, N_KV_GROUPS, HEAD_DIM
    tm = min(tm, S)
    assert S % tm == 0
    n_in = w_in.shape[1]
    n_pad = -(-n_in // LANES) * LANES
    wp = jnp.pad(w_in, ((0, 0), (0, n_pad - n_in))).astype(BF16)
    half = DH // 2
    freqs = ROPE_THETA ** (-jnp.arange(half, dtype=F32) / half)
    ang = jnp.arange(S, dtype=F32)[:, None] * freqs[None, :]
    cos = jnp.tile(jnp.cos(ang), (1, 2 * LANES // DH))
    sin = jnp.tile(jnp.concatenate([-jnp.sin(ang), jnp.sin(ang)], axis=1), (1, LANES // DH))
    tiles_per_seq = S // tm
    hd = lambda n, dt: jax.ShapeDtypeStruct((n, M, DH), dt)
    hspec = lambda n: pl.BlockSpec((n, tm, DH), lambda i: (0, i, 0))
    vd = jax.ShapeDtypeStruct((G, M, LANES), BF16)
    vspec = pl.BlockSpec((G, tm, LANES), lambda i: (0, i, 0))
    return pl.pallas_call(
        functools.partial(_nsa_proj_body, tm=tm, tiles_per_seq=tiles_per_seq),
        grid=(M // tm,),
        in_specs=[pl.BlockSpec((tm, D), lambda i: (i, 0)), _resident((1, D)), _resident((D, n_pad)),
                  pl.BlockSpec((tm, LANES), lambda i: (i % tiles_per_seq, 0)),
                  pl.BlockSpec((tm, LANES), lambda i: (i % tiles_per_seq, 0))],
        out_specs=[hspec(H), hspec(G), hspec(G),
                   pl.BlockSpec((G, tm, 2 * LANES), lambda i: (0, i, 0)),
                   vspec, hspec(G), vspec,
                   pl.BlockSpec((tm, LANES), lambda i: (i, 0))],
        out_shape=[hd(H, BF16), hd(G, F32), hd(G, F32),
                   jax.ShapeDtypeStruct((G, M, 2 * LANES), BF16),
                   vd, hd(G, BF16), vd,
                   jax.ShapeDtypeStruct((M, LANES), F32)],
        compiler_params=_params("arbitrary"),
        name="nsa_proj",
    )(x2, gn.reshape(1, D), wp, cos, sin)


def _compress_body(kc_ref, vc_ref, pe_ref, w1_ref, w2_ref, ko_ref, vo_ref, *, nc):
    half = CMP_BLOCK // 2
    for kv, (src, dst) in enumerate(((kc_ref, ko_ref), (vc_ref, vo_ref))):
        top = jnp.zeros((nc, w1_ref.shape[2]), F32)
        bot = jnp.zeros((nc, w1_ref.shape[2]), F32)
        for j in range(half):
            xj = src[0, pl.ds(j, nc, stride=CMP_STRIDE), :]
            top = top + _dot((xj + pe_ref[kv, j:j + 1, :]).astype(BF16),
                             w1_ref[kv, j * HEAD_DIM:(j + 1) * HEAD_DIM, :])
            bot = bot + _dot((xj + pe_ref[kv, half + j:half + j + 1, :]).astype(BF16),
                             w1_ref[kv, (half + j) * HEAD_DIM:(half + j + 1) * HEAD_DIM, :])
        hid = top + pltpu.roll(bot, nc - 1, 0)
        dst[0] = _dot(jax.nn.gelu(hid).astype(BF16), w2_ref[kv]).astype(BF16)


def _compress(kc, vc, pe, w1, w2, B, S):
    assert CMP_BLOCK == 2 * CMP_STRIDE
    G, M, DH = kc.shape
    nc = S // CMP_STRIDE
    spec_in = pl.BlockSpec((1, S, DH), lambda b, g: (g, b, 0))
    spec_out = pl.BlockSpec((1, nc, DH), lambda b, g: (g, b, 0))
    out = jax.ShapeDtypeStruct((G, B * nc, DH), BF16)
    return pl.pallas_call(
        functools.partial(_compress_body, nc=nc),
        grid=(B, G),
        in_specs=[spec_in, spec_in, _resident(pe.shape), _resident(w1.shape), _resident(w2.shape)],
        out_specs=[spec_out, spec_out],
        out_shape=[out, out],
        compiler_params=_params("arbitrary", "arbitrary"),
        name="nsa_compress",
    )(kc, vc, pe, w1.astype(BF16), w2.astype(BF16))


def _store_head(o_ref, oh, gates, g, h, branch):
    lane = lax.broadcasted_iota(jnp.int32, gates.shape, 1)
    col = 3 * (HEADS_PER_GROUP * g + h) + branch
    o_ref[:, h * HEAD_DIM:(h + 1) * HEAD_DIM] = (oh * _lane_col(gates, lane, col)).astype(o_ref.dtype)


def _normalized_head(acc):
    return acc[:, :HEAD_DIM] * (1.0 / acc[:, HEAD_DIM:HEAD_DIM + 1])


def _cmp_body(q_ref, kc_ref, vc_ref, gates_ref, ov_ref, o_ref, bias_ref, *, tq, nc, n_s, k_sel):
    g = pl.program_id(1)
    i = pl.program_id(2)
    gates = gates_ref[...]
    kc = kc_ref[0]
    vc = vc_ref[0]
    t = i * tq + lax.broadcasted_iota(jnp.int32, (tq, 1), 0)
    cend = lax.broadcasted_iota(jnp.int32, (1, nc), 1) * CMP_STRIDE + (CMP_BLOCK - 1)
    mask = cend <= t
    ps = None
    for h in range(HEADS_PER_GROUP):
        s = lax.dot_general(q_ref[h], kc, _NT, preferred_element_type=F32)
        s = jnp.where(mask, s, MASK_VALUE)
        m = jnp.max(s, axis=1, keepdims=True)
        m = jnp.where(m > 0.5 * MASK_VALUE, m, 0.0)
        e = jnp.exp2(s - m)
        p = e * (1.0 / jnp.maximum(jnp.sum(e, axis=1, keepdims=True), 1e-30))
        _store_head(o_ref, _dot(p.astype(BF16), vc), gates, g, h, 0)
        ps = p if ps is None else ps + p

    ph = ps.astype(BF16)
    pl_ = (ps - ph.astype(F32)).astype(BF16)
    imp = _dot(ph, ov_ref[...]) + _dot(pl_, ov_ref[...])
    j = lax.broadcasted_iota(jnp.int32, (tq, LANES), 1)
    cur = t // SEL_BLOCK
    forced = (j == 0) | (j == cur) | (j == cur - 1)
    valid = j * SEL_BLOCK <= t
    score = jnp.where(forced, FORCE_SCORE, jnp.where(valid, imp, -1.0))
    score = jnp.where(j < n_s, score, -jnp.inf)
    x = score.T
    blk = lax.broadcasted_iota(jnp.int32, (LANES, tq), 0).astype(F32)
    sel = jnp.zeros((LANES, tq), F32)
    for _ in range(k_sel):
        mx = jnp.max(x, axis=0, keepdims=True)
        idx = jnp.min(jnp.where(x == mx, blk, float(LANES)), axis=0, keepdims=True)
        hit = blk == idx
        sel = jnp.where(hit, 1.0, sel)
        x = jnp.where(hit, -jnp.inf, x)
    bias_ref[0] = jnp.where(sel.T > 0.5, 0.0, MASK_VALUE).astype(BF16)


def _overlap_matrix(nc, n_s):
    r = CMP_BLOCK // CMP_STRIDE
    qn = SEL_BLOCK // CMP_STRIDE
    m = np.zeros((nc, LANES), np.float32)
    n_c = nc - r + 1
    chunks = np.arange(n_c)[:, None] + np.arange(r)[None, :]
    np.add.at(m, (np.repeat(np.arange(n_c), r), (chunks // qn).ravel()), 1.0)
    return m


def _cmp_attention(q, kcmp, vcmp, gates, B, S, tq=256):
    H, M, DH = q.shape
    G = N_KV_GROUPS
    tq = min(tq, S)
    nq = S // tq
    nc = S // CMP_STRIDE
    n_s = S // SEL_BLOCK
    assert n_s <= LANES and tq & (tq - 1) == 0
    k_sel = min(N_SEL, n_s)
    ov = jnp.asarray(_overlap_matrix(nc, n_s), BF16)
    return pl.pallas_call(
        functools.partial(_cmp_body, tq=tq, nc=nc, n_s=n_s, k_sel=k_sel),
        grid=(B, G, nq),
        in_specs=[pl.BlockSpec((HEADS_PER_GROUP, tq, DH), lambda b, g, i: (g, b * nq + i, 0)),
                  pl.BlockSpec((1, nc, DH), lambda b, g, i: (g, b, 0)),
                  pl.BlockSpec((1, nc, DH), lambda b, g, i: (g, b, 0)),
                  pl.BlockSpec((tq, LANES), lambda b, g, i: (b * nq + i, 0)),
                  _resident((nc, LANES))],
        out_specs=[pl.BlockSpec((tq, HEADS_PER_GROUP * DH), lambda b, g, i: (b * nq + i, g)),
                   pl.BlockSpec((1, tq, LANES), lambda b, g, i: (g, b * nq + i, 0))],
        out_shape=[jax.ShapeDtypeStruct((M, H * DH), BF16), jax.ShapeDtypeStruct((G, M, LANES), BF16)],
        compiler_params=_params("arbitrary", "arbitrary", "arbitrary"),
        name="nsa_cmp_select",
    )(q, kcmp, vcmp, gates, ov)


def _sel_body(q_ref, bias_ref, k_ref, v_ref, gates_ref, o_ref, qcat, m_ref, acc_ref, *, tq):
    g = pl.program_id(1)
    i = pl.program_id(2)
    hp = HEADS_PER_GROUP
    rows = hp * tq
    qcat[:, 0:LANES] = jnp.zeros((rows, LANES), BF16)
    qcat[:, 0:HEAD_DIM] = q_ref[...].reshape(rows, HEAD_DIM)
    bias = bias_ref[0]
    for h in range(hp):
        qcat[h * tq:(h + 1) * tq, LANES:2 * LANES] = bias
    m_ref[...] = jnp.full((rows, LANES), MASK_VALUE, F32)
    acc_ref[...] = jnp.zeros((rows, LANES), F32)

    def tile(j, causal):
        start = pl.multiple_of(j * tq, tq)
        kt = k_ref[0, pl.ds(start, tq), :]
        vt = v_ref[0, pl.ds(start, tq), :]
        for h in range(hp):
            r0 = h * tq
            s = lax.dot_general(qcat[r0:r0 + tq, :], kt, _NT, preferred_element_type=F32)
            if causal:
                r = lax.broadcasted_iota(jnp.int32, (tq, 1), 0)
                c = lax.broadcasted_iota(jnp.int32, (1, tq), 1)
                s = jnp.where(c <= r, s, MASK_VALUE)
            m_prev = m_ref[r0:r0 + tq, :]
            m_new = jnp.maximum(m_prev, jnp.max(s, axis=1, keepdims=True))
            alpha = jnp.exp2(m_prev - m_new)
            p = jnp.exp2(s - jnp.concatenate([m_new] * (tq // LANES), axis=1))
            acc_ref[r0:r0 + tq, :] = alpha * acc_ref[r0:r0 + tq, :] + _dot(p.astype(BF16), vt)
            m_ref[r0:r0 + tq, :] = m_new

    def full_tile(j, carry):
        tile(j, False)
        return carry

    lax.fori_loop(0, i, full_tile, 0)
    tile(i, True)
    gates = gates_ref[...]
    for h in range(hp):
        _store_head(o_ref, _normalized_head(acc_ref[h * tq:(h + 1) * tq, :]), gates, g, h, 1)


def _sel_attention(q, bias, kscat, vs, gates, B, S, tq=512):
    H, M, DH = q.shape
    G = N_KV_GROUPS
    tq = min(tq, S)
    nq = S // tq
    assert tq % LANES == 0 and tq % SEL_BLOCK == 0 and S % tq == 0
    rows = HEADS_PER_GROUP * tq
    return pl.pallas_call(
        functools.partial(_sel_body, tq=tq),
        grid=(B, G, nq),
        in_specs=[pl.BlockSpec((HEADS_PER_GROUP, tq, DH), lambda b, g, i: (g, b * nq + i, 0)),
                  pl.BlockSpec((1, tq, LANES), lambda b, g, i: (g, b * nq + i, 0)),
                  pl.BlockSpec((1, S, 2 * LANES), lambda b, g, i: (g, b, 0)),
                  pl.BlockSpec((1, S, LANES), lambda b, g, i: (g, b, 0)),
                  pl.BlockSpec((tq, LANES), lambda b, g, i: (b * nq + i, 0))],
        out_specs=pl.BlockSpec((tq, HEADS_PER_GROUP * DH), lambda b, g, i: (b * nq + i, g)),
        out_shape=jax.ShapeDtypeStruct((M, H * DH), BF16),
        scratch_shapes=[pltpu.VMEM((rows, 2 * LANES), BF16), pltpu.VMEM((rows, LANES), F32),
                        pltpu.VMEM((rows, LANES), F32)],
        compiler_params=_params("arbitrary", "arbitrary", "arbitrary"),
        name="nsa_selected",
    )(q, bias, kscat, vs, gates)


def _win_body(q_ref, k_ref, v_ref, gates_ref, o_ref, *, tq, n_back):
    g = pl.program_id(1)
    i = pl.program_id(2)
    gates = gates_ref[...]
    r = lax.broadcasted_iota(jnp.int32, (tq, 1), 0)
    c = lax.broadcasted_iota(jnp.int32, (1, tq), 1)
    tiles = []
    for back in range(n_back, -1, -1):
        jt = i - back
        start = pl.multiple_of(jnp.maximum(jt, 0) * tq, tq)
        d = r - c + back * tq
        ok = (d >= 0) & (d < WINDOW) & (jt >= 0)
        tiles.append((k_ref[0, pl.ds(start, tq), :], v_ref[0, pl.ds(start, tq), :], ok))
    for h in range(HEADS_PER_GROUP):
        q = q_ref[h]
        scores = [jnp.where(ok, lax.dot_general(q, kt, _NT, preferred_element_type=F32), MASK_VALUE)
                  for kt, _, ok in tiles]
        m = scores[0].max(axis=1, keepdims=True)
        for s in scores[1:]:
            m = jnp.maximum(m, s.max(axis=1, keepdims=True))
        acc = jnp.zeros((tq, LANES), F32)
        for s, (_, vt, _) in zip(scores, tiles):
            acc = acc + _dot(jnp.exp2(s - m).astype(BF16), vt)
        _store_head(o_ref, _normalized_head(acc), gates, g, h, 2)


def _win_attention(q, kw, vw, gates, B, S, tq=256):
    H, M, DH = q.shape
    G = N_KV_GROUPS
    tq = min(tq, S)
    nq = S // tq
    assert tq & (tq - 1) == 0
    n_back = -(-WINDOW // tq)
    return pl.pallas_call(
        functools.partial(_win_body, tq=tq, n_back=n_back),
        grid=(B, G, nq),
        in_specs=[pl.BlockSpec((HEADS_PER_GROUP, tq, DH), lambda b, g, i: (g, b * nq + i, 0)),
                  pl.BlockSpec((1, S, DH), lambda b, g, i: (g, b, 0)),
                  pl.BlockSpec((1, S, LANES), lambda b, g, i: (g, b, 0)),
                  pl.BlockSpec((tq, LANES), lambda b, g, i: (b * nq + i, 0))],
        out_specs=pl.BlockSpec((tq, HEADS_PER_GROUP * DH), lambda b, g, i: (b * nq + i, g)),
        out_shape=jax.ShapeDtypeStruct((M, H * DH), BF16),
        compiler_params=_params("arbitrary", "arbitrary", "arbitrary"),
        name="nsa_window",
    )(q, kw, vw, gates)


def _outproj_body(x_ref, a_ref, b_ref, c_ref, w_ref, o_ref):
    o = a_ref[...].astype(F32) + b_ref[...].astype(F32) + c_ref[...].astype(F32)
    o_ref[...] = x_ref[...] + _dot(o.astype(BF16), w_ref[...])


def _outproj(x2, oc, os_, ow, w_out, tm=512):
    M, D = x2.shape
    K = w_out.shape[0]
    tm = min(tm, M)
    spec_o = pl.BlockSpec((tm, K), lambda i: (i, 0))
    return pl.pallas_call(
        _outproj_body,
        grid=(M // tm,),
        in_specs=[pl.BlockSpec((tm, D), lambda i: (i, 0)), spec_o, spec_o, spec_o, _resident((K, D))],
        out_specs=pl.BlockSpec((tm, D), lambda i: (i, 0)),
        out_shape=jax.ShapeDtypeStruct((M, D), F32),
        compiler_params=_params("arbitrary"),
        name="nsa_outproj",
    )(x2, oc, os_, ow, w_out.astype(BF16))


def _nsa_layer(x, gn, w_in, cmp_pe, cmp_w1, cmp_w2, w_out):
    B, S, D = x.shape
    x2 = x.reshape(B * S, D)
    q, kc, vc, kscat, vs, kw, vw, gates = _nsa_proj(x2, gn, w_in, S)
    kcmp, vcmp = _compress(kc, vc, cmp_pe, cmp_w1, cmp_w2, B, S)
    o_cmp, bias = _cmp_attention(q, kcmp, vcmp, gates, B, S)
    o_sel = _sel_attention(q, bias, kscat, vs, gates, B, S)
    o_win = _win_attention(q, kw, vw, gates, B, S)
    return _outproj(x2, o_cmp, o_sel, o_win, w_out).reshape(B, S, D)


def kernel(x, norm_mix, norm_ffn, norm_final, lru_w_in, lru_b_in, lru_conv_w, lru_conv_b, lru_w_a, lru_b_a, lru_w_i, lru_b_i, lru_lambda, lru_w_out, lru_b_out, nsa_w_in, nsa_cmp_pe, nsa_cmp_w1, nsa_cmp_w2, nsa_w_out, ffn_w_gate, ffn_w_up, ffn_w_down, moe_w_router, moe_w_gate, moe_w_up, moe_w_down):
    B, S, D = x.shape
    assert norm_mix.shape[0] == 2 and lru_w_in.shape[0] == 1 and nsa_w_in.shape[0] == 1
    x = _lru_layer(x, norm_mix[0], lru_w_in[0], lru_b_in[0], lru_conv_w[0], lru_conv_b[0], lru_w_a[0],
                   lru_b_a[0], lru_w_i[0], lru_b_i[0], lru_lambda[0], lru_w_out[0], lru_b_out[0])
    x2 = _ffn_layer(x.reshape(B * S, D), norm_ffn[0], ffn_w_gate, ffn_w_up, ffn_w_down)
    x2 = _nsa_layer(x2.reshape(B, S, D), norm_mix[1], nsa_w_in[0], nsa_cmp_pe[0], nsa_cmp_w1[0],
                    nsa_cmp_w2[0], nsa_w_out[0]).reshape(B * S, D)
    out = _moe_layer(x2, norm_ffn[1], moe_w_router[0], moe_w_gate[0], moe_w_up[0], moe_w_down[0], norm_final)
    return out.reshape(B, S, D)
```

```python
import functools

import numpy as np
import jax
import jax.numpy as jnp
from jax import lax
from jax.experimental import pallas as pl
from jax.experimental.pallas import tpu as pltpu

F32 = jnp.float32
BF16 = jnp.bfloat16

RMS_EPS = 1e-6
LRU_C = 8.0
CONV_WIDTH = 4
N_HEADS = 16
HEAD_DIM = 64
N_KV_GROUPS = 4
HEADS_PER_GROUP = N_HEADS // N_KV_GROUPS
CMP_BLOCK = 32
CMP_STRIDE = 16
SEL_BLOCK = 64
N_SEL = 16
WINDOW = 512
FORCE_SCORE = 1.0e4
ROPE_THETA = 10000.0
TOP_K = 2

LANES = 128
MASK_VALUE = -1.0e30
VMEM_LIMIT_BYTES = 56 * 1024 * 1024

_NT = (((1,), (1,)), ((), ()))
Q_SCALE = HEAD_DIM ** -0.5 * 1.4426950408889634


def _params(*semantics):
    return pltpu.CompilerParams(dimension_semantics=semantics, vmem_limit_bytes=VMEM_LIMIT_BYTES)


def _resident(shape):
    zeros = (0,) * len(shape)
    return pl.BlockSpec(shape, lambda *_: zeros, pipeline_mode=pl.Buffered(1))


def _rms(x, g):
    return x * lax.rsqrt(jnp.mean(x * x, axis=-1, keepdims=True) + RMS_EPS) * g


def _dot(a, b):
    return jnp.dot(a, b, preferred_element_type=F32)


def _lane_col(vals, lane, idx):
    return jnp.sum(jnp.where(lane == idx, vals, 0.0), axis=1, keepdims=True)


def _lru_body(x_ref, gn_ref, win_ref, bin_ref, cw_ref, cb_ref, wg_ref, bg_ref, lam_ref,
              wout_ref, bout_ref, o_ref, xbuf, hcar, *, ts, dr):
    t = pl.program_id(1)

    @pl.when(t == 0)
    def _():
        xbuf[0:8, :] = jnp.zeros((8, dr), F32)
        hcar[...] = jnp.zeros_like(hcar)

    x = x_ref[0]
    hn = _rms(x, gn_ref[...]).astype(BF16)
    proj = _dot(hn, win_ref[...]) + bin_ref[...]
    gate = jax.nn.gelu(proj[:, :dr])
    xr = proj[:, dr:]

    xbuf[8:8 + ts, :] = xr
    xc = cb_ref[...] + xr * cw_ref[CONV_WIDTH - 1:CONV_WIDTH, :]
    for lag in range(1, CONV_WIDTH):
        k = CONV_WIDTH - 1 - lag
        xc = xc + xbuf[pl.ds(8 - lag, ts), :] * cw_ref[k:k + 1, :]
    xbuf[0:8, :] = xbuf[ts:ts + 8, :]

    gl = _dot(xc.astype(BF16), wg_ref[...]) + bg_ref[...]
    r = jax.nn.sigmoid(gl[:, :dr])
    i = jax.nn.sigmoid(gl[:, dr:])
    z = -lam_ref[...]
    softplus = jnp.maximum(z, 0.0) + jnp.log(1.0 + jnp.exp(-jnp.abs(z)))
    log_a = (-LRU_C * r) * softplus
    a = jnp.exp(log_a)
    mult = jnp.sqrt(1.0 - a * a)
    row = lax.broadcasted_iota(jnp.int32, (ts, 1), 0)
    mult = jnp.where((row == 0) & (t == 0), 1.0, mult)
    u = mult * (i * xc)

    shift = 1
    while shift < ts:
        keep = row >= shift
        a_prev = jnp.where(keep, pltpu.roll(a, shift, 0), 1.0)
        u_prev = jnp.where(keep, pltpu.roll(u, shift, 0), 0.0)
        u = a * u_prev + u
        a = a * a_prev
        shift *= 2
    h = u + a * hcar[...]
    hcar[...] = h[ts - 1:ts, :]

    y = (h * gate).astype(BF16)
    o_ref[0] = x + _dot(y, wout_ref[...]) + bout_ref[...]


def _lru_layer(x, gn, w_in, b_in, conv_w, conv_b, w_a, b_a, w_i, b_i, lam, w_out, b_out):
    B, S, D = x.shape
    dr = w_out.shape[0]
    ts = min(256, S)
    assert S % ts == 0 and ts % 8 == 0
    wg = jnp.concatenate([jax.scipy.linalg.block_diag(*w_a), jax.scipy.linalg.block_diag(*w_i)], axis=1)
    row = lambda v: v.reshape(1, -1)
    body = functools.partial(_lru_body, ts=ts, dr=dr)
    return pl.pallas_call(
        body,
        grid=(B, S // ts),
        in_specs=[
            pl.BlockSpec((1, ts, D), lambda b, t: (b, t, 0)),
            _resident((1, D)),
            _resident((D, 2 * dr)), _resident((1, 2 * dr)),
            _resident((CONV_WIDTH, dr)), _resident((1, dr)),
            _resident((dr, 2 * dr)), _resident((1, 2 * dr)),
            _resident((1, dr)),
            _resident((dr, D)), _resident((1, D)),
        ],
        out_specs=pl.BlockSpec((1, ts, D), lambda b, t: (b, t, 0)),
        out_shape=jax.ShapeDtypeStruct((B, S, D), F32),
        scratch_shapes=[pltpu.VMEM((ts + 8, dr), F32), pltpu.VMEM((1, dr), F32)],
        compiler_params=_params("arbitrary", "arbitrary"),
        name="lru_mixer",
    )(x, row(gn), w_in.astype(BF16), row(b_in), conv_w, row(conv_b), wg.astype(BF16),
      row(jnp.concatenate([b_a, b_i])), row(lam), w_out.astype(BF16), row(b_out))


def _ffn_body(*refs, n_e, n_f, use_gates, final_norm):
    x_ref, gn_ref = refs[0], refs[1]
    k = 2
    gates_ref = gfin_ref = None
    if use_gates:
        gates_ref = refs[k]; k += 1
    wg_ref, wu_ref, wd_ref = refs[k:k + 3]; k += 3
    if final_norm:
        gfin_ref = refs[k]; k += 1
    o_ref, hn_ref, acc_ref = refs[k:k + 3]
    e = pl.program_id(1)
    f = pl.program_id(2)

    @pl.when((e == 0) & (f == 0))
    def _():
        hn_ref[...] = _rms(x_ref[...], gn_ref[...]).astype(BF16)
        acc_ref[...] = jnp.zeros_like(acc_ref)

    hn = hn_ref[...]
    g = _dot(hn, wg_ref[0])
    u = _dot(hn, wu_ref[0])
    act = (g * jax.nn.sigmoid(g)) * u
    if use_gates:
        gates = gates_ref[...]
        lane = lax.broadcasted_iota(jnp.int32, gates.shape, 1)
        act = act * _lane_col(gates, lane, e)
    acc_ref[...] += _dot(act.astype(BF16), wd_ref[0])

    @pl.when((e == n_e - 1) & (f == n_f - 1))
    def _():
        out = x_ref[...] + acc_ref[...]
        if final_norm:
            out = _rms(out, gfin_ref[...])
        o_ref[...] = out


def _ffn_layer(x2, gn, w_gate, w_up, w_down, gates=None, g_final=None, tm=1024, tf=512):
    M, D = x2.shape
    n_e, _, F = w_gate.shape
    tm = min(tm, M)
    tf = min(tf, F)
    assert M % tm == 0 and F % tf == 0
    n_f = F // tf
    use_gates = gates is not None
    final_norm = g_final is not None
    in_specs = [pl.BlockSpec((tm, D), lambda i, e, f: (i, 0)), _resident((1, D))]
    args = [x2, gn.reshape(1, D)]
    if use_gates:
        in_specs.append(pl.BlockSpec((tm, LANES), lambda i, e, f: (i, 0)))
        args.append(gates)
    in_specs += [
        pl.BlockSpec((1, D, tf), lambda i, e, f: (e, 0, f)),
        pl.BlockSpec((1, D, tf), lambda i, e, f: (e, 0, f)),
        pl.BlockSpec((1, tf, D), lambda i, e, f: (e, f, 0)),
    ]
    args += [w_gate.astype(BF16), w_up.astype(BF16), w_down.astype(BF16)]
    if final_norm:
        in_specs.append(_resident((1, D)))
        args.append(g_final.reshape(1, D))
    body = functools.partial(_ffn_body, n_e=n_e, n_f=n_f, use_gates=use_gates, final_norm=final_norm)
    return pl.pallas_call(
        body,
        grid=(M // tm, n_e, n_f),
        in_specs=in_specs,
        out_specs=pl.BlockSpec((tm, D), lambda i, e, f: (i, 0)),
        out_shape=jax.ShapeDtypeStruct((M, D), F32),
        scratch_shapes=[pltpu.VMEM((tm, D), BF16), pltpu.VMEM((tm, D), F32)],
        compiler_params=_params("arbitrary", "arbitrary", "arbitrary"),
        name="moe_swiglu" if use_gates else "dense_swiglu",
    )(*args)


def _router_body(x_ref, gn_ref, wh_ref, wl_ref, tril_ref, o_ref, cnt_ref, *, n_experts):
    @pl.when(pl.program_id(0) == 0)
    def _():
        cnt_ref[...] = jnp.zeros_like(cnt_ref)

    hn = _rms(x_ref[...], gn_ref[...])
    hh = hn.astype(BF16)
    hl = (hn - hh.astype(F32)).astype(BF16)
    wh = wh_ref[...]
    logits = _dot(hh, wh) + _dot(hh, wl_ref[...]) + _dot(hl, wh)
    lane = lax.broadcasted_iota(jnp.int32, logits.shape, 1)
    lg = jnp.where(lane < n_experts, logits, -jnp.inf)
    m0 = jnp.max(lg, axis=1, keepdims=True)
    i0 = jnp.min(jnp.where(lg == m0, lane, LANES), axis=1, keepdims=True)
    lg = jnp.where(lane == i0, -jnp.inf, lg)
    m1 = jnp.max(lg, axis=1, keepdims=True)
    i1 = jnp.min(jnp.where(lg == m1, lane, LANES), axis=1, keepdims=True)
    e1 = jnp.exp(m1 - m0)
    w0 = 1.0 / (1.0 + e1)
    routed = ((lane == i0) | (lane == i1)).astype(F32)
    incl = _dot(tril_ref[...], routed.astype(BF16))
    excl = incl - routed + cnt_ref[0:1, :]

    def put(col, v):
        return jnp.where(lane == col, v, 0.0)

    o_ref[...] = (put(META_I0, i0.astype(F32)) + put(META_I1, i1.astype(F32)) + put(META_W0, w0)
                  + put(META_W1, e1 * w0) + put(META_R0, _lane_col(excl, lane, i0))
                  + put(META_R1, _lane_col(excl, lane, i1)))
    cnt_ref[...] = jnp.broadcast_to(cnt_ref[0:1, :] + incl[incl.shape[0] - 1:, :], cnt_ref.shape)


META_I0, META_I1, META_W0, META_W1, META_R0, META_R1 = range(6)


def _router(x2, gn, w_router, tm=512):
    M, D = x2.shape
    n_experts = w_router.shape[1]
    tm = min(tm, M)
    wpad = jnp.pad(w_router, ((0, 0), (0, LANES - n_experts)))
    wh = wpad.astype(BF16)
    wl = (wpad - wh.astype(F32)).astype(BF16)
    tril = jnp.tril(jnp.ones((tm, tm), BF16))
    return pl.pallas_call(
        functools.partial(_router_body, n_experts=n_experts),
        grid=(M // tm,),
        in_specs=[pl.BlockSpec((tm, D), lambda i: (i, 0)), _resident((1, D)),
                  _resident((D, LANES)), _resident((D, LANES)), _resident((tm, tm))],
        out_specs=[pl.BlockSpec((tm, LANES), lambda i: (i, 0)), pl.BlockSpec((8, LANES), lambda i: (0, 0))],
        out_shape=[jax.ShapeDtypeStruct((M, LANES), F32), jax.ShapeDtypeStruct((8, LANES), F32)],
        compiler_params=_params("arbitrary"),
        name="moe_router",
    )(x2, gn.reshape(1, D), wh, wl, tril)


def _combine_body(pos_ref, x_ref, meta_ref, y_ref, gfin_ref, o_ref, ybuf, sem, *, tm):
    def row_copy(k, r):
        return pltpu.make_async_copy(y_ref.at[pl.ds(pos_ref[0, 0, k * tm + r], 1)], ybuf.at[k, pl.ds(r, 1)], sem)

    def start(r, c):
        row_copy(0, r).start()
        row_copy(1, r).start()
        return c

    def wait(r, c):
        row_copy(0, r).wait()
        row_copy(1, r).wait()
        return c

    lax.fori_loop(0, tm, start, 0, unroll=8)
    lax.fori_loop(0, tm, wait, 0, unroll=8)
    meta = meta_ref[...]
    lane = lax.broadcasted_iota(jnp.int32, meta.shape, 1)
    out = (x_ref[...] + _lane_col(meta, lane, META_W0) * ybuf[0] + _lane_col(meta, lane, META_W1) * ybuf[1])
    o_ref[...] = _rms(out, gfin_ref[...])


def _combine(x2, meta, y, pos0, pos1, g_final, tm=512):
    M, D = x2.shape
    tm = min(tm, M)
    pos = jnp.concatenate([pos0.reshape(M // tm, 1, tm), pos1.reshape(M // tm, 1, tm)], axis=2)
    return pl.pallas_call(
        functools.partial(_combine_body, tm=tm),
        grid=(M // tm,),
        in_specs=[pl.BlockSpec((1, 1, 2 * tm), lambda i: (i, 0, 0), memory_space=pltpu.SMEM),
                  pl.BlockSpec((tm, D), lambda i: (i, 0)),
                  pl.BlockSpec((tm, LANES), lambda i: (i, 0)),
                  pl.BlockSpec(memory_space=pl.ANY),
                  _resident((1, D))],
        out_specs=pl.BlockSpec((tm, D), lambda i: (i, 0)),
        out_shape=jax.ShapeDtypeStruct((M, D), F32),
        scratch_shapes=[pltpu.VMEM((2, tm, D), F32), pltpu.SemaphoreType.DMA(())],
        compiler_params=_params("arbitrary"),
        name="moe_combine",
    )(pos, x2, meta, y, g_final.reshape(1, D))


def _grouped_ffn_body(te_ref, nused_ref, src_ref, src_next_ref, x_ref, gn_ref, wg_ref, wu_ref, wd_ref, o_ref,
                      xbuf, hn_ref, acc_ref, sem, *, tm, n_f, n_tiles):
    j = pl.program_id(0)
    f = pl.program_id(1)
    used = j < nused_ref[0]
    slot = j % 2

    def row_copy(idx_ref, r, s):
        return pltpu.make_async_copy(x_ref.at[pl.ds(idx_ref[0, 0, r], 1)], xbuf.at[s, pl.ds(r, 1)], sem.at[s])

    def start_tile(idx_ref, s):
        def body(r, c):
            row_copy(idx_ref, r, s).start()
            return c
        lax.fori_loop(0, tm, body, 0, unroll=8)

    def wait_tile(idx_ref, s):
        def body(r, c):
            row_copy(idx_ref, r, s).wait()
            return c
        lax.fori_loop(0, tm, body, 0, unroll=8)

    @pl.when((j == 0) & (f == 0))
    def _():
        start_tile(src_ref, 0)

    @pl.when(f == 0)
    def _():
        wait_tile(src_ref, slot)

        @pl.when(j + 1 < n_tiles)
        def _():
            start_tile(src_next_ref, 1 - slot)

        hn_ref[...] = _rms(xbuf[slot], gn_ref[...]).astype(BF16)
        acc_ref[...] = jnp.zeros_like(acc_ref)

    @pl.when(used)
    def _():
        hn = hn_ref[...]
        g = _dot(hn, wg_ref[0])
        u = _dot(hn, wu_ref[0])
        acc_ref[...] += _dot(((g * jax.nn.sigmoid(g)) * u).astype(BF16), wd_ref[0])

    @pl.when(f == n_f - 1)
    def _():
        o_ref[...] = acc_ref[...]


def _grouped_ffn(x2, src, gn, w_gate, w_up, w_down, tile_expert, n_used, tm, tf=512):
    P = src.shape[0]
    D = x2.shape[1]
    F = w_gate.shape[2]
    tf = min(tf, F)
    n_f = F // tf
    n_tiles = P // tm
    src3 = src.reshape(n_tiles, 1, tm)
    grid_spec = pltpu.PrefetchScalarGridSpec(
        num_scalar_prefetch=2,
        grid=(n_tiles, n_f),
        in_specs=[pl.BlockSpec((1, 1, tm), lambda j, f, te, nu: (j, 0, 0), memory_space=pltpu.SMEM),
                  pl.BlockSpec((1, 1, tm), lambda j, f, te, nu: (jnp.minimum(j + 1, n_tiles - 1), 0, 0),
                               memory_space=pltpu.SMEM),
                  pl.BlockSpec(memory_space=pl.ANY),
                  pl.BlockSpec((1, D), lambda j, f, te, nu: (0, 0)),
                  pl.BlockSpec((1, D, tf), lambda j, f, te, nu: (te[j], 0, f)),
                  pl.BlockSpec((1, D, tf), lambda j, f, te, nu: (te[j], 0, f)),
                  pl.BlockSpec((1, tf, D), lambda j, f, te, nu: (te[j], f, 0))],
        out_specs=pl.BlockSpec((tm, D), lambda j, f, te, nu: (j, 0)),
        scratch_shapes=[pltpu.VMEM((2, tm, D), F32), pltpu.VMEM((tm, D), BF16), pltpu.VMEM((tm, D), F32),
                        pltpu.SemaphoreType.DMA((2,))])
    return pl.pallas_call(
        functools.partial(_grouped_ffn_body, tm=tm, n_f=n_f, n_tiles=n_tiles),
        grid_spec=grid_spec,
        out_shape=jax.ShapeDtypeStruct((P, D), F32),
        compiler_params=_params("arbitrary", "arbitrary"),
        name="moe_grouped_swiglu",
    )(tile_expert, n_used, src3, src3, x2, gn.reshape(1, D), w_gate.astype(BF16), w_up.astype(BF16),
      w_down.astype(BF16))


def _moe_layer(x2, gn, w_router, w_gate, w_up, w_down, g_final, tm=512):
    M, D = x2.shape
    n_e = w_router.shape[1]
    tm = min(tm, M)
    meta, counts = _router(x2, gn, w_router)
    as_int = lambda col: meta[:, col].astype(jnp.int32)
    i0, i1, r0, r1 = as_int(META_I0), as_int(META_I1), as_int(META_R0), as_int(META_R1)
    padded = (counts[0, :n_e].astype(jnp.int32) + tm - 1) // tm * tm
    ends = jnp.cumsum(padded)
    offsets = ends - padded
    pos0 = offsets[i0] + r0
    pos1 = offsets[i1] + r1
    P = TOP_K * M + n_e * tm
    tok = jnp.arange(M, dtype=jnp.int32)
    src = jnp.zeros((P,), jnp.int32).at[jnp.concatenate([pos0, pos1])].set(jnp.concatenate([tok, tok]))
    tile_start = jnp.arange(P // tm, dtype=jnp.int32) * tm
    tile_expert = jnp.minimum(jnp.searchsorted(ends, tile_start, side="right"), n_e - 1).astype(jnp.int32)
    n_used = (ends[n_e - 1:] // tm).astype(jnp.int32)
    y = _grouped_ffn(x2, src, gn, w_gate, w_up, w_down, tile_expert, n_used, tm)
    return _combine(x2, meta, y, pos0, pos1, g_final, tm)


def _nsa_proj_body(x_ref, gn_ref, w_ref, cos_ref, sin_ref,
                   q_ref, kc_ref, vc_ref, ks_ref, vs_ref, kw_ref, vw_ref, g_ref, *, tm, tiles_per_seq):
    i = pl.program_id(0)
    hn = _rms(x_ref[...], gn_ref[...]).astype(BF16)
    proj = _dot(hn, w_ref[...])
    lane = lax.broadcasted_iota(jnp.int32, (tm, LANES), 1)
    first_half = (lane & (HEAD_DIM - 1)) < HEAD_DIM // 2
    cos = cos_ref[...]
    sin = sin_ref[...]

    def chunk(c):
        return proj[:, c * LANES:(c + 1) * LANES]

    def rope(v):
        rot = jnp.where(first_half, pltpu.roll(v, LANES - HEAD_DIM // 2, 1), pltpu.roll(v, HEAD_DIM // 2, 1))
        return v * cos + rot * sin

    def heads(v):
        return v[:, :HEAD_DIM], pltpu.roll(v, HEAD_DIM, 1)[:, :HEAD_DIM]

    c = 0
    for cc in range(N_HEADS // 2):
        lo, hi = heads(rope(chunk(c)) * Q_SCALE); c += 1
        q_ref[2 * cc] = lo.astype(BF16)
        q_ref[2 * cc + 1] = hi.astype(BF16)
    for cc in range(N_KV_GROUPS // 2):
        lo, hi = heads(rope(chunk(c))); c += 1
        kc_ref[2 * cc] = lo
        kc_ref[2 * cc + 1] = hi
    for cc in range(N_KV_GROUPS // 2):
        lo, hi = heads(chunk(c)); c += 1
        vc_ref[2 * cc] = lo
        vc_ref[2 * cc + 1] = hi
    pos = (i % tiles_per_seq) * tm + lax.broadcasted_iota(jnp.int32, (tm, 1), 0)
    onehot = (lane == pos // SEL_BLOCK).astype(BF16)
    for cc in range(N_KV_GROUPS // 2):
        v = rope(chunk(c)); c += 1
        for j, vv in enumerate((v, pltpu.roll(v, HEAD_DIM, 1))):
            ks_ref[2 * cc + j, :, 0:LANES] = jnp.where(lane < HEAD_DIM, vv, 0.0).astype(BF16)
            ks_ref[2 * cc + j, :, LANES:2 * LANES] = onehot
    ones_col = jnp.where(lane == HEAD_DIM, 1.0, 0.0)

    def store_values(ref):
        nonlocal c
        for cc in range(N_KV_GROUPS // 2):
            v = chunk(c); c += 1
            for j, vv in enumerate((v, pltpu.roll(v, HEAD_DIM, 1))):
                ref[2 * cc + j] = jnp.where(lane < HEAD_DIM, vv, ones_col).astype(BF16)

    store_values(vs_ref)
    for cc in range(N_KV_GROUPS // 2):
        lo, hi = heads(rope(chunk(c))); c += 1
        kw_ref[2 * cc] = lo.astype(BF16)
        kw_ref[2 * cc + 1] = hi.astype(BF16)
    store_values(vw_ref)
    g_ref[...] = jax.nn.sigmoid(chunk(c))


def _nsa_proj(x2, gn, w_in, S, tm=256):
    M, D = x2.shape
    H, G, DH = N_HEADS, N_KV_GROUPS, HEAD_DIM
    tm = min(tm, S)
    assert S % tm == 0
    n_in = w_in.shape[1]
    n_pad = -(-n_in // LANES) * LANES
    wp = jnp.pad(w_in, ((0, 0), (0, n_pad - n_in))).astype(BF16)
    half = DH // 2
    freqs = ROPE_THETA ** (-jnp.arange(half, dtype=F32) / half)
    ang = jnp.arange(S, dtype=F32)[:, None] * freqs[None, :]
    cos = jnp.tile(jnp.cos(ang), (1, 2 * LANES // DH))
    sin = jnp.tile(jnp.concatenate([-jnp.sin(ang), jnp.sin(ang)], axis=1), (1, LANES // DH))
    tiles_per_seq = S // tm
    hd = lambda n, dt: jax.ShapeDtypeStruct((n, M, DH), dt)
    hspec = lambda n: pl.BlockSpec((n, tm, DH), lambda i: (0, i, 0))
    vd = jax.ShapeDtypeStruct((G, M, LANES), BF16)
    vspec = pl.BlockSpec((G, tm, LANES), lambda i: (0, i, 0))
    return pl.pallas_call(
        functools.partial(_nsa_proj_body, tm=tm, tiles_per_seq=tiles_per_seq),
        grid=(M // tm,),
        in_specs=[pl.BlockSpec((tm, D), lambda i: (i, 0)), _resident((1, D)), _resident((D, n_pad)),
                  pl.BlockSpec((tm, LANES), lambda i: (i % tiles_per_seq, 0)),
                  pl.BlockSpec((tm, LANES), lambda i: (i % tiles_per_seq, 0))],
        out_specs=[hspec(H), hspec(G), hspec(G),
                   pl.BlockSpec((G, tm, 2 * LANES), lambda i: (0, i, 0)),
                   vspec, hspec(G), vspec,
                   pl.BlockSpec((tm, LANES), lambda i: (i, 0))],
        out_shape=[hd(H, BF16), hd(G, F32), hd(G, F32),
                   jax.ShapeDtypeStruct((G, M, 2 * LANES), BF16),
                   vd, hd(G, BF16), vd,
                   jax.ShapeDtypeStruct((M, LANES), F32)],
        compiler_params=_params("arbitrary"),
        name="nsa_proj",
    )(x2, gn.reshape(1, D), wp, cos, sin)


def _compress_body(kc_ref, vc_ref, pe_ref, w1_ref, w2_ref, ko_ref, vo_ref, *, nc):
    half = CMP_BLOCK // 2
    for kv, (src, dst) in enumerate(((kc_ref, ko_ref), (vc_ref, vo_ref))):
        top = jnp.zeros((nc, w1_ref.shape[2]), F32)
        bot = jnp.zeros((nc, w1_ref.shape[2]), F32)
        for j in range(half):
            xj = src[0, pl.ds(j, nc, stride=CMP_STRIDE), :]
            top = top + _dot((xj + pe_ref[kv, j:j + 1, :]).astype(BF16),
                             w1_ref[kv, j * HEAD_DIM:(j + 1) * HEAD_DIM, :])
            bot = bot + _dot((xj + pe_ref[kv, half + j:half + j + 1, :]).astype(BF16),
                             w1_ref[kv, (half + j) * HEAD_DIM:(half + j + 1) * HEAD_DIM, :])
        hid = top + pltpu.roll(bot, nc - 1, 0)
        dst[0] = _dot(jax.nn.gelu(hid).astype(BF16), w2_ref[kv]).astype(BF16)


def _compress(kc, vc, pe, w1, w2, B, S):
    assert CMP_BLOCK == 2 * CMP_STRIDE
    G, M, DH = kc.shape
    nc = S // CMP_STRIDE
    spec_in = pl.BlockSpec((1, S, DH), lambda b, g: (g, b, 0))
    spec_out = pl.BlockSpec((1, nc, DH), lambda b, g: (g, b, 0))
    out = jax.ShapeDtypeStruct((G, B * nc, DH), BF16)
    return pl.pallas_call(
        functools.partial(_compress_body, nc=nc),
        grid=(B, G),
        in_specs=[spec_in, spec_in, _resident(pe.shape), _resident(w1.shape), _resident(w2.shape)],
        out_specs=[spec_out, spec_out],
        out_shape=[out, out],
        compiler_params=_params("arbitrary", "arbitrary"),
        name="nsa_compress",
    )(kc, vc, pe, w1.astype(BF16), w2.astype(BF16))


def _store_head(o_ref, oh, gates, g, h, branch):
    lane = lax.broadcasted_iota(jnp.int32, gates.shape, 1)
    col = 3 * (HEADS_PER_GROUP * g + h) + branch
    o_ref[:, h * HEAD_DIM:(h + 1) * HEAD_DIM] = (oh * _lane_col(gates, lane, col)).astype(o_ref.dtype)


def _normalized_head(acc):
    return acc[:, :HEAD_DIM] * (1.0 / acc[:, HEAD_DIM:HEAD_DIM + 1])


def _cmp_body(q_ref, kc_ref, vc_ref, gates_ref, ov_ref, o_ref, bias_ref, *, tq, nc, n_s, k_sel):
    g = pl.program_id(1)
    i = pl.program_id(2)
    gates = gates_ref[...]
    kc = kc_ref[0]
    vc = vc_ref[0]
    t = i * tq + lax.broadcasted_iota(jnp.int32, (tq, 1), 0)
    cend = lax.broadcasted_iota(jnp.int32, (1, nc), 1) * CMP_STRIDE + (CMP_BLOCK - 1)
    mask = cend <= t
    ps = None
    for h in range(HEADS_PER_GROUP):
        s = lax.dot_general(q_ref[h], kc, _NT, preferred_element_type=F32)
        s = jnp.where(mask, s, MASK_VALUE)
        m = jnp.max(s, axis=1, keepdims=True)
        m = jnp.where(m > 0.5 * MASK_VALUE, m, 0.0)
        e = jnp.exp2(s - m)
        p = e * (1.0 / jnp.maximum(jnp.sum(e, axis=1, keepdims=True), 1e-30))
        _store_head(o_ref, _dot(p.astype(BF16), vc), gates, g, h, 0)
        ps = p if ps is None else ps + p

    ph = ps.astype(BF16)
    pl_ = (ps - ph.astype(F32)).astype(BF16)
    imp = _dot(ph, ov_ref[...]) + _dot(pl_, ov_ref[...])
    j = lax.broadcasted_iota(jnp.int32, (tq, LANES), 1)
    cur = t // SEL_BLOCK
    forced = (j == 0) | (j == cur) | (j == cur - 1)
    valid = j * SEL_BLOCK <= t
    score = jnp.where(forced, FORCE_SCORE, jnp.where(valid, imp, -1.0))
    score = jnp.where(j < n_s, score, -jnp.inf)
    x = score.T
    blk = lax.broadcasted_iota(jnp.int32, (LANES, tq), 0).astype(F32)
    sel = jnp.zeros((LANES, tq), F32)
    for _ in range(k_sel):
        mx = jnp.max(x, axis=0, keepdims=True)
        idx = jnp.min(jnp.where(x == mx, blk, float(LANES)), axis=0, keepdims=True)
        hit = blk == idx
        sel = jnp.where(hit, 1.0, sel)
        x = jnp.where(hit, -jnp.inf, x)
    bias_ref[0] = jnp.where(sel.T > 0.5, 0.0, MASK_VALUE).astype(BF16)


def _overlap_matrix(nc, n_s):
    r = CMP_BLOCK // CMP_STRIDE
    qn = SEL_BLOCK // CMP_STRIDE
    m = np.zeros((nc, LANES), np.float32)
    n_c = nc - r + 1
    chunks = np.arange(n_c)[:, None] + np.arange(r)[None, :]
    np.add.at(m, (np.repeat(np.arange(n_c), r), (chunks // qn).ravel()), 1.0)
    return m


def _cmp_attention(q, kcmp, vcmp, gates, B, S, tq=256):
    H, M, DH = q.shape
    G = N_KV_GROUPS
    tq = min(tq, S)
    nq = S // tq
    nc = S // CMP_STRIDE
    n_s = S // SEL_BLOCK
    assert n_s <= LANES and tq & (tq - 1) == 0
    k_sel = min(N_SEL, n_s)
    ov = jnp.asarray(_overlap_matrix(nc, n_s), BF16)
    return pl.pallas_call(
        functools.partial(_cmp_body, tq=tq, nc=nc, n_s=n_s, k_sel=k_sel),
        grid=(B, G, nq),
        in_specs=[pl.BlockSpec((HEADS_PER_GROUP, tq, DH), lambda b, g, i: (g, b * nq + i, 0)),
                  pl.BlockSpec((1, nc, DH), lambda b, g, i: (g, b, 0)),
                  pl.BlockSpec((1, nc, DH), lambda b, g, i: (g, b, 0)),
                  pl.BlockSpec((tq, LANES), lambda b, g, i: (b * nq + i, 0)),
                  _resident((nc, LANES))],
        out_specs=[pl.BlockSpec((tq, HEADS_PER_GROUP * DH), lambda b, g, i: (b * nq + i, g)),
                   pl.BlockSpec((1, tq, LANES), lambda b, g, i: (g, b * nq + i, 0))],
        out_shape=[jax.ShapeDtypeStruct((M, H * DH), BF16), jax.ShapeDtypeStruct((G, M, LANES), BF16)],
        compiler_params=_params("arbitrary", "arbitrary", "arbitrary"),
        name="nsa_cmp_select",
    )(q, kcmp, vcmp, gates, ov)


def _sel_body(q_ref, bias_ref, k_ref, v_ref, gates_ref, o_ref, qcat, m_ref, acc_ref, *, tq):
    g = pl.program_id(1)
    i = pl.program_id(2)
    hp = HEADS_PER_GROUP
    rows = hp * tq
    qcat[:, 0:LANES] = jnp.zeros((rows, LANES), BF16)
    qcat[:, 0:HEAD_DIM] = q_ref[...].reshape(rows, HEAD_DIM)
    bias = bias_ref[0]
    for h in range(hp):
        qcat[h * tq:(h + 1) * tq, LANES:2 * LANES] = bias
    m_ref[...] = jnp.full((rows, LANES), MASK_VALUE, F32)
    acc_ref[...] = jnp.zeros((rows, LANES), F32)

    def tile(j, causal):
        start = pl.multiple_of(j * tq, tq)
        kt = k_ref[0, pl.ds(start, tq), :]
        vt = v_ref[0, pl.ds(start, tq), :]
        for h in range(hp):
            r0 = h * tq
            s = lax.dot_general(qcat[r0:r0 + tq, :], kt, _NT, preferred_element_type=F32)
            if causal:
                r = lax.broadcasted_iota(jnp.int32, (tq, 1), 0)
                c = lax.broadcasted_iota(jnp.int32, (1, tq), 1)
                s = jnp.where(c <= r, s, MASK_VALUE)
            m_prev = m_ref[r0:r0 + tq, :]
            m_new = jnp.maximum(m_prev, jnp.max(s, axis=1, keepdims=True))
            alpha = jnp.exp2(m_prev - m_new)
            p = jnp.exp2(s - jnp.concatenate([m_new] * (tq // LANES), axis=1))
            acc_ref[r0:r0 + tq, :] = alpha * acc_ref[r0:r0 + tq, :] + _dot(p.astype(BF16), vt)
            m_ref[r0:r0 + tq, :] = m_new

    def full_tile(j, carry):
        tile(j, False)
        return carry

    lax.fori_loop(0, i, full_tile, 0)
    tile(i, True)
    gates = gates_ref[...]
    for h in range(hp):
        _store_head(o_ref, _normalized_head(acc_ref[h * tq:(h + 1) * tq, :]), gates, g, h, 1)


def _sel_attention(q, bias, kscat, vs, gates, B, S, tq=512):
    H, M, DH = q.shape
    G = N_KV_GROUPS
    tq = min(tq, S)
    nq = S // tq
    assert tq % LANES == 0 and tq % SEL_BLOCK == 0 and S % tq == 0
    rows = HEADS_PER_GROUP * tq
    return pl.pallas_call(
        functools.partial(_sel_body, tq=tq),
        grid=(B, G, nq),
        in_specs=[pl.BlockSpec((HEADS_PER_GROUP, tq, DH), lambda b, g, i: (g, b * nq + i, 0)),
                  pl.BlockSpec((1, tq, LANES), lambda b, g, i: (g, b * nq + i, 0)),
                  pl.BlockSpec((1, S, 2 * LANES), lambda b, g, i: (g, b, 0)),
                  pl.BlockSpec((1, S, LANES), lambda b, g, i: (g, b, 0)),
                  pl.BlockSpec((tq, LANES), lambda b, g, i: (b * nq + i, 0))],
        out_specs=pl.BlockSpec((tq, HEADS_PER_GROUP * DH), lambda b, g, i: (b * nq + i, g)),
        out_shape=jax.ShapeDtypeStruct((M, H * DH), BF16),
        scratch_shapes=[pltpu.VMEM((rows, 2 * LANES), BF16), pltpu.VMEM((rows, LANES), F32),
                        pltpu.VMEM((rows, LANES), F32)],
        compiler_params=_params("arbitrary", "arbitrary", "arbitrary"),
        name="nsa_selected",
    )(q, bias, kscat, vs, gates)


def _win_body(q_ref, k_ref, v_ref, gates_ref, o_ref, *, tq, n_back):
    g = pl.program_id(1)
    i = pl.program_id(2)
    gates = gates_ref[...]
    r = lax.broadcasted_iota(jnp.int32, (tq, 1), 0)
    c = lax.broadcasted_iota(jnp.int32, (1, tq), 1)
    tiles = []
    for back in range(n_back, -1, -1):
        jt = i - back
        start = pl.multiple_of(jnp.maximum(jt, 0) * tq, tq)
        d = r - c + back * tq
        ok = (d >= 0) & (d < WINDOW) & (jt >= 0)
        tiles.append((k_ref[0, pl.ds(start, tq), :], v_ref[0, pl.ds(start, tq), :], ok))
    for h in range(HEADS_PER_GROUP):
        q = q_ref[h]
        scores = [jnp.where(ok, lax.dot_general(q, kt, _NT, preferred_element_type=F32), MASK_VALUE)
                  for kt, _, ok in tiles]
        m = scores[0].max(axis=1, keepdims=True)
        for s in scores[1:]:
            m = jnp.maximum(m, s.max(axis=1, keepdims=True))
        acc = jnp.zeros((tq, LANES), F32)
        for s, (_, vt, _) in zip(scores, tiles):
            acc = acc + _dot(jnp.exp2(s - m).astype(BF16), vt)
        _store_head(o_ref, _normalized_head(acc), gates, g, h, 2)


def _win_attention(q, kw, vw, gates, B, S, tq=256):
    H, M, DH = q.shape
    G = N_KV_GROUPS
    tq = min(tq, S)
    nq = S // tq
    assert tq & (tq - 1) == 0
    n_back = -(-WINDOW // tq)
    return pl.pallas_call(
        functools.partial(_win_body, tq=tq, n_back=n_back),
        grid=(B, G, nq),
        in_specs=[pl.BlockSpec((HEADS_PER_GROUP, tq, DH), lambda b, g, i: (g, b * nq + i, 0)),
                  pl.BlockSpec((1, S, DH), lambda b, g, i: (g, b, 0)),
                  pl.BlockSpec((1, S, LANES), lambda b, g, i: (g, b, 0)),
                  pl.BlockSpec((tq, LANES), lambda b, g, i: (b * nq + i, 0))],
        out_specs=pl.BlockSpec((tq, HEADS_PER_GROUP * DH), lambda b, g, i: (b * nq + i, g)),
        out_shape=jax.ShapeDtypeStruct((M, H * DH), BF16),
        compiler_params=_params("arbitrary", "arbitrary", "arbitrary"),
        name="nsa_window",
    )(q, kw, vw, gates)


def _outproj_body(x_ref, a_ref, b_ref, c_ref, w_ref, o_ref):
    o = a_ref[...].astype(F32) + b_ref[...].astype(F32) + c_ref[...].astype(F32)
    o_ref[...] = x_ref[...] + _dot(o.astype(BF16), w_ref[...])


def _outproj(x2, oc, os_, ow, w_out, tm=512):
    M, D = x2.shape
    K = w_out.shape[0]
    tm = min(tm, M)
    spec_o = pl.BlockSpec((tm, K), lambda i: (i, 0))
    return pl.pallas_call(
        _outproj_body,
        grid=(M // tm,),
        in_specs=[pl.BlockSpec((tm, D), lambda i: (i, 0)), spec_o, spec_o, spec_o, _resident((K, D))],
        out_specs=pl.BlockSpec((tm, D), lambda i: (i, 0)),
        out_shape=jax.ShapeDtypeStruct((M, D), F32),
        compiler_params=_params("arbitrary"),
        name="nsa_outproj",
    )(x2, oc, os_, ow, w_out.astype(BF16))


def _nsa_layer(x, gn, w_in, cmp_pe, cmp_w1, cmp_w2, w_out):
    B, S, D = x.shape
    x2 = x.reshape(B * S, D)
    q, kc, vc, kscat, vs, kw, vw, gates = _nsa_proj(x2, gn, w_in, S)
    kcmp, vcmp = _compress(kc, vc, cmp_pe, cmp_w1, cmp_w2, B, S)
    o_cmp, bias = _cmp_attention(q, kcmp, vcmp, gates, B, S)
    o_sel = _sel_attention(q, bias, kscat, vs, gates, B, S)
    o_win = _win_attention(q, kw, vw, gates, B, S)
    return _outproj(x2, o_cmp, o_sel, o_win, w_out).reshape(B, S, D)


def kernel(x, norm_mix, norm_ffn, norm_final, lru_w_in, lru_b_in, lru_conv_w, lru_conv_b, lru_w_a, lru_b_a, lru_w_i, lru_b_i, lru_lambda, lru_w_out, lru_b_out, nsa_w_in, nsa_cmp_pe, nsa_cmp_w1, nsa_cmp_w2, nsa_w_out, ffn_w_gate, ffn_w_up, ffn_w_down, moe_w_router, moe_w_gate, moe_w_up, moe_w_down):
    B, S, D = x.shape
    assert norm_mix.shape[0] == 2 and lru_w_in.shape[0] == 1 and nsa_w_in.shape[0] == 1
    x = _lru_layer(x, norm_mix[0], lru_w_in[0], lru_b_in[0], lru_conv_w[0], lru_conv_b[0], lru_w_a[0],
                   lru_b_a[0], lru_w_i[0], lru_b_i[0], lru_lambda[0], lru_w_out[0], lru_b_out[0])
    x2 = _ffn_layer(x.reshape(B * S, D), norm_ffn[0], ffn_w_gate, ffn_w_up, ffn_w_down)
    x2 = _nsa_layer(x2.reshape(B, S, D), norm_mix[1], nsa_w_in[0], nsa_cmp_pe[0], nsa_cmp_w1[0],
                    nsa_cmp_w2[0], nsa_w_out[0]).reshape(B * S, D)
    out = _moe_layer(x2, norm_ffn[1], moe_w_router[0], moe_w_gate[0], moe_w_up[0], moe_w_down[0], norm_final)
    return out.reshape(B, S, D)
```

```python
import functools

import numpy as np
import jax
import jax.numpy as jnp
from jax import lax
from jax.experimental import pallas as pl
from jax.experimental.pallas import tpu as pltpu

F32 = jnp.float32
BF16 = jnp.bfloat16

RMS_EPS = 1e-6
LRU_C = 8.0
CONV_WIDTH = 4
N_HEADS = 16
HEAD_DIM = 64
N_KV_GROUPS = 4
HEADS_PER_GROUP = N_HEADS // N_KV_GROUPS
CMP_BLOCK = 32
CMP_STRIDE = 16
SEL_BLOCK = 64
N_SEL = 16
WINDOW = 512
FORCE_SCORE = 1.0e4
ROPE_THETA = 10000.0
TOP_K = 2

LANES = 128
MASK_VALUE = -1.0e30
VMEM_LIMIT_BYTES = 56 * 1024 * 1024

_NT = (((1,), (1,)), ((), ()))
Q_SCALE = HEAD_DIM ** -0.5 * 1.4426950408889634


def _params(*semantics):
    return pltpu.CompilerParams(dimension_semantics=semantics, vmem_limit_bytes=VMEM_LIMIT_BYTES)


def _resident(shape):
    zeros = (0,) * len(shape)
    return pl.BlockSpec(shape, lambda *_: zeros, pipeline_mode=pl.Buffered(1))


def _rms(x, g):
    return x * lax.rsqrt(jnp.mean(x * x, axis=-1, keepdims=True) + RMS_EPS) * g


def _dot(a, b):
    return jnp.dot(a, b, preferred_element_type=F32)


def _lane_col(vals, lane, idx):
    return jnp.sum(jnp.where(lane == idx, vals, 0.0), axis=1, keepdims=True)


def _lru_body(x_ref, gn_ref, win_ref, bin_ref, cw_ref, cb_ref, wg_ref, bg_ref, lam_ref,
              wout_ref, bout_ref, o_ref, xbuf, hcar, *, ts, dr):
    t = pl.program_id(1)

    @pl.when(t == 0)
    def _():
        xbuf[0:8, :] = jnp.zeros((8, dr), F32)
        hcar[...] = jnp.zeros_like(hcar)

    x = x_ref[0]
    hn = _rms(x, gn_ref[...]).astype(BF16)
    proj = _dot(hn, win_ref[...]) + bin_ref[...]
    gate = jax.nn.gelu(proj[:, :dr])
    xr = proj[:, dr:]

    xbuf[8:8 + ts, :] = xr
    xc = cb_ref[...] + xr * cw_ref[CONV_WIDTH - 1:CONV_WIDTH, :]
    for lag in range(1, CONV_WIDTH):
        k = CONV_WIDTH - 1 - lag
        xc = xc + xbuf[pl.ds(8 - lag, ts), :] * cw_ref[k:k + 1, :]
    xbuf[0:8, :] = xbuf[ts:ts + 8, :]

    gl = _dot(xc.astype(BF16), wg_ref[...]) + bg_ref[...]
    r = jax.nn.sigmoid(gl[:, :dr])
    i = jax.nn.sigmoid(gl[:, dr:])
    z = -lam_ref[...]
    softplus = jnp.maximum(z, 0.0) + jnp.log(1.0 + jnp.exp(-jnp.abs(z)))
    log_a = (-LRU_C * r) * softplus
    a = jnp.exp(log_a)
    mult = jnp.sqrt(1.0 - a * a)
    row = lax.broadcasted_iota(jnp.int32, (ts, 1), 0)
    mult = jnp.where((row == 0) & (t == 0), 1.0, mult)
    u = mult * (i * xc)

    shift = 1
    while shift < ts:
        keep = row >= shift
        a_prev = jnp.where(keep, pltpu.roll(a, shift, 0), 1.0)
        u_prev = jnp.where(keep, pltpu.roll(u, shift, 0), 0.0)
        u = a * u_prev + u
        a = a * a_prev
        shift *= 2
    h = u + a * hcar[...]
    hcar[...] = h[ts - 1:ts, :]

    y = (h * gate).astype(BF16)
    o_ref[0] = x + _dot(y, wout_ref[...]) + bout_ref[...]


def _lru_layer(x, gn, w_in, b_in, conv_w, conv_b, w_a, b_a, w_i, b_i, lam, w_out, b_out):
    B, S, D = x.shape
    dr = w_out.shape[0]
    ts = min(256, S)
    assert S % ts == 0 and ts % 8 == 0
    wg = jnp.concatenate([jax.scipy.linalg.block_diag(*w_a), jax.scipy.linalg.block_diag(*w_i)], axis=1)
    row = lambda v: v.reshape(1, -1)
    body = functools.partial(_lru_body, ts=ts, dr=dr)
    return pl.pallas_call(
        body,
        grid=(B, S // ts),
        in_specs=[
            pl.BlockSpec((1, ts, D), lambda b, t: (b, t, 0)),
            _resident((1, D)),
            _resident((D, 2 * dr)), _resident((1, 2 * dr)),
            _resident((CONV_WIDTH, dr)), _resident((1, dr)),
            _resident((dr, 2 * dr)), _resident((1, 2 * dr)),
            _resident((1, dr)),
            _resident((dr, D)), _resident((1, D)),
        ],
        out_specs=pl.BlockSpec((1, ts, D), lambda b, t: (b, t, 0)),
        out_shape=jax.ShapeDtypeStruct((B, S, D), F32),
        scratch_shapes=[pltpu.VMEM((ts + 8, dr), F32), pltpu.VMEM((1, dr), F32)],
        compiler_params=_params("arbitrary", "arbitrary"),
        name="lru_mixer",
    )(x, row(gn), w_in.astype(BF16), row(b_in), conv_w, row(conv_b), wg.astype(BF16),
      row(jnp.concatenate([b_a, b_i])), row(lam), w_out.astype(BF16), row(b_out))


def _ffn_body(*refs, n_e, n_f, use_gates, final_norm):
    x_ref, gn_ref = refs[0], refs[1]
    k = 2
    gates_ref = gfin_ref = None
    if use_gates:
        gates_ref = refs[k]; k += 1
    wg_ref, wu_ref, wd_ref = refs[k:k + 3]; k += 3
    if final_norm:
        gfin_ref = refs[k]; k += 1
    o_ref, hn_ref, acc_ref = refs[k:k + 3]
    e = pl.program_id(1)
    f = pl.program_id(2)

    @pl.when((e == 0) & (f == 0))
    def _():
        hn_ref[...] = _rms(x_ref[...], gn_ref[...]).astype(BF16)
        acc_ref[...] = jnp.zeros_like(acc_ref)

    hn = hn_ref[...]
    g = _dot(hn, wg_ref[0])
    u = _dot(hn, wu_ref[0])
    act = (g * jax.nn.sigmoid(g)) * u
    if use_gates:
        gates = gates_ref[...]
        lane = lax.broadcasted_iota(jnp.int32, gates.shape, 1)
        act = act * _lane_col(gates, lane, e)
    acc_ref[...] += _dot(act.astype(BF16), wd_ref[0])

    @pl.when((e == n_e - 1) & (f == n_f - 1))
    def _():
        out = x_ref[...] + acc_ref[...]
        if final_norm:
            out = _rms(out, gfin_ref[...])
        o_ref[...] = out


def _ffn_layer(x2, gn, w_gate, w_up, w_down, gates=None, g_final=None, tm=1024, tf=512):
    M, D = x2.shape
    n_e, _, F = w_gate.shape
    tm = min(tm, M)
    tf = min(tf, F)
    assert M % tm == 0 and F % tf == 0
    n_f = F // tf
    use_gates = gates is not None
    final_norm = g_final is not None
    in_specs = [pl.BlockSpec((tm, D), lambda i, e, f: (i, 0)), _resident((1, D))]
    args = [x2, gn.reshape(1, D)]
    if use_gates:
        in_specs.append(pl.BlockSpec((tm, LANES), lambda i, e, f: (i, 0)))
        args.append(gates)
    in_specs += [
        pl.BlockSpec((1, D, tf), lambda i, e, f: (e, 0, f)),
        pl.BlockSpec((1, D, tf), lambda i, e, f: (e, 0, f)),
        pl.BlockSpec((1, tf, D), lambda i, e, f: (e, f, 0)),
    ]
    args += [w_gate.astype(BF16), w_up.astype(BF16), w_down.astype(BF16)]
    if final_norm:
        in_specs.append(_resident((1, D)))
        args.append(g_final.reshape(1, D))
    body = functools.partial(_ffn_body, n_e=n_e, n_f=n_f, use_gates=use_gates, final_norm=final_norm)
    return pl.pallas_call(
        body,
        grid=(M // tm, n_e, n_f),
        in_specs=in_specs,
        out_specs=pl.BlockSpec((tm, D), lambda i, e, f: (i, 0)),
        out_shape=jax.ShapeDtypeStruct((M, D), F32),
        scratch_shapes=[pltpu.VMEM((tm, D), BF16), pltpu.VMEM((tm, D), F32)],
        compiler_params=_params("arbitrary", "arbitrary", "arbitrary"),
        name="moe_swiglu" if use_gates else "dense_swiglu",
    )(*args)


def _router_body(x_ref, gn_ref, wh_ref, wl_ref, tril_ref, o_ref, cnt_ref, *, n_experts):
    @pl.when(pl.program_id(0) == 0)
    def _():
        cnt_ref[...] = jnp.zeros_like(cnt_ref)

    hn = _rms(x_ref[...], gn_ref[...])
    hh = hn.astype(BF16)
    hl = (hn - hh.astype(F32)).astype(BF16)
    wh = wh_ref[...]
    logits = _dot(hh, wh) + _dot(hh, wl_ref[...]) + _dot(hl, wh)
    lane = lax.broadcasted_iota(jnp.int32, logits.shape, 1)
    lg = jnp.where(lane < n_experts, logits, -jnp.inf)
    m0 = jnp.max(lg, axis=1, keepdims=True)
    i0 = jnp.min(jnp.where(lg == m0, lane, LANES), axis=1, keepdims=True)
    lg = jnp.where(lane == i0, -jnp.inf, lg)
    m1 = jnp.max(lg, axis=1, keepdims=True)
    i1 = jnp.min(jnp.where(lg == m1, lane, LANES), axis=1, keepdims=True)
    e1 = jnp.exp(m1 - m0)
    w0 = 1.0 / (1.0 + e1)
    routed = ((lane == i0) | (lane == i1)).astype(F32)
    incl = _dot(tril_ref[...], routed.astype(BF16))
    excl = incl - routed + cnt_ref[0:1, :]

    def put(col, v):
        return jnp.where(lane == col, v, 0.0)

    o_ref[...] = (put(META_I0, i0.astype(F32)) + put(META_I1, i1.astype(F32)) + put(META_W0, w0)
                  + put(META_W1, e1 * w0) + put(META_R0, _lane_col(excl, lane, i0))
                  + put(META_R1, _lane_col(excl, lane, i1)))
    cnt_ref[...] = jnp.broadcast_to(cnt_ref[0:1, :] + incl[incl.shape[0] - 1:, :], cnt_ref.shape)


META_I0, META_I1, META_W0, META_W1, META_R0, META_R1 = range(6)


def _router(x2, gn, w_router, tm=512):
    M, D = x2.shape
    n_experts = w_router.shape[1]
    tm = min(tm, M)
    wpad = jnp.pad(w_router, ((0, 0), (0, LANES - n_experts)))
    wh = wpad.astype(BF16)
    wl = (wpad - wh.astype(F32)).astype(BF16)
    tril = jnp.tril(jnp.ones((tm, tm), BF16))
    return pl.pallas_call(
        functools.partial(_router_body, n_experts=n_experts),
        grid=(M // tm,),
        in_specs=[pl.BlockSpec((tm, D), lambda i: (i, 0)), _resident((1, D)),
                  _resident((D, LANES)), _resident((D, LANES)), _resident((tm, tm))],
        out_specs=[pl.BlockSpec((tm, LANES), lambda i: (i, 0)), pl.BlockSpec((8, LANES), lambda i: (0, 0))],
        out_shape=[jax.ShapeDtypeStruct((M, LANES), F32), jax.ShapeDtypeStruct((8, LANES), F32)],
        compiler_params=_params("arbitrary"),
        name="moe_router",
    )(x2, gn.reshape(1, D), wh, wl, tril)


def _combine_body(pos_ref, x_ref, meta_ref, y_ref, gfin_ref, o_ref, ybuf, sem, *, tm):
    def row_copy(k, r):
        return pltpu.make_async_copy(y_ref.at[pl.ds(pos_ref[0, 0, k * tm + r], 1)], ybuf.at[k, pl.ds(r, 1)], sem)

    def start(r, c):
        row_copy(0, r).start()
        row_copy(1, r).start()
        return c

    lax.fori_loop(0, tm, start, 0, unroll=8)
    pltpu.make_async_copy(ybuf, ybuf, sem).wait()
    meta = meta_ref[...]
    lane = lax.broadcasted_iota(jnp.int32, meta.shape, 1)
    out = (x_ref[...] + _lane_col(meta, lane, META_W0) * ybuf[0] + _lane_col(meta, lane, META_W1) * ybuf[1])
    o_ref[...] = _rms(out, gfin_ref[...])


def _combine(x2, meta, y, pos0, pos1, g_final, tm=512):
    M, D = x2.shape
    tm = min(tm, M)
    pos = jnp.concatenate([pos0.reshape(M // tm, 1, tm), pos1.reshape(M // tm, 1, tm)], axis=2)
    return pl.pallas_call(
        functools.partial(_combine_body, tm=tm),
        grid=(M // tm,),
        in_specs=[pl.BlockSpec((1, 1, 2 * tm), lambda i: (i, 0, 0), memory_space=pltpu.SMEM),
                  pl.BlockSpec((tm, D), lambda i: (i, 0)),
                  pl.BlockSpec((tm, LANES), lambda i: (i, 0)),
                  pl.BlockSpec(memory_space=pl.ANY),
                  _resident((1, D))],
        out_specs=pl.BlockSpec((tm, D), lambda i: (i, 0)),
        out_shape=jax.ShapeDtypeStruct((M, D), F32),
        scratch_shapes=[pltpu.VMEM((2, tm, D), F32), pltpu.SemaphoreType.DMA(())],
        compiler_params=_params("arbitrary"),
        name="moe_combine",
    )(pos, x2, meta, y, g_final.reshape(1, D))


def _grouped_ffn_body(te_ref, nused_ref, src_ref, src_next_ref, x_ref, gn_ref, wg_ref, wu_ref, wd_ref, o_ref,
                      xbuf, hn_ref, acc_ref, sem, *, tm, n_f, n_tiles):
    j = pl.program_id(0)
    f = pl.program_id(1)
    used = j < nused_ref[0]
    slot = j % 2

    def row_copy(idx_ref, r, s):
        return pltpu.make_async_copy(x_ref.at[pl.ds(idx_ref[0, 0, r], 1)], xbuf.at[s, pl.ds(r, 1)], sem.at[s])

    def start_tile(idx_ref, s):
        def body(r, c):
            row_copy(idx_ref, r, s).start()
            return c
        lax.fori_loop(0, tm, body, 0, unroll=8)

    def wait_tile(s):
        pltpu.make_async_copy(xbuf.at[s], xbuf.at[s], sem.at[s]).wait()

    @pl.when((j == 0) & (f == 0))
    def _():
        start_tile(src_ref, 0)

    @pl.when(f == 0)
    def _():
        wait_tile(slot)

        @pl.when(j + 1 < n_tiles)
        def _():
            start_tile(src_next_ref, 1 - slot)

        hn_ref[...] = _rms(xbuf[slot], gn_ref[...]).astype(BF16)
        acc_ref[...] = jnp.zeros_like(acc_ref)

    @pl.when(used)
    def _():
        hn = hn_ref[...]
        g = _dot(hn, wg_ref[0])
        u = _dot(hn, wu_ref[0])
        acc_ref[...] += _dot(((g * jax.nn.sigmoid(g)) * u).astype(BF16), wd_ref[0])

    @pl.when(f == n_f - 1)
    def _():
        o_ref[...] = acc_ref[...]


def _grouped_ffn(x2, src, gn, w_gate, w_up, w_down, tile_expert, n_used, tm, tf=512):
    P = src.shape[0]
    D = x2.shape[1]
    F = w_gate.shape[2]
    tf = min(tf, F)
    n_f = F // tf
    n_tiles = P // tm
    src3 = src.reshape(n_tiles, 1, tm)
    grid_spec = pltpu.PrefetchScalarGridSpec(
        num_scalar_prefetch=2,
        grid=(n_tiles, n_f),
        in_specs=[pl.BlockSpec((1, 1, tm), lambda j, f, te, nu: (j, 0, 0), memory_space=pltpu.SMEM),
                  pl.BlockSpec((1, 1, tm), lambda j, f, te, nu: (jnp.minimum(j + 1, n_tiles - 1), 0, 0),
                               memory_space=pltpu.SMEM),
                  pl.BlockSpec(memory_space=pl.ANY),
                  pl.BlockSpec((1, D), lambda j, f, te, nu: (0, 0)),
                  pl.BlockSpec((1, D, tf), lambda j, f, te, nu: (te[j], 0, f)),
                  pl.BlockSpec((1, D, tf), lambda j, f, te, nu: (te[j], 0, f)),
                  pl.BlockSpec((1, tf, D), lambda j, f, te, nu: (te[j], f, 0))],
        out_specs=pl.BlockSpec((tm, D), lambda j, f, te, nu: (j, 0)),
        scratch_shapes=[pltpu.VMEM((2, tm, D), F32), pltpu.VMEM((tm, D), BF16), pltpu.VMEM((tm, D), F32),
                        pltpu.SemaphoreType.DMA((2,))])
    return pl.pallas_call(
        functools.partial(_grouped_ffn_body, tm=tm, n_f=n_f, n_tiles=n_tiles),
        grid_spec=grid_spec,
        out_shape=jax.ShapeDtypeStruct((P, D), F32),
        compiler_params=_params("arbitrary", "arbitrary"),
        name="moe_grouped_swiglu",
    )(tile_expert, n_used, src3, src3, x2, gn.reshape(1, D), w_gate.astype(BF16), w_up.astype(BF16),
      w_down.astype(BF16))


def _moe_layer(x2, gn, w_router, w_gate, w_up, w_down, g_final, tm=512):
    M, D = x2.shape
    n_e = w_router.shape[1]
    tm = min(tm, M)
    meta, counts = _router(x2, gn, w_router)
    as_int = lambda col: meta[:, col].astype(jnp.int32)
    i0, i1, r0, r1 = as_int(META_I0), as_int(META_I1), as_int(META_R0), as_int(META_R1)
    padded = (counts[0, :n_e].astype(jnp.int32) + tm - 1) // tm * tm
    ends = jnp.cumsum(padded)
    offsets = ends - padded
    pos0 = offsets[i0] + r0
    pos1 = offsets[i1] + r1
    P = TOP_K * M + n_e * tm
    tok = jnp.arange(M, dtype=jnp.int32)
    src = jnp.zeros((P,), jnp.int32).at[jnp.concatenate([pos0, pos1])].set(jnp.concatenate([tok, tok]))
    tile_start = jnp.arange(P // tm, dtype=jnp.int32) * tm
    tile_expert = jnp.minimum(jnp.searchsorted(ends, tile_start, side="right"), n_e - 1).astype(jnp.int32)
    n_used = (ends[n_e - 1:] // tm).astype(jnp.int32)
    y = _grouped_ffn(x2, src, gn, w_gate, w_up, w_down, tile_expert, n_used, tm)
    return _combine(x2, meta, y, pos0, pos1, g_final, tm)


def _nsa_proj_body(x_ref, gn_ref, w_ref, cos_ref, sin_ref,
                   q_ref, kc_ref, vc_ref, ks_ref, vs_ref, kw_ref, vw_ref, g_ref, *, tm, tiles_per_seq):
    i = pl.program_id(0)
    hn = _rms(x_ref[...], gn_ref[...]).astype(BF16)
    proj = _dot(hn, w_ref[...])
    lane = lax.broadcasted_iota(jnp.int32, (tm, LANES), 1)
    first_half = (lane & (HEAD_DIM - 1)) < HEAD_DIM // 2
    cos = cos_ref[...]
    sin = sin_ref[...]

    def chunk(c):
        return proj[:, c * LANES:(c + 1) * LANES]

    def rope(v):
        rot = jnp.where(first_half, pltpu.roll(v, LANES - HEAD_DIM // 2, 1), pltpu.roll(v, HEAD_DIM // 2, 1))
        return v * cos + rot * sin

    def heads(v):
        return v[:, :HEAD_DIM], pltpu.roll(v, HEAD_DIM, 1)[:, :HEAD_DIM]

    c = 0
    for cc in range(N_HEADS // 2):
        lo, hi = heads(rope(chunk(c)) * Q_SCALE); c += 1
        q_ref[2 * cc] = lo.astype(BF16)
        q_ref[2 * cc + 1] = hi.astype(BF16)
    for cc in range(N_KV_GROUPS // 2):
        lo, hi = heads(rope(chunk(c))); c += 1
        kc_ref[2 * cc] = lo
        kc_ref[2 * cc + 1] = hi
    for cc in range(N_KV_GROUPS // 2):
        lo, hi = heads(chunk(c)); c += 1
        vc_ref[2 * cc] = lo
        vc_ref[2 * cc + 1] = hi
    pos = (i % tiles_per_seq) * tm + lax.broadcasted_iota(jnp.int32, (tm, 1), 0)
    onehot = (lane == pos // SEL_BLOCK).astype(BF16)
    for cc in range(N_KV_GROUPS // 2):
        v = rope(chunk(c)); c += 1
        for j, vv in enumerate((v, pltpu.roll(v, HEAD_DIM, 1))):
            ks_ref[2 * cc + j, :, 0:LANES] = jnp.where(lane < HEAD_DIM, vv, 0.0).astype(BF16)
            ks_ref[2 * cc + j, :, LANES:2 * LANES] = onehot
    ones_col = jnp.where(lane == HEAD_DIM, 1.0, 0.0)

    def store_values(ref):
        nonlocal c
        for cc in range(N_KV_GROUPS // 2):
            v = chunk(c); c += 1
            for j, vv in enumerate((v, pltpu.roll(v, HEAD_DIM, 1))):
                ref[2 * cc + j] = jnp.where(lane < HEAD_DIM, vv, ones_col).astype(BF16)

    store_values(vs_ref)
    for cc in range(N_KV_GROUPS // 2):
        lo, hi = heads(rope(chunk(c))); c += 1
        kw_ref[2 * cc] = lo.astype(BF16)
        kw_ref[2 * cc + 1] = hi.astype(BF16)
    store_values(vw_ref)
    g_ref[...] = jax.nn.sigmoid(chunk(c))


def _nsa_proj(x2, gn, w_in, S, tm=256):
    M, D = x2.shape
    H, G, DH = N_HEADS, N_KV_GROUPS, HEAD_DIM
    tm = min(tm, S)
    assert S % tm == 0
    n_in = w_in.shape[1]
    n_pad = -(-n_in // LANES) * LANES
    wp = jnp.pad(w_in, ((0, 0), (0, n_pad - n_in))).astype(BF16)
    half = DH // 2
    freqs = ROPE_THETA ** (-jnp.arange(half, dtype=F32) / half)
    ang = jnp.arange(S, dtype=F32)[:, None] * freqs[None, :]
    cos = jnp.tile(jnp.cos(ang), (1, 2 * LANES // DH))
    sin = jnp.tile(jnp.concatenate([-jnp.sin(ang), jnp.sin(ang)], axis=1), (1, LANES // DH))
    tiles_per_seq = S // tm
    hd = lambda n, dt: jax.ShapeDtypeStruct((n, M, DH), dt)
    hspec = lambda n: pl.BlockSpec((n, tm, DH), lambda i: (0, i, 0))
    vd = jax.ShapeDtypeStruct((G, M, LANES), BF16)
    vspec = pl.BlockSpec((G, tm, LANES), lambda i: (0, i, 0))
    return pl.pallas_call(
        functools.partial(_nsa_proj_body, tm=tm, tiles_per_seq=tiles_per_seq),
        grid=(M // tm,),
        in_specs=[pl.BlockSpec((tm, D), lambda i: (i, 0)), _resident((1, D)), _resident((D, n_pad)),
                  pl.BlockSpec((tm, LANES), lambda i: (i % tiles_per_seq, 0)),
                  pl.BlockSpec((tm, LANES), lambda i: (i % tiles_per_seq, 0))],
        out_specs=[hspec(H), hspec(G), hspec(G),
                   pl.BlockSpec((G, tm, 2 * LANES), lambda i: (0, i, 0)),
                   vspec, hspec(G), vspec,
                   pl.BlockSpec((tm, LANES), lambda i: (i, 0))],
        out_shape=[hd(H, BF16), hd(G, F32), hd(G, F32),
                   jax.ShapeDtypeStruct((G, M, 2 * LANES), BF16),
                   vd, hd(G, BF16), vd,
                   jax.ShapeDtypeStruct((M, LANES), F32)],
        compiler_params=_params("arbitrary"),
        name="nsa_proj",
    )(x2, gn.reshape(1, D), wp, cos, sin)


def _compress_body(kc_ref, vc_ref, pe_ref, w1_ref, w2_ref, ko_ref, vo_ref, *, nc):
    half = CMP_BLOCK // 2
    for kv, (src, dst) in enumerate(((kc_ref, ko_ref), (vc_ref, vo_ref))):
        top = jnp.zeros((nc, w1_ref.shape[2]), F32)
        bot = jnp.zeros((nc, w1_ref.shape[2]), F32)
        for j in range(half):
            xj = src[0, pl.ds(j, nc, stride=CMP_STRIDE), :]
            top = top + _dot((xj + pe_ref[kv, j:j + 1, :]).astype(BF16),
                             w1_ref[kv, j * HEAD_DIM:(j + 1) * HEAD_DIM, :])
            bot = bot + _dot((xj + pe_ref[kv, half + j:half + j + 1, :]).astype(BF16),
                             w1_ref[kv, (half + j) * HEAD_DIM:(half + j + 1) * HEAD_DIM, :])
        hid = top + pltpu.roll(bot, nc - 1, 0)
        dst[0] = _dot(jax.nn.gelu(hid).astype(BF16), w2_ref[kv]).astype(BF16)


def _compress(kc, vc, pe, w1, w2, B, S):
    assert CMP_BLOCK == 2 * CMP_STRIDE
    G, M, DH = kc.shape
    nc = S // CMP_STRIDE
    spec_in = pl.BlockSpec((1, S, DH), lambda b, g: (g, b, 0))
    spec_out = pl.BlockSpec((1, nc, DH), lambda b, g: (g, b, 0))
    out = jax.ShapeDtypeStruct((G, B * nc, DH), BF16)
    return pl.pallas_call(
        functools.partial(_compress_body, nc=nc),
        grid=(B, G),
        in_specs=[spec_in, spec_in, _resident(pe.shape), _resident(w1.shape), _resident(w2.shape)],
        out_specs=[spec_out, spec_out],
        out_shape=[out, out],
        compiler_params=_params("arbitrary", "arbitrary"),
        name="nsa_compress",
    )(kc, vc, pe, w1.astype(BF16), w2.astype(BF16))


def _store_head(o_ref, oh, gates, g, h, branch, rows=slice(None)):
    lane = lax.broadcasted_iota(jnp.int32, gates.shape, 1)
    col = 3 * (HEADS_PER_GROUP * g + h) + branch
    o_ref[rows, h * HEAD_DIM:(h + 1) * HEAD_DIM] = (oh * _lane_col(gates, lane, col)).astype(o_ref.dtype)


def _normalized_head(acc):
    return acc[:, :HEAD_DIM] * (1.0 / acc[:, HEAD_DIM:HEAD_DIM + 1])


def _cmp_body(q_ref, kc_ref, vc_ref, gates_ref, ov_ref, o_ref, bias_ref, *, tq, nc, n_s, k_sel):
    g = pl.program_id(1)
    i = pl.program_id(2)
    gates = gates_ref[...]
    kc = kc_ref[0]
    vc = vc_ref[0]
    t = i * tq + lax.broadcasted_iota(jnp.int32, (tq, 1), 0)
    cend = lax.broadcasted_iota(jnp.int32, (1, nc), 1) * CMP_STRIDE + (CMP_BLOCK - 1)
    mask = cend <= t
    ps = None
    for h in range(HEADS_PER_GROUP):
        s = lax.dot_general(q_ref[h], kc, _NT, preferred_element_type=F32)
        s = jnp.where(mask, s, MASK_VALUE)
        m = jnp.max(s, axis=1, keepdims=True)
        m = jnp.where(m > 0.5 * MASK_VALUE, m, 0.0)
        e = jnp.exp2(s - m)
        p = e * (1.0 / jnp.maximum(jnp.sum(e, axis=1, keepdims=True), 1e-30))
        _store_head(o_ref, _dot(p.astype(BF16), vc), gates, g, h, 0)
        ps = p if ps is None else ps + p

    ph = ps.astype(BF16)
    pl_ = (ps - ph.astype(F32)).astype(BF16)
    imp = _dot(ph, ov_ref[...]) + _dot(pl_, ov_ref[...])
    j = lax.broadcasted_iota(jnp.int32, (tq, LANES), 1)
    cur = t // SEL_BLOCK
    forced = (j == 0) | (j == cur) | (j == cur - 1)
    valid = j * SEL_BLOCK <= t
    score = jnp.where(forced, FORCE_SCORE, jnp.where(valid, imp, -1.0))
    score = jnp.where(j < n_s, score, -jnp.inf)
    x = score.T
    blk = lax.broadcasted_iota(jnp.int32, (LANES, tq), 0).astype(F32)
    sel = jnp.zeros((LANES, tq), F32)
    for _ in range(k_sel):
        mx = jnp.max(x, axis=0, keepdims=True)
        idx = jnp.min(jnp.where(x == mx, blk, float(LANES)), axis=0, keepdims=True)
        hit = blk == idx
        sel = jnp.where(hit, 1.0, sel)
        x = jnp.where(hit, -jnp.inf, x)
    bias_ref[0] = jnp.where(sel.T > 0.5, 0.0, MASK_VALUE).astype(BF16)


def _overlap_matrix(nc, n_s):
    r = CMP_BLOCK // CMP_STRIDE
    qn = SEL_BLOCK // CMP_STRIDE
    m = np.zeros((nc, LANES), np.float32)
    n_c = nc - r + 1
    chunks = np.arange(n_c)[:, None] + np.arange(r)[None, :]
    np.add.at(m, (np.repeat(np.arange(n_c), r), (chunks // qn).ravel()), 1.0)
    return m


def _cmp_attention(q, kcmp, vcmp, gates, B, S, tq=256):
    H, M, DH = q.shape
    G = N_KV_GROUPS
    tq = min(tq, S)
    nq = S // tq
    nc = S // CMP_STRIDE
    n_s = S // SEL_BLOCK
    assert n_s <= LANES and tq & (tq - 1) == 0
    k_sel = min(N_SEL, n_s)
    ov = jnp.asarray(_overlap_matrix(nc, n_s), BF16)
    return pl.pallas_call(
        functools.partial(_cmp_body, tq=tq, nc=nc, n_s=n_s, k_sel=k_sel),
        grid=(B, G, nq),
        in_specs=[pl.BlockSpec((HEADS_PER_GROUP, tq, DH), lambda b, g, i: (g, b * nq + i, 0)),
                  pl.BlockSpec((1, nc, DH), lambda b, g, i: (g, b, 0)),
                  pl.BlockSpec((1, nc, DH), lambda b, g, i: (g, b, 0)),
                  pl.BlockSpec((tq, LANES), lambda b, g, i: (b * nq + i, 0)),
                  _resident((nc, LANES))],
        out_specs=[pl.BlockSpec((tq, HEADS_PER_GROUP * DH), lambda b, g, i: (b * nq + i, g)),
                   pl.BlockSpec((1, tq, LANES), lambda b, g, i: (g, b * nq + i, 0))],
        out_shape=[jax.ShapeDtypeStruct((M, H * DH), BF16), jax.ShapeDtypeStruct((G, M, LANES), BF16)],
        compiler_params=_params("arbitrary", "arbitrary", "arbitrary"),
        name="nsa_cmp_select",
    )(q, kcmp, vcmp, gates, ov)


def _sel_body(q_ref, bias_ref, k_ref, v_ref, gates_ref, o_ref, qcat, m_ref, acc_ref, *, tq):
    g = pl.program_id(1)
    i = pl.program_id(2)
    hp = HEADS_PER_GROUP
    rows = hp * tq
    qcat[:, 0:LANES] = jnp.zeros((rows, LANES), BF16)
    qcat[:, 0:HEAD_DIM] = q_ref[...].reshape(rows, HEAD_DIM)
    bias = bias_ref[0]
    for h in range(hp):
        qcat[h * tq:(h + 1) * tq, LANES:2 * LANES] = bias
    m_ref[...] = jnp.full((rows, LANES), MASK_VALUE, F32)
    acc_ref[...] = jnp.zeros((rows, LANES), F32)

    def tile(j, width, causal):
        start = pl.multiple_of(j * tq, tq)
        kt = k_ref[0, pl.ds(start, width), :]
        vt = v_ref[0, pl.ds(start, width), :]
        for h in range(hp):
            r0 = h * tq
            s = lax.dot_general(qcat[r0:r0 + tq, :], kt, _NT, preferred_element_type=F32)
            if causal:
                r = lax.broadcasted_iota(jnp.int32, (tq, 1), 0)
                c = lax.broadcasted_iota(jnp.int32, (1, width), 1)
                s = jnp.where(c <= r, s, MASK_VALUE)
            m_prev = m_ref[r0:r0 + tq, :]
            m_new = jnp.maximum(m_prev, jnp.max(s, axis=1, keepdims=True))
            alpha = jnp.exp2(m_prev - m_new)
            p = jnp.exp2(s - jnp.concatenate([m_new] * (width // LANES), axis=1))
            acc_ref[r0:r0 + tq, :] = alpha * acc_ref[r0:r0 + tq, :] + _dot(p.astype(BF16), vt)
            m_ref[r0:r0 + tq, :] = m_new

    def tile_pair(jp, carry):
        tile(2 * jp, 2 * tq, False)
        return carry

    lax.fori_loop(0, i // 2, tile_pair, 0)

    @pl.when(i % 2 == 1)
    def _():
        tile(i - 1, tq, False)

    tile(i, tq, True)
    gates = gates_ref[...]
    for h in range(hp):
        _store_head(o_ref, _normalized_head(acc_ref[h * tq:(h + 1) * tq, :]), gates, g, h, 1)


def _sel_attention(q, bias, kscat, vs, gates, B, S, tq=512):
    H, M, DH = q.shape
    G = N_KV_GROUPS
    tq = min(tq, S)
    nq = S // tq
    assert tq % LANES == 0 and tq % SEL_BLOCK == 0 and S % tq == 0
    rows = HEADS_PER_GROUP * tq
    return pl.pallas_call(
        functools.partial(_sel_body, tq=tq),
        grid=(B, G, nq),
        in_specs=[pl.BlockSpec((HEADS_PER_GROUP, tq, DH), lambda b, g, i: (g, b * nq + i, 0)),
                  pl.BlockSpec((1, tq, LANES), lambda b, g, i: (g, b * nq + i, 0)),
                  pl.BlockSpec((1, S, 2 * LANES), lambda b, g, i: (g, b, 0)),
                  pl.BlockSpec((1, S, LANES), lambda b, g, i: (g, b, 0)),
                  pl.BlockSpec((tq, LANES), lambda b, g, i: (b * nq + i, 0))],
        out_specs=pl.BlockSpec((tq, HEADS_PER_GROUP * DH), lambda b, g, i: (b * nq + i, g)),
        out_shape=jax.ShapeDtypeStruct((M, H * DH), BF16),
        scratch_shapes=[pltpu.VMEM((rows, 2 * LANES), BF16), pltpu.VMEM((rows, LANES), F32),
                        pltpu.VMEM((rows, LANES), F32)],
        compiler_params=_params("arbitrary", "arbitrary", "arbitrary"),
        name="nsa_selected",
    )(q, bias, kscat, vs, gates)


def _win_body(q_ref, k_ref, v_ref, gates_ref, o_ref, *, tq, n_back, n_sub):
    g = pl.program_id(1)
    r = lax.broadcasted_iota(jnp.int32, (tq, 1), 0)
    c = lax.broadcasted_iota(jnp.int32, (1, tq), 1)
    for sub in range(n_sub):
        i = pl.program_id(2) * n_sub + sub
        rows = slice(sub * tq, (sub + 1) * tq)
        gates = gates_ref[rows, :]
        tiles = []
        for back in range(n_back, -1, -1):
            jt = i - back
            start = pl.multiple_of(jnp.maximum(jt, 0) * tq, tq)
            d = r - c + back * tq
            ok = (d >= 0) & (d < WINDOW) & (jt >= 0)
            tiles.append((k_ref[0, pl.ds(start, tq), :], v_ref[0, pl.ds(start, tq), :], ok))
        for h in range(HEADS_PER_GROUP):
            q = q_ref[h, rows, :]
            scores = [jnp.where(ok, lax.dot_general(q, kt, _NT, preferred_element_type=F32), MASK_VALUE)
                      for kt, _, ok in tiles]
            m = scores[0].max(axis=1, keepdims=True)
            for s in scores[1:]:
                m = jnp.maximum(m, s.max(axis=1, keepdims=True))
            acc = jnp.zeros((tq, LANES), F32)
            for s, (_, vt, _) in zip(scores, tiles):
                acc = acc + _dot(jnp.exp2(s - m).astype(BF16), vt)
            _store_head(o_ref, _normalized_head(acc), gates, g, h, 2, rows)


def _win_attention(q, kw, vw, gates, B, S, tq=256, n_sub=2):
    H, M, DH = q.shape
    G = N_KV_GROUPS
    tq = min(tq, S)
    n_sub = min(n_sub, S // tq)
    tile = tq
    tq = tile * n_sub
    nq = S // tq
    n_back = -(-WINDOW // tile)
    return pl.pallas_call(
        functools.partial(_win_body, tq=tile, n_back=n_back, n_sub=n_sub),
        grid=(B, G, nq),
        in_specs=[pl.BlockSpec((HEADS_PER_GROUP, tq, DH), lambda b, g, i: (g, b * nq + i, 0)),
                  pl.BlockSpec((1, S, DH), lambda b, g, i: (g, b, 0)),
                  pl.BlockSpec((1, S, LANES), lambda b, g, i: (g, b, 0)),
                  pl.BlockSpec((tq, LANES), lambda b, g, i: (b * nq + i, 0))],
        out_specs=pl.BlockSpec((tq, HEADS_PER_GROUP * DH), lambda b, g, i: (b * nq + i, g)),
        out_shape=jax.ShapeDtypeStruct((M, H * DH), BF16),
        compiler_params=_params("arbitrary", "arbitrary", "arbitrary"),
        name="nsa_window",
    )(q, kw, vw, gates)


def _outproj_body(x_ref, a_ref, b_ref, c_ref, w_ref, o_ref):
    o = a_ref[...].astype(F32) + b_ref[...].astype(F32) + c_ref[...].astype(F32)
    o_ref[...] = x_ref[...] + _dot(o.astype(BF16), w_ref[...])


def _outproj(x2, oc, os_, ow, w_out, tm=512):
    M, D = x2.shape
    K = w_out.shape[0]
    tm = min(tm, M)
    spec_o = pl.BlockSpec((tm, K), lambda i: (i, 0))
    return pl.pallas_call(
        _outproj_body,
        grid=(M // tm,),
        in_specs=[pl.BlockSpec((tm, D), lambda i: (i, 0)), spec_o, spec_o, spec_o, _resident((K, D))],
        out_specs=pl.BlockSpec((tm, D), lambda i: (i, 0)),
        out_shape=jax.ShapeDtypeStruct((M, D), F32),
        compiler_params=_params("arbitrary"),
        name="nsa_outproj",
    )(x2, oc, os_, ow, w_out.astype(BF16))


def _nsa_layer(x, gn, w_in, cmp_pe, cmp_w1, cmp_w2, w_out):
    B, S, D = x.shape
    x2 = x.reshape(B * S, D)
    q, kc, vc, kscat, vs, kw, vw, gates = _nsa_proj(x2, gn, w_in, S)
    kcmp, vcmp = _compress(kc, vc, cmp_pe, cmp_w1, cmp_w2, B, S)
    o_cmp, bias = _cmp_attention(q, kcmp, vcmp, gates, B, S)
    o_sel = _sel_attention(q, bias, kscat, vs, gates, B, S)
    o_win = _win_attention(q, kw, vw, gates, B, S)
    return _outproj(x2, o_cmp, o_sel, o_win, w_out).reshape(B, S, D)


def kernel(x, norm_mix, norm_ffn, norm_final, lru_w_in, lru_b_in, lru_conv_w, lru_conv_b, lru_w_a, lru_b_a, lru_w_i, lru_b_i, lru_lambda, lru_w_out, lru_b_out, nsa_w_in, nsa_cmp_pe, nsa_cmp_w1, nsa_cmp_w2, nsa_w_out, ffn_w_gate, ffn_w_up, ffn_w_down, moe_w_router, moe_w_gate, moe_w_up, moe_w_down):
    B, S, D = x.shape
    assert norm_mix.shape[0] == 2 and lru_w_in.shape[0] == 1 and nsa_w_in.shape[0] == 1
    x = _lru_layer(x, norm_mix[0], lru_w_in[0], lru_b_in[0], lru_conv_w[0], lru_conv_b[0], lru_w_a[0],
                   lru_b_a[0], lru_w_i[0], lru_b_i[0], lru_lambda[0], lru_w_out[0], lru_b_out[0])
    x2 = _ffn_layer(x.reshape(B * S, D), norm_ffn[0], ffn_w_gate, ffn_w_up, ffn_w_down)
    x2 = _nsa_layer(x2.reshape(B, S, D), norm_mix[1], nsa_w_in[0], nsa_cmp_pe[0], nsa_cmp_w1[0],
                    nsa_cmp_w2[0], nsa_w_out[0]).reshape(B * S, D)
    out = _moe_layer(x2, norm_ffn[1], moe_w_router[0], moe_w_gate[0], moe_w_up[0], moe_w_down[0], norm_final)
    return out.reshape(B, S, D)
```

```python
import functools

import numpy as np
import jax
import jax.numpy as jnp
from jax import lax
from jax.experimental import pallas as pl
from jax.experimental.pallas import tpu as pltpu

F32 = jnp.float32
BF16 = jnp.bfloat16

RMS_EPS = 1e-6
LRU_C = 8.0
CONV_WIDTH = 4
N_HEADS = 16
HEAD_DIM = 64
N_KV_GROUPS = 4
HEADS_PER_GROUP = N_HEADS // N_KV_GROUPS
CMP_BLOCK = 32
CMP_STRIDE = 16
SEL_BLOCK = 64
N_SEL = 16
WINDOW = 512
FORCE_SCORE = 1.0e4
ROPE_THETA = 10000.0
TOP_K = 2

LANES = 128
MASK_VALUE = -1.0e30
VMEM_LIMIT_BYTES = 56 * 1024 * 1024

_NT = (((1,), (1,)), ((), ()))
Q_SCALE = HEAD_DIM ** -0.5 * 1.4426950408889634


def _params(*semantics):
    return pltpu.CompilerParams(dimension_semantics=semantics, vmem_limit_bytes=VMEM_LIMIT_BYTES)


def _resident(shape):
    zeros = (0,) * len(shape)
    return pl.BlockSpec(shape, lambda *_: zeros, pipeline_mode=pl.Buffered(1))


def _rms(x, g):
    return x * lax.rsqrt(jnp.mean(x * x, axis=-1, keepdims=True) + RMS_EPS) * g


def _dot(a, b):
    return jnp.dot(a, b, preferred_element_type=F32)


def _lane_col(vals, lane, idx):
    return jnp.sum(jnp.where(lane == idx, vals, 0.0), axis=1, keepdims=True)


def _lru_body(x_ref, gn_ref, win_ref, bin_ref, cw_ref, cb_ref, wg_ref, bg_ref, lam_ref,
              wout_ref, bout_ref, o_ref, xbuf, hcar, *, ts, dr):
    t = pl.program_id(1)

    @pl.when(t == 0)
    def _():
        xbuf[0:8, :] = jnp.zeros((8, dr), F32)
        hcar[...] = jnp.zeros_like(hcar)

    x = x_ref[0]
    hn = _rms(x, gn_ref[...]).astype(BF16)
    proj = _dot(hn, win_ref[...]) + bin_ref[...]
    gate = jax.nn.gelu(proj[:, :dr])
    xr = proj[:, dr:]

    xbuf[8:8 + ts, :] = xr
    xc = cb_ref[...] + xr * cw_ref[CONV_WIDTH - 1:CONV_WIDTH, :]
    for lag in range(1, CONV_WIDTH):
        k = CONV_WIDTH - 1 - lag
        xc = xc + xbuf[pl.ds(8 - lag, ts), :] * cw_ref[k:k + 1, :]
    xbuf[0:8, :] = xbuf[ts:ts + 8, :]

    gl = _dot(xc.astype(BF16), wg_ref[...]) + bg_ref[...]
    r = jax.nn.sigmoid(gl[:, :dr])
    i = jax.nn.sigmoid(gl[:, dr:])
    z = -lam_ref[...]
    softplus = jnp.maximum(z, 0.0) + jnp.log(1.0 + jnp.exp(-jnp.abs(z)))
    log_a = (-LRU_C * r) * softplus
    a = jnp.exp(log_a)
    mult = jnp.sqrt(1.0 - a * a)
    row = lax.broadcasted_iota(jnp.int32, (ts, 1), 0)
    mult = jnp.where((row == 0) & (t == 0), 1.0, mult)
    u = mult * (i * xc)

    shift = 1
    while shift < ts:
        keep = row >= shift
        a_prev = jnp.where(keep, pltpu.roll(a, shift, 0), 1.0)
        u_prev = jnp.where(keep, pltpu.roll(u, shift, 0), 0.0)
        u = a * u_prev + u
        a = a * a_prev
        shift *= 2
    h = u + a * hcar[...]
    hcar[...] = h[ts - 1:ts, :]

    y = (h * gate).astype(BF16)
    o_ref[0] = x + _dot(y, wout_ref[...]) + bout_ref[...]


def _lru_layer(x, gn, w_in, b_in, conv_w, conv_b, w_a, b_a, w_i, b_i, lam, w_out, b_out):
    B, S, D = x.shape
    dr = w_out.shape[0]
    ts = min(256, S)
    assert S % ts == 0 and ts % 8 == 0
    wg = jnp.concatenate([jax.scipy.linalg.block_diag(*w_a), jax.scipy.linalg.block_diag(*w_i)], axis=1)
    row = lambda v: v.reshape(1, -1)
    body = functools.partial(_lru_body, ts=ts, dr=dr)
    return pl.pallas_call(
        body,
        grid=(B, S // ts),
        in_specs=[
            pl.BlockSpec((1, ts, D), lambda b, t: (b, t, 0)),
            _resident((1, D)),
            _resident((D, 2 * dr)), _resident((1, 2 * dr)),
            _resident((CONV_WIDTH, dr)), _resident((1, dr)),
            _resident((dr, 2 * dr)), _resident((1, 2 * dr)),
            _resident((1, dr)),
            _resident((dr, D)), _resident((1, D)),
        ],
        out_specs=pl.BlockSpec((1, ts, D), lambda b, t: (b, t, 0)),
        out_shape=jax.ShapeDtypeStruct((B, S, D), F32),
        scratch_shapes=[pltpu.VMEM((ts + 8, dr), F32), pltpu.VMEM((1, dr), F32)],
        compiler_params=_params("arbitrary", "arbitrary"),
        name="lru_mixer",
    )(x, row(gn), w_in.astype(BF16), row(b_in), conv_w, row(conv_b), wg.astype(BF16),
      row(jnp.concatenate([b_a, b_i])), row(lam), w_out.astype(BF16), row(b_out))


def _ffn_body(*refs, n_e, n_f, use_gates, final_norm):
    x_ref, gn_ref = refs[0], refs[1]
    k = 2
    gates_ref = gfin_ref = None
    if use_gates:
        gates_ref = refs[k]; k += 1
    wg_ref, wu_ref, wd_ref = refs[k:k + 3]; k += 3
    if final_norm:
        gfin_ref = refs[k]; k += 1
    o_ref, hn_ref, acc_ref = refs[k:k + 3]
    e = pl.program_id(1)
    f = pl.program_id(2)

    @pl.when((e == 0) & (f == 0))
    def _():
        hn_ref[...] = _rms(x_ref[...], gn_ref[...]).astype(BF16)
        acc_ref[...] = jnp.zeros_like(acc_ref)

    hn = hn_ref[...]
    g = _dot(hn, wg_ref[0])
    u = _dot(hn, wu_ref[0])
    act = (g * jax.nn.sigmoid(g)) * u
    if use_gates:
        gates = gates_ref[...]
        lane = lax.broadcasted_iota(jnp.int32, gates.shape, 1)
        act = act * _lane_col(gates, lane, e)
    acc_ref[...] += _dot(act.astype(BF16), wd_ref[0])

    @pl.when((e == n_e - 1) & (f == n_f - 1))
    def _():
        out = x_ref[...] + acc_ref[...]
        if final_norm:
            out = _rms(out, gfin_ref[...])
        o_ref[...] = out


def _ffn_layer(x2, gn, w_gate, w_up, w_down, gates=None, g_final=None, tm=1024, tf=512):
    M, D = x2.shape
    n_e, _, F = w_gate.shape
    tm = min(tm, M)
    tf = min(tf, F)
    assert M % tm == 0 and F % tf == 0
    n_f = F // tf
    use_gates = gates is not None
    final_norm = g_final is not None
    in_specs = [pl.BlockSpec((tm, D), lambda i, e, f: (i, 0)), _resident((1, D))]
    args = [x2, gn.reshape(1, D)]
    if use_gates:
        in_specs.append(pl.BlockSpec((tm, LANES), lambda i, e, f: (i, 0)))
        args.append(gates)
    in_specs += [
        pl.BlockSpec((1, D, tf), lambda i, e, f: (e, 0, f)),
        pl.BlockSpec((1, D, tf), lambda i, e, f: (e, 0, f)),
        pl.BlockSpec((1, tf, D), lambda i, e, f: (e, f, 0)),
    ]
    args += [w_gate.astype(BF16), w_up.astype(BF16), w_down.astype(BF16)]
    if final_norm:
        in_specs.append(_resident((1, D)))
        args.append(g_final.reshape(1, D))
    body = functools.partial(_ffn_body, n_e=n_e, n_f=n_f, use_gates=use_gates, final_norm=final_norm)
    return pl.pallas_call(
        body,
        grid=(M // tm, n_e, n_f),
        in_specs=in_specs,
        out_specs=pl.BlockSpec((tm, D), lambda i, e, f: (i, 0)),
        out_shape=jax.ShapeDtypeStruct((M, D), F32),
        scratch_shapes=[pltpu.VMEM((tm, D), BF16), pltpu.VMEM((tm, D), F32)],
        compiler_params=_params("arbitrary", "arbitrary", "arbitrary"),
        name="moe_swiglu" if use_gates else "dense_swiglu",
    )(*args)


def _router_body(x_ref, gn_ref, wh_ref, wl_ref, tril_ref, o_ref, cnt_ref, *, n_experts):
    @pl.when(pl.program_id(0) == 0)
    def _():
        cnt_ref[...] = jnp.zeros_like(cnt_ref)

    hn = _rms(x_ref[...], gn_ref[...])
    hh = hn.astype(BF16)
    hl = (hn - hh.astype(F32)).astype(BF16)
    wh = wh_ref[...]
    logits = _dot(hh, wh) + _dot(hh, wl_ref[...]) + _dot(hl, wh)
    lane = lax.broadcasted_iota(jnp.int32, logits.shape, 1)
    lg = jnp.where(lane < n_experts, logits, -jnp.inf)
    m0 = jnp.max(lg, axis=1, keepdims=True)
    i0 = jnp.min(jnp.where(lg == m0, lane, LANES), axis=1, keepdims=True)
    lg = jnp.where(lane == i0, -jnp.inf, lg)
    m1 = jnp.max(lg, axis=1, keepdims=True)
    i1 = jnp.min(jnp.where(lg == m1, lane, LANES), axis=1, keepdims=True)
    e1 = jnp.exp(m1 - m0)
    w0 = 1.0 / (1.0 + e1)
    routed = ((lane == i0) | (lane == i1)).astype(F32)
    incl = _dot(tril_ref[...], routed.astype(BF16))
    excl = incl - routed + cnt_ref[0:1, :]

    def put(col, v):
        return jnp.where(lane == col, v, 0.0)

    o_ref[...] = (put(META_I0, i0.astype(F32)) + put(META_I1, i1.astype(F32)) + put(META_W0, w0)
                  + put(META_W1, e1 * w0) + put(META_R0, _lane_col(excl, lane, i0))
                  + put(META_R1, _lane_col(excl, lane, i1)))
    cnt_ref[...] = jnp.broadcast_to(cnt_ref[0:1, :] + incl[incl.shape[0] - 1:, :], cnt_ref.shape)


META_I0, META_I1, META_W0, META_W1, META_R0, META_R1 = range(6)


def _router(x2, gn, w_router, tm=512):
    M, D = x2.shape
    n_experts = w_router.shape[1]
    tm = min(tm, M)
    wpad = jnp.pad(w_router, ((0, 0), (0, LANES - n_experts)))
    wh = wpad.astype(BF16)
    wl = (wpad - wh.astype(F32)).astype(BF16)
    tril = jnp.tril(jnp.ones((tm, tm), BF16))
    return pl.pallas_call(
        functools.partial(_router_body, n_experts=n_experts),
        grid=(M // tm,),
        in_specs=[pl.BlockSpec((tm, D), lambda i: (i, 0)), _resident((1, D)),
                  _resident((D, LANES)), _resident((D, LANES)), _resident((tm, tm))],
        out_specs=[pl.BlockSpec((tm, LANES), lambda i: (i, 0)), pl.BlockSpec((8, LANES), lambda i: (0, 0))],
        out_shape=[jax.ShapeDtypeStruct((M, LANES), F32), jax.ShapeDtypeStruct((8, LANES), F32)],
        compiler_params=_params("arbitrary"),
        name="moe_router",
    )(x2, gn.reshape(1, D), wh, wl, tril)


def _combine_body(pos_ref, x_ref, meta_ref, y_ref, gfin_ref, o_ref, ybuf, sem, *, tm):
    def row_copy(k, r):
        return pltpu.make_async_copy(y_ref.at[pl.ds(pos_ref[0, 0, k * tm + r], 1)], ybuf.at[k, pl.ds(r, 1)], sem)

    for r in range(tm):
        row_copy(0, r).start()
        row_copy(1, r).start()
    pltpu.make_async_copy(ybuf, ybuf, sem).wait()
    meta = meta_ref[...]
    lane = lax.broadcasted_iota(jnp.int32, meta.shape, 1)
    out = (x_ref[...] + _lane_col(meta, lane, META_W0) * ybuf[0] + _lane_col(meta, lane, META_W1) * ybuf[1])
    o_ref[...] = _rms(out, gfin_ref[...])


def _combine(x2, meta, y, pos0, pos1, g_final, tm=512):
    M, D = x2.shape
    tm = min(tm, M)
    pos = jnp.concatenate([pos0.reshape(M // tm, 1, tm), pos1.reshape(M // tm, 1, tm)], axis=2)
    return pl.pallas_call(
        functools.partial(_combine_body, tm=tm),
        grid=(M // tm,),
        in_specs=[pl.BlockSpec((1, 1, 2 * tm), lambda i: (i, 0, 0), memory_space=pltpu.SMEM),
                  pl.BlockSpec((tm, D), lambda i: (i, 0)),
                  pl.BlockSpec((tm, LANES), lambda i: (i, 0)),
                  pl.BlockSpec(memory_space=pl.ANY),
                  _resident((1, D))],
        out_specs=pl.BlockSpec((tm, D), lambda i: (i, 0)),
        out_shape=jax.ShapeDtypeStruct((M, D), F32),
        scratch_shapes=[pltpu.VMEM((2, tm, D), F32), pltpu.SemaphoreType.DMA(())],
        compiler_params=_params("arbitrary"),
        name="moe_combine",
    )(pos, x2, meta, y, g_final.reshape(1, D))


def _grouped_ffn_body(te_ref, nused_ref, src_ref, src_next_ref, x_ref, gn_ref, wg_ref, wu_ref, wd_ref, o_ref,
                      xbuf, hn_ref, acc_ref, sem, *, tm, n_f, n_tiles):
    j = pl.program_id(0)
    f = pl.program_id(1)
    n_used = nused_ref[0]
    used = j < n_used
    gathered = (j == 0) | (j <= n_used)
    slot = j % 2
    chunk = tm // (n_f + 1)

    def row_copy(idx_ref, r, s):
        return pltpu.make_async_copy(x_ref.at[pl.ds(idx_ref[0, 0, r], 1)], xbuf.at[s, pl.ds(r, 1)], sem.at[s])

    def start_chunk(first):
        for k in range(chunk):
            row_copy(src_next_ref, first + k, 1 - slot).start()

    @pl.when((j == 0) & (f == 0))
    def _():
        def body(r, c):
            row_copy(src_ref, r, 0).start()
            return c
        lax.fori_loop(0, tm, body, 0, unroll=8)

    @pl.when(f == 0)
    def _():
        acc_ref[...] = jnp.zeros_like(acc_ref)

        @pl.when(gathered)
        def _():
            pltpu.make_async_copy(xbuf.at[slot], xbuf.at[slot], sem.at[slot]).wait()
            hn_ref[...] = _rms(xbuf[slot], gn_ref[...]).astype(BF16)

        @pl.when(used)
        def _():
            start_chunk(0)

    @pl.when(used)
    def _():
        hn = hn_ref[...]
        g = _dot(hn, wg_ref[0])
        u = _dot(hn, wu_ref[0])
        acc_ref[...] += _dot(((g * jax.nn.sigmoid(g)) * u).astype(BF16), wd_ref[0])
        start_chunk((f + 1) * chunk)

    @pl.when(f == n_f - 1)
    def _():
        o_ref[...] = acc_ref[...]


def _grouped_ffn(x2, src, gn, w_gate, w_up, w_down, tile_expert, n_used, tm, tf=512):
    P = src.shape[0]
    D = x2.shape[1]
    F = w_gate.shape[2]
    tf = min(tf, F)
    n_f = F // tf
    n_tiles = P // tm
    assert F % tf == 0 and tm % (n_f + 1) == 0
    src3 = src.reshape(n_tiles, 1, tm)
    grid_spec = pltpu.PrefetchScalarGridSpec(
        num_scalar_prefetch=2,
        grid=(n_tiles, n_f),
        in_specs=[pl.BlockSpec((1, 1, tm), lambda j, f, te, nu: (j, 0, 0), memory_space=pltpu.SMEM),
                  pl.BlockSpec((1, 1, tm), lambda j, f, te, nu: (jnp.minimum(j + 1, n_tiles - 1), 0, 0),
                               memory_space=pltpu.SMEM),
                  pl.BlockSpec(memory_space=pl.ANY),
                  pl.BlockSpec((1, D), lambda j, f, te, nu: (0, 0)),
                  pl.BlockSpec((1, D, tf), lambda j, f, te, nu: (te[j], 0, f)),
                  pl.BlockSpec((1, D, tf), lambda j, f, te, nu: (te[j], 0, f)),
                  pl.BlockSpec((1, tf, D), lambda j, f, te, nu: (te[j], f, 0))],
        out_specs=pl.BlockSpec((tm, D), lambda j, f, te, nu: (j, 0)),
        scratch_shapes=[pltpu.VMEM((2, tm, D), F32), pltpu.VMEM((tm, D), BF16), pltpu.VMEM((tm, D), F32),
                        pltpu.SemaphoreType.DMA((2,))])
    return pl.pallas_call(
        functools.partial(_grouped_ffn_body, tm=tm, n_f=n_f, n_tiles=n_tiles),
        grid_spec=grid_spec,
        out_shape=jax.ShapeDtypeStruct((P, D), F32),
        compiler_params=_params("arbitrary", "arbitrary"),
        name="moe_grouped_swiglu",
    )(tile_expert, n_used, src3, src3, x2, gn.reshape(1, D), w_gate.astype(BF16), w_up.astype(BF16),
      w_down.astype(BF16))


def _moe_layer(x2, gn, w_router, w_gate, w_up, w_down, g_final, tm=512):
    M, D = x2.shape
    n_e = w_router.shape[1]
    tm = min(tm, M)
    meta, counts = _router(x2, gn, w_router)
    as_int = lambda col: meta[:, col].astype(jnp.int32)
    i0, i1, r0, r1 = as_int(META_I0), as_int(META_I1), as_int(META_R0), as_int(META_R1)
    padded = (counts[0, :n_e].astype(jnp.int32) + tm - 1) // tm * tm
    ends = jnp.cumsum(padded)
    offsets = ends - padded
    pos0 = offsets[i0] + r0
    pos1 = offsets[i1] + r1
    P = TOP_K * M + n_e * tm
    tok = jnp.arange(M, dtype=jnp.int32)
    src = jnp.zeros((P,), jnp.int32).at[jnp.concatenate([pos0, pos1])].set(jnp.concatenate([tok, tok]))
    tile_start = jnp.arange(P // tm, dtype=jnp.int32) * tm
    tile_expert = jnp.minimum(jnp.searchsorted(ends, tile_start, side="right"), n_e - 1).astype(jnp.int32)
    n_used = (ends[n_e - 1:] // tm).astype(jnp.int32)
    y = _grouped_ffn(x2, src, gn, w_gate, w_up, w_down, tile_expert, n_used, tm)
    return _combine(x2, meta, y, pos0, pos1, g_final, tm)


def _nsa_proj_body(x_ref, gn_ref, w_ref, cos_ref, sin_ref,
                   q_ref, kc_ref, vc_ref, ks_ref, vs_ref, kw_ref, vw_ref, g_ref, *, tm, tiles_per_seq):
    i = pl.program_id(0)
    hn = _rms(x_ref[...], gn_ref[...]).astype(BF16)
    proj = _dot(hn, w_ref[...])
    lane = lax.broadcasted_iota(jnp.int32, (tm, LANES), 1)
    first_half = (lane & (HEAD_DIM - 1)) < HEAD_DIM // 2
    cos = cos_ref[...]
    sin = sin_ref[...]

    def chunk(c):
        return proj[:, c * LANES:(c + 1) * LANES]

    def rope(v):
        rot = jnp.where(first_half, pltpu.roll(v, LANES - HEAD_DIM // 2, 1), pltpu.roll(v, HEAD_DIM // 2, 1))
        return v * cos + rot * sin

    def heads(v):
        return v[:, :HEAD_DIM], pltpu.roll(v, HEAD_DIM, 1)[:, :HEAD_DIM]

    c = 0
    for cc in range(N_HEADS // 2):
        lo, hi = heads(rope(chunk(c)) * Q_SCALE); c += 1
        q_ref[2 * cc] = lo.astype(BF16)
        q_ref[2 * cc + 1] = hi.astype(BF16)
    for cc in range(N_KV_GROUPS // 2):
        lo, hi = heads(rope(chunk(c))); c += 1
        kc_ref[2 * cc] = lo
        kc_ref[2 * cc + 1] = hi
    for cc in range(N_KV_GROUPS // 2):
        lo, hi = heads(chunk(c)); c += 1
        vc_ref[2 * cc] = lo
        vc_ref[2 * cc + 1] = hi
    pos = (i % tiles_per_seq) * tm + lax.broadcasted_iota(jnp.int32, (tm, 1), 0)
    onehot = (lane == pos // SEL_BLOCK).astype(BF16)
    for cc in range(N_KV_GROUPS // 2):
        v = rope(chunk(c)); c += 1
        for j, vv in enumerate((v, pltpu.roll(v, HEAD_DIM, 1))):
            ks_ref[2 * cc + j, :, 0:LANES] = jnp.where(lane < HEAD_DIM, vv, 0.0).astype(BF16)
            ks_ref[2 * cc + j, :, LANES:2 * LANES] = onehot
    ones_col = jnp.where(lane == HEAD_DIM, 1.0, 0.0)

    def store_values(ref):
        nonlocal c
        for cc in range(N_KV_GROUPS // 2):
            v = chunk(c); c += 1
            for j, vv in enumerate((v, pltpu.roll(v, HEAD_DIM, 1))):
                ref[2 * cc + j] = jnp.where(lane < HEAD_DIM, vv, ones_col).astype(BF16)

    store_values(vs_ref)
    for cc in range(N_KV_GROUPS // 2):
        lo, hi = heads(rope(chunk(c))); c += 1
        kw_ref[2 * cc] = lo.astype(BF16)
        kw_ref[2 * cc + 1] = hi.astype(BF16)
    store_values(vw_ref)
    g_ref[...] = jax.nn.sigmoid(chunk(c))


def _nsa_proj(x2, gn, w_in, S, tm=256):
    M, D = x2.shape
    H, G, DH = N_HEADS, N_KV_GROUPS, HEAD_DIM
    tm = min(tm, S)
    assert S % tm == 0
    n_in = w_in.shape[1]
    n_pad = -(-n_in // LANES) * LANES
    wp = jnp.pad(w_in, ((0, 0), (0, n_pad - n_in))).astype(BF16)
    half = DH // 2
    freqs = ROPE_THETA ** (-jnp.arange(half, dtype=F32) / half)
    ang = jnp.arange(S, dtype=F32)[:, None] * freqs[None, :]
    cos = jnp.tile(jnp.cos(ang), (1, 2 * LANES // DH))
    sin = jnp.tile(jnp.concatenate([-jnp.sin(ang), jnp.sin(ang)], axis=1), (1, LANES // DH))
    tiles_per_seq = S // tm
    hd = lambda n, dt: jax.ShapeDtypeStruct((n, M, DH), dt)
    hspec = lambda n: pl.BlockSpec((n, tm, DH), lambda i: (0, i, 0))
    vd = jax.ShapeDtypeStruct((G, M, LANES), BF16)
    vspec = pl.BlockSpec((G, tm, LANES), lambda i: (0, i, 0))
    return pl.pallas_call(
        functools.partial(_nsa_proj_body, tm=tm, tiles_per_seq=tiles_per_seq),
        grid=(M // tm,),
        in_specs=[pl.BlockSpec((tm, D), lambda i: (i, 0)), _resident((1, D)), _resident((D, n_pad)),
                  pl.BlockSpec((tm, LANES), lambda i: (i % tiles_per_seq, 0)),
                  pl.BlockSpec((tm, LANES), lambda i: (i % tiles_per_seq, 0))],
        out_specs=[hspec(H), hspec(G), hspec(G),
                   pl.BlockSpec((G, tm, 2 * LANES), lambda i: (0, i, 0)),
                   vspec, hspec(G), vspec,
                   pl.BlockSpec((tm, LANES), lambda i: (i, 0))],
        out_shape=[hd(H, BF16), hd(G, F32), hd(G, F32),
                   jax.ShapeDtypeStruct((G, M, 2 * LANES), BF16),
                   vd, hd(G, BF16), vd,
                   jax.ShapeDtypeStruct((M, LANES), F32)],
        compiler_params=_params("arbitrary"),
        name="nsa_proj",
    )(x2, gn.reshape(1, D), wp, cos, sin)


def _compress_body(kc_ref, vc_ref, pe_ref, w1_ref, w2_ref, ko_ref, vo_ref, *, nc):
    half = CMP_BLOCK // 2
    for kv, (src, dst) in enumerate(((kc_ref, ko_ref), (vc_ref, vo_ref))):
        top = jnp.zeros((nc, w1_ref.shape[2]), F32)
        bot = jnp.zeros((nc, w1_ref.shape[2]), F32)
        for j in range(half):
            xj = src[0, pl.ds(j, nc, stride=CMP_STRIDE), :]
            top = top + _dot((xj + pe_ref[kv, j:j + 1, :]).astype(BF16),
                             w1_ref[kv, j * HEAD_DIM:(j + 1) * HEAD_DIM, :])
            bot = bot + _dot((xj + pe_ref[kv, half + j:half + j + 1, :]).astype(BF16),
                             w1_ref[kv, (half + j) * HEAD_DIM:(half + j + 1) * HEAD_DIM, :])
        hid = top + pltpu.roll(bot, nc - 1, 0)
        dst[0] = _dot(jax.nn.gelu(hid).astype(BF16), w2_ref[kv]).astype(BF16)


def _compress(kc, vc, pe, w1, w2, B, S):
    assert CMP_BLOCK == 2 * CMP_STRIDE
    G, M, DH = kc.shape
    nc = S // CMP_STRIDE
    spec_in = pl.BlockSpec((1, S, DH), lambda b, g: (g, b, 0))
    spec_out = pl.BlockSpec((1, nc, DH), lambda b, g: (g, b, 0))
    out = jax.ShapeDtypeStruct((G, B * nc, DH), BF16)
    return pl.pallas_call(
        functools.partial(_compress_body, nc=nc),
        grid=(B, G),
        in_specs=[spec_in, spec_in, _resident(pe.shape), _resident(w1.shape), _resident(w2.shape)],
        out_specs=[spec_out, spec_out],
        out_shape=[out, out],
        compiler_params=_params("arbitrary", "arbitrary"),
        name="nsa_compress",
    )(kc, vc, pe, w1.astype(BF16), w2.astype(BF16))


def _store_head(o_ref, oh, gates, g, h, branch, rows=slice(None)):
    lane = lax.broadcasted_iota(jnp.int32, gates.shape, 1)
    col = 3 * (HEADS_PER_GROUP * g + h) + branch
    o_ref[rows, h * HEAD_DIM:(h + 1) * HEAD_DIM] = (oh * _lane_col(gates, lane, col)).astype(o_ref.dtype)


def _normalized_head(acc):
    return acc[:, :HEAD_DIM] * (1.0 / acc[:, HEAD_DIM:HEAD_DIM + 1])


def _cmp_body(q_ref, kc_ref, vc_ref, gates_ref, ov_ref, o_ref, bias_ref, imp_ref, *, tq, nc, n_s, k_sel):
    g = pl.program_id(1)
    i = pl.program_id(2)
    gates = gates_ref[...]
    t = i * tq + lax.broadcasted_iota(jnp.int32, (tq, 1), 0)

    def attend(w):
        kc = kc_ref[0, 0:w, :]
        vc = vc_ref[0, 0:w, :]
        cend = lax.broadcasted_iota(jnp.int32, (1, w), 1) * CMP_STRIDE + (CMP_BLOCK - 1)
        mask = cend <= t
        ps = None
        for h in range(HEADS_PER_GROUP):
            s = lax.dot_general(q_ref[h], kc, _NT, preferred_element_type=F32)
            s = jnp.where(mask, s, MASK_VALUE)
            m = jnp.max(s, axis=1, keepdims=True)
            m = jnp.where(m > 0.5 * MASK_VALUE, m, 0.0)
            e = jnp.exp2(s - m)
            p = e * (1.0 / jnp.maximum(jnp.sum(e, axis=1, keepdims=True), 1e-30))
            _store_head(o_ref, _dot(p.astype(BF16), vc), gates, g, h, 0)
            ps = p if ps is None else ps + p
        ph = ps.astype(BF16)
        pl_ = (ps - ph.astype(F32)).astype(BF16)
        imp_ref[...] = _dot(ph, ov_ref[0:w, :]) + _dot(pl_, ov_ref[0:w, :])

    n_chunks = nc // LANES
    need = ((i + 1) * (tq // CMP_STRIDE) + LANES - 1) // LANES
    for k in range(1, n_chunks + 1):
        pl.when((need == k) if k < n_chunks else (need >= k))(functools.partial(attend, k * LANES))

    j = lax.broadcasted_iota(jnp.int32, (tq, LANES), 1)
    cur = t // SEL_BLOCK
    forced = (j == 0) | (j == cur) | (j == cur - 1)
    valid = j * SEL_BLOCK <= t
    score = jnp.where(forced, FORCE_SCORE, jnp.where(valid, imp_ref[...], -1.0))
    score = jnp.where(j < n_s, score, -jnp.inf)
    x = score.T
    blk_i = lax.broadcasted_iota(jnp.int32, (LANES, tq), 0)
    blk = blk_i.astype(F32)

    def topk(x, sel, n_iter):
        for _ in range(n_iter):
            mx = jnp.max(x, axis=0, keepdims=True)
            idx = jnp.min(jnp.where(x == mx, blk, float(LANES)), axis=0, keepdims=True)
            hit = blk == idx
            sel = jnp.where(hit, 1.0, sel)
            x = jnp.where(hit, -jnp.inf, x)
        bias_ref[0] = jnp.where(sel.T > 0.5, 0.0, MASK_VALUE).astype(BF16)

    n_forced = 3
    direct = (i * tq >= 2 * SEL_BLOCK) & (k_sel > n_forced)

    @pl.when(direct)
    def _():
        cur_t = (i * tq + lax.broadcasted_iota(jnp.int32, (1, tq), 1)) // SEL_BLOCK
        forced_t = (blk_i == 0) | (blk_i == cur_t) | (blk_i == cur_t - 1)
        topk(jnp.where(forced_t, -jnp.inf, x), forced_t.astype(F32), k_sel - n_forced)

    @pl.when(jnp.logical_not(direct))
    def _():
        topk(x, jnp.zeros((LANES, tq), F32), k_sel)


def _overlap_matrix(nc, n_s):
    r = CMP_BLOCK // CMP_STRIDE
    qn = SEL_BLOCK // CMP_STRIDE
    m = np.zeros((nc, LANES), np.float32)
    n_c = nc - r + 1
    chunks = np.arange(n_c)[:, None] + np.arange(r)[None, :]
    np.add.at(m, (np.repeat(np.arange(n_c), r), (chunks // qn).ravel()), 1.0)
    return m


def _cmp_attention(q, kcmp, vcmp, gates, B, S, tq=256):
    H, M, DH = q.shape
    G = N_KV_GROUPS
    tq = min(tq, S)
    nq = S // tq
    nc = S // CMP_STRIDE
    n_s = S // SEL_BLOCK
    assert n_s <= LANES and tq & (tq - 1) == 0
    k_sel = min(N_SEL, n_s)
    ov = jnp.asarray(_overlap_matrix(nc, n_s), BF16)
    return pl.pallas_call(
        functools.partial(_cmp_body, tq=tq, nc=nc, n_s=n_s, k_sel=k_sel),
        grid=(B, G, nq),
        in_specs=[pl.BlockSpec((HEADS_PER_GROUP, tq, DH), lambda b, g, i: (g, b * nq + i, 0)),
                  pl.BlockSpec((1, nc, DH), lambda b, g, i: (g, b, 0)),
                  pl.BlockSpec((1, nc, DH), lambda b, g, i: (g, b, 0)),
                  pl.BlockSpec((tq, LANES), lambda b, g, i: (b * nq + i, 0)),
                  _resident((nc, LANES))],
        out_specs=[pl.BlockSpec((tq, HEADS_PER_GROUP * DH), lambda b, g, i: (b * nq + i, g)),
                   pl.BlockSpec((1, tq, LANES), lambda b, g, i: (g, b * nq + i, 0))],
        out_shape=[jax.ShapeDtypeStruct((M, H * DH), BF16), jax.ShapeDtypeStruct((G, M, LANES), BF16)],
        scratch_shapes=[pltpu.VMEM((tq, LANES), F32)],
        compiler_params=_params("arbitrary", "arbitrary", "arbitrary"),
        name="nsa_cmp_select",
    )(q, kcmp, vcmp, gates, ov)


def _sel_body(q_ref, bias_ref, k_ref, v_ref, gates_ref, o_ref, qcat, m_ref, acc_ref, *, tq):
    g = pl.program_id(1)
    i = pl.program_id(2)
    hp = HEADS_PER_GROUP
    rows = hp * tq
    qcat[:, 0:LANES] = jnp.zeros((rows, LANES), BF16)
    qcat[:, 0:HEAD_DIM] = q_ref[...].reshape(rows, HEAD_DIM)
    bias = bias_ref[0]
    for h in range(hp):
        qcat[h * tq:(h + 1) * tq, LANES:2 * LANES] = bias
    m_ref[...] = jnp.full((rows, LANES), MASK_VALUE, F32)
    acc_ref[...] = jnp.zeros((rows, LANES), F32)

    def tile(j, width, causal):
        start = pl.multiple_of(j * tq, tq)
        kt = k_ref[0, pl.ds(start, width), :]
        vt = v_ref[0, pl.ds(start, width), :]
        for h in range(hp):
            r0 = h * tq
            s = lax.dot_general(qcat[r0:r0 + tq, :], kt, _NT, preferred_element_type=F32)
            if causal:
                r = lax.broadcasted_iota(jnp.int32, (tq, 1), 0)
                c = lax.broadcasted_iota(jnp.int32, (1, width), 1)
                s = jnp.where(c <= r, s, MASK_VALUE)
            m_prev = m_ref[r0:r0 + tq, :]
            m_new = jnp.maximum(m_prev, jnp.max(s, axis=1, keepdims=True))
            alpha = jnp.exp2(m_prev - m_new)
            p = jnp.exp2(s - jnp.concatenate([m_new] * (width // LANES), axis=1))
            acc_ref[r0:r0 + tq, :] = alpha * acc_ref[r0:r0 + tq, :] + _dot(p.astype(BF16), vt)
            m_ref[r0:r0 + tq, :] = m_new

    def tile_pair(jp, carry):
        tile(2 * jp, 2 * tq, False)
        return carry

    lax.fori_loop(0, i // 2, tile_pair, 0)

    @pl.when(i % 2 == 1)
    def _():
        tile(i - 1, tq, False)

    tile(i, tq, True)
    gates = gates_ref[...]
    for h in range(hp):
        _store_head(o_ref, _normalized_head(acc_ref[h * tq:(h + 1) * tq, :]), gates, g, h, 1)


def _sel_attention(q, bias, kscat, vs, gates, B, S, tq=512):
    H, M, DH = q.shape
    G = N_KV_GROUPS
    tq = min(tq, S)
    nq = S // tq
    assert tq % LANES == 0 and tq % SEL_BLOCK == 0 and S % tq == 0
    rows = HEADS_PER_GROUP * tq
    return pl.pallas_call(
        functools.partial(_sel_body, tq=tq),
        grid=(B, G, nq),
        in_specs=[pl.BlockSpec((HEADS_PER_GROUP, tq, DH), lambda b, g, i: (g, b * nq + i, 0)),
                  pl.BlockSpec((1, tq, LANES), lambda b, g, i: (g, b * nq + i, 0)),
                  pl.BlockSpec((1, S, 2 * LANES), lambda b, g, i: (g, b, 0)),
                  pl.BlockSpec((1, S, LANES), lambda b, g, i: (g, b, 0)),
                  pl.BlockSpec((tq, LANES), lambda b, g, i: (b * nq + i, 0))],
        out_specs=pl.BlockSpec((tq, HEADS_PER_GROUP * DH), lambda b, g, i: (b * nq + i, g)),
        out_shape=jax.ShapeDtypeStruct((M, H * DH), BF16),
        scratch_shapes=[pltpu.VMEM((rows, 2 * LANES), BF16), pltpu.VMEM((rows, LANES), F32),
                        pltpu.VMEM((rows, LANES), F32)],
        compiler_params=_params("arbitrary", "arbitrary", "arbitrary"),
        name="nsa_selected",
    )(q, bias, kscat, vs, gates)


def _win_body(q_ref, k_ref, v_ref, gates_ref, o_ref, *, tq, n_back, n_sub):
    g = pl.program_id(1)
    r = lax.broadcasted_iota(jnp.int32, (tq, 1), 0)
    c = lax.broadcasted_iota(jnp.int32, (1, tq), 1)
    for sub in range(n_sub):
        i = pl.program_id(2) * n_sub + sub
        rows = slice(sub * tq, (sub + 1) * tq)
        gates = gates_ref[rows, :]
        tiles = []
        for back in range(n_back, -1, -1):
            jt = i - back
            start = pl.multiple_of(jnp.maximum(jt, 0) * tq, tq)
            d = r - c + back * tq
            ok = (d >= 0) & (d < WINDOW) & (jt >= 0)
            tiles.append((k_ref[0, pl.ds(start, tq), :], v_ref[0, pl.ds(start, tq), :], ok))
        for h in range(HEADS_PER_GROUP):
            q = q_ref[h, rows, :]
            scores = [jnp.where(ok, lax.dot_general(q, kt, _NT, preferred_element_type=F32), MASK_VALUE)
                      for kt, _, ok in tiles]
            m = scores[0].max(axis=1, keepdims=True)
            for s in scores[1:]:
                m = jnp.maximum(m, s.max(axis=1, keepdims=True))
            acc = jnp.zeros((tq, LANES), F32)
            for s, (_, vt, _) in zip(scores, tiles):
                acc = acc + _dot(jnp.exp2(s - m).astype(BF16), vt)
            _store_head(o_ref, _normalized_head(acc), gates, g, h, 2, rows)


def _win_attention(q, kw, vw, gates, B, S, tq=256, n_sub=2):
    H, M, DH = q.shape
    G = N_KV_GROUPS
    tq = min(tq, S)
    n_sub = min(n_sub, S // tq)
    tile = tq
    tq = tile * n_sub
    nq = S // tq
    n_back = -(-WINDOW // tile)
    return pl.pallas_call(
        functools.partial(_win_body, tq=tile, n_back=n_back, n_sub=n_sub),
        grid=(B, G, nq),
        in_specs=[pl.BlockSpec((HEADS_PER_GROUP, tq, DH), lambda b, g, i: (g, b * nq + i, 0)),
                  pl.BlockSpec((1, S, DH), lambda b, g, i: (g, b, 0)),
                  pl.BlockSpec((1, S, LANES), lambda b, g, i: (g, b, 0)),
                  pl.BlockSpec((tq, LANES), lambda b, g, i: (b * nq + i, 0))],
        out_specs=pl.BlockSpec((tq, HEADS_PER_GROUP * DH), lambda b, g, i: (b * nq + i, g)),
        out_shape=jax.ShapeDtypeStruct((M, H * DH), BF16),
        compiler_params=_params("arbitrary", "arbitrary", "arbitrary"),
        name="nsa_window",
    )(q, kw, vw, gates)


def _outproj_body(x_ref, a_ref, b_ref, c_ref, w_ref, o_ref):
    o = a_ref[...].astype(F32) + b_ref[...].astype(F32) + c_ref[...].astype(F32)
    o_ref[...] = x_ref[...] + _dot(o.astype(BF16), w_ref[...])


def _outproj(x2, oc, os_, ow, w_out, tm=512):
    M, D = x2.shape
    K = w_out.shape[0]
    tm = min(tm, M)
    spec_o = pl.BlockSpec((tm, K), lambda i: (i, 0))
    return pl.pallas_call(
        _outproj_body,
        grid=(M // tm,),
        in_specs=[pl.BlockSpec((tm, D), lambda i: (i, 0)), spec_o, spec_o, spec_o, _resident((K, D))],
        out_specs=pl.BlockSpec((tm, D), lambda i: (i, 0)),
        out_shape=jax.ShapeDtypeStruct((M, D), F32),
        compiler_params=_params("arbitrary"),
        name="nsa_outproj",
    )(x2, oc, os_, ow, w_out.astype(BF16))


def _nsa_layer(x, gn, w_in, cmp_pe, cmp_w1, cmp_w2, w_out):
    B, S, D = x.shape
    x2 = x.reshape(B * S, D)
    q, kc, vc, kscat, vs, kw, vw, gates = _nsa_proj(x2, gn, w_in, S)
    kcmp, vcmp = _compress(kc, vc, cmp_pe, cmp_w1, cmp_w2, B, S)
    o_cmp, bias = _cmp_attention(q, kcmp, vcmp, gates, B, S)
    o_sel = _sel_attention(q, bias, kscat, vs, gates, B, S)
    o_win = _win_attention(q, kw, vw, gates, B, S)
    return _outproj(x2, o_cmp, o_sel, o_win, w_out).reshape(B, S, D)


def kernel(x, norm_mix, norm_ffn, norm_final, lru_w_in, lru_b_in, lru_conv_w, lru_conv_b, lru_w_a, lru_b_a, lru_w_i, lru_b_i, lru_lambda, lru_w_out, lru_b_out, nsa_w_in, nsa_cmp_pe, nsa_cmp_w1, nsa_cmp_w2, nsa_w_out, ffn_w_gate, ffn_w_up, ffn_w_down, moe_w_router, moe_w_gate, moe_w_up, moe_w_down):
    B, S, D = x.shape
    assert norm_mix.shape[0] == 2 and lru_w_in.shape[0] == 1 and nsa_w_in.shape[0] == 1
    x = _lru_layer(x, norm_mix[0], lru_w_in[0], lru_b_in[0], lru_conv_w[0], lru_conv_b[0], lru_w_a[0],
                   lru_b_a[0], lru_w_i[0], lru_b_i[0], lru_lambda[0], lru_w_out[0], lru_b_out[0])
    x2 = _ffn_layer(x.reshape(B * S, D), norm_ffn[0], ffn_w_gate, ffn_w_up, ffn_w_down)
    x2 = _nsa_layer(x2.reshape(B, S, D), norm_mix[1], nsa_w_in[0], nsa_cmp_pe[0], nsa_cmp_w1[0],
                    nsa_cmp_w2[0], nsa_w_out[0]).reshape(B * S, D)
    out = _moe_layer(x2, norm_ffn[1], moe_w_router[0], moe_w_gate[0], moe_w_up[0], moe_w_down[0], norm_final)
    return out.reshape(B, S, D)
```

```python
import functools

import numpy as np
import jax
import jax.numpy as jnp
from jax import lax
from jax.experimental import pallas as pl
from jax.experimental.pallas import tpu as pltpu

F32 = jnp.float32
BF16 = jnp.bfloat16

RMS_EPS = 1e-6
LRU_C = 8.0
CONV_WIDTH = 4
N_HEADS = 16
HEAD_DIM = 64
N_KV_GROUPS = 4
HEADS_PER_GROUP = N_HEADS // N_KV_GROUPS
CMP_BLOCK = 32
CMP_STRIDE = 16
SEL_BLOCK = 64
N_SEL = 16
WINDOW = 512
FORCE_SCORE = 1.0e4
ROPE_THETA = 10000.0
TOP_K = 2

LANES = 128
SUBLANES = 8
MXU_WIDTH = 256
MASK_VALUE = -1.0e30
VMEM_LIMIT_BYTES = 56 * 1024 * 1024

_NT = (((1,), (1,)), ((), ()))
Q_SCALE = HEAD_DIM ** -0.5 * 1.4426950408889634


def _params(*semantics):
    return pltpu.CompilerParams(dimension_semantics=semantics, vmem_limit_bytes=VMEM_LIMIT_BYTES)


def _resident(shape):
    zeros = (0,) * len(shape)
    return pl.BlockSpec(shape, lambda *_: zeros, pipeline_mode=pl.Buffered(1))


def _rms(x, g):
    return x * lax.rsqrt(jnp.mean(x * x, axis=-1, keepdims=True) + RMS_EPS) * g


def _dot(a, b):
    return jnp.dot(a, b, preferred_element_type=F32)


def _gelu_tanh(x):
    two_z = x * (2.0 * 0.7978845608028654 + (2.0 * 0.7978845608028654 * 0.044715) * (x * x))
    return x * jax.nn.sigmoid(two_z)


def _block_diag_dot(xb, w_ref, col0, n, bw):
    outs = []
    for c0 in range(0, n, MXU_WIDTH):
        w = min(MXU_WIDTH, n - c0)
        lo = (c0 // bw) * bw // LANES * LANES
        hi = min(n, -(-(((c0 + w - 1) // bw + 1) * bw) // LANES) * LANES)
        outs.append(_dot(xb[:, lo:hi], w_ref[lo:hi, col0 + c0:col0 + c0 + w]))
    return jnp.concatenate(outs, axis=1)


def _lane_col(vals, lane, idx):
    return jnp.sum(jnp.where(lane == idx, vals, 0.0), axis=1, keepdims=True)


def _lru_body(x_ref, gn_ref, win_ref, bin_ref, cw_ref, cb_ref, wg_ref, bg_ref, lam_ref,
              wout_ref, bout_ref, o_ref, xbuf, hcar, *, ts, dr, bw):
    t = pl.program_id(1)

    @pl.when(t == 0)
    def _():
        xbuf[0:8, :] = jnp.zeros((8, dr), F32)
        hcar[...] = jnp.zeros_like(hcar)

    x = x_ref[0]
    hn = _rms(x, gn_ref[...]).astype(BF16)
    proj = _dot(hn, win_ref[...]) + bin_ref[...]
    gate = _gelu_tanh(proj[:, :dr])
    xr = proj[:, dr:]

    xbuf[8:8 + ts, :] = xr
    xc = cb_ref[...] + xr * cw_ref[CONV_WIDTH - 1:CONV_WIDTH, :]
    for lag in range(1, CONV_WIDTH):
        k = CONV_WIDTH - 1 - lag
        xc = xc + xbuf[pl.ds(8 - lag, ts), :] * cw_ref[k:k + 1, :]
    xbuf[0:8, :] = xbuf[ts:ts + 8, :]

    xcb = xc.astype(BF16)
    r = jax.nn.sigmoid(_block_diag_dot(xcb, wg_ref, 0, dr, bw) + bg_ref[:, :dr])
    i = jax.nn.sigmoid(_block_diag_dot(xcb, wg_ref, dr, dr, bw) + bg_ref[:, dr:])
    z = -lam_ref[...]
    softplus = jnp.maximum(z, 0.0) + jnp.log(1.0 + jnp.exp(-jnp.abs(z)))
    log_a = (-LRU_C * r) * softplus
    a = jnp.exp(log_a)
    mult = jnp.sqrt(1.0 - a * a)
    row = lax.broadcasted_iota(jnp.int32, (ts, 1), 0)
    mult = jnp.where((row == 0) & (t == 0), 1.0, mult)
    u = mult * (i * xc)

    sub = row & (SUBLANES - 1)
    for shift in (1, 2, 4):
        keep = sub >= shift
        a_prev = jnp.where(keep, pltpu.roll(a, shift, 0), 1.0)
        u_prev = jnp.where(keep, pltpu.roll(u, shift, 0), 0.0)
        u = a * u_prev + u
        a = a * a_prev
    carry = hcar[...]
    groups = []
    for k in range(ts // SUBLANES):
        rows = slice(k * SUBLANES, (k + 1) * SUBLANES)
        hk = u[rows] + a[rows] * carry
        carry = hk[SUBLANES - 1:SUBLANES]
        groups.append(hk)
    h = jnp.concatenate(groups, axis=0)
    hcar[...] = carry

    y = (h * gate).astype(BF16)
    o_ref[0] = x + _dot(y, wout_ref[...]) + bout_ref[...]


def _lru_layer(x, gn, w_in, b_in, conv_w, conv_b, w_a, b_a, w_i, b_i, lam, w_out, b_out):
    B, S, D = x.shape
    dr = w_out.shape[0]
    ts = min(256, S)
    assert S % ts == 0 and ts % 8 == 0
    wg = jnp.concatenate([jax.scipy.linalg.block_diag(*w_a), jax.scipy.linalg.block_diag(*w_i)], axis=1)
    row = lambda v: v.reshape(1, -1)
    body = functools.partial(_lru_body, ts=ts, dr=dr, bw=w_a.shape[1])
    return pl.pallas_call(
        body,
        grid=(B, S // ts),
        in_specs=[
            pl.BlockSpec((1, ts, D), lambda b, t: (b, t, 0)),
            _resident((1, D)),
            _resident((D, 2 * dr)), _resident((1, 2 * dr)),
            _resident((CONV_WIDTH, dr)), _resident((1, dr)),
            _resident((dr, 2 * dr)), _resident((1, 2 * dr)),
            _resident((1, dr)),
            _resident((dr, D)), _resident((1, D)),
        ],
        out_specs=pl.BlockSpec((1, ts, D), lambda b, t: (b, t, 0)),
        out_shape=jax.ShapeDtypeStruct((B, S, D), F32),
        scratch_shapes=[pltpu.VMEM((ts + 8, dr), F32), pltpu.VMEM((1, dr), F32)],
        compiler_params=_params("arbitrary", "arbitrary"),
        name="lru_mixer",
    )(x, row(gn), w_in.astype(BF16), row(b_in), conv_w, row(conv_b), wg.astype(BF16),
      row(jnp.concatenate([b_a, b_i])), row(lam), w_out.astype(BF16), row(b_out))


def _ffn_body(*refs, n_e, n_f, use_gates, final_norm):
    x_ref, gn_ref = refs[0], refs[1]
    k = 2
    gates_ref = gfin_ref = None
    if use_gates:
        gates_ref = refs[k]; k += 1
    wg_ref, wu_ref, wd_ref = refs[k:k + 3]; k += 3
    if final_norm:
        gfin_ref = refs[k]; k += 1
    o_ref, hn_ref, acc_ref = refs[k:k + 3]
    e = pl.program_id(1)
    f = pl.program_id(2)

    @pl.when((e == 0) & (f == 0))
    def _():
        hn_ref[...] = _rms(x_ref[...], gn_ref[...]).astype(BF16)
        acc_ref[...] = jnp.zeros_like(acc_ref)

    hn = hn_ref[...]
    g = _dot(hn, wg_ref[0])
    u = _dot(hn, wu_ref[0])
    act = (g * jax.nn.sigmoid(g)) * u
    if use_gates:
        gates = gates_ref[...]
        lane = lax.broadcasted_iota(jnp.int32, gates.shape, 1)
        act = act * _lane_col(gates, lane, e)
    acc_ref[...] += _dot(act.astype(BF16), wd_ref[0])

    @pl.when((e == n_e - 1) & (f == n_f - 1))
    def _():
        out = x_ref[...] + acc_ref[...]
        if final_norm:
            out = _rms(out, gfin_ref[...])
        o_ref[...] = out


def _ffn_layer(x2, gn, w_gate, w_up, w_down, gates=None, g_final=None, tm=1024, tf=512):
    M, D = x2.shape
    n_e, _, F = w_gate.shape
    tm = min(tm, M)
    tf = min(tf, F)
    assert M % tm == 0 and F % tf == 0
    n_f = F // tf
    use_gates = gates is not None
    final_norm = g_final is not None
    in_specs = [pl.BlockSpec((tm, D), lambda i, e, f: (i, 0)), _resident((1, D))]
    args = [x2, gn.reshape(1, D)]
    if use_gates:
        in_specs.append(pl.BlockSpec((tm, LANES), lambda i, e, f: (i, 0)))
        args.append(gates)
    in_specs += [
        pl.BlockSpec((1, D, tf), lambda i, e, f: (e, 0, f)),
        pl.BlockSpec((1, D, tf), lambda i, e, f: (e, 0, f)),
        pl.BlockSpec((1, tf, D), lambda i, e, f: (e, f, 0)),
    ]
    args += [w_gate.astype(BF16), w_up.astype(BF16), w_down.astype(BF16)]
    if final_norm:
        in_specs.append(_resident((1, D)))
        args.append(g_final.reshape(1, D))
    body = functools.partial(_ffn_body, n_e=n_e, n_f=n_f, use_gates=use_gates, final_norm=final_norm)
    return pl.pallas_call(
        body,
        grid=(M // tm, n_e, n_f),
        in_specs=in_specs,
        out_specs=pl.BlockSpec((tm, D), lambda i, e, f: (i, 0)),
        out_shape=jax.ShapeDtypeStruct((M, D), F32),
        scratch_shapes=[pltpu.VMEM((tm, D), BF16), pltpu.VMEM((tm, D), F32)],
        compiler_params=_params("arbitrary", "arbitrary", "arbitrary"),
        name="moe_swiglu" if use_gates else "dense_swiglu",
    )(*args)


def _router_body(x_ref, gn_ref, wh_ref, wl_ref, tril_ref, o_ref, cnt_ref, *, n_experts):
    @pl.when(pl.program_id(0) == 0)
    def _():
        cnt_ref[...] = jnp.zeros_like(cnt_ref)

    hn = _rms(x_ref[...], gn_ref[...])
    hh = hn.astype(BF16)
    hl = (hn - hh.astype(F32)).astype(BF16)
    wh = wh_ref[...]
    logits = _dot(hh, wh) + _dot(hh, wl_ref[...]) + _dot(hl, wh)
    lane = lax.broadcasted_iota(jnp.int32, logits.shape, 1)
    lg = jnp.where(lane < n_experts, logits, -jnp.inf)
    m0 = jnp.max(lg, axis=1, keepdims=True)
    i0 = jnp.min(jnp.where(lg == m0, lane, LANES), axis=1, keepdims=True)
    lg = jnp.where(lane == i0, -jnp.inf, lg)
    m1 = jnp.max(lg, axis=1, keepdims=True)
    i1 = jnp.min(jnp.where(lg == m1, lane, LANES), axis=1, keepdims=True)
    e1 = jnp.exp(m1 - m0)
    w0 = 1.0 / (1.0 + e1)
    routed = ((lane == i0) | (lane == i1)).astype(F32)
    incl = _dot(tril_ref[...], routed.astype(BF16))
    excl = incl - routed + cnt_ref[0:1, :]

    def put(col, v):
        return jnp.where(lane == col, v, 0.0)

    o_ref[...] = (put(META_I0, i0.astype(F32)) + put(META_I1, i1.astype(F32)) + put(META_W0, w0)
                  + put(META_W1, e1 * w0) + put(META_R0, _lane_col(excl, lane, i0))
                  + put(META_R1, _lane_col(excl, lane, i1)))
    cnt_ref[...] = jnp.broadcast_to(cnt_ref[0:1, :] + incl[incl.shape[0] - 1:, :], cnt_ref.shape)


META_I0, META_I1, META_W0, META_W1, META_R0, META_R1 = range(6)


def _router(x2, gn, w_router, tm=512):
    M, D = x2.shape
    n_experts = w_router.shape[1]
    tm = min(tm, M)
    wpad = jnp.pad(w_router, ((0, 0), (0, LANES - n_experts)))
    wh = wpad.astype(BF16)
    wl = (wpad - wh.astype(F32)).astype(BF16)
    tril = jnp.tril(jnp.ones((tm, tm), BF16))
    return pl.pallas_call(
        functools.partial(_router_body, n_experts=n_experts),
        grid=(M // tm,),
        in_specs=[pl.BlockSpec((tm, D), lambda i: (i, 0)), _resident((1, D)),
                  _resident((D, LANES)), _resident((D, LANES)), _resident((tm, tm))],
        out_specs=[pl.BlockSpec((tm, LANES), lambda i: (i, 0)), pl.BlockSpec((8, LANES), lambda i: (0, 0))],
        out_shape=[jax.ShapeDtypeStruct((M, LANES), F32), jax.ShapeDtypeStruct((8, LANES), F32)],
        compiler_params=_params("arbitrary"),
        name="moe_router",
    )(x2, gn.reshape(1, D), wh, wl, tril)


def _combine_body(pos_ref, x_ref, meta_ref, y_ref, gfin_ref, o_ref, ybuf, sem, *, tm):
    def row_copy(k, r):
        return pltpu.make_async_copy(y_ref.at[pl.ds(pos_ref[0, 0, k * tm + r], 1)], ybuf.at[k, pl.ds(r, 1)], sem)

    for r in range(tm):
        row_copy(0, r).start()
        row_copy(1, r).start()
    pltpu.make_async_copy(ybuf, ybuf, sem).wait()
    meta = meta_ref[...]
    lane = lax.broadcasted_iota(jnp.int32, meta.shape, 1)
    out = (x_ref[...] + _lane_col(meta, lane, META_W0) * ybuf[0] + _lane_col(meta, lane, META_W1) * ybuf[1])
    o_ref[...] = _rms(out, gfin_ref[...])


def _combine(x2, meta, y, pos0, pos1, g_final, tm=512):
    M, D = x2.shape
    tm = min(tm, M)
    pos = jnp.concatenate([pos0.reshape(M // tm, 1, tm), pos1.reshape(M // tm, 1, tm)], axis=2)
    return pl.pallas_call(
        functools.partial(_combine_body, tm=tm),
        grid=(M // tm,),
        in_specs=[pl.BlockSpec((1, 1, 2 * tm), lambda i: (i, 0, 0), memory_space=pltpu.SMEM),
                  pl.BlockSpec((tm, D), lambda i: (i, 0)),
                  pl.BlockSpec((tm, LANES), lambda i: (i, 0)),
                  pl.BlockSpec(memory_space=pl.ANY),
                  _resident((1, D))],
        out_specs=pl.BlockSpec((tm, D), lambda i: (i, 0)),
        out_shape=jax.ShapeDtypeStruct((M, D), F32),
        scratch_shapes=[pltpu.VMEM((2, tm, D), F32), pltpu.SemaphoreType.DMA(())],
        compiler_params=_params("arbitrary"),
        name="moe_combine",
    )(pos, x2, meta, y, g_final.reshape(1, D))


def _grouped_ffn_body(te_ref, nused_ref, src_ref, src_next_ref, x_ref, gn_ref, wg_ref, wu_ref, wd_ref, o_ref,
                      xbuf, hn_ref, acc_ref, sem, *, tm, n_f, n_tiles):
    j = pl.program_id(0)
    f = pl.program_id(1)
    n_used = nused_ref[0]
    used = j < n_used
    gathered = (j == 0) | (j <= n_used)
    slot = j % 2
    chunk = tm // (n_f + 1)

    def row_copy(idx_ref, r, s):
        return pltpu.make_async_copy(x_ref.at[pl.ds(idx_ref[0, 0, r], 1)], xbuf.at[s, pl.ds(r, 1)], sem.at[s])

    def start_chunk(first):
        for k in range(chunk):
            row_copy(src_next_ref, first + k, 1 - slot).start()

    @pl.when((j == 0) & (f == 0))
    def _():
        def body(r, c):
            row_copy(src_ref, r, 0).start()
            return c
        lax.fori_loop(0, tm, body, 0, unroll=8)

    @pl.when(f == 0)
    def _():
        acc_ref[...] = jnp.zeros_like(acc_ref)

        @pl.when(gathered)
        def _():
            pltpu.make_async_copy(xbuf.at[slot], xbuf.at[slot], sem.at[slot]).wait()
            hn_ref[...] = _rms(xbuf[slot], gn_ref[...]).astype(BF16)

        @pl.when(used)
        def _():
            start_chunk(0)

    @pl.when(used)
    def _():
        hn = hn_ref[...]
        g = _dot(hn, wg_ref[0])
        u = _dot(hn, wu_ref[0])
        acc_ref[...] += _dot(((g * jax.nn.sigmoid(g)) * u).astype(BF16), wd_ref[0])
        start_chunk((f + 1) * chunk)

    @pl.when(f == n_f - 1)
    def _():
        o_ref[...] = acc_ref[...]


def _grouped_ffn(x2, src, gn, w_gate, w_up, w_down, tile_expert, n_used, tm, tf=512):
    P = src.shape[0]
    D = x2.shape[1]
    F = w_gate.shape[2]
    tf = min(tf, F)
    n_f = F // tf
    n_tiles = P // tm
    assert F % tf == 0 and tm % (n_f + 1) == 0
    src3 = src.reshape(n_tiles, 1, tm)
    grid_spec = pltpu.PrefetchScalarGridSpec(
        num_scalar_prefetch=2,
        grid=(n_tiles, n_f),
        in_specs=[pl.BlockSpec((1, 1, tm), lambda j, f, te, nu: (j, 0, 0), memory_space=pltpu.SMEM),
                  pl.BlockSpec((1, 1, tm), lambda j, f, te, nu: (jnp.minimum(j + 1, n_tiles - 1), 0, 0),
                               memory_space=pltpu.SMEM),
                  pl.BlockSpec(memory_space=pl.ANY),
                  pl.BlockSpec((1, D), lambda j, f, te, nu: (0, 0)),
                  pl.BlockSpec((1, D, tf), lambda j, f, te, nu: (te[j], 0, f)),
                  pl.BlockSpec((1, D, tf), lambda j, f, te, nu: (te[j], 0, f)),
                  pl.BlockSpec((1, tf, D), lambda j, f, te, nu: (te[j], f, 0))],
        out_specs=pl.BlockSpec((tm, D), lambda j, f, te, nu: (j, 0)),
        scratch_shapes=[pltpu.VMEM((2, tm, D), F32), pltpu.VMEM((tm, D), BF16), pltpu.VMEM((tm, D), F32),
                        pltpu.SemaphoreType.DMA((2,))])
    return pl.pallas_call(
        functools.partial(_grouped_ffn_body, tm=tm, n_f=n_f, n_tiles=n_tiles),
        grid_spec=grid_spec,
        out_shape=jax.ShapeDtypeStruct((P, D), F32),
        compiler_params=_params("arbitrary", "arbitrary"),
        name="moe_grouped_swiglu",
    )(tile_expert, n_used, src3, src3, x2, gn.reshape(1, D), w_gate.astype(BF16), w_up.astype(BF16),
      w_down.astype(BF16))


def _moe_layer(x2, gn, w_router, w_gate, w_up, w_down, g_final, tm=1024):
    M, D = x2.shape
    n_e = w_router.shape[1]
    tm = min(tm, M)
    meta, counts = _router(x2, gn, w_router)
    as_int = lambda col: meta[:, col].astype(jnp.int32)
    i0, i1, r0, r1 = as_int(META_I0), as_int(META_I1), as_int(META_R0), as_int(META_R1)
    padded = (counts[0, :n_e].astype(jnp.int32) + tm - 1) // tm * tm
    ends = jnp.cumsum(padded)
    offsets = ends - padded
    pos0 = offsets[i0] + r0
    pos1 = offsets[i1] + r1
    P = TOP_K * M + n_e * tm
    tok = jnp.arange(M, dtype=jnp.int32)
    src = jnp.zeros((P,), jnp.int32).at[jnp.concatenate([pos0, pos1])].set(jnp.concatenate([tok, tok]))
    tile_start = jnp.arange(P // tm, dtype=jnp.int32) * tm
    tile_expert = jnp.minimum(jnp.searchsorted(ends, tile_start, side="right"), n_e - 1).astype(jnp.int32)
    n_used = (ends[n_e - 1:] // tm).astype(jnp.int32)
    y = _grouped_ffn(x2, src, gn, w_gate, w_up, w_down, tile_expert, n_used, tm)
    return _combine(x2, meta, y, pos0, pos1, g_final)


def _nsa_proj_body(x_ref, gn_ref, w_ref, cos_ref, sin_ref,
                   q_ref, kc_ref, vc_ref, ks_ref, vs_ref, kw_ref, vw_ref, g_ref, *, tm, tiles_per_seq):
    i = pl.program_id(0)
    hn = _rms(x_ref[...], gn_ref[...]).astype(BF16)
    proj = _dot(hn, w_ref[...])
    lane = lax.broadcasted_iota(jnp.int32, (tm, LANES), 1)
    first_half = (lane & (HEAD_DIM - 1)) < HEAD_DIM // 2
    cos = cos_ref[...]
    sin = sin_ref[...]

    def chunk(c):
        return proj[:, c * LANES:(c + 1) * LANES]

    def rope(v):
        rot = jnp.where(first_half, pltpu.roll(v, LANES - HEAD_DIM // 2, 1), pltpu.roll(v, HEAD_DIM // 2, 1))
        return v * cos + rot * sin

    def heads(v):
        return v[:, :HEAD_DIM], pltpu.roll(v, HEAD_DIM, 1)[:, :HEAD_DIM]

    c = 0
    for cc in range(N_HEADS // 2):
        lo, hi = heads(rope(chunk(c)) * Q_SCALE); c += 1
        q_ref[2 * cc] = lo.astype(BF16)
        q_ref[2 * cc + 1] = hi.astype(BF16)
    for cc in range(N_KV_GROUPS // 2):
        lo, hi = heads(rope(chunk(c))); c += 1
        kc_ref[2 * cc] = lo
        kc_ref[2 * cc + 1] = hi
    for cc in range(N_KV_GROUPS // 2):
        lo, hi = heads(chunk(c)); c += 1
        vc_ref[2 * cc] = lo
        vc_ref[2 * cc + 1] = hi
    pos = (i % tiles_per_seq) * tm + lax.broadcasted_iota(jnp.int32, (tm, 1), 0)
    onehot = (lane == pos // SEL_BLOCK).astype(BF16)
    for cc in range(N_KV_GROUPS // 2):
        v = rope(chunk(c)); c += 1
        for j, vv in enumerate((v, pltpu.roll(v, HEAD_DIM, 1))):
            ks_ref[2 * cc + j, :, 0:LANES] = jnp.where(lane < HEAD_DIM, vv, 0.0).astype(BF16)
            ks_ref[2 * cc + j, :, LANES:2 * LANES] = onehot
    ones_col = jnp.where(lane == HEAD_DIM, 1.0, 0.0)

    def store_values(ref):
        nonlocal c
        for cc in range(N_KV_GROUPS // 2):
            v = chunk(c); c += 1
            for j, vv in enumerate((v, pltpu.roll(v, HEAD_DIM, 1))):
                ref[2 * cc + j] = jnp.where(lane < HEAD_DIM, vv, ones_col).astype(BF16)

    store_values(vs_ref)
    for cc in range(N_KV_GROUPS // 2):
        lo, hi = heads(rope(chunk(c))); c += 1
        kw_ref[2 * cc] = lo.astype(BF16)
        kw_ref[2 * cc + 1] = hi.astype(BF16)
    store_values(vw_ref)
    g_ref[...] = jax.nn.sigmoid(chunk(c))


def _nsa_proj(x2, gn, w_in, S, tm=256):
    M, D = x2.shape
    H, G, DH = N_HEADS, N_KV_GROUPS, HEAD_DIM
    tm = min(tm, S)
    assert S % tm == 0
    n_in = w_in.shape[1]
    n_pad = -(-n_in // LANES) * LANES
    wp = jnp.pad(w_in, ((0, 0), (0, n_pad - n_in))).astype(BF16)
    half = DH // 2
    freqs = ROPE_THETA ** (-jnp.arange(half, dtype=F32) / half)
    ang = jnp.arange(S, dtype=F32)[:, None] * freqs[None, :]
    cos = jnp.tile(jnp.cos(ang), (1, 2 * LANES // DH))
    sin = jnp.tile(jnp.concatenate([-jnp.sin(ang), jnp.sin(ang)], axis=1), (1, LANES // DH))
    tiles_per_seq = S // tm
    hd = lambda n, dt: jax.ShapeDtypeStruct((n, M, DH), dt)
    hspec = lambda n: pl.BlockSpec((n, tm, DH), lambda i: (0, i, 0))
    vd = jax.ShapeDtypeStruct((G, M, LANES), BF16)
    vspec = pl.BlockSpec((G, tm, LANES), lambda i: (0, i, 0))
    return pl.pallas_call(
        functools.partial(_nsa_proj_body, tm=tm, tiles_per_seq=tiles_per_seq),
        grid=(M // tm,),
        in_specs=[pl.BlockSpec((tm, D), lambda i: (i, 0)), _resident((1, D)), _resident((D, n_pad)),
                  pl.BlockSpec((tm, LANES), lambda i: (i % tiles_per_seq, 0)),
                  pl.BlockSpec((tm, LANES), lambda i: (i % tiles_per_seq, 0))],
        out_specs=[hspec(H), hspec(G), hspec(G),
                   pl.BlockSpec((G, tm, 2 * LANES), lambda i: (0, i, 0)),
                   vspec, hspec(G), vspec,
                   pl.BlockSpec((tm, LANES), lambda i: (i, 0))],
        out_shape=[hd(H, BF16), hd(G, F32), hd(G, F32),
                   jax.ShapeDtypeStruct((G, M, 2 * LANES), BF16),
                   vd, hd(G, BF16), vd,
                   jax.ShapeDtypeStruct((M, LANES), F32)],
        compiler_params=_params("arbitrary"),
        name="nsa_proj",
    )(x2, gn.reshape(1, D), wp, cos, sin)


def _compress_body(kc_ref, vc_ref, pe_ref, w1_ref, w2_ref, ko_ref, vo_ref, *, nc):
    half = CMP_BLOCK // 2
    for kv, (src, dst) in enumerate(((kc_ref, ko_ref), (vc_ref, vo_ref))):
        top = jnp.zeros((nc, w1_ref.shape[2]), F32)
        bot = jnp.zeros((nc, w1_ref.shape[2]), F32)
        for j in range(half):
            xj = src[0, pl.ds(j, nc, stride=CMP_STRIDE), :]
            top = top + _dot((xj + pe_ref[kv, j:j + 1, :]).astype(BF16),
                             w1_ref[kv, j * HEAD_DIM:(j + 1) * HEAD_DIM, :])
            bot = bot + _dot((xj + pe_ref[kv, half + j:half + j + 1, :]).astype(BF16),
                             w1_ref[kv, (half + j) * HEAD_DIM:(half + j + 1) * HEAD_DIM, :])
        hid = top + pltpu.roll(bot, nc - 1, 0)
        dst[0] = _dot(_gelu_tanh(hid).astype(BF16), w2_ref[kv]).astype(BF16)


def _compress(kc, vc, pe, w1, w2, B, S):
    assert CMP_BLOCK == 2 * CMP_STRIDE
    G, M, DH = kc.shape
    nc = S // CMP_STRIDE
    spec_in = pl.BlockSpec((1, S, DH), lambda b, g: (g, b, 0))
    spec_out = pl.BlockSpec((1, nc, DH), lambda b, g: (g, b, 0))
    out = jax.ShapeDtypeStruct((G, B * nc, DH), BF16)
    return pl.pallas_call(
        functools.partial(_compress_body, nc=nc),
        grid=(B, G),
        in_specs=[spec_in, spec_in, _resident(pe.shape), _resident(w1.shape), _resident(w2.shape)],
        out_specs=[spec_out, spec_out],
        out_shape=[out, out],
        compiler_params=_params("arbitrary", "arbitrary"),
        name="nsa_compress",
    )(kc, vc, pe, w1.astype(BF16), w2.astype(BF16))


def _store_head(o_ref, oh, gates, g, h, branch, rows=slice(None)):
    lane = lax.broadcasted_iota(jnp.int32, gates.shape, 1)
    col = 3 * (HEADS_PER_GROUP * g + h) + branch
    o_ref[rows, h * HEAD_DIM:(h + 1) * HEAD_DIM] = (oh * _lane_col(gates, lane, col)).astype(o_ref.dtype)


def _normalized_head(acc):
    return acc[:, :HEAD_DIM] * (1.0 / acc[:, HEAD_DIM:HEAD_DIM + 1])


def _cmp_body(q_ref, kc_ref, vc_ref, gates_ref, ov_ref, o_ref, bias_ref, imp_ref, *, tq, nc, n_s, k_sel):
    g = pl.program_id(1)
    i = pl.program_id(2)
    gates = gates_ref[...]
    t = i * tq + lax.broadcasted_iota(jnp.int32, (tq, 1), 0)

    def attend(w):
        kc = kc_ref[0, 0:w, :]
        vc = vc_ref[0, 0:w, :]
        cend = lax.broadcasted_iota(jnp.int32, (1, w), 1) * CMP_STRIDE + (CMP_BLOCK - 1)
        mask = cend <= t
        ps = None
        for h in range(HEADS_PER_GROUP):
            s = lax.dot_general(q_ref[h], kc, _NT, preferred_element_type=F32)
            s = jnp.where(mask, s, MASK_VALUE)
            m = jnp.max(s, axis=1, keepdims=True)
            m = jnp.where(m > 0.5 * MASK_VALUE, m, 0.0)
            e = jnp.exp2(s - m)
            p = e * (1.0 / jnp.maximum(jnp.sum(e, axis=1, keepdims=True), 1e-30))
            _store_head(o_ref, _dot(p.astype(BF16), vc), gates, g, h, 0)
            ps = p if ps is None else ps + p
        ph = ps.astype(BF16)
        pl_ = (ps - ph.astype(F32)).astype(BF16)
        imp_ref[...] = _dot(ph, ov_ref[0:w, :]) + _dot(pl_, ov_ref[0:w, :])

    n_chunks = nc // LANES
    need = ((i + 1) * (tq // CMP_STRIDE) + LANES - 1) // LANES
    for k in range(1, n_chunks + 1):
        pl.when((need == k) if k < n_chunks else (need >= k))(functools.partial(attend, k * LANES))

    j = lax.broadcasted_iota(jnp.int32, (tq, LANES), 1)
    cur = t // SEL_BLOCK
    forced = (j == 0) | (j == cur) | (j == cur - 1)
    valid = j * SEL_BLOCK <= t
    score = jnp.where(forced, FORCE_SCORE, jnp.where(valid, imp_ref[...], -1.0))
    score = jnp.where(j < n_s, score, -jnp.inf)
    x = score.T
    blk_i = lax.broadcasted_iota(jnp.int32, (LANES, tq), 0)
    blk = blk_i.astype(F32)

    def topk(x, sel, n_iter):
        for _ in range(n_iter):
            mx = jnp.max(x, axis=0, keepdims=True)
            idx = jnp.min(jnp.where(x == mx, blk, float(LANES)), axis=0, keepdims=True)
            hit = blk == idx
            sel = jnp.where(hit, 1.0, sel)
            x = jnp.where(hit, -jnp.inf, x)
        bias_ref[0] = jnp.where(sel.T > 0.5, 0.0, MASK_VALUE).astype(BF16)

    n_forced = 3
    direct = (i * tq >= 2 * SEL_BLOCK) & (k_sel > n_forced)

    @pl.when(direct)
    def _():
        cur_t = (i * tq + lax.broadcasted_iota(jnp.int32, (1, tq), 1)) // SEL_BLOCK
        forced_t = (blk_i == 0) | (blk_i == cur_t) | (blk_i == cur_t - 1)
        topk(jnp.where(forced_t, -jnp.inf, x), forced_t.astype(F32), k_sel - n_forced)

    @pl.when(jnp.logical_not(direct))
    def _():
        topk(x, jnp.zeros((LANES, tq), F32), k_sel)


def _overlap_matrix(nc, n_s):
    r = CMP_BLOCK // CMP_STRIDE
    qn = SEL_BLOCK // CMP_STRIDE
    m = np.zeros((nc, LANES), np.float32)
    n_c = nc - r + 1
    chunks = np.arange(n_c)[:, None] + np.arange(r)[None, :]
    np.add.at(m, (np.repeat(np.arange(n_c), r), (chunks // qn).ravel()), 1.0)
    return m


def _cmp_attention(q, kcmp, vcmp, gates, B, S, tq=256):
    H, M, DH = q.shape
    G = N_KV_GROUPS
    tq = min(tq, S)
    nq = S // tq
    nc = S // CMP_STRIDE
    n_s = S // SEL_BLOCK
    assert n_s <= LANES and tq & (tq - 1) == 0
    k_sel = min(N_SEL, n_s)
    ov = jnp.asarray(_overlap_matrix(nc, n_s), BF16)
    return pl.pallas_call(
        functools.partial(_cmp_body, tq=tq, nc=nc, n_s=n_s, k_sel=k_sel),
        grid=(B, G, nq),
        in_specs=[pl.BlockSpec((HEADS_PER_GROUP, tq, DH), lambda b, g, i: (g, b * nq + i, 0)),
                  pl.BlockSpec((1, nc, DH), lambda b, g, i: (g, b, 0)),
                  pl.BlockSpec((1, nc, DH), lambda b, g, i: (g, b, 0)),
                  pl.BlockSpec((tq, LANES), lambda b, g, i: (b * nq + i, 0)),
                  _resident((nc, LANES))],
        out_specs=[pl.BlockSpec((tq, HEADS_PER_GROUP * DH), lambda b, g, i: (b * nq + i, g)),
                   pl.BlockSpec((1, tq, LANES), lambda b, g, i: (g, b * nq + i, 0))],
        out_shape=[jax.ShapeDtypeStruct((M, H * DH), BF16), jax.ShapeDtypeStruct((G, M, LANES), BF16)],
        scratch_shapes=[pltpu.VMEM((tq, LANES), F32)],
        compiler_params=_params("arbitrary", "arbitrary", "arbitrary"),
        name="nsa_cmp_select",
    )(q, kcmp, vcmp, gates, ov)


def _sel_body(q_ref, bias_ref, k_ref, v_ref, gates_ref, o_ref, qcat, m_ref, acc_ref, *, tq):
    g = pl.program_id(1)
    i = pl.program_id(2)
    hp = HEADS_PER_GROUP
    rows = hp * tq
    qcat[:, 0:LANES] = jnp.zeros((rows, LANES), BF16)
    qcat[:, 0:HEAD_DIM] = q_ref[...].reshape(rows, HEAD_DIM)
    bias = bias_ref[0]
    for h in range(hp):
        qcat[h * tq:(h + 1) * tq, LANES:2 * LANES] = bias
    m_ref[...] = jnp.full((rows, LANES), MASK_VALUE, F32)
    acc_ref[...] = jnp.zeros((rows, LANES), F32)

    def tile(j, width, causal):
        start = pl.multiple_of(j * tq, tq)
        kt = k_ref[0, pl.ds(start, width), :]
        vt = v_ref[0, pl.ds(start, width), :]
        for h in range(hp):
            r0 = h * tq
            s = lax.dot_general(qcat[r0:r0 + tq, :], kt, _NT, preferred_element_type=F32)
            if causal:
                r = lax.broadcasted_iota(jnp.int32, (tq, 1), 0)
                c = lax.broadcasted_iota(jnp.int32, (1, width), 1)
                s = jnp.where(c <= r, s, MASK_VALUE)
            m_prev = m_ref[r0:r0 + tq, :]
            m_new = jnp.maximum(m_prev, jnp.max(s, axis=1, keepdims=True))
            alpha = jnp.exp2(m_prev - m_new)
            p = jnp.exp2(s - jnp.concatenate([m_new] * (width // LANES), axis=1))
            acc_ref[r0:r0 + tq, :] = alpha * acc_ref[r0:r0 + tq, :] + _dot(p.astype(BF16), vt)
            m_ref[r0:r0 + tq, :] = m_new

    def tile_pair(jp, carry):
        tile(2 * jp, 2 * tq, False)
        return carry

    lax.fori_loop(0, i // 2, tile_pair, 0)

    @pl.when(i % 2 == 1)
    def _():
        tile(i - 1, tq, False)

    tile(i, tq, True)
    gates = gates_ref[...]
    for h in range(hp):
        _store_head(o_ref, _normalized_head(acc_ref[h * tq:(h + 1) * tq, :]), gates, g, h, 1)


def _sel_attention(q, bias, kscat, vs, gates, B, S, tq=512):
    H, M, DH = q.shape
    G = N_KV_GROUPS
    tq = min(tq, S)
    nq = S // tq
    assert tq % LANES == 0 and tq % SEL_BLOCK == 0 and S % tq == 0
    rows = HEADS_PER_GROUP * tq
    return pl.pallas_call(
        functools.partial(_sel_body, tq=tq),
        grid=(B, G, nq),
        in_specs=[pl.BlockSpec((HEADS_PER_GROUP, tq, DH), lambda b, g, i: (g, b * nq + i, 0)),
                  pl.BlockSpec((1, tq, LANES), lambda b, g, i: (g, b * nq + i, 0)),
                  pl.BlockSpec((1, S, 2 * LANES), lambda b, g, i: (g, b, 0)),
                  pl.BlockSpec((1, S, LANES), lambda b, g, i: (g, b, 0)),
                  pl.BlockSpec((tq, LANES), lambda b, g, i: (b * nq + i, 0))],
        out_specs=pl.BlockSpec((tq, HEADS_PER_GROUP * DH), lambda b, g, i: (b * nq + i, g)),
        out_shape=jax.ShapeDtypeStruct((M, H * DH), BF16),
        scratch_shapes=[pltpu.VMEM((rows, 2 * LANES), BF16), pltpu.VMEM((rows, LANES), F32),
                        pltpu.VMEM((rows, LANES), F32)],
        compiler_params=_params("arbitrary", "arbitrary", "arbitrary"),
        name="nsa_selected",
    )(q, bias, kscat, vs, gates)


def _win_body(q_ref, k_ref, v_ref, gates_ref, o_ref, *, tq, n_back, n_sub):
    g = pl.program_id(1)
    r = lax.broadcasted_iota(jnp.int32, (tq, 1), 0)
    c = lax.broadcasted_iota(jnp.int32, (1, tq), 1)
    for sub in range(n_sub):
        i = pl.program_id(2) * n_sub + sub
        rows = slice(sub * tq, (sub + 1) * tq)
        gates = gates_ref[rows, :]
        tiles = []
        for back in range(n_back, -1, -1):
            jt = i - back
            start = pl.multiple_of(jnp.maximum(jt, 0) * tq, tq)
            d = r - c + back * tq
            ok = (d >= 0) & (d < WINDOW) & (jt >= 0)
            tiles.append((k_ref[0, pl.ds(start, tq), :], v_ref[0, pl.ds(start, tq), :], ok))
        for h in range(HEADS_PER_GROUP):
            q = q_ref[h, rows, :]
            scores = [jnp.where(ok, lax.dot_general(q, kt, _NT, preferred_element_type=F32), MASK_VALUE)
                      for kt, _, ok in tiles]
            m = scores[0].max(axis=1, keepdims=True)
            for s in scores[1:]:
                m = jnp.maximum(m, s.max(axis=1, keepdims=True))
            acc = jnp.zeros((tq, LANES), F32)
            for s, (_, vt, _) in zip(scores, tiles):
                acc = acc + _dot(jnp.exp2(s - m).astype(BF16), vt)
            _store_head(o_ref, _normalized_head(acc), gates, g, h, 2, rows)


def _win_attention(q, kw, vw, gates, B, S, tq=256, n_sub=2):
    H, M, DH = q.shape
    G = N_KV_GROUPS
    tq = min(tq, S)
    n_sub = min(n_sub, S // tq)
    tile = tq
    tq = tile * n_sub
    nq = S // tq
    n_back = -(-WINDOW // tile)
    return pl.pallas_call(
        functools.partial(_win_body, tq=tile, n_back=n_back, n_sub=n_sub),
        grid=(B, G, nq),
        in_specs=[pl.BlockSpec((HEADS_PER_GROUP, tq, DH), lambda b, g, i: (g, b * nq + i, 0)),
                  pl.BlockSpec((1, S, DH), lambda b, g, i: (g, b, 0)),
                  pl.BlockSpec((1, S, LANES), lambda b, g, i: (g, b, 0)),
                  pl.BlockSpec((tq, LANES), lambda b, g, i: (b * nq + i, 0))],
        out_specs=pl.BlockSpec((tq, HEADS_PER_GROUP * DH), lambda b, g, i: (b * nq + i, g)),
        out_shape=jax.ShapeDtypeStruct((M, H * DH), BF16),
        compiler_params=_params("arbitrary", "arbitrary", "arbitrary"),
        name="nsa_window",
    )(q, kw, vw, gates)


def _outproj_body(x_ref, a_ref, b_ref, c_ref, w_ref, o_ref):
    o = a_ref[...].astype(F32) + b_ref[...].astype(F32) + c_ref[...].astype(F32)
    o_ref[...] = x_ref[...] + _dot(o.astype(BF16), w_ref[...])


def _outproj(x2, oc, os_, ow, w_out, tm=512):
    M, D = x2.shape
    K = w_out.shape[0]
    tm = min(tm, M)
    spec_o = pl.BlockSpec((tm, K), lambda i: (i, 0))
    return pl.pallas_call(
        _outproj_body,
        grid=(M // tm,),
        in_specs=[pl.BlockSpec((tm, D), lambda i: (i, 0)), spec_o, spec_o, spec_o, _resident((K, D))],
        out_specs=pl.BlockSpec((tm, D), lambda i: (i, 0)),
        out_shape=jax.ShapeDtypeStruct((M, D), F32),
        compiler_params=_params("arbitrary"),
        name="nsa_outproj",
    )(x2, oc, os_, ow, w_out.astype(BF16))


def _nsa_layer(x, gn, w_in, cmp_pe, cmp_w1, cmp_w2, w_out):
    B, S, D = x.shape
    x2 = x.reshape(B * S, D)
    q, kc, vc, kscat, vs, kw, vw, gates = _nsa_proj(x2, gn, w_in, S)
    kcmp, vcmp = _compress(kc, vc, cmp_pe, cmp_w1, cmp_w2, B, S)
    o_cmp, bias = _cmp_attention(q, kcmp, vcmp, gates, B, S)
    o_sel = _sel_attention(q, bias, kscat, vs, gates, B, S)
    o_win = _win_attention(q, kw, vw, gates, B, S)
    return _outproj(x2, o_cmp, o_sel, o_win, w_out).reshape(B, S, D)


def kernel(x, norm_mix, norm_ffn, norm_final, lru_w_in, lru_b_in, lru_conv_w, lru_conv_b, lru_w_a, lru_b_a, lru_w_i, lru_b_i, lru_lambda, lru_w_out, lru_b_out, nsa_w_in, nsa_cmp_pe, nsa_cmp_w1, nsa_cmp_w2, nsa_w_out, ffn_w_gate, ffn_w_up, ffn_w_down, moe_w_router, moe_w_gate, moe_w_up, moe_w_down):
    B, S, D = x.shape
    assert norm_mix.shape[0] == 2 and lru_w_in.shape[0] == 1 and nsa_w_in.shape[0] == 1
    x = _lru_layer(x, norm_mix[0], lru_w_in[0], lru_b_in[0], lru_conv_w[0], lru_conv_b[0], lru_w_a[0],
                   lru_b_a[0], lru_w_i[0], lru_b_i[0], lru_lambda[0], lru_w_out[0], lru_b_out[0])
    x2 = _ffn_layer(x.reshape(B * S, D), norm_ffn[0], ffn_w_gate, ffn_w_up, ffn_w_down)
    x2 = _nsa_layer(x2.reshape(B, S, D), norm_mix[1], nsa_w_in[0], nsa_cmp_pe[0], nsa_cmp_w1[0],
                    nsa_cmp_w2[0], nsa_w_out[0]).reshape(B * S, D)
    out = _moe_layer(x2, norm_ffn[1], moe_w_router[0], moe_w_gate[0], moe_w_up[0], moe_w_down[0], norm_final)
    return out.reshape(B, S, D)
```

```python
import functools

import numpy as np
import jax
import jax.numpy as jnp
from jax import lax
from jax.experimental import pallas as pl
from jax.experimental.pallas import tpu as pltpu

F32 = jnp.float32
BF16 = jnp.bfloat16

RMS_EPS = 1e-6
LRU_C = 8.0
CONV_WIDTH = 4
N_HEADS = 16
HEAD_DIM = 64
N_KV_GROUPS = 4
HEADS_PER_GROUP = N_HEADS // N_KV_GROUPS
CMP_BLOCK = 32
CMP_STRIDE = 16
SEL_BLOCK = 64
N_SEL = 16
WINDOW = 512
FORCE_SCORE = 1.0e4
ROPE_THETA = 10000.0
TOP_K = 2

LANES = 128
SUBLANES = 8
MXU_WIDTH = 256
MASK_VALUE = -1.0e30
VMEM_LIMIT_BYTES = 56 * 1024 * 1024

_NT = (((1,), (1,)), ((), ()))
Q_SCALE = HEAD_DIM ** -0.5 * 1.4426950408889634
SEL_WIDE = 4


def _params(*semantics):
    return pltpu.CompilerParams(dimension_semantics=semantics, vmem_limit_bytes=VMEM_LIMIT_BYTES)


def _resident(shape):
    zeros = (0,) * len(shape)
    return pl.BlockSpec(shape, lambda *_: zeros, pipeline_mode=pl.Buffered(1))


def _rms(x, g):
    return x * lax.rsqrt(jnp.mean(x * x, axis=-1, keepdims=True) + RMS_EPS) * g


def _dot(a, b):
    return jnp.dot(a, b, preferred_element_type=F32)


def _gelu_tanh(x):
    two_z = x * (2.0 * 0.7978845608028654 + (2.0 * 0.7978845608028654 * 0.044715) * (x * x))
    return x * jax.nn.sigmoid(two_z)


def _block_diag_dot(xb, w_ref, col0, n, bw):
    outs = []
    for c0 in range(0, n, MXU_WIDTH):
        w = min(MXU_WIDTH, n - c0)
        lo = (c0 // bw) * bw // LANES * LANES
        hi = min(n, -(-(((c0 + w - 1) // bw + 1) * bw) // LANES) * LANES)
        outs.append(_dot(xb[:, lo:hi], w_ref[lo:hi, col0 + c0:col0 + c0 + w]))
    return jnp.concatenate(outs, axis=1)


def _lane_col(vals, lane, idx):
    return jnp.sum(jnp.where(lane == idx, vals, 0.0), axis=1, keepdims=True)


def _lru_body(x_ref, gn_ref, win_ref, bin_ref, cw_ref, cb_ref, wg_ref, bg_ref, lam_ref,
              wout_ref, bout_ref, o_ref, xbuf, hcar, *, ts, dr, bw):
    t = pl.program_id(1)

    @pl.when(t == 0)
    def _():
        xbuf[0:8, :] = jnp.zeros((8, dr), F32)
        hcar[...] = jnp.zeros_like(hcar)

    x = x_ref[0]
    hn = _rms(x, gn_ref[...]).astype(BF16)
    proj = _dot(hn, win_ref[...]) + bin_ref[...]
    gate = _gelu_tanh(proj[:, :dr])
    xr = proj[:, dr:]

    xbuf[8:8 + ts, :] = xr
    xc = cb_ref[...] + xr * cw_ref[CONV_WIDTH - 1:CONV_WIDTH, :]
    for lag in range(1, CONV_WIDTH):
        k = CONV_WIDTH - 1 - lag
        xc = xc + xbuf[pl.ds(8 - lag, ts), :] * cw_ref[k:k + 1, :]
    xbuf[0:8, :] = xbuf[ts:ts + 8, :]

    xcb = xc.astype(BF16)
    r = jax.nn.sigmoid(_block_diag_dot(xcb, wg_ref, 0, dr, bw) + bg_ref[:, :dr])
    i = jax.nn.sigmoid(_block_diag_dot(xcb, wg_ref, dr, dr, bw) + bg_ref[:, dr:])
    z = -lam_ref[...]
    softplus = jnp.maximum(z, 0.0) + jnp.log(1.0 + jnp.exp(-jnp.abs(z)))
    log_a = (-LRU_C * r) * softplus
    a = jnp.exp(log_a)
    mult = jnp.sqrt(1.0 - a * a)
    row = lax.broadcasted_iota(jnp.int32, (ts, 1), 0)
    mult = jnp.where((row == 0) & (t == 0), 1.0, mult)
    u = mult * (i * xc)

    sub = row & (SUBLANES - 1)
    for shift in (1, 2, 4):
        keep = sub >= shift
        a_prev = jnp.where(keep, pltpu.roll(a, shift, 0), 1.0)
        u_prev = jnp.where(keep, pltpu.roll(u, shift, 0), 0.0)
        u = a * u_prev + u
        a = a * a_prev
    carry = hcar[...]
    groups = []
    for k in range(ts // SUBLANES):
        rows = slice(k * SUBLANES, (k + 1) * SUBLANES)
        hk = u[rows] + a[rows] * carry
        carry = hk[SUBLANES - 1:SUBLANES]
        groups.append(hk)
    h = jnp.concatenate(groups, axis=0)
    hcar[...] = carry

    y = (h * gate).astype(BF16)
    o_ref[0] = x + _dot(y, wout_ref[...]) + bout_ref[...]


def _lru_layer(x, gn, w_in, b_in, conv_w, conv_b, w_a, b_a, w_i, b_i, lam, w_out, b_out):
    B, S, D = x.shape
    dr = w_out.shape[0]
    ts = min(256, S)
    assert S % ts == 0 and ts % 8 == 0
    wg = jnp.concatenate([jax.scipy.linalg.block_diag(*w_a), jax.scipy.linalg.block_diag(*w_i)], axis=1)
    row = lambda v: v.reshape(1, -1)
    body = functools.partial(_lru_body, ts=ts, dr=dr, bw=w_a.shape[1])
    return pl.pallas_call(
        body,
        grid=(B, S // ts),
        in_specs=[
            pl.BlockSpec((1, ts, D), lambda b, t: (b, t, 0)),
            _resident((1, D)),
            _resident((D, 2 * dr)), _resident((1, 2 * dr)),
            _resident((CONV_WIDTH, dr)), _resident((1, dr)),
            _resident((dr, 2 * dr)), _resident((1, 2 * dr)),
            _resident((1, dr)),
            _resident((dr, D)), _resident((1, D)),
        ],
        out_specs=pl.BlockSpec((1, ts, D), lambda b, t: (b, t, 0)),
        out_shape=jax.ShapeDtypeStruct((B, S, D), F32),
        scratch_shapes=[pltpu.VMEM((ts + 8, dr), F32), pltpu.VMEM((1, dr), F32)],
        compiler_params=_params("arbitrary", "arbitrary"),
        name="lru_mixer",
    )(x, row(gn), w_in.astype(BF16), row(b_in), conv_w, row(conv_b), wg.astype(BF16),
      row(jnp.concatenate([b_a, b_i])), row(lam), w_out.astype(BF16), row(b_out))


def _ffn_body(*refs, n_e, n_f, use_gates, final_norm):
    x_ref, gn_ref = refs[0], refs[1]
    k = 2
    gates_ref = gfin_ref = None
    if use_gates:
        gates_ref = refs[k]; k += 1
    wg_ref, wu_ref, wd_ref = refs[k:k + 3]; k += 3
    if final_norm:
        gfin_ref = refs[k]; k += 1
    o_ref, hn_ref, acc_ref = refs[k:k + 3]
    e = pl.program_id(1)
    f = pl.program_id(2)

    @pl.when((e == 0) & (f == 0))
    def _():
        hn_ref[...] = _rms(x_ref[...], gn_ref[...]).astype(BF16)
        acc_ref[...] = jnp.zeros_like(acc_ref)

    hn = hn_ref[...]
    g = _dot(hn, wg_ref[0])
    u = _dot(hn, wu_ref[0])
    act = (g * jax.nn.sigmoid(g)) * u
    if use_gates:
        gates = gates_ref[...]
        lane = lax.broadcasted_iota(jnp.int32, gates.shape, 1)
        act = act * _lane_col(gates, lane, e)
    acc_ref[...] += _dot(act.astype(BF16), wd_ref[0])

    @pl.when((e == n_e - 1) & (f == n_f - 1))
    def _():
        out = x_ref[...] + acc_ref[...]
        if final_norm:
            out = _rms(out, gfin_ref[...])
        o_ref[...] = out


def _ffn_layer(x2, gn, w_gate, w_up, w_down, gates=None, g_final=None, tm=1024, tf=512):
    M, D = x2.shape
    n_e, _, F = w_gate.shape
    tm = min(tm, M)
    tf = min(tf, F)
    assert M % tm == 0 and F % tf == 0
    n_f = F // tf
    use_gates = gates is not None
    final_norm = g_final is not None
    in_specs = [pl.BlockSpec((tm, D), lambda i, e, f: (i, 0)), _resident((1, D))]
    args = [x2, gn.reshape(1, D)]
    if use_gates:
        in_specs.append(pl.BlockSpec((tm, LANES), lambda i, e, f: (i, 0)))
        args.append(gates)
    in_specs += [
        pl.BlockSpec((1, D, tf), lambda i, e, f: (e, 0, f)),
        pl.BlockSpec((1, D, tf), lambda i, e, f: (e, 0, f)),
        pl.BlockSpec((1, tf, D), lambda i, e, f: (e, f, 0)),
    ]
    args += [w_gate.astype(BF16), w_up.astype(BF16), w_down.astype(BF16)]
    if final_norm:
        in_specs.append(_resident((1, D)))
        args.append(g_final.reshape(1, D))
    body = functools.partial(_ffn_body, n_e=n_e, n_f=n_f, use_gates=use_gates, final_norm=final_norm)
    return pl.pallas_call(
        body,
        grid=(M // tm, n_e, n_f),
        in_specs=in_specs,
        out_specs=pl.BlockSpec((tm, D), lambda i, e, f: (i, 0)),
        out_shape=jax.ShapeDtypeStruct((M, D), F32),
        scratch_shapes=[pltpu.VMEM((tm, D), BF16), pltpu.VMEM((tm, D), F32)],
        compiler_params=_params("arbitrary", "arbitrary", "arbitrary"),
        name="moe_swiglu" if use_gates else "dense_swiglu",
    )(*args)


def _router_body(x_ref, gn_ref, wh_ref, wl_ref, tril_ref, o_ref, cnt_ref, *, n_experts):
    @pl.when(pl.program_id(0) == 0)
    def _():
        cnt_ref[...] = jnp.zeros_like(cnt_ref)

    hn = _rms(x_ref[...], gn_ref[...])
    hh = hn.astype(BF16)
    hl = (hn - hh.astype(F32)).astype(BF16)
    wh = wh_ref[...]
    logits = _dot(hh, wh) + _dot(hh, wl_ref[...]) + _dot(hl, wh)
    lane = lax.broadcasted_iota(jnp.int32, logits.shape, 1)
    lg = jnp.where(lane < n_experts, logits, -jnp.inf)
    m0 = jnp.max(lg, axis=1, keepdims=True)
    i0 = jnp.min(jnp.where(lg == m0, lane, LANES), axis=1, keepdims=True)
    lg = jnp.where(lane == i0, -jnp.inf, lg)
    m1 = jnp.max(lg, axis=1, keepdims=True)
    i1 = jnp.min(jnp.where(lg == m1, lane, LANES), axis=1, keepdims=True)
    e1 = jnp.exp(m1 - m0)
    w0 = 1.0 / (1.0 + e1)
    routed = ((lane == i0) | (lane == i1)).astype(F32)
    incl = _dot(tril_ref[...], routed.astype(BF16))
    excl = incl - routed + cnt_ref[0:1, :]

    def put(col, v):
        return jnp.where(lane == col, v, 0.0)

    o_ref[...] = (put(META_I0, i0.astype(F32)) + put(META_I1, i1.astype(F32)) + put(META_W0, w0)
                  + put(META_W1, e1 * w0) + put(META_R0, _lane_col(excl, lane, i0))
                  + put(META_R1, _lane_col(excl, lane, i1)))
    cnt_ref[...] = jnp.broadcast_to(cnt_ref[0:1, :] + incl[incl.shape[0] - 1:, :], cnt_ref.shape)


META_I0, META_I1, META_W0, META_W1, META_R0, META_R1 = range(6)


def _router(x2, gn, w_router, tm=512):
    M, D = x2.shape
    n_experts = w_router.shape[1]
    tm = min(tm, M)
    wpad = jnp.pad(w_router, ((0, 0), (0, LANES - n_experts)))
    wh = wpad.astype(BF16)
    wl = (wpad - wh.astype(F32)).astype(BF16)
    tril = jnp.tril(jnp.ones((tm, tm), BF16))
    return pl.pallas_call(
        functools.partial(_router_body, n_experts=n_experts),
        grid=(M // tm,),
        in_specs=[pl.BlockSpec((tm, D), lambda i: (i, 0)), _resident((1, D)),
                  _resident((D, LANES)), _resident((D, LANES)), _resident((tm, tm))],
        out_specs=[pl.BlockSpec((tm, LANES), lambda i: (i, 0)), pl.BlockSpec((8, LANES), lambda i: (0, 0))],
        out_shape=[jax.ShapeDtypeStruct((M, LANES), F32), jax.ShapeDtypeStruct((8, LANES), F32)],
        compiler_params=_params("arbitrary"),
        name="moe_router",
    )(x2, gn.reshape(1, D), wh, wl, tril)


def _combine_body(pos_ref, pos_next_ref, x_ref, meta_ref, y_ref, gfin_ref, o_ref, ybuf, sem, *, tm, n_tiles):
    i = pl.program_id(0)
    slot = i % 2

    def row_copy(idx_ref, k, r, s):
        return pltpu.make_async_copy(y_ref.at[pl.ds(idx_ref[0, 0, k * tm + r], 1)], ybuf.at[s, k, pl.ds(r, 1)],
                                     sem.at[s])

    @pl.when(i == 0)
    def _():
        def body(r, c):
            row_copy(pos_ref, 0, r, 0).start()
            row_copy(pos_ref, 1, r, 0).start()
            return c
        lax.fori_loop(0, tm, body, 0, unroll=8)

    @pl.when(i + 1 < n_tiles)
    def _():
        for r in range(tm):
            row_copy(pos_next_ref, 0, r, 1 - slot).start()
            row_copy(pos_next_ref, 1, r, 1 - slot).start()

    pltpu.make_async_copy(ybuf.at[slot], ybuf.at[slot], sem.at[slot]).wait()
    meta = meta_ref[...]
    lane = lax.broadcasted_iota(jnp.int32, meta.shape, 1)
    out = (x_ref[...] + _lane_col(meta, lane, META_W0) * ybuf[slot, 0]
           + _lane_col(meta, lane, META_W1) * ybuf[slot, 1])
    o_ref[...] = _rms(out, gfin_ref[...])


def _combine(x2, meta, y, pos0, pos1, g_final, tm=512):
    M, D = x2.shape
    tm = min(tm, M)
    n_tiles = M // tm
    pos = jnp.concatenate([pos0.reshape(n_tiles, 1, tm), pos1.reshape(n_tiles, 1, tm)], axis=2)
    return pl.pallas_call(
        functools.partial(_combine_body, tm=tm, n_tiles=n_tiles),
        grid=(n_tiles,),
        in_specs=[pl.BlockSpec((1, 1, 2 * tm), lambda i: (i, 0, 0), memory_space=pltpu.SMEM),
                  pl.BlockSpec((1, 1, 2 * tm), lambda i: (jnp.minimum(i + 1, n_tiles - 1), 0, 0),
                               memory_space=pltpu.SMEM),
                  pl.BlockSpec((tm, D), lambda i: (i, 0)),
                  pl.BlockSpec((tm, LANES), lambda i: (i, 0)),
                  pl.BlockSpec(memory_space=pl.ANY),
                  _resident((1, D))],
        out_specs=pl.BlockSpec((tm, D), lambda i: (i, 0)),
        out_shape=jax.ShapeDtypeStruct((M, D), F32),
        scratch_shapes=[pltpu.VMEM((2, 2, tm, D), F32), pltpu.SemaphoreType.DMA((2,))],
        compiler_params=_params("arbitrary"),
        name="moe_combine",
    )(pos, pos, x2, meta, y, g_final.reshape(1, D))


def _grouped_ffn_body(te_ref, nused_ref, src_ref, src_next_ref, x_ref, gn_ref, wg_ref, wu_ref, wd_ref, o_ref,
                      xbuf, hn_ref, acc_ref, sem, *, tm, n_f, n_tiles):
    j = pl.program_id(0)
    f = pl.program_id(1)
    n_used = nused_ref[0]
    used = j < n_used
    gathered = (j == 0) | (j <= n_used)
    slot = j % 2
    chunk = tm // (n_f + 1)

    def row_copy(idx_ref, r, s):
        return pltpu.make_async_copy(x_ref.at[pl.ds(idx_ref[0, 0, r], 1)], xbuf.at[s, pl.ds(r, 1)], sem.at[s])

    def start_chunk(first):
        for k in range(chunk):
            row_copy(src_next_ref, first + k, 1 - slot).start()

    @pl.when((j == 0) & (f == 0))
    def _():
        def body(r, c):
            row_copy(src_ref, r, 0).start()
            return c
        lax.fori_loop(0, tm, body, 0, unroll=8)

    @pl.when(f == 0)
    def _():
        acc_ref[...] = jnp.zeros_like(acc_ref)

        @pl.when(gathered)
        def _():
            pltpu.make_async_copy(xbuf.at[slot], xbuf.at[slot], sem.at[slot]).wait()
            hn_ref[...] = _rms(xbuf[slot], gn_ref[...]).astype(BF16)

        @pl.when(used)
        def _():
            start_chunk(0)

    @pl.when(used)
    def _():
        hn = hn_ref[...]
        g = _dot(hn, wg_ref[0])
        u = _dot(hn, wu_ref[0])
        acc_ref[...] += _dot(((g * jax.nn.sigmoid(g)) * u).astype(BF16), wd_ref[0])
        start_chunk((f + 1) * chunk)

    @pl.when(f == n_f - 1)
    def _():
        o_ref[...] = acc_ref[...]


def _grouped_ffn(x2, src, gn, w_gate, w_up, w_down, tile_expert, n_used, tm, tf=512):
    P = src.shape[0]
    D = x2.shape[1]
    F = w_gate.shape[2]
    tf = min(tf, F)
    n_f = F // tf
    n_tiles = P // tm
    assert F % tf == 0 and tm % (n_f + 1) == 0
    src3 = src.reshape(n_tiles, 1, tm)
    grid_spec = pltpu.PrefetchScalarGridSpec(
        num_scalar_prefetch=2,
        grid=(n_tiles, n_f),
        in_specs=[pl.BlockSpec((1, 1, tm), lambda j, f, te, nu: (j, 0, 0), memory_space=pltpu.SMEM),
                  pl.BlockSpec((1, 1, tm), lambda j, f, te, nu: (jnp.minimum(j + 1, n_tiles - 1), 0, 0),
                               memory_space=pltpu.SMEM),
                  pl.BlockSpec(memory_space=pl.ANY),
                  pl.BlockSpec((1, D), lambda j, f, te, nu: (0, 0)),
                  pl.BlockSpec((1, D, tf), lambda j, f, te, nu: (te[j], 0, f)),
                  pl.BlockSpec((1, D, tf), lambda j, f, te, nu: (te[j], 0, f)),
                  pl.BlockSpec((1, tf, D), lambda j, f, te, nu: (te[j], f, 0))],
        out_specs=pl.BlockSpec((tm, D), lambda j, f, te, nu: (j, 0)),
        scratch_shapes=[pltpu.VMEM((2, tm, D), F32), pltpu.VMEM((tm, D), BF16), pltpu.VMEM((tm, D), F32),
                        pltpu.SemaphoreType.DMA((2,))])
    return pl.pallas_call(
        functools.partial(_grouped_ffn_body, tm=tm, n_f=n_f, n_tiles=n_tiles),
        grid_spec=grid_spec,
        out_shape=jax.ShapeDtypeStruct((P, D), F32),
        compiler_params=_params("arbitrary", "arbitrary"),
        name="moe_grouped_swiglu",
    )(tile_expert, n_used, src3, src3, x2, gn.reshape(1, D), w_gate.astype(BF16), w_up.astype(BF16),
      w_down.astype(BF16))


def _moe_layer(x2, gn, w_router, w_gate, w_up, w_down, g_final, tm=1024):
    M, D = x2.shape
    n_e = w_router.shape[1]
    tm = min(tm, M)
    meta, counts = _router(x2, gn, w_router)
    as_int = lambda col: meta[:, col].astype(jnp.int32)
    i0, i1, r0, r1 = as_int(META_I0), as_int(META_I1), as_int(META_R0), as_int(META_R1)
    padded = (counts[0, :n_e].astype(jnp.int32) + tm - 1) // tm * tm
    ends = jnp.cumsum(padded)
    offsets = ends - padded
    pos0 = offsets[i0] + r0
    pos1 = offsets[i1] + r1
    P = TOP_K * M + n_e * tm
    tok = jnp.arange(M, dtype=jnp.int32)
    src = jnp.zeros((P,), jnp.int32).at[jnp.concatenate([pos0, pos1])].set(jnp.concatenate([tok, tok]))
    tile_start = jnp.arange(P // tm, dtype=jnp.int32) * tm
    tile_expert = jnp.minimum(jnp.searchsorted(ends, tile_start, side="right"), n_e - 1).astype(jnp.int32)
    n_used = (ends[n_e - 1:] // tm).astype(jnp.int32)
    y = _grouped_ffn(x2, src, gn, w_gate, w_up, w_down, tile_expert, n_used, tm)
    return _combine(x2, meta, y, pos0, pos1, g_final)


def _nsa_proj_body(x_ref, gn_ref, w_ref, cos_ref, sin_ref,
                   q_ref, kc_ref, vc_ref, ks_ref, vs_ref, kw_ref, vw_ref, g_ref, *, tm, tiles_per_seq):
    i = pl.program_id(0)
    hn = _rms(x_ref[...], gn_ref[...]).astype(BF16)
    proj = _dot(hn, w_ref[...])
    lane = lax.broadcasted_iota(jnp.int32, (tm, LANES), 1)
    first_half = (lane & (HEAD_DIM - 1)) < HEAD_DIM // 2
    cos = cos_ref[...]
    sin = sin_ref[...]

    def chunk(c):
        return proj[:, c * LANES:(c + 1) * LANES]

    def rope(v):
        rot = jnp.where(first_half, pltpu.roll(v, LANES - HEAD_DIM // 2, 1), pltpu.roll(v, HEAD_DIM // 2, 1))
        return v * cos + rot * sin

    def heads(v):
        return v[:, :HEAD_DIM], pltpu.roll(v, HEAD_DIM, 1)[:, :HEAD_DIM]

    c = 0
    for cc in range(N_HEADS // 2):
        lo, hi = heads(rope(chunk(c)) * Q_SCALE); c += 1
        q_ref[2 * cc] = lo.astype(BF16)
        q_ref[2 * cc + 1] = hi.astype(BF16)
    for cc in range(N_KV_GROUPS // 2):
        lo, hi = heads(rope(chunk(c))); c += 1
        kc_ref[2 * cc] = lo
        kc_ref[2 * cc + 1] = hi
    for cc in range(N_KV_GROUPS // 2):
        lo, hi = heads(chunk(c)); c += 1
        vc_ref[2 * cc] = lo
        vc_ref[2 * cc + 1] = hi
    pos = (i % tiles_per_seq) * tm + lax.broadcasted_iota(jnp.int32, (tm, 1), 0)
    onehot = (lane == pos // SEL_BLOCK).astype(BF16)
    for cc in range(N_KV_GROUPS // 2):
        v = rope(chunk(c)); c += 1
        for j, vv in enumerate((v, pltpu.roll(v, HEAD_DIM, 1))):
            ks_ref[2 * cc + j, :, 0:LANES] = jnp.where(lane < HEAD_DIM, vv, 0.0).astype(BF16)
            ks_ref[2 * cc + j, :, LANES:2 * LANES] = onehot
    ones_col = jnp.where(lane == HEAD_DIM, 1.0, 0.0)

    def store_values(ref):
        nonlocal c
        for cc in range(N_KV_GROUPS // 2):
            v = chunk(c); c += 1
            for j, vv in enumerate((v, pltpu.roll(v, HEAD_DIM, 1))):
                ref[2 * cc + j] = jnp.where(lane < HEAD_DIM, vv, ones_col).astype(BF16)

    store_values(vs_ref)
    for cc in range(N_KV_GROUPS // 2):
        lo, hi = heads(rope(chunk(c))); c += 1
        kw_ref[2 * cc] = lo.astype(BF16)
        kw_ref[2 * cc + 1] = hi.astype(BF16)
    store_values(vw_ref)
    g_ref[...] = jax.nn.sigmoid(chunk(c))


def _nsa_proj(x2, gn, w_in, S, tm=256):
    M, D = x2.shape
    H, G, DH = N_HEADS, N_KV_GROUPS, HEAD_DIM
    tm = min(tm, S)
    assert S % tm == 0
    n_in = w_in.shape[1]
    n_pad = -(-n_in // LANES) * LANES
    wp = jnp.pad(w_in, ((0, 0), (0, n_pad - n_in))).astype(BF16)
    half = DH // 2
    freqs = ROPE_THETA ** (-jnp.arange(half, dtype=F32) / half)
    ang = jnp.arange(S, dtype=F32)[:, None] * freqs[None, :]
    cos = jnp.tile(jnp.cos(ang), (1, 2 * LANES // DH))
    sin = jnp.tile(jnp.concatenate([-jnp.sin(ang), jnp.sin(ang)], axis=1), (1, LANES // DH))
    tiles_per_seq = S // tm
    hd = lambda n, dt: jax.ShapeDtypeStruct((n, M, DH), dt)
    hspec = lambda n: pl.BlockSpec((n, tm, DH), lambda i: (0, i, 0))
    vd = jax.ShapeDtypeStruct((G, M, LANES), BF16)
    vspec = pl.BlockSpec((G, tm, LANES), lambda i: (0, i, 0))
    return pl.pallas_call(
        functools.partial(_nsa_proj_body, tm=tm, tiles_per_seq=tiles_per_seq),
        grid=(M // tm,),
        in_specs=[pl.BlockSpec((tm, D), lambda i: (i, 0)), _resident((1, D)), _resident((D, n_pad)),
                  pl.BlockSpec((tm, LANES), lambda i: (i % tiles_per_seq, 0)),
                  pl.BlockSpec((tm, LANES), lambda i: (i % tiles_per_seq, 0))],
        out_specs=[hspec(H), hspec(G), hspec(G),
                   pl.BlockSpec((G, tm, 2 * LANES), lambda i: (0, i, 0)),
                   vspec, hspec(G), vspec,
                   pl.BlockSpec((tm, LANES), lambda i: (i, 0))],
        out_shape=[hd(H, BF16), hd(G, F32), hd(G, F32),
                   jax.ShapeDtypeStruct((G, M, 2 * LANES), BF16),
                   vd, hd(G, BF16), vd,
                   jax.ShapeDtypeStruct((M, LANES), F32)],
        compiler_params=_params("arbitrary"),
        name="nsa_proj",
    )(x2, gn.reshape(1, D), wp, cos, sin)


def _compress_body(kc_ref, vc_ref, pe_ref, w1_ref, w2_ref, ko_ref, vo_ref, *, nc):
    half = CMP_BLOCK // 2
    for kv, (src, dst) in enumerate(((kc_ref, ko_ref), (vc_ref, vo_ref))):
        top = jnp.zeros((nc, w1_ref.shape[2]), F32)
        bot = jnp.zeros((nc, w1_ref.shape[2]), F32)
        for j in range(half):
            xj = src[0, pl.ds(j, nc, stride=CMP_STRIDE), :]
            top = top + _dot((xj + pe_ref[kv, j:j + 1, :]).astype(BF16),
                             w1_ref[kv, j * HEAD_DIM:(j + 1) * HEAD_DIM, :])
            bot = bot + _dot((xj + pe_ref[kv, half + j:half + j + 1, :]).astype(BF16),
                             w1_ref[kv, (half + j) * HEAD_DIM:(half + j + 1) * HEAD_DIM, :])
        hid = top + pltpu.roll(bot, nc - 1, 0)
        dst[0] = _dot(_gelu_tanh(hid).astype(BF16), w2_ref[kv]).astype(BF16)


def _compress(kc, vc, pe, w1, w2, B, S):
    assert CMP_BLOCK == 2 * CMP_STRIDE
    G, M, DH = kc.shape
    nc = S // CMP_STRIDE
    spec_in = pl.BlockSpec((1, S, DH), lambda b, g: (g, b, 0))
    spec_out = pl.BlockSpec((1, nc, DH), lambda b, g: (g, b, 0))
    out = jax.ShapeDtypeStruct((G, B * nc, DH), BF16)
    return pl.pallas_call(
        functools.partial(_compress_body, nc=nc),
        grid=(B, G),
        in_specs=[spec_in, spec_in, _resident(pe.shape), _resident(w1.shape), _resident(w2.shape)],
        out_specs=[spec_out, spec_out],
        out_shape=[out, out],
        compiler_params=_params("arbitrary", "arbitrary"),
        name="nsa_compress",
    )(kc, vc, pe, w1.astype(BF16), w2.astype(BF16))


def _store_head(o_ref, oh, gates, g, h, branch, rows=slice(None)):
    lane = lax.broadcasted_iota(jnp.int32, gates.shape, 1)
    col = 3 * (HEADS_PER_GROUP * g + h) + branch
    o_ref[rows, h * HEAD_DIM:(h + 1) * HEAD_DIM] = (oh * _lane_col(gates, lane, col)).astype(o_ref.dtype)


def _normalized_head(acc):
    return acc[:, :HEAD_DIM] * (1.0 / acc[:, HEAD_DIM:HEAD_DIM + 1])


def _cmp_body(q_ref, kc_ref, vc_ref, gates_ref, ov_ref, o_ref, bias_ref, imp_ref, *, tq, nc, n_s, k_sel):
    g = pl.program_id(1)
    i = pl.program_id(2)
    gates = gates_ref[...]
    t = i * tq + lax.broadcasted_iota(jnp.int32, (tq, 1), 0)

    def attend(w):
        kc = kc_ref[0, 0:w, :]
        vc = vc_ref[0, 0:w, :]
        cend = lax.broadcasted_iota(jnp.int32, (1, w), 1) * CMP_STRIDE + (CMP_BLOCK - 1)
        mask = cend <= t
        ps = None
        for h in range(HEADS_PER_GROUP):
            s = lax.dot_general(q_ref[h], kc, _NT, preferred_element_type=F32)
            s = jnp.where(mask, s, MASK_VALUE)
            m = jnp.max(s, axis=1, keepdims=True)
            m = jnp.where(m > 0.5 * MASK_VALUE, m, 0.0)
            e = jnp.exp2(s - m)
            p = e * (1.0 / jnp.maximum(jnp.sum(e, axis=1, keepdims=True), 1e-30))
            _store_head(o_ref, _dot(p.astype(BF16), vc), gates, g, h, 0)
            ps = p if ps is None else ps + p
        ph = ps.astype(BF16)
        pl_ = (ps - ph.astype(F32)).astype(BF16)
        imp_ref[...] = _dot(ph, ov_ref[0:w, :]) + _dot(pl_, ov_ref[0:w, :])

    n_chunks = nc // LANES
    need = ((i + 1) * (tq // CMP_STRIDE) + LANES - 1) // LANES
    for k in range(1, n_chunks + 1):
        pl.when((need == k) if k < n_chunks else (need >= k))(functools.partial(attend, k * LANES))

    j = lax.broadcasted_iota(jnp.int32, (tq, LANES), 1)
    cur = t // SEL_BLOCK
    forced = (j == 0) | (j == cur) | (j == cur - 1)
    valid = j * SEL_BLOCK <= t
    score = jnp.where(forced, FORCE_SCORE, jnp.where(valid, imp_ref[...], -1.0))
    score = jnp.where(j < n_s, score, -jnp.inf)
    x = score.T
    blk_i = lax.broadcasted_iota(jnp.int32, (LANES, tq), 0)
    blk = blk_i.astype(F32)

    def topk(x, sel, n_iter):
        for _ in range(n_iter):
            mx = jnp.max(x, axis=0, keepdims=True)
            idx = jnp.min(jnp.where(x == mx, blk, float(LANES)), axis=0, keepdims=True)
            hit = blk == idx
            sel = jnp.where(hit, 1.0, sel)
            x = jnp.where(hit, -jnp.inf, x)
        bias_ref[0] = jnp.where(sel.T > 0.5, 0.0, MASK_VALUE).astype(BF16)

    n_forced = 3
    direct = (i * tq >= 2 * SEL_BLOCK) & (k_sel > n_forced)

    @pl.when(direct)
    def _():
        cur_t = (i * tq + lax.broadcasted_iota(jnp.int32, (1, tq), 1)) // SEL_BLOCK
        forced_t = (blk_i == 0) | (blk_i == cur_t) | (blk_i == cur_t - 1)
        topk(jnp.where(forced_t, -jnp.inf, x), forced_t.astype(F32), k_sel - n_forced)

    @pl.when(jnp.logical_not(direct))
    def _():
        topk(x, jnp.zeros((LANES, tq), F32), k_sel)


def _overlap_matrix(nc, n_s):
    r = CMP_BLOCK // CMP_STRIDE
    qn = SEL_BLOCK // CMP_STRIDE
    m = np.zeros((nc, LANES), np.float32)
    n_c = nc - r + 1
    chunks = np.arange(n_c)[:, None] + np.arange(r)[None, :]
    np.add.at(m, (np.repeat(np.arange(n_c), r), (chunks // qn).ravel()), 1.0)
    return m


def _cmp_attention(q, kcmp, vcmp, gates, B, S, tq=256):
    H, M, DH = q.shape
    G = N_KV_GROUPS
    tq = min(tq, S)
    nq = S // tq
    nc = S // CMP_STRIDE
    n_s = S // SEL_BLOCK
    assert n_s <= LANES and tq & (tq - 1) == 0
    k_sel = min(N_SEL, n_s)
    ov = jnp.asarray(_overlap_matrix(nc, n_s), BF16)
    return pl.pallas_call(
        functools.partial(_cmp_body, tq=tq, nc=nc, n_s=n_s, k_sel=k_sel),
        grid=(B, G, nq),
        in_specs=[pl.BlockSpec((HEADS_PER_GROUP, tq, DH), lambda b, g, i: (g, b * nq + i, 0)),
                  pl.BlockSpec((1, nc, DH), lambda b, g, i: (g, b, 0)),
                  pl.BlockSpec((1, nc, DH), lambda b, g, i: (g, b, 0)),
                  pl.BlockSpec((tq, LANES), lambda b, g, i: (b * nq + i, 0)),
                  _resident((nc, LANES))],
        out_specs=[pl.BlockSpec((tq, HEADS_PER_GROUP * DH), lambda b, g, i: (b * nq + i, g)),
                   pl.BlockSpec((1, tq, LANES), lambda b, g, i: (g, b * nq + i, 0))],
        out_shape=[jax.ShapeDtypeStruct((M, H * DH), BF16), jax.ShapeDtypeStruct((G, M, LANES), BF16)],
        scratch_shapes=[pltpu.VMEM((tq, LANES), F32)],
        compiler_params=_params("arbitrary", "arbitrary", "arbitrary"),
        name="nsa_cmp_select",
    )(q, kcmp, vcmp, gates, ov)


def _sel_body(q_ref, bias_ref, k_ref, v_ref, gates_ref, o_ref, qcat, m_ref, acc_ref, *, tq):
    g = pl.program_id(1)
    i = pl.program_id(2)
    hp = HEADS_PER_GROUP
    rows = hp * tq
    qcat[:, 0:LANES] = jnp.zeros((rows, LANES), BF16)
    qcat[:, 0:HEAD_DIM] = q_ref[...].reshape(rows, HEAD_DIM)
    bias = bias_ref[0]
    for h in range(hp):
        qcat[h * tq:(h + 1) * tq, LANES:2 * LANES] = bias
    m_ref[...] = jnp.full((rows, LANES), MASK_VALUE, F32)
    acc_ref[...] = jnp.zeros((rows, LANES), F32)

    def tile(j, width, causal):
        start = pl.multiple_of(j * tq, tq)
        kt = k_ref[0, pl.ds(start, width), :]
        vt = v_ref[0, pl.ds(start, width), :]
        for h in range(hp):
            r0 = h * tq
            s = lax.dot_general(qcat[r0:r0 + tq, :], kt, _NT, preferred_element_type=F32)
            if causal:
                r = lax.broadcasted_iota(jnp.int32, (tq, 1), 0)
                c = lax.broadcasted_iota(jnp.int32, (1, width), 1)
                s = jnp.where(c <= r, s, MASK_VALUE)
            m_prev = m_ref[r0:r0 + tq, :]
            m_new = jnp.maximum(m_prev, jnp.max(s, axis=1, keepdims=True))
            alpha = jnp.exp2(m_prev - m_new)
            p = jnp.exp2(s - jnp.concatenate([m_new] * (width // LANES), axis=1))
            acc_ref[r0:r0 + tq, :] = alpha * acc_ref[r0:r0 + tq, :] + _dot(p.astype(BF16), vt)
            m_ref[r0:r0 + tq, :] = m_new

    def wide_tile(jw, carry):
        tile(SEL_WIDE * jw, SEL_WIDE * tq, False)
        return carry

    n_wide = i // SEL_WIDE
    lax.fori_loop(0, n_wide, wide_tile, 0)
    for rem in range(1, SEL_WIDE):
        pl.when(i % SEL_WIDE == rem)(functools.partial(tile, SEL_WIDE * n_wide, rem * tq, False))

    tile(i, tq, True)
    gates = gates_ref[...]
    for h in range(hp):
        _store_head(o_ref, _normalized_head(acc_ref[h * tq:(h + 1) * tq, :]), gates, g, h, 1)


def _sel_attention(q, bias, kscat, vs, gates, B, S, tq=512):
    H, M, DH = q.shape
    G = N_KV_GROUPS
    tq = min(tq, S)
    nq = S // tq
    assert tq % LANES == 0 and tq % SEL_BLOCK == 0 and S % tq == 0
    rows = HEADS_PER_GROUP * tq
    return pl.pallas_call(
        functools.partial(_sel_body, tq=tq),
        grid=(B, G, nq),
        in_specs=[pl.BlockSpec((HEADS_PER_GROUP, tq, DH), lambda b, g, i: (g, b * nq + i, 0)),
                  pl.BlockSpec((1, tq, LANES), lambda b, g, i: (g, b * nq + i, 0)),
                  pl.BlockSpec((1, S, 2 * LANES), lambda b, g, i: (g, b, 0)),
                  pl.BlockSpec((1, S, LANES), lambda b, g, i: (g, b, 0)),
                  pl.BlockSpec((tq, LANES), lambda b, g, i: (b * nq + i, 0))],
        out_specs=pl.BlockSpec((tq, HEADS_PER_GROUP * DH), lambda b, g, i: (b * nq + i, g)),
        out_shape=jax.ShapeDtypeStruct((M, H * DH), BF16),
        scratch_shapes=[pltpu.VMEM((rows, 2 * LANES), BF16), pltpu.VMEM((rows, LANES), F32),
                        pltpu.VMEM((rows, LANES), F32)],
        compiler_params=_params("arbitrary", "arbitrary", "arbitrary"),
        name="nsa_selected",
    )(q, bias, kscat, vs, gates)


def _win_body(q_ref, k_ref, v_ref, gates_ref, o_ref, *, tq, n_back, n_sub):
    g = pl.program_id(1)
    r = lax.broadcasted_iota(jnp.int32, (tq, 1), 0)
    c = lax.broadcasted_iota(jnp.int32, (1, tq), 1)
    for sub in range(n_sub):
        i = pl.program_id(2) * n_sub + sub
        rows = slice(sub * tq, (sub + 1) * tq)
        gates = gates_ref[rows, :]
        tiles = []
        for back in range(n_back, -1, -1):
            jt = i - back
            start = pl.multiple_of(jnp.maximum(jt, 0) * tq, tq)
            d = r - c + back * tq
            ok = (d >= 0) & (d < WINDOW) & (jt >= 0)
            tiles.append((k_ref[0, pl.ds(start, tq), :], v_ref[0, pl.ds(start, tq), :], ok))
        for h in range(HEADS_PER_GROUP):
            q = q_ref[h, rows, :]
            scores = [jnp.where(ok, lax.dot_general(q, kt, _NT, preferred_element_type=F32), MASK_VALUE)
                      for kt, _, ok in tiles]
            m = functools.reduce(jnp.maximum, scores).max(axis=1, keepdims=True)
            acc = jnp.zeros((tq, LANES), F32)
            for s, (_, vt, _) in zip(scores, tiles):
                acc = acc + _dot(jnp.exp2(s - m).astype(BF16), vt)
            _store_head(o_ref, _normalized_head(acc), gates, g, h, 2, rows)


def _win_attention(q, kw, vw, gates, B, S, tq=256, n_sub=2):
    H, M, DH = q.shape
    G = N_KV_GROUPS
    tq = min(tq, S)
    n_sub = min(n_sub, S // tq)
    tile = tq
    tq = tile * n_sub
    nq = S // tq
    n_back = -(-WINDOW // tile)
    return pl.pallas_call(
        functools.partial(_win_body, tq=tile, n_back=n_back, n_sub=n_sub),
        grid=(B, G, nq),
        in_specs=[pl.BlockSpec((HEADS_PER_GROUP, tq, DH), lambda b, g, i: (g, b * nq + i, 0)),
                  pl.BlockSpec((1, S, DH), lambda b, g, i: (g, b, 0)),
                  pl.BlockSpec((1, S, LANES), lambda b, g, i: (g, b, 0)),
                  pl.BlockSpec((tq, LANES), lambda b, g, i: (b * nq + i, 0))],
        out_specs=pl.BlockSpec((tq, HEADS_PER_GROUP * DH), lambda b, g, i: (b * nq + i, g)),
        out_shape=jax.ShapeDtypeStruct((M, H * DH), BF16),
        compiler_params=_params("arbitrary", "arbitrary", "arbitrary"),
        name="nsa_window",
    )(q, kw, vw, gates)


def _outproj_body(x_ref, a_ref, b_ref, c_ref, w_ref, o_ref):
    o = a_ref[...].astype(F32) + b_ref[...].astype(F32) + c_ref[...].astype(F32)
    o_ref[...] = x_ref[...] + _dot(o.astype(BF16), w_ref[...])


def _outproj(x2, oc, os_, ow, w_out, tm=512):
    M, D = x2.shape
    K = w_out.shape[0]
    tm = min(tm, M)
    spec_o = pl.BlockSpec((tm, K), lambda i: (i, 0))
    return pl.pallas_call(
        _outproj_body,
        grid=(M // tm,),
        in_specs=[pl.BlockSpec((tm, D), lambda i: (i, 0)), spec_o, spec_o, spec_o, _resident((K, D))],
        out_specs=pl.BlockSpec((tm, D), lambda i: (i, 0)),
        out_shape=jax.ShapeDtypeStruct((M, D), F32),
        compiler_params=_params("arbitrary"),
        name="nsa_outproj",
    )(x2, oc, os_, ow, w_out.astype(BF16))


def _nsa_layer(x, gn, w_in, cmp_pe, cmp_w1, cmp_w2, w_out):
    B, S, D = x.shape
    x2 = x.reshape(B * S, D)
    q, kc, vc, kscat, vs, kw, vw, gates = _nsa_proj(x2, gn, w_in, S)
    kcmp, vcmp = _compress(kc, vc, cmp_pe, cmp_w1, cmp_w2, B, S)
    o_cmp, bias = _cmp_attention(q, kcmp, vcmp, gates, B, S)
    o_sel = _sel_attention(q, bias, kscat, vs, gates, B, S)
    o_win = _win_attention(q, kw, vw, gates, B, S)
    return _outproj(x2, o_cmp, o_sel, o_win, w_out).reshape(B, S, D)


def kernel(x, norm_mix, norm_ffn, norm_final, lru_w_in, lru_b_in, lru_conv_w, lru_conv_b, lru_w_a, lru_b_a, lru_w_i, lru_b_i, lru_lambda, lru_w_out, lru_b_out, nsa_w_in, nsa_cmp_pe, nsa_cmp_w1, nsa_cmp_w2, nsa_w_out, ffn_w_gate, ffn_w_up, ffn_w_down, moe_w_router, moe_w_gate, moe_w_up, moe_w_down):
    B, S, D = x.shape
    assert norm_mix.shape[0] == 2 and lru_w_in.shape[0] == 1 and nsa_w_in.shape[0] == 1
    x = _lru_layer(x, norm_mix[0], lru_w_in[0], lru_b_in[0], lru_conv_w[0], lru_conv_b[0], lru_w_a[0],
                   lru_b_a[0], lru_w_i[0], lru_b_i[0], lru_lambda[0], lru_w_out[0], lru_b_out[0])
    x2 = _ffn_layer(x.reshape(B * S, D), norm_ffn[0], ffn_w_gate, ffn_w_up, ffn_w_down)
    x2 = _nsa_layer(x2.reshape(B, S, D), norm_mix[1], nsa_w_in[0], nsa_cmp_pe[0], nsa_cmp_w1[0],
                    nsa_cmp_w2[0], nsa_w_out[0]).reshape(B * S, D)
    out = _moe_layer(x2, norm_ffn[1], moe_w_router[0], moe_w_gate[0], moe_w_up[0], moe_w_down[0], norm_final)
    return out.reshape(B, S, D)
```

```python
import functools

import numpy as np
import jax
import jax.numpy as jnp
from jax import lax
from jax.experimental import pallas as pl
from jax.experimental.pallas import tpu as pltpu

F32 = jnp.float32
BF16 = jnp.bfloat16

RMS_EPS = 1e-6
LRU_C = 8.0
CONV_WIDTH = 4
N_HEADS = 16
HEAD_DIM = 64
N_KV_GROUPS = 4
HEADS_PER_GROUP = N_HEADS // N_KV_GROUPS
CMP_BLOCK = 32
CMP_STRIDE = 16
SEL_BLOCK = 64
N_SEL = 16
WINDOW = 512
FORCE_SCORE = 1.0e4
ROPE_THETA = 10000.0
TOP_K = 2

LANES = 128
SUBLANES = 8
MXU_WIDTH = 256
MASK_VALUE = -1.0e30
VMEM_LIMIT_BYTES = 56 * 1024 * 1024

_NT = (((1,), (1,)), ((), ()))
Q_SCALE = HEAD_DIM ** -0.5 * 1.4426950408889634
SEL_WIDE = 4


def _params(*semantics):
    return pltpu.CompilerParams(dimension_semantics=semantics, vmem_limit_bytes=VMEM_LIMIT_BYTES)


def _resident(shape):
    zeros = (0,) * len(shape)
    return pl.BlockSpec(shape, lambda *_: zeros, pipeline_mode=pl.Buffered(1))


def _rms(x, g):
    return x * lax.rsqrt(jnp.mean(x * x, axis=-1, keepdims=True) + RMS_EPS) * g


def _dot(a, b):
    return jnp.dot(a, b, preferred_element_type=F32)


def _gelu_tanh(x):
    two_z = x * (2.0 * 0.7978845608028654 + (2.0 * 0.7978845608028654 * 0.044715) * (x * x))
    return x * jax.nn.sigmoid(two_z)


def _block_diag_dot(xb, w_ref, col0, n, bw):
    outs = []
    for c0 in range(0, n, MXU_WIDTH):
        w = min(MXU_WIDTH, n - c0)
        lo = (c0 // bw) * bw // LANES * LANES
        hi = min(n, -(-(((c0 + w - 1) // bw + 1) * bw) // LANES) * LANES)
        outs.append(_dot(xb[:, lo:hi], w_ref[lo:hi, col0 + c0:col0 + c0 + w]))
    return jnp.concatenate(outs, axis=1)


def _lane_col(vals, lane, idx):
    return jnp.sum(jnp.where(lane == idx, vals, 0.0), axis=1, keepdims=True)


def _lru_body(x_ref, gn_ref, win_ref, bin_ref, cw_ref, cb_ref, wg_ref, bg_ref, lam_ref,
              wout_ref, bout_ref, o_ref, xbuf, hcar, *, ts, dr, bw):
    t = pl.program_id(1)

    @pl.when(t == 0)
    def _():
        xbuf[0:8, :] = jnp.zeros((8, dr), F32)
        hcar[...] = jnp.zeros_like(hcar)

    x = x_ref[0]
    hn = _rms(x, gn_ref[...]).astype(BF16)
    proj = _dot(hn, win_ref[...]) + bin_ref[...]
    gate = _gelu_tanh(proj[:, :dr])
    xr = proj[:, dr:]

    xbuf[8:8 + ts, :] = xr
    xc = cb_ref[...] + xr * cw_ref[CONV_WIDTH - 1:CONV_WIDTH, :]
    for lag in range(1, CONV_WIDTH):
        k = CONV_WIDTH - 1 - lag
        xc = xc + xbuf[pl.ds(8 - lag, ts), :] * cw_ref[k:k + 1, :]
    xbuf[0:8, :] = xbuf[ts:ts + 8, :]

    xcb = xc.astype(BF16)
    r = jax.nn.sigmoid(_block_diag_dot(xcb, wg_ref, 0, dr, bw) + bg_ref[:, :dr])
    i = jax.nn.sigmoid(_block_diag_dot(xcb, wg_ref, dr, dr, bw) + bg_ref[:, dr:])
    z = -lam_ref[...]
    softplus = jnp.maximum(z, 0.0) + jnp.log(1.0 + jnp.exp(-jnp.abs(z)))
    log_a = (-LRU_C * r) * softplus
    a = jnp.exp(log_a)
    mult = jnp.sqrt(1.0 - a * a)
    row = lax.broadcasted_iota(jnp.int32, (ts, 1), 0)
    mult = jnp.where((row == 0) & (t == 0), 1.0, mult)
    u = mult * (i * xc)

    sub = row & (SUBLANES - 1)
    for shift in (1, 2, 4):
        keep = sub >= shift
        a_prev = jnp.where(keep, pltpu.roll(a, shift, 0), 1.0)
        u_prev = jnp.where(keep, pltpu.roll(u, shift, 0), 0.0)
        u = a * u_prev + u
        a = a * a_prev
    carry = hcar[...]
    groups = []
    for k in range(ts // SUBLANES):
        rows = slice(k * SUBLANES, (k + 1) * SUBLANES)
        hk = u[rows] + a[rows] * carry
        carry = hk[SUBLANES - 1:SUBLANES]
        groups.append(hk)
    h = jnp.concatenate(groups, axis=0)
    hcar[...] = carry

    y = (h * gate).astype(BF16)
    o_ref[0] = x + _dot(y, wout_ref[...]) + bout_ref[...]


def _lru_layer(x, gn, w_in, b_in, conv_w, conv_b, w_a, b_a, w_i, b_i, lam, w_out, b_out):
    B, S, D = x.shape
    dr = w_out.shape[0]
    ts = min(256, S)
    assert S % ts == 0 and ts % 8 == 0
    wg = jnp.concatenate([jax.scipy.linalg.block_diag(*w_a), jax.scipy.linalg.block_diag(*w_i)], axis=1)
    row = lambda v: v.reshape(1, -1)
    body = functools.partial(_lru_body, ts=ts, dr=dr, bw=w_a.shape[1])
    return pl.pallas_call(
        body,
        grid=(B, S // ts),
        in_specs=[
            pl.BlockSpec((1, ts, D), lambda b, t: (b, t, 0)),
            _resident((1, D)),
            _resident((D, 2 * dr)), _resident((1, 2 * dr)),
            _resident((CONV_WIDTH, dr)), _resident((1, dr)),
            _resident((dr, 2 * dr)), _resident((1, 2 * dr)),
            _resident((1, dr)),
            _resident((dr, D)), _resident((1, D)),
        ],
        out_specs=pl.BlockSpec((1, ts, D), lambda b, t: (b, t, 0)),
        out_shape=jax.ShapeDtypeStruct((B, S, D), F32),
        scratch_shapes=[pltpu.VMEM((ts + 8, dr), F32), pltpu.VMEM((1, dr), F32)],
        compiler_params=_params("arbitrary", "arbitrary"),
        name="lru_mixer",
    )(x, row(gn), w_in.astype(BF16), row(b_in), conv_w, row(conv_b), wg.astype(BF16),
      row(jnp.concatenate([b_a, b_i])), row(lam), w_out.astype(BF16), row(b_out))


def _ffn_body(*refs, n_e, n_f, use_gates, final_norm):
    x_ref, gn_ref = refs[0], refs[1]
    k = 2
    gates_ref = gfin_ref = None
    if use_gates:
        gates_ref = refs[k]; k += 1
    wg_ref, wu_ref, wd_ref = refs[k:k + 3]; k += 3
    if final_norm:
        gfin_ref = refs[k]; k += 1
    o_ref, hn_ref, acc_ref = refs[k:k + 3]
    e = pl.program_id(1)
    f = pl.program_id(2)

    @pl.when((e == 0) & (f == 0))
    def _():
        hn_ref[...] = _rms(x_ref[...], gn_ref[...]).astype(BF16)
        acc_ref[...] = jnp.zeros_like(acc_ref)

    hn = hn_ref[...]
    g = _dot(hn, wg_ref[0])
    u = _dot(hn, wu_ref[0])
    act = (g * jax.nn.sigmoid(g)) * u
    if use_gates:
        gates = gates_ref[...]
        lane = lax.broadcasted_iota(jnp.int32, gates.shape, 1)
        act = act * _lane_col(gates, lane, e)
    acc_ref[...] += _dot(act.astype(BF16), wd_ref[0])

    @pl.when((e == n_e - 1) & (f == n_f - 1))
    def _():
        out = x_ref[...] + acc_ref[...]
        if final_norm:
            out = _rms(out, gfin_ref[...])
        o_ref[...] = out


def _ffn_layer(x2, gn, w_gate, w_up, w_down, gates=None, g_final=None, tm=1024, tf=512):
    M, D = x2.shape
    n_e, _, F = w_gate.shape
    tm = min(tm, M)
    tf = min(tf, F)
    assert M % tm == 0 and F % tf == 0
    n_f = F // tf
    use_gates = gates is not None
    final_norm = g_final is not None
    in_specs = [pl.BlockSpec((tm, D), lambda i, e, f: (i, 0)), _resident((1, D))]
    args = [x2, gn.reshape(1, D)]
    if use_gates:
        in_specs.append(pl.BlockSpec((tm, LANES), lambda i, e, f: (i, 0)))
        args.append(gates)
    in_specs += [
        pl.BlockSpec((1, D, tf), lambda i, e, f: (e, 0, f)),
        pl.BlockSpec((1, D, tf), lambda i, e, f: (e, 0, f)),
        pl.BlockSpec((1, tf, D), lambda i, e, f: (e, f, 0)),
    ]
    args += [w_gate.astype(BF16), w_up.astype(BF16), w_down.astype(BF16)]
    if final_norm:
        in_specs.append(_resident((1, D)))
        args.append(g_final.reshape(1, D))
    body = functools.partial(_ffn_body, n_e=n_e, n_f=n_f, use_gates=use_gates, final_norm=final_norm)
    return pl.pallas_call(
        body,
        grid=(M // tm, n_e, n_f),
        in_specs=in_specs,
        out_specs=pl.BlockSpec((tm, D), lambda i, e, f: (i, 0)),
        out_shape=jax.ShapeDtypeStruct((M, D), F32),
        scratch_shapes=[pltpu.VMEM((tm, D), BF16), pltpu.VMEM((tm, D), F32)],
        compiler_params=_params("arbitrary", "arbitrary", "arbitrary"),
        name="moe_swiglu" if use_gates else "dense_swiglu",
    )(*args)


def _router_body(x_ref, gn_ref, wh_ref, wl_ref, tril_ref, o_ref, cnt_ref, *, n_experts):
    @pl.when(pl.program_id(0) == 0)
    def _():
        cnt_ref[...] = jnp.zeros_like(cnt_ref)

    hn = _rms(x_ref[...], gn_ref[...])
    hh = hn.astype(BF16)
    hl = (hn - hh.astype(F32)).astype(BF16)
    wh = wh_ref[...]
    logits = _dot(hh, wh) + _dot(hh, wl_ref[...]) + _dot(hl, wh)
    lane = lax.broadcasted_iota(jnp.int32, logits.shape, 1)
    lg = jnp.where(lane < n_experts, logits, -jnp.inf)
    m0 = jnp.max(lg, axis=1, keepdims=True)
    i0 = jnp.min(jnp.where(lg == m0, lane, LANES), axis=1, keepdims=True)
    lg = jnp.where(lane == i0, -jnp.inf, lg)
    m1 = jnp.max(lg, axis=1, keepdims=True)
    i1 = jnp.min(jnp.where(lg == m1, lane, LANES), axis=1, keepdims=True)
    e1 = jnp.exp(m1 - m0)
    w0 = 1.0 / (1.0 + e1)
    routed = ((lane == i0) | (lane == i1)).astype(F32)
    incl = _dot(tril_ref[...], routed.astype(BF16))
    excl = incl - routed + cnt_ref[0:1, :]

    def put(col, v):
        return jnp.where(lane == col, v, 0.0)

    o_ref[...] = (put(META_I0, i0.astype(F32)) + put(META_I1, i1.astype(F32)) + put(META_W0, w0)
                  + put(META_W1, e1 * w0) + put(META_R0, _lane_col(excl, lane, i0))
                  + put(META_R1, _lane_col(excl, lane, i1)))
    cnt_ref[...] = jnp.broadcast_to(cnt_ref[0:1, :] + incl[incl.shape[0] - 1:, :], cnt_ref.shape)


META_I0, META_I1, META_W0, META_W1, META_R0, META_R1 = range(6)


def _router(x2, gn, w_router, tm=512):
    M, D = x2.shape
    n_experts = w_router.shape[1]
    tm = min(tm, M)
    wpad = jnp.pad(w_router, ((0, 0), (0, LANES - n_experts)))
    wh = wpad.astype(BF16)
    wl = (wpad - wh.astype(F32)).astype(BF16)
    tril = jnp.tril(jnp.ones((tm, tm), BF16))
    return pl.pallas_call(
        functools.partial(_router_body, n_experts=n_experts),
        grid=(M // tm,),
        in_specs=[pl.BlockSpec((tm, D), lambda i: (i, 0)), _resident((1, D)),
                  _resident((D, LANES)), _resident((D, LANES)), _resident((tm, tm))],
        out_specs=[pl.BlockSpec((tm, LANES), lambda i: (i, 0)), pl.BlockSpec((8, LANES), lambda i: (0, 0))],
        out_shape=[jax.ShapeDtypeStruct((M, LANES), F32), jax.ShapeDtypeStruct((8, LANES), F32)],
        compiler_params=_params("arbitrary"),
        name="moe_router",
    )(x2, gn.reshape(1, D), wh, wl, tril)


def _combine_body(pos_ref, pos_next_ref, x_ref, meta_ref, y_ref, gfin_ref, o_ref, ybuf, sem, *, tm, n_tiles):
    i = pl.program_id(0)
    slot = i % 2

    def row_copy(idx_ref, k, r, s):
        return pltpu.make_async_copy(y_ref.at[pl.ds(idx_ref[0, 0, k * tm + r], 1)], ybuf.at[s, k, pl.ds(r, 1)],
                                     sem.at[s])

    @pl.when(i == 0)
    def _():
        def body(r, c):
            row_copy(pos_ref, 0, r, 0).start()
            row_copy(pos_ref, 1, r, 0).start()
            return c
        lax.fori_loop(0, tm, body, 0, unroll=8)

    @pl.when(i + 1 < n_tiles)
    def _():
        for r in range(tm):
            row_copy(pos_next_ref, 0, r, 1 - slot).start()
            row_copy(pos_next_ref, 1, r, 1 - slot).start()

    pltpu.make_async_copy(ybuf.at[slot], ybuf.at[slot], sem.at[slot]).wait()
    meta = meta_ref[...]
    lane = lax.broadcasted_iota(jnp.int32, meta.shape, 1)
    out = (x_ref[...] + _lane_col(meta, lane, META_W0) * ybuf[slot, 0]
           + _lane_col(meta, lane, META_W1) * ybuf[slot, 1])
    o_ref[...] = _rms(out, gfin_ref[...])


def _combine(x2, meta, y, pos0, pos1, g_final, tm=512):
    M, D = x2.shape
    tm = min(tm, M)
    n_tiles = M // tm
    pos = jnp.concatenate([pos0.reshape(n_tiles, 1, tm), pos1.reshape(n_tiles, 1, tm)], axis=2)
    return pl.pallas_call(
        functools.partial(_combine_body, tm=tm, n_tiles=n_tiles),
        grid=(n_tiles,),
        in_specs=[pl.BlockSpec((1, 1, 2 * tm), lambda i: (i, 0, 0), memory_space=pltpu.SMEM),
                  pl.BlockSpec((1, 1, 2 * tm), lambda i: (jnp.minimum(i + 1, n_tiles - 1), 0, 0),
                               memory_space=pltpu.SMEM),
                  pl.BlockSpec((tm, D), lambda i: (i, 0)),
                  pl.BlockSpec((tm, LANES), lambda i: (i, 0)),
                  pl.BlockSpec(memory_space=pl.ANY),
                  _resident((1, D))],
        out_specs=pl.BlockSpec((tm, D), lambda i: (i, 0)),
        out_shape=jax.ShapeDtypeStruct((M, D), F32),
        scratch_shapes=[pltpu.VMEM((2, 2, tm, D), F32), pltpu.SemaphoreType.DMA((2,))],
        compiler_params=_params("arbitrary"),
        name="moe_combine",
    )(pos, pos, x2, meta, y, g_final.reshape(1, D))


GATHER_AHEAD = 2


def _grouped_ffn_body(te_ref, nused_ref, src_ref, src_next_ref, src_ahead_ref, x_ref, gn_ref, wg_ref, wu_ref,
                      wd_ref, o_ref, xbuf, hn_ref, acc_ref, sem, *, tm, n_f):
    j = pl.program_id(0)
    f = pl.program_id(1)
    n_used = nused_ref[0]
    used = j < n_used
    gathered = j < n_used + GATHER_AHEAD
    slot = j % (GATHER_AHEAD + 1)
    slot_ahead = (j + GATHER_AHEAD) % (GATHER_AHEAD + 1)
    chunk = tm // (n_f + 1)

    def row_copy(idx_ref, r, s):
        return pltpu.make_async_copy(x_ref.at[pl.ds(idx_ref[0, 0, r], 1)], xbuf.at[s, pl.ds(r, 1)], sem.at[s])

    def start_chunk(first):
        for k in range(chunk):
            row_copy(src_ahead_ref, first + k, slot_ahead).start()

    @pl.when((j == 0) & (f == 0))
    def _():
        def body(r, c):
            row_copy(src_ref, r, 0).start()
            row_copy(src_next_ref, r, 1).start()
            return c
        lax.fori_loop(0, tm, body, 0, unroll=8)

    @pl.when(f == 0)
    def _():
        acc_ref[...] = jnp.zeros_like(acc_ref)

        @pl.when(gathered)
        def _():
            pltpu.make_async_copy(xbuf.at[slot], xbuf.at[slot], sem.at[slot]).wait()
            hn_ref[...] = _rms(xbuf[slot], gn_ref[...]).astype(BF16)

        @pl.when(used)
        def _():
            start_chunk(0)

    @pl.when(used)
    def _():
        hn = hn_ref[...]
        g = _dot(hn, wg_ref[0])
        u = _dot(hn, wu_ref[0])
        acc_ref[...] += _dot(((g * jax.nn.sigmoid(g)) * u).astype(BF16), wd_ref[0])
        start_chunk((f + 1) * chunk)

    @pl.when(f == n_f - 1)
    def _():
        o_ref[...] = acc_ref[...]


def _grouped_ffn(x2, src, gn, w_gate, w_up, w_down, tile_expert, n_used, tm, tf=512):
    P = src.shape[0]
    D = x2.shape[1]
    F = w_gate.shape[2]
    tf = min(tf, F)
    n_f = F // tf
    n_tiles = P // tm
    assert F % tf == 0 and tm % (n_f + 1) == 0
    src3 = src.reshape(n_tiles, 1, tm)

    def src_spec(ahead):
        return pl.BlockSpec((1, 1, tm), lambda j, f, te, nu: (jnp.minimum(j + ahead, n_tiles - 1), 0, 0),
                            memory_space=pltpu.SMEM)

    grid_spec = pltpu.PrefetchScalarGridSpec(
        num_scalar_prefetch=2,
        grid=(n_tiles, n_f),
        in_specs=[src_spec(a) for a in range(GATHER_AHEAD + 1)] + [
                  pl.BlockSpec(memory_space=pl.ANY),
                  pl.BlockSpec((1, D), lambda j, f, te, nu: (0, 0)),
                  pl.BlockSpec((1, D, tf), lambda j, f, te, nu: (te[j], 0, f)),
                  pl.BlockSpec((1, D, tf), lambda j, f, te, nu: (te[j], 0, f)),
                  pl.BlockSpec((1, tf, D), lambda j, f, te, nu: (te[j], f, 0))],
        out_specs=pl.BlockSpec((tm, D), lambda j, f, te, nu: (j, 0)),
        scratch_shapes=[pltpu.VMEM((GATHER_AHEAD + 1, tm, D), F32), pltpu.VMEM((tm, D), BF16),
                        pltpu.VMEM((tm, D), F32), pltpu.SemaphoreType.DMA((GATHER_AHEAD + 1,))])
    return pl.pallas_call(
        functools.partial(_grouped_ffn_body, tm=tm, n_f=n_f),
        grid_spec=grid_spec,
        out_shape=jax.ShapeDtypeStruct((P, D), F32),
        compiler_params=_params("arbitrary", "arbitrary"),
        name="moe_grouped_swiglu",
    )(tile_expert, n_used, *([src3] * (GATHER_AHEAD + 1)), x2, gn.reshape(1, D), w_gate.astype(BF16),
      w_up.astype(BF16), w_down.astype(BF16))


def _moe_layer(x2, gn, w_router, w_gate, w_up, w_down, g_final, tm=1024):
    M, D = x2.shape
    n_e = w_router.shape[1]
    tm = min(tm, M)
    meta, counts = _router(x2, gn, w_router)
    as_int = lambda col: meta[:, col].astype(jnp.int32)
    i0, i1, r0, r1 = as_int(META_I0), as_int(META_I1), as_int(META_R0), as_int(META_R1)
    padded = (counts[0, :n_e].astype(jnp.int32) + tm - 1) // tm * tm
    ends = jnp.cumsum(padded)
    offsets = ends - padded
    pos0 = offsets[i0] + r0
    pos1 = offsets[i1] + r1
    P = TOP_K * M + (n_e - 1 + GATHER_AHEAD) * tm
    tok = jnp.arange(M, dtype=jnp.int32)
    src = jnp.zeros((P,), jnp.int32).at[jnp.concatenate([pos0, pos1])].set(jnp.concatenate([tok, tok]))
    tile_start = jnp.arange(P // tm, dtype=jnp.int32) * tm
    tile_expert = jnp.minimum(jnp.searchsorted(ends, tile_start, side="right"), n_e - 1).astype(jnp.int32)
    n_used = (ends[n_e - 1:] // tm).astype(jnp.int32)
    y = _grouped_ffn(x2, src, gn, w_gate, w_up, w_down, tile_expert, n_used, tm)
    return _combine(x2, meta, y, pos0, pos1, g_final)


def _nsa_proj_body(x_ref, gn_ref, w_ref, cos_ref, sin_ref,
                   q_ref, kc_ref, vc_ref, ks_ref, vs_ref, kw_ref, vw_ref, g_ref, *, tm, tiles_per_seq):
    i = pl.program_id(0)
    hn = _rms(x_ref[...], gn_ref[...]).astype(BF16)
    proj = _dot(hn, w_ref[...])
    lane = lax.broadcasted_iota(jnp.int32, (tm, LANES), 1)
    first_half = (lane & (HEAD_DIM - 1)) < HEAD_DIM // 2
    cos = cos_ref[...]
    sin = sin_ref[...]

    def chunk(c):
        return proj[:, c * LANES:(c + 1) * LANES]

    def rope(v):
        rot = jnp.where(first_half, pltpu.roll(v, LANES - HEAD_DIM // 2, 1), pltpu.roll(v, HEAD_DIM // 2, 1))
        return v * cos + rot * sin

    def heads(v):
        return v[:, :HEAD_DIM], pltpu.roll(v, HEAD_DIM, 1)[:, :HEAD_DIM]

    c = 0
    for cc in range(N_HEADS // 2):
        lo, hi = heads(rope(chunk(c)) * Q_SCALE); c += 1
        q_ref[2 * cc] = lo.astype(BF16)
        q_ref[2 * cc + 1] = hi.astype(BF16)
    for cc in range(N_KV_GROUPS // 2):
        lo, hi = heads(rope(chunk(c))); c += 1
        kc_ref[2 * cc] = lo
        kc_ref[2 * cc + 1] = hi
    for cc in range(N_KV_GROUPS // 2):
        lo, hi = heads(chunk(c)); c += 1
        vc_ref[2 * cc] = lo
        vc_ref[2 * cc + 1] = hi
    pos = (i % tiles_per_seq) * tm + lax.broadcasted_iota(jnp.int32, (tm, 1), 0)
    onehot = (lane == pos // SEL_BLOCK).astype(BF16)
    for cc in range(N_KV_GROUPS // 2):
        v = rope(chunk(c)); c += 1
        for j, vv in enumerate((v, pltpu.roll(v, HEAD_DIM, 1))):
            ks_ref[2 * cc + j, :, 0:LANES] = jnp.where(lane < HEAD_DIM, vv, 0.0).astype(BF16)
            ks_ref[2 * cc + j, :, LANES:2 * LANES] = onehot
    ones_col = jnp.where(lane == HEAD_DIM, 1.0, 0.0)

    def store_values(ref):
        nonlocal c
        for cc in range(N_KV_GROUPS // 2):
            v = chunk(c); c += 1
            for j, vv in enumerate((v, pltpu.roll(v, HEAD_DIM, 1))):
                ref[2 * cc + j] = jnp.where(lane < HEAD_DIM, vv, ones_col).astype(BF16)

    store_values(vs_ref)
    for cc in range(N_KV_GROUPS // 2):
        lo, hi = heads(rope(chunk(c))); c += 1
        kw_ref[2 * cc] = lo.astype(BF16)
        kw_ref[2 * cc + 1] = hi.astype(BF16)
    store_values(vw_ref)
    g_ref[...] = jax.nn.sigmoid(chunk(c))


def _nsa_proj(x2, gn, w_in, S, tm=256):
    M, D = x2.shape
    H, G, DH = N_HEADS, N_KV_GROUPS, HEAD_DIM
    tm = min(tm, S)
    assert S % tm == 0
    n_in = w_in.shape[1]
    n_pad = -(-n_in // LANES) * LANES
    wp = jnp.pad(w_in, ((0, 0), (0, n_pad - n_in))).astype(BF16)
    half = DH // 2
    freqs = ROPE_THETA ** (-jnp.arange(half, dtype=F32) / half)
    ang = jnp.arange(S, dtype=F32)[:, None] * freqs[None, :]
    cos = jnp.tile(jnp.cos(ang), (1, 2 * LANES // DH))
    sin = jnp.tile(jnp.concatenate([-jnp.sin(ang), jnp.sin(ang)], axis=1), (1, LANES // DH))
    tiles_per_seq = S // tm
    hd = lambda n, dt: jax.ShapeDtypeStruct((n, M, DH), dt)
    hspec = lambda n: pl.BlockSpec((n, tm, DH), lambda i: (0, i, 0))
    vd = jax.ShapeDtypeStruct((G, M, LANES), BF16)
    vspec = pl.BlockSpec((G, tm, LANES), lambda i: (0, i, 0))
    return pl.pallas_call(
        functools.partial(_nsa_proj_body, tm=tm, tiles_per_seq=tiles_per_seq),
        grid=(M // tm,),
        in_specs=[pl.BlockSpec((tm, D), lambda i: (i, 0)), _resident((1, D)), _resident((D, n_pad)),
                  pl.BlockSpec((tm, LANES), lambda i: (i % tiles_per_seq, 0)),
                  pl.BlockSpec((tm, LANES), lambda i: (i % tiles_per_seq, 0))],
        out_specs=[hspec(H), hspec(G), hspec(G),
                   pl.BlockSpec((G, tm, 2 * LANES), lambda i: (0, i, 0)),
                   vspec, hspec(G), vspec,
                   pl.BlockSpec((tm, LANES), lambda i: (i, 0))],
        out_shape=[hd(H, BF16), hd(G, F32), hd(G, F32),
                   jax.ShapeDtypeStruct((G, M, 2 * LANES), BF16),
                   vd, hd(G, BF16), vd,
                   jax.ShapeDtypeStruct((M, LANES), F32)],
        compiler_params=_params("arbitrary"),
        name="nsa_proj",
    )(x2, gn.reshape(1, D), wp, cos, sin)


def _compress_body(kc_ref, vc_ref, pe_ref, w1_ref, w2_ref, ko_ref, vo_ref, *, nc):
    half = CMP_BLOCK // 2
    for kv, (src, dst) in enumerate(((kc_ref, ko_ref), (vc_ref, vo_ref))):
        top = jnp.zeros((nc, w1_ref.shape[2]), F32)
        bot = jnp.zeros((nc, w1_ref.shape[2]), F32)
        for j in range(half):
            xj = src[0, pl.ds(j, nc, stride=CMP_STRIDE), :]
            top = top + _dot((xj + pe_ref[kv, j:j + 1, :]).astype(BF16),
                             w1_ref[kv, j * HEAD_DIM:(j + 1) * HEAD_DIM, :])
            bot = bot + _dot((xj + pe_ref[kv, half + j:half + j + 1, :]).astype(BF16),
                             w1_ref[kv, (half + j) * HEAD_DIM:(half + j + 1) * HEAD_DIM, :])
        hid = top + pltpu.roll(bot, nc - 1, 0)
        dst[0] = _dot(_gelu_tanh(hid).astype(BF16), w2_ref[kv]).astype(BF16)


def _compress(kc, vc, pe, w1, w2, B, S):
    assert CMP_BLOCK == 2 * CMP_STRIDE
    G, M, DH = kc.shape
    nc = S // CMP_STRIDE
    spec_in = pl.BlockSpec((1, S, DH), lambda b, g: (g, b, 0))
    spec_out = pl.BlockSpec((1, nc, DH), lambda b, g: (g, b, 0))
    out = jax.ShapeDtypeStruct((G, B * nc, DH), BF16)
    return pl.pallas_call(
        functools.partial(_compress_body, nc=nc),
        grid=(B, G),
        in_specs=[spec_in, spec_in, _resident(pe.shape), _resident(w1.shape), _resident(w2.shape)],
        out_specs=[spec_out, spec_out],
        out_shape=[out, out],
        compiler_params=_params("arbitrary", "arbitrary"),
        name="nsa_compress",
    )(kc, vc, pe, w1.astype(BF16), w2.astype(BF16))


def _store_head(o_ref, oh, gates, g, h, branch, rows=slice(None)):
    lane = lax.broadcasted_iota(jnp.int32, gates.shape, 1)
    col = 3 * (HEADS_PER_GROUP * g + h) + branch
    o_ref[rows, h * HEAD_DIM:(h + 1) * HEAD_DIM] = (oh * _lane_col(gates, lane, col)).astype(o_ref.dtype)


def _normalized_head(acc):
    return acc[:, :HEAD_DIM] * (1.0 / acc[:, HEAD_DIM:HEAD_DIM + 1])


def _cmp_body(q_ref, kc_ref, vc_ref, gates_ref, ov_ref, o_ref, bias_ref, imp_ref, *, tq, nc, n_s, k_sel):
    g = pl.program_id(1)
    i = pl.program_id(2)
    gates = gates_ref[...]
    t = i * tq + lax.broadcasted_iota(jnp.int32, (tq, 1), 0)

    def attend(w):
        kc = kc_ref[0, 0:w, :]
        vc = vc_ref[0, 0:w, :]
        cend = lax.broadcasted_iota(jnp.int32, (1, w), 1) * CMP_STRIDE + (CMP_BLOCK - 1)
        mask = cend <= t
        ps = None
        for h in range(HEADS_PER_GROUP):
            s = lax.dot_general(q_ref[h], kc, _NT, preferred_element_type=F32)
            s = jnp.where(mask, s, MASK_VALUE)
            m = jnp.max(s, axis=1, keepdims=True)
            m = jnp.where(m > 0.5 * MASK_VALUE, m, 0.0)
            e = jnp.exp2(s - m)
            p = e * (1.0 / jnp.maximum(jnp.sum(e, axis=1, keepdims=True), 1e-30))
            _store_head(o_ref, _dot(p.astype(BF16), vc), gates, g, h, 0)
            ps = p if ps is None else ps + p
        ph = ps.astype(BF16)
        pl_ = (ps - ph.astype(F32)).astype(BF16)
        imp_ref[...] = _dot(ph, ov_ref[0:w, :]) + _dot(pl_, ov_ref[0:w, :])

    n_chunks = nc // LANES
    need = ((i + 1) * (tq // CMP_STRIDE) + LANES - 1) // LANES
    for k in range(1, n_chunks + 1):
        pl.when((need == k) if k < n_chunks else (need >= k))(functools.partial(attend, k * LANES))

    j = lax.broadcasted_iota(jnp.int32, (tq, LANES), 1)
    cur = t // SEL_BLOCK
    forced = (j == 0) | (j == cur) | (j == cur - 1)
    valid = j * SEL_BLOCK <= t
    score = jnp.where(forced, FORCE_SCORE, jnp.where(valid, imp_ref[...], -1.0))
    score = jnp.where(j < n_s, score, -jnp.inf)
    x = score.T
    blk_i = lax.broadcasted_iota(jnp.int32, (LANES, tq), 0)
    blk = blk_i.astype(F32)

    def topk(x, n_iter):
        for _ in range(n_iter):
            mx = jnp.max(x, axis=0, keepdims=True)
            idx = jnp.min(jnp.where(x == mx, blk, float(LANES)), axis=0, keepdims=True)
            x = jnp.where(blk == idx, -jnp.inf, x)
        sel = (x == -jnp.inf) & (blk_i < n_s)
        bias_ref[0] = jnp.where(sel, 0.0, MASK_VALUE).T.astype(BF16)

    n_forced = 3
    direct = (i * tq >= 2 * SEL_BLOCK) & (k_sel > n_forced)

    @pl.when(direct)
    def _():
        cur_t = (i * tq + lax.broadcasted_iota(jnp.int32, (1, tq), 1)) // SEL_BLOCK
        forced_t = (blk_i == 0) | (blk_i == cur_t) | (blk_i == cur_t - 1)
        topk(jnp.where(forced_t, -jnp.inf, x), k_sel - n_forced)

    @pl.when(jnp.logical_not(direct))
    def _():
        topk(x, k_sel)


def _overlap_matrix(nc, n_s):
    r = CMP_BLOCK // CMP_STRIDE
    qn = SEL_BLOCK // CMP_STRIDE
    m = np.zeros((nc, LANES), np.float32)
    n_c = nc - r + 1
    chunks = np.arange(n_c)[:, None] + np.arange(r)[None, :]
    np.add.at(m, (np.repeat(np.arange(n_c), r), (chunks // qn).ravel()), 1.0)
    return m


def _cmp_attention(q, kcmp, vcmp, gates, B, S, tq=1024):
    H, M, DH = q.shape
    G = N_KV_GROUPS
    tq = min(tq, S)
    nq = S // tq
    nc = S // CMP_STRIDE
    n_s = S // SEL_BLOCK
    assert n_s <= LANES and tq & (tq - 1) == 0
    k_sel = min(N_SEL, n_s)
    ov = jnp.asarray(_overlap_matrix(nc, n_s), BF16)
    return pl.pallas_call(
        functools.partial(_cmp_body, tq=tq, nc=nc, n_s=n_s, k_sel=k_sel),
        grid=(B, G, nq),
        in_specs=[pl.BlockSpec((HEADS_PER_GROUP, tq, DH), lambda b, g, i: (g, b * nq + i, 0)),
                  pl.BlockSpec((1, nc, DH), lambda b, g, i: (g, b, 0)),
                  pl.BlockSpec((1, nc, DH), lambda b, g, i: (g, b, 0)),
                  pl.BlockSpec((tq, LANES), lambda b, g, i: (b * nq + i, 0)),
                  _resident((nc, LANES))],
        out_specs=[pl.BlockSpec((tq, HEADS_PER_GROUP * DH), lambda b, g, i: (b * nq + i, g)),
                   pl.BlockSpec((1, tq, LANES), lambda b, g, i: (g, b * nq + i, 0))],
        out_shape=[jax.ShapeDtypeStruct((M, H * DH), BF16), jax.ShapeDtypeStruct((G, M, LANES), BF16)],
        scratch_shapes=[pltpu.VMEM((tq, LANES), F32)],
        compiler_params=_params("arbitrary", "arbitrary", "arbitrary"),
        name="nsa_cmp_select",
    )(q, kcmp, vcmp, gates, ov)


def _sel_body(q_ref, bias_ref, k_ref, v_ref, gates_ref, o_ref, qcat, m_ref, acc_ref, *, tq):
    g = pl.program_id(1)
    i = pl.program_id(2)
    hp = HEADS_PER_GROUP
    rows = hp * tq
    qcat[:, 0:LANES] = jnp.zeros((rows, LANES), BF16)
    qcat[:, 0:HEAD_DIM] = q_ref[...].reshape(rows, HEAD_DIM)
    bias = bias_ref[0]
    for h in range(hp):
        qcat[h * tq:(h + 1) * tq, LANES:2 * LANES] = bias
    m_ref[...] = jnp.full((rows, LANES), MASK_VALUE, F32)
    acc_ref[...] = jnp.zeros((rows, LANES), F32)

    def tile(j, width, causal):
        start = pl.multiple_of(j * tq, tq)
        kt = k_ref[0, pl.ds(start, width), :]
        vt = v_ref[0, pl.ds(start, width), :]
        for h in range(hp):
            r0 = h * tq
            s = lax.dot_general(qcat[r0:r0 + tq, :], kt, _NT, preferred_element_type=F32)
            if causal:
                r = lax.broadcasted_iota(jnp.int32, (tq, 1), 0)
                c = lax.broadcasted_iota(jnp.int32, (1, width), 1)
                s = jnp.where(c <= r, s, MASK_VALUE)
            m_prev = m_ref[r0:r0 + tq, :]
            m_new = jnp.maximum(m_prev, jnp.max(s, axis=1, keepdims=True))
            alpha = jnp.exp2(m_prev - m_new)
            p = jnp.exp2(s - jnp.concatenate([m_new] * (width // LANES), axis=1))
            acc_ref[r0:r0 + tq, :] = alpha * acc_ref[r0:r0 + tq, :] + _dot(p.astype(BF16), vt)
            m_ref[r0:r0 + tq, :] = m_new

    def wide_tile(jw, carry):
        tile(SEL_WIDE * jw, SEL_WIDE * tq, False)
        return carry

    n_wide = i // SEL_WIDE
    lax.fori_loop(0, n_wide, wide_tile, 0)
    for rem in range(1, SEL_WIDE):
        pl.when(i % SEL_WIDE == rem)(functools.partial(tile, SEL_WIDE * n_wide, rem * tq, False))

    tile(i, tq, True)
    gates = gates_ref[...]
    for h in range(hp):
        _store_head(o_ref, _normalized_head(acc_ref[h * tq:(h + 1) * tq, :]), gates, g, h, 1)


def _sel_attention(q, bias, kscat, vs, gates, B, S, tq=512):
    H, M, DH = q.shape
    G = N_KV_GROUPS
    tq = min(tq, S)
    nq = S // tq
    assert tq % LANES == 0 and tq % SEL_BLOCK == 0 and S % tq == 0
    rows = HEADS_PER_GROUP * tq
    return pl.pallas_call(
        functools.partial(_sel_body, tq=tq),
        grid=(B, G, nq),
        in_specs=[pl.BlockSpec((HEADS_PER_GROUP, tq, DH), lambda b, g, i: (g, b * nq + i, 0)),
                  pl.BlockSpec((1, tq, LANES), lambda b, g, i: (g, b * nq + i, 0)),
                  pl.BlockSpec((1, S, 2 * LANES), lambda b, g, i: (g, b, 0)),
                  pl.BlockSpec((1, S, LANES), lambda b, g, i: (g, b, 0)),
                  pl.BlockSpec((tq, LANES), lambda b, g, i: (b * nq + i, 0))],
        out_specs=pl.BlockSpec((tq, HEADS_PER_GROUP * DH), lambda b, g, i: (b * nq + i, g)),
        out_shape=jax.ShapeDtypeStruct((M, H * DH), BF16),
        scratch_shapes=[pltpu.VMEM((rows, 2 * LANES), BF16), pltpu.VMEM((rows, LANES), F32),
                        pltpu.VMEM((rows, LANES), F32)],
        compiler_params=_params("arbitrary", "arbitrary", "arbitrary"),
        name="nsa_selected",
    )(q, bias, kscat, vs, gates)


def _win_body(q_ref, k_ref, v_ref, gates_ref, o_ref, *, tq, n_back, n_sub):
    g = pl.program_id(1)
    r = lax.broadcasted_iota(jnp.int32, (tq, 1), 0)
    c = lax.broadcasted_iota(jnp.int32, (1, tq), 1)
    for sub in range(n_sub):
        i = pl.program_id(2) * n_sub + sub
        rows = slice(sub * tq, (sub + 1) * tq)
        gates = gates_ref[rows, :]
        tiles = []
        for back in range(n_back, -1, -1):
            jt = i - back
            start = pl.multiple_of(jnp.maximum(jt, 0) * tq, tq)
            d = r - c + back * tq
            ok = (d >= 0) & (d < WINDOW) & (jt >= 0)
            tiles.append((k_ref[0, pl.ds(start, tq), :], v_ref[0, pl.ds(start, tq), :], ok))
        for h in range(HEADS_PER_GROUP):
            q = q_ref[h, rows, :]
            scores = [jnp.where(ok, lax.dot_general(q, kt, _NT, preferred_element_type=F32), MASK_VALUE)
                      for kt, _, ok in tiles]
            m = functools.reduce(jnp.maximum, scores).max(axis=1, keepdims=True)
            acc = jnp.zeros((tq, LANES), F32)
            for s, (_, vt, _) in zip(scores, tiles):
                acc = acc + _dot(jnp.exp2(s - m).astype(BF16), vt)
            _store_head(o_ref, _normalized_head(acc), gates, g, h, 2, rows)


def _win_attention(q, kw, vw, gates, B, S, tq=256, n_sub=2):
    H, M, DH = q.shape
    G = N_KV_GROUPS
    tq = min(tq, S)
    n_sub = min(n_sub, S // tq)
    tile = tq
    tq = tile * n_sub
    nq = S // tq
    n_back = -(-WINDOW // tile)
    return pl.pallas_call(
        functools.partial(_win_body, tq=tile, n_back=n_back, n_sub=n_sub),
        grid=(B, G, nq),
        in_specs=[pl.BlockSpec((HEADS_PER_GROUP, tq, DH), lambda b, g, i: (g, b * nq + i, 0)),
                  pl.BlockSpec((1, S, DH), lambda b, g, i: (g, b, 0)),
                  pl.BlockSpec((1, S, LANES), lambda b, g, i: (g, b, 0)),
                  pl.BlockSpec((tq, LANES), lambda b, g, i: (b * nq + i, 0))],
        out_specs=pl.BlockSpec((tq, HEADS_PER_GROUP * DH), lambda b, g, i: (b * nq + i, g)),
        out_shape=jax.ShapeDtypeStruct((M, H * DH), BF16),
        compiler_params=_params("arbitrary", "arbitrary", "arbitrary"),
        name="nsa_window",
    )(q, kw, vw, gates)


def _outproj_body(x_ref, a_ref, b_ref, c_ref, w_ref, o_ref):
    o = a_ref[...].astype(F32) + b_ref[...].astype(F32) + c_ref[...].astype(F32)
    o_ref[...] = x_ref[...] + _dot(o.astype(BF16), w_ref[...])


def _outproj(x2, oc, os_, ow, w_out, tm=512):
    M, D = x2.shape
    K = w_out.shape[0]
    tm = min(tm, M)
    spec_o = pl.BlockSpec((tm, K), lambda i: (i, 0))
    return pl.pallas_call(
        _outproj_body,
        grid=(M // tm,),
        in_specs=[pl.BlockSpec((tm, D), lambda i: (i, 0)), spec_o, spec_o, spec_o, _resident((K, D))],
        out_specs=pl.BlockSpec((tm, D), lambda i: (i, 0)),
        out_shape=jax.ShapeDtypeStruct((M, D), F32),
        compiler_params=_params("arbitrary"),
        name="nsa_outproj",
    )(x2, oc, os_, ow, w_out.astype(BF16))


def _nsa_layer(x, gn, w_in, cmp_pe, cmp_w1, cmp_w2, w_out):
    B, S, D = x.shape
    x2 = x.reshape(B * S, D)
    q, kc, vc, kscat, vs, kw, vw, gates = _nsa_proj(x2, gn, w_in, S)
    kcmp, vcmp = _compress(kc, vc, cmp_pe, cmp_w1, cmp_w2, B, S)
    o_cmp, bias = _cmp_attention(q, kcmp, vcmp, gates, B, S)
    o_sel = _sel_attention(q, bias, kscat, vs, gates, B, S)
    o_win = _win_attention(q, kw, vw, gates, B, S)
    return _outproj(x2, o_cmp, o_sel, o_win, w_out).reshape(B, S, D)


def kernel(x, norm_mix, norm_ffn, norm_final, lru_w_in, lru_b_in, lru_conv_w, lru_conv_b, lru_w_a, lru_b_a, lru_w_i, lru_b_i, lru_lambda, lru_w_out, lru_b_out, nsa_w_in, nsa_cmp_pe, nsa_cmp_w1, nsa_cmp_w2, nsa_w_out, ffn_w_gate, ffn_w_up, ffn_w_down, moe_w_router, moe_w_gate, moe_w_up, moe_w_down):
    B, S, D = x.shape
    assert norm_mix.shape[0] == 2 and lru_w_in.shape[0] == 1 and nsa_w_in.shape[0] == 1
    x = _lru_layer(x, norm_mix[0], lru_w_in[0], lru_b_in[0], lru_conv_w[0], lru_conv_b[0], lru_w_a[0],
                   lru_b_a[0], lru_w_i[0], lru_b_i[0], lru_lambda[0], lru_w_out[0], lru_b_out[0])
    x2 = _ffn_layer(x.reshape(B * S, D), norm_ffn[0], ffn_w_gate, ffn_w_up, ffn_w_down)
    x2 = _nsa_layer(x2.reshape(B, S, D), norm_mix[1], nsa_w_in[0], nsa_cmp_pe[0], nsa_cmp_w1[0],
                    nsa_cmp_w2[0], nsa_w_out[0]).reshape(B * S, D)
    out = _moe_layer(x2, norm_ffn[1], moe_w_router[0], moe_w_gate[0], moe_w_up[0], moe_w_down[0], norm_final)
    return out.reshape(B, S, D)
```

```python
import functools

import numpy as np
import jax
import jax.numpy as jnp
from jax import lax
from jax.experimental import pallas as pl
from jax.experimental.pallas import tpu as pltpu

F32 = jnp.float32
BF16 = jnp.bfloat16

RMS_EPS = 1e-6
LRU_C = 8.0
CONV_WIDTH = 4
N_HEADS = 16
HEAD_DIM = 64
N_KV_GROUPS = 4
HEADS_PER_GROUP = N_HEADS // N_KV_GROUPS
CMP_BLOCK = 32
CMP_STRIDE = 16
SEL_BLOCK = 64
N_SEL = 16
WINDOW = 512
FORCE_SCORE = 1.0e4
ROPE_THETA = 10000.0
TOP_K = 2

LANES = 128
SUBLANES = 8
MXU_WIDTH = 256
MASK_VALUE = -1.0e30
VMEM_LIMIT_BYTES = 56 * 1024 * 1024

_NT = (((1,), (1,)), ((), ()))
Q_SCALE = HEAD_DIM ** -0.5 * 1.4426950408889634
SEL_WIDE = 4


def _params(*semantics):
    return pltpu.CompilerParams(dimension_semantics=semantics, vmem_limit_bytes=VMEM_LIMIT_BYTES)


def _resident(shape):
    zeros = (0,) * len(shape)
    return pl.BlockSpec(shape, lambda *_: zeros, pipeline_mode=pl.Buffered(1))


def _rms(x, g):
    return x * lax.rsqrt(jnp.mean(x * x, axis=-1, keepdims=True) + RMS_EPS) * g


def _dot(a, b):
    return jnp.dot(a, b, preferred_element_type=F32)


def _gelu_tanh(x):
    two_z = x * (2.0 * 0.7978845608028654 + (2.0 * 0.7978845608028654 * 0.044715) * (x * x))
    return x * jax.nn.sigmoid(two_z)


def _block_diag_dot(xb, w_ref, col0, n, bw):
    outs = []
    for c0 in range(0, n, MXU_WIDTH):
        w = min(MXU_WIDTH, n - c0)
        lo = (c0 // bw) * bw // LANES * LANES
        hi = min(n, -(-(((c0 + w - 1) // bw + 1) * bw) // LANES) * LANES)
        outs.append(_dot(xb[:, lo:hi], w_ref[lo:hi, col0 + c0:col0 + c0 + w]))
    return jnp.concatenate(outs, axis=1)


def _lane_col(vals, lane, idx):
    return jnp.sum(jnp.where(lane == idx, vals, 0.0), axis=1, keepdims=True)


def _lru_body(x_ref, gn_ref, win_ref, bin_ref, cw_ref, cb_ref, wg_ref, bg_ref, lam_ref,
              wout_ref, bout_ref, o_ref, xbuf, hcar, *, ts, dr, bw):
    t = pl.program_id(1)

    @pl.when(t == 0)
    def _():
        xbuf[0:8, :] = jnp.zeros((8, dr), F32)
        hcar[...] = jnp.zeros_like(hcar)

    x = x_ref[0]
    hn = _rms(x, gn_ref[...]).astype(BF16)
    proj = _dot(hn, win_ref[...]) + bin_ref[...]
    gate = _gelu_tanh(proj[:, :dr])
    xr = proj[:, dr:]

    xbuf[8:8 + ts, :] = xr
    xc = cb_ref[...] + xr * cw_ref[CONV_WIDTH - 1:CONV_WIDTH, :]
    for lag in range(1, CONV_WIDTH):
        k = CONV_WIDTH - 1 - lag
        xc = xc + xbuf[pl.ds(8 - lag, ts), :] * cw_ref[k:k + 1, :]
    xbuf[0:8, :] = xbuf[ts:ts + 8, :]

    xcb = xc.astype(BF16)
    r = jax.nn.sigmoid(_block_diag_dot(xcb, wg_ref, 0, dr, bw) + bg_ref[:, :dr])
    i = jax.nn.sigmoid(_block_diag_dot(xcb, wg_ref, dr, dr, bw) + bg_ref[:, dr:])
    z = -lam_ref[...]
    softplus = jnp.maximum(z, 0.0) + jnp.log(1.0 + jnp.exp(-jnp.abs(z)))
    log_a = (-LRU_C * r) * softplus
    a = jnp.exp(log_a)
    mult = jnp.sqrt(1.0 - a * a)
    row = lax.broadcasted_iota(jnp.int32, (ts, 1), 0)
    mult = jnp.where((row == 0) & (t == 0), 1.0, mult)
    u = mult * (i * xc)

    sub = row & (SUBLANES - 1)
    for shift in (1, 2, 4):
        keep = sub >= shift
        a_prev = jnp.where(keep, pltpu.roll(a, shift, 0), 1.0)
        u_prev = jnp.where(keep, pltpu.roll(u, shift, 0), 0.0)
        u = a * u_prev + u
        a = a * a_prev
    carry = hcar[...]
    groups = []
    for k in range(ts // SUBLANES):
        rows = slice(k * SUBLANES, (k + 1) * SUBLANES)
        hk = u[rows] + a[rows] * carry
        carry = hk[SUBLANES - 1:SUBLANES]
        groups.append(hk)
    h = jnp.concatenate(groups, axis=0)
    hcar[...] = carry

    y = (h * gate).astype(BF16)
    o_ref[0] = x + _dot(y, wout_ref[...]) + bout_ref[...]


def _lru_layer(x, gn, w_in, b_in, conv_w, conv_b, w_a, b_a, w_i, b_i, lam, w_out, b_out):
    B, S, D = x.shape
    dr = w_out.shape[0]
    ts = min(256, S)
    assert S % ts == 0 and ts % 8 == 0
    wg = jnp.concatenate([jax.scipy.linalg.block_diag(*w_a), jax.scipy.linalg.block_diag(*w_i)], axis=1)
    row = lambda v: v.reshape(1, -1)
    body = functools.partial(_lru_body, ts=ts, dr=dr, bw=w_a.shape[1])
    return pl.pallas_call(
        body,
        grid=(B, S // ts),
        in_specs=[
            pl.BlockSpec((1, ts, D), lambda b, t: (b, t, 0)),
            _resident((1, D)),
            _resident((D, 2 * dr)), _resident((1, 2 * dr)),
            _resident((CONV_WIDTH, dr)), _resident((1, dr)),
            _resident((dr, 2 * dr)), _resident((1, 2 * dr)),
            _resident((1, dr)),
            _resident((dr, D)), _resident((1, D)),
        ],
        out_specs=pl.BlockSpec((1, ts, D), lambda b, t: (b, t, 0)),
        out_shape=jax.ShapeDtypeStruct((B, S, D), F32),
        scratch_shapes=[pltpu.VMEM((ts + 8, dr), F32), pltpu.VMEM((1, dr), F32)],
        compiler_params=_params("arbitrary", "arbitrary"),
        name="lru_mixer",
    )(x, row(gn), w_in.astype(BF16), row(b_in), conv_w, row(conv_b), wg.astype(BF16),
      row(jnp.concatenate([b_a, b_i])), row(lam), w_out.astype(BF16), row(b_out))


def _ffn_body(*refs, n_e, n_f, use_gates, final_norm):
    x_ref, gn_ref = refs[0], refs[1]
    k = 2
    gates_ref = gfin_ref = None
    if use_gates:
        gates_ref = refs[k]; k += 1
    wg_ref, wu_ref, wd_ref = refs[k:k + 3]; k += 3
    if final_norm:
        gfin_ref = refs[k]; k += 1
    o_ref, hn_ref, acc_ref = refs[k:k + 3]
    e = pl.program_id(1)
    f = pl.program_id(2)

    @pl.when((e == 0) & (f == 0))
    def _():
        hn_ref[...] = _rms(x_ref[...], gn_ref[...]).astype(BF16)
        acc_ref[...] = jnp.zeros_like(acc_ref)

    hn = hn_ref[...]
    g = _dot(hn, wg_ref[0])
    u = _dot(hn, wu_ref[0])
    act = (g * jax.nn.sigmoid(g)) * u
    if use_gates:
        gates = gates_ref[...]
        lane = lax.broadcasted_iota(jnp.int32, gates.shape, 1)
        act = act * _lane_col(gates, lane, e)
    acc_ref[...] += _dot(act.astype(BF16), wd_ref[0])

    @pl.when((e == n_e - 1) & (f == n_f - 1))
    def _():
        out = x_ref[...] + acc_ref[...]
        if final_norm:
            out = _rms(out, gfin_ref[...])
        o_ref[...] = out


def _ffn_layer(x2, gn, w_gate, w_up, w_down, gates=None, g_final=None, tm=1024, tf=512):
    M, D = x2.shape
    n_e, _, F = w_gate.shape
    tm = min(tm, M)
    tf = min(tf, F)
    assert M % tm == 0 and F % tf == 0
    n_f = F // tf
    use_gates = gates is not None
    final_norm = g_final is not None
    in_specs = [pl.BlockSpec((tm, D), lambda i, e, f: (i, 0)), _resident((1, D))]
    args = [x2, gn.reshape(1, D)]
    if use_gates:
        in_specs.append(pl.BlockSpec((tm, LANES), lambda i, e, f: (i, 0)))
        args.append(gates)
    in_specs += [
        pl.BlockSpec((1, D, tf), lambda i, e, f: (e, 0, f)),
        pl.BlockSpec((1, D, tf), lambda i, e, f: (e, 0, f)),
        pl.BlockSpec((1, tf, D), lambda i, e, f: (e, f, 0)),
    ]
    args += [w_gate.astype(BF16), w_up.astype(BF16), w_down.astype(BF16)]
    if final_norm:
        in_specs.append(_resident((1, D)))
        args.append(g_final.reshape(1, D))
    body = functools.partial(_ffn_body, n_e=n_e, n_f=n_f, use_gates=use_gates, final_norm=final_norm)
    return pl.pallas_call(
        body,
        grid=(M // tm, n_e, n_f),
        in_specs=in_specs,
        out_specs=pl.BlockSpec((tm, D), lambda i, e, f: (i, 0)),
        out_shape=jax.ShapeDtypeStruct((M, D), F32),
        scratch_shapes=[pltpu.VMEM((tm, D), BF16), pltpu.VMEM((tm, D), F32)],
        compiler_params=_params("arbitrary", "arbitrary", "arbitrary"),
        name="moe_swiglu" if use_gates else "dense_swiglu",
    )(*args)


def _router_body(x_ref, gn_ref, wh_ref, wl_ref, tril_ref, o_ref, cnt_ref, *, n_experts):
    @pl.when(pl.program_id(0) == 0)
    def _():
        cnt_ref[...] = jnp.zeros_like(cnt_ref)

    hn = _rms(x_ref[...], gn_ref[...])
    hh = hn.astype(BF16)
    hl = (hn - hh.astype(F32)).astype(BF16)
    wh = wh_ref[...]
    logits = _dot(hh, wh) + _dot(hh, wl_ref[...]) + _dot(hl, wh)
    lane = lax.broadcasted_iota(jnp.int32, logits.shape, 1)
    lg = jnp.where(lane < n_experts, logits, -jnp.inf)
    m0 = jnp.max(lg, axis=1, keepdims=True)
    i0 = jnp.min(jnp.where(lg == m0, lane, LANES), axis=1, keepdims=True)
    lg = jnp.where(lane == i0, -jnp.inf, lg)
    m1 = jnp.max(lg, axis=1, keepdims=True)
    i1 = jnp.min(jnp.where(lg == m1, lane, LANES), axis=1, keepdims=True)
    e1 = jnp.exp(m1 - m0)
    w0 = 1.0 / (1.0 + e1)
    routed = ((lane == i0) | (lane == i1)).astype(F32)
    incl = _dot(tril_ref[...], routed.astype(BF16))
    excl = incl - routed + cnt_ref[0:1, :]

    def put(col, v):
        return jnp.where(lane == col, v, 0.0)

    o_ref[...] = (put(META_I0, i0.astype(F32)) + put(META_I1, i1.astype(F32)) + put(META_W0, w0)
                  + put(META_W1, e1 * w0) + put(META_R0, _lane_col(excl, lane, i0))
                  + put(META_R1, _lane_col(excl, lane, i1)))
    cnt_ref[...] = jnp.broadcast_to(cnt_ref[0:1, :] + incl[incl.shape[0] - 1:, :], cnt_ref.shape)


META_I0, META_I1, META_W0, META_W1, META_R0, META_R1 = range(6)


def _router(x2, gn, w_router, tm=512):
    M, D = x2.shape
    n_experts = w_router.shape[1]
    tm = min(tm, M)
    wpad = jnp.pad(w_router, ((0, 0), (0, LANES - n_experts)))
    wh = wpad.astype(BF16)
    wl = (wpad - wh.astype(F32)).astype(BF16)
    tril = jnp.tril(jnp.ones((tm, tm), BF16))
    return pl.pallas_call(
        functools.partial(_router_body, n_experts=n_experts),
        grid=(M // tm,),
        in_specs=[pl.BlockSpec((tm, D), lambda i: (i, 0)), _resident((1, D)),
                  _resident((D, LANES)), _resident((D, LANES)), _resident((tm, tm))],
        out_specs=[pl.BlockSpec((tm, LANES), lambda i: (i, 0)), pl.BlockSpec((8, LANES), lambda i: (0, 0))],
        out_shape=[jax.ShapeDtypeStruct((M, LANES), F32), jax.ShapeDtypeStruct((8, LANES), F32)],
        compiler_params=_params("arbitrary"),
        name="moe_router",
    )(x2, gn.reshape(1, D), wh, wl, tril)


def _combine_body(pos_ref, pos_next_ref, x_ref, meta_ref, y_ref, gfin_ref, o_ref, ybuf, sem, *, tm, n_tiles):
    i = pl.program_id(0)
    slot = i % 2

    def row_copy(idx_ref, k, r, s):
        return pltpu.make_async_copy(y_ref.at[pl.ds(idx_ref[0, 0, k * tm + r], 1)], ybuf.at[s, k, pl.ds(r, 1)],
                                     sem.at[s])

    @pl.when(i == 0)
    def _():
        def body(r, c):
            row_copy(pos_ref, 0, r, 0).start()
            row_copy(pos_ref, 1, r, 0).start()
            return c
        lax.fori_loop(0, tm, body, 0, unroll=8)

    @pl.when(i + 1 < n_tiles)
    def _():
        for r in range(tm):
            row_copy(pos_next_ref, 0, r, 1 - slot).start(priority=0)
            row_copy(pos_next_ref, 1, r, 1 - slot).start(priority=1)

    pltpu.make_async_copy(ybuf.at[slot], ybuf.at[slot], sem.at[slot]).wait()
    meta = meta_ref[...]
    lane = lax.broadcasted_iota(jnp.int32, meta.shape, 1)
    out = (x_ref[...] + _lane_col(meta, lane, META_W0) * ybuf[slot, 0]
           + _lane_col(meta, lane, META_W1) * ybuf[slot, 1])
    o_ref[...] = _rms(out, gfin_ref[...])


def _combine(x2, meta, y, pos0, pos1, g_final, tm=512):
    M, D = x2.shape
    tm = min(tm, M)
    n_tiles = M // tm
    pos = jnp.concatenate([pos0.reshape(n_tiles, 1, tm), pos1.reshape(n_tiles, 1, tm)], axis=2)
    return pl.pallas_call(
        functools.partial(_combine_body, tm=tm, n_tiles=n_tiles),
        grid=(n_tiles,),
        in_specs=[pl.BlockSpec((1, 1, 2 * tm), lambda i: (i, 0, 0), memory_space=pltpu.SMEM),
                  pl.BlockSpec((1, 1, 2 * tm), lambda i: (jnp.minimum(i + 1, n_tiles - 1), 0, 0),
                               memory_space=pltpu.SMEM),
                  pl.BlockSpec((tm, D), lambda i: (i, 0)),
                  pl.BlockSpec((tm, LANES), lambda i: (i, 0)),
                  pl.BlockSpec(memory_space=pl.ANY),
                  _resident((1, D))],
        out_specs=pl.BlockSpec((tm, D), lambda i: (i, 0)),
        out_shape=jax.ShapeDtypeStruct((M, D), F32),
        scratch_shapes=[pltpu.VMEM((2, 2, tm, D), F32), pltpu.SemaphoreType.DMA((2,))],
        compiler_params=_params("arbitrary"),
        name="moe_combine",
    )(pos, pos, x2, meta, y, g_final.reshape(1, D))


GATHER_AHEAD = 2


def _grouped_ffn_body(te_ref, nused_ref, src_ref, src_next_ref, src_ahead_ref, x_ref, gn_ref, wg_ref, wu_ref,
                      wd_ref, o_ref, xbuf, hn_ref, acc_ref, sem, *, tm, n_f):
    j = pl.program_id(0)
    f = pl.program_id(1)
    n_used = nused_ref[0]
    used = j < n_used
    gathered = j < n_used + GATHER_AHEAD
    slot = j % (GATHER_AHEAD + 1)
    slot_ahead = (j + GATHER_AHEAD) % (GATHER_AHEAD + 1)
    chunk = tm // (n_f + 1)

    def row_copy(idx_ref, r, s):
        return pltpu.make_async_copy(x_ref.at[pl.ds(idx_ref[0, 0, r], 1)], xbuf.at[s, pl.ds(r, 1)], sem.at[s])

    def start_chunk(first):
        for k in range(chunk):
            row_copy(src_ahead_ref, first + k, slot_ahead).start()

    @pl.when((j == 0) & (f == 0))
    def _():
        def body(r, c):
            row_copy(src_ref, r, 0).start()
            row_copy(src_next_ref, r, 1).start()
            return c
        lax.fori_loop(0, tm, body, 0, unroll=8)

    @pl.when(f == 0)
    def _():
        acc_ref[...] = jnp.zeros_like(acc_ref)

        @pl.when(gathered)
        def _():
            pltpu.make_async_copy(xbuf.at[slot], xbuf.at[slot], sem.at[slot]).wait()
            hn_ref[...] = _rms(xbuf[slot], gn_ref[...]).astype(BF16)

        @pl.when(used)
        def _():
            start_chunk(0)

    @pl.when(used)
    def _():
        hn = hn_ref[...]
        g = _dot(hn, wg_ref[0])
        u = _dot(hn, wu_ref[0])
        acc_ref[...] += _dot(((g * jax.nn.sigmoid(g)) * u).astype(BF16), wd_ref[0])
        start_chunk((f + 1) * chunk)

    @pl.when(f == n_f - 1)
    def _():
        o_ref[...] = acc_ref[...]


def _grouped_ffn(x2, src, gn, w_gate, w_up, w_down, tile_expert, n_used, tm, tf=512):
    P = src.shape[0]
    D = x2.shape[1]
    F = w_gate.shape[2]
    tf = min(tf, F)
    n_f = F // tf
    n_tiles = P // tm
    assert F % tf == 0 and tm % (n_f + 1) == 0
    src3 = src.reshape(n_tiles, 1, tm)

    def src_spec(ahead):
        return pl.BlockSpec((1, 1, tm), lambda j, f, te, nu: (jnp.minimum(j + ahead, n_tiles - 1), 0, 0),
                            memory_space=pltpu.SMEM)

    grid_spec = pltpu.PrefetchScalarGridSpec(
        num_scalar_prefetch=2,
        grid=(n_tiles, n_f),
        in_specs=[src_spec(a) for a in range(GATHER_AHEAD + 1)] + [
                  pl.BlockSpec(memory_space=pl.ANY),
                  pl.BlockSpec((1, D), lambda j, f, te, nu: (0, 0)),
                  pl.BlockSpec((1, D, tf), lambda j, f, te, nu: (te[j], 0, f)),
                  pl.BlockSpec((1, D, tf), lambda j, f, te, nu: (te[j], 0, f)),
                  pl.BlockSpec((1, tf, D), lambda j, f, te, nu: (te[j], f, 0))],
        out_specs=pl.BlockSpec((tm, D), lambda j, f, te, nu: (j, 0)),
        scratch_shapes=[pltpu.VMEM((GATHER_AHEAD + 1, tm, D), F32), pltpu.VMEM((tm, D), BF16),
                        pltpu.VMEM((tm, D), F32), pltpu.SemaphoreType.DMA((GATHER_AHEAD + 1,))])
    return pl.pallas_call(
        functools.partial(_grouped_ffn_body, tm=tm, n_f=n_f),
        grid_spec=grid_spec,
        out_shape=jax.ShapeDtypeStruct((P, D), F32),
        compiler_params=_params("arbitrary", "arbitrary"),
        name="moe_grouped_swiglu",
    )(tile_expert, n_used, *([src3] * (GATHER_AHEAD + 1)), x2, gn.reshape(1, D), w_gate.astype(BF16),
      w_up.astype(BF16), w_down.astype(BF16))


def _moe_layer(x2, gn, w_router, w_gate, w_up, w_down, g_final, tm=1024):
    M, D = x2.shape
    n_e = w_router.shape[1]
    tm = min(tm, M)
    meta, counts = _router(x2, gn, w_router)
    as_int = lambda col: meta[:, col].astype(jnp.int32)
    i0, i1, r0, r1 = as_int(META_I0), as_int(META_I1), as_int(META_R0), as_int(META_R1)
    padded = (counts[0, :n_e].astype(jnp.int32) + tm - 1) // tm * tm
    ends = jnp.cumsum(padded)
    offsets = ends - padded
    pos0 = offsets[i0] + r0
    pos1 = offsets[i1] + r1
    P = TOP_K * M + (n_e - 1 + GATHER_AHEAD) * tm
    tok = jnp.arange(M, dtype=jnp.int32)
    src = jnp.zeros((P,), jnp.int32).at[jnp.concatenate([pos0, pos1])].set(jnp.concatenate([tok, tok]))
    tile_start = jnp.arange(P // tm, dtype=jnp.int32) * tm
    tile_expert = jnp.minimum(jnp.searchsorted(ends, tile_start, side="right"), n_e - 1).astype(jnp.int32)
    n_used = (ends[n_e - 1:] // tm).astype(jnp.int32)
    y = _grouped_ffn(x2, src, gn, w_gate, w_up, w_down, tile_expert, n_used, tm)
    return _combine(x2, meta, y, pos0, pos1, g_final)


def _nsa_proj_body(x_ref, gn_ref, w_ref, cos_ref, sin_ref,
                   q_ref, kc_ref, vc_ref, ks_ref, vs_ref, kw_ref, vw_ref, g_ref, *, tm, tiles_per_seq):
    i = pl.program_id(0)
    hn = _rms(x_ref[...], gn_ref[...]).astype(BF16)
    proj = _dot(hn, w_ref[...])
    lane = lax.broadcasted_iota(jnp.int32, (tm, LANES), 1)
    first_half = (lane & (HEAD_DIM - 1)) < HEAD_DIM // 2
    cos = cos_ref[...]
    sin = sin_ref[...]

    def chunk(c):
        return proj[:, c * LANES:(c + 1) * LANES]

    def rope(v):
        rot = jnp.where(first_half, pltpu.roll(v, LANES - HEAD_DIM // 2, 1), pltpu.roll(v, HEAD_DIM // 2, 1))
        return v * cos + rot * sin

    def heads(v):
        return v[:, :HEAD_DIM], pltpu.roll(v, HEAD_DIM, 1)[:, :HEAD_DIM]

    c = 0
    for cc in range(N_HEADS // 2):
        lo, hi = heads(rope(chunk(c)) * Q_SCALE); c += 1
        q_ref[2 * cc] = lo.astype(BF16)
        q_ref[2 * cc + 1] = hi.astype(BF16)
    for cc in range(N_KV_GROUPS // 2):
        lo, hi = heads(rope(chunk(c))); c += 1
        kc_ref[2 * cc] = lo
        kc_ref[2 * cc + 1] = hi
    for cc in range(N_KV_GROUPS // 2):
        lo, hi = heads(chunk(c)); c += 1
        vc_ref[2 * cc] = lo
        vc_ref[2 * cc + 1] = hi
    pos = (i % tiles_per_seq) * tm + lax.broadcasted_iota(jnp.int32, (tm, 1), 0)
    onehot = (lane == pos // SEL_BLOCK).astype(BF16)
    for cc in range(N_KV_GROUPS // 2):
        v = rope(chunk(c)); c += 1
        for j, vv in enumerate((v, pltpu.roll(v, HEAD_DIM, 1))):
            ks_ref[2 * cc + j, :, 0:LANES] = jnp.where(lane < HEAD_DIM, vv, 0.0).astype(BF16)
            ks_ref[2 * cc + j, :, LANES:2 * LANES] = onehot
    ones_col = jnp.where(lane == HEAD_DIM, 1.0, 0.0)

    def store_values(ref):
        nonlocal c
        for cc in range(N_KV_GROUPS // 2):
            v = chunk(c); c += 1
            for j, vv in enumerate((v, pltpu.roll(v, HEAD_DIM, 1))):
                ref[2 * cc + j] = jnp.where(lane < HEAD_DIM, vv, ones_col).astype(BF16)

    store_values(vs_ref)
    for cc in range(N_KV_GROUPS // 2):
        lo, hi = heads(rope(chunk(c))); c += 1
        kw_ref[2 * cc] = lo.astype(BF16)
        kw_ref[2 * cc + 1] = hi.astype(BF16)
    store_values(vw_ref)
    g_ref[...] = jax.nn.sigmoid(chunk(c))


def _nsa_proj(x2, gn, w_in, S, tm=256):
    M, D = x2.shape
    H, G, DH = N_HEADS, N_KV_GROUPS, HEAD_DIM
    tm = min(tm, S)
    assert S % tm == 0
    n_in = w_in.shape[1]
    n_pad = -(-n_in // LANES) * LANES
    wp = jnp.pad(w_in, ((0, 0), (0, n_pad - n_in))).astype(BF16)
    half = DH // 2
    freqs = ROPE_THETA ** (-jnp.arange(half, dtype=F32) / half)
    ang = jnp.arange(S, dtype=F32)[:, None] * freqs[None, :]
    cos = jnp.tile(jnp.cos(ang), (1, 2 * LANES // DH))
    sin = jnp.tile(jnp.concatenate([-jnp.sin(ang), jnp.sin(ang)], axis=1), (1, LANES // DH))
    tiles_per_seq = S // tm
    hd = lambda n, dt: jax.ShapeDtypeStruct((n, M, DH), dt)
    hspec = lambda n: pl.BlockSpec((n, tm, DH), lambda i: (0, i, 0))
    vd = jax.ShapeDtypeStruct((G, M, LANES), BF16)
    vspec = pl.BlockSpec((G, tm, LANES), lambda i: (0, i, 0))
    return pl.pallas_call(
        functools.partial(_nsa_proj_body, tm=tm, tiles_per_seq=tiles_per_seq),
        grid=(M // tm,),
        in_specs=[pl.BlockSpec((tm, D), lambda i: (i, 0)), _resident((1, D)), _resident((D, n_pad)),
                  pl.BlockSpec((tm, LANES), lambda i: (i % tiles_per_seq, 0)),
                  pl.BlockSpec((tm, LANES), lambda i: (i % tiles_per_seq, 0))],
        out_specs=[hspec(H), hspec(G), hspec(G),
                   pl.BlockSpec((G, tm, 2 * LANES), lambda i: (0, i, 0)),
                   vspec, hspec(G), vspec,
                   pl.BlockSpec((tm, LANES), lambda i: (i, 0))],
        out_shape=[hd(H, BF16), hd(G, F32), hd(G, F32),
                   jax.ShapeDtypeStruct((G, M, 2 * LANES), BF16),
                   vd, hd(G, BF16), vd,
                   jax.ShapeDtypeStruct((M, LANES), F32)],
        compiler_params=_params("arbitrary"),
        name="nsa_proj",
    )(x2, gn.reshape(1, D), wp, cos, sin)


def _compress_body(kc_ref, vc_ref, pe_ref, w1_ref, w2_ref, ko_ref, vo_ref, *, nc):
    half = CMP_BLOCK // 2
    for kv, (src, dst) in enumerate(((kc_ref, ko_ref), (vc_ref, vo_ref))):
        top = jnp.zeros((nc, w1_ref.shape[2]), F32)
        bot = jnp.zeros((nc, w1_ref.shape[2]), F32)
        for j in range(half):
            xj = src[0, pl.ds(j, nc, stride=CMP_STRIDE), :]
            top = top + _dot((xj + pe_ref[kv, j:j + 1, :]).astype(BF16),
                             w1_ref[kv, j * HEAD_DIM:(j + 1) * HEAD_DIM, :])
            bot = bot + _dot((xj + pe_ref[kv, half + j:half + j + 1, :]).astype(BF16),
                             w1_ref[kv, (half + j) * HEAD_DIM:(half + j + 1) * HEAD_DIM, :])
        hid = top + pltpu.roll(bot, nc - 1, 0)
        dst[0] = _dot(_gelu_tanh(hid).astype(BF16), w2_ref[kv]).astype(BF16)


def _compress(kc, vc, pe, w1, w2, B, S):
    assert CMP_BLOCK == 2 * CMP_STRIDE
    G, M, DH = kc.shape
    nc = S // CMP_STRIDE
    spec_in = pl.BlockSpec((1, S, DH), lambda b, g: (g, b, 0))
    spec_out = pl.BlockSpec((1, nc, DH), lambda b, g: (g, b, 0))
    out = jax.ShapeDtypeStruct((G, B * nc, DH), BF16)
    return pl.pallas_call(
        functools.partial(_compress_body, nc=nc),
        grid=(B, G),
        in_specs=[spec_in, spec_in, _resident(pe.shape), _resident(w1.shape), _resident(w2.shape)],
        out_specs=[spec_out, spec_out],
        out_shape=[out, out],
        compiler_params=_params("arbitrary", "arbitrary"),
        name="nsa_compress",
    )(kc, vc, pe, w1.astype(BF16), w2.astype(BF16))


def _store_head(o_ref, oh, gates, g, h, branch, rows=slice(None)):
    lane = lax.broadcasted_iota(jnp.int32, gates.shape, 1)
    col = 3 * (HEADS_PER_GROUP * g + h) + branch
    o_ref[rows, h * HEAD_DIM:(h + 1) * HEAD_DIM] = (oh * _lane_col(gates, lane, col)).astype(o_ref.dtype)


def _normalized_head(acc):
    return acc[:, :HEAD_DIM] * (1.0 / acc[:, HEAD_DIM:HEAD_DIM + 1])


def _cmp_body(q_ref, kc_ref, vc_ref, gates_ref, ov_ref, o_ref, bias_ref, imp_ref, *, tq, nc, n_s, k_sel):
    g = pl.program_id(1)
    i = pl.program_id(2)
    gates = gates_ref[...]
    t = i * tq + lax.broadcasted_iota(jnp.int32, (tq, 1), 0)

    def attend(w):
        kc = kc_ref[0, 0:w, :]
        vc = vc_ref[0, 0:w, :]
        cend = lax.broadcasted_iota(jnp.int32, (1, w), 1) * CMP_STRIDE + (CMP_BLOCK - 1)
        mask = cend <= t
        ps = None
        for h in range(HEADS_PER_GROUP):
            s = lax.dot_general(q_ref[h], kc, _NT, preferred_element_type=F32)
            s = jnp.where(mask, s, MASK_VALUE)
            m = jnp.max(s, axis=1, keepdims=True)
            m = jnp.where(m > 0.5 * MASK_VALUE, m, 0.0)
            e = jnp.exp2(s - m)
            p = e * (1.0 / jnp.maximum(jnp.sum(e, axis=1, keepdims=True), 1e-30))
            _store_head(o_ref, _dot(p.astype(BF16), vc), gates, g, h, 0)
            ps = p if ps is None else ps + p
        ph = ps.astype(BF16)
        pl_ = (ps - ph.astype(F32)).astype(BF16)
        imp_ref[...] = _dot(ph, ov_ref[0:w, :]) + _dot(pl_, ov_ref[0:w, :])

    n_chunks = nc // LANES
    need = ((i + 1) * (tq // CMP_STRIDE) + LANES - 1) // LANES
    for k in range(1, n_chunks + 1):
        pl.when((need == k) if k < n_chunks else (need >= k))(functools.partial(attend, k * LANES))

    j = lax.broadcasted_iota(jnp.int32, (tq, LANES), 1)
    cur = t // SEL_BLOCK
    forced = (j == 0) | (j == cur) | (j == cur - 1)
    valid = j * SEL_BLOCK <= t
    score = jnp.where(forced, FORCE_SCORE, jnp.where(valid, imp_ref[...], -1.0))
    score = jnp.where(j < n_s, score, -jnp.inf)
    x = score.T
    blk_i = lax.broadcasted_iota(jnp.int32, (LANES, tq), 0)
    blk = blk_i.astype(F32)

    def topk(x, n_iter):
        for _ in range(n_iter):
            mx = jnp.max(x, axis=0, keepdims=True)
            idx = jnp.min(jnp.where(x == mx, blk, float(LANES)), axis=0, keepdims=True)
            x = jnp.where(blk == idx, -jnp.inf, x)
        sel = (x == -jnp.inf) & (blk_i < n_s)
        bias_ref[0] = jnp.where(sel, 0.0, MASK_VALUE).T.astype(BF16)

    n_forced = 3
    direct = (i * tq >= 2 * SEL_BLOCK) & (k_sel > n_forced)

    @pl.when(direct)
    def _():
        cur_t = (i * tq + lax.broadcasted_iota(jnp.int32, (1, tq), 1)) // SEL_BLOCK
        forced_t = (blk_i == 0) | (blk_i == cur_t) | (blk_i == cur_t - 1)
        topk(jnp.where(forced_t, -jnp.inf, x), k_sel - n_forced)

    @pl.when(jnp.logical_not(direct))
    def _():
        topk(x, k_sel)


def _overlap_matrix(nc, n_s):
    r = CMP_BLOCK // CMP_STRIDE
    qn = SEL_BLOCK // CMP_STRIDE
    m = np.zeros((nc, LANES), np.float32)
    n_c = nc - r + 1
    chunks = np.arange(n_c)[:, None] + np.arange(r)[None, :]
    np.add.at(m, (np.repeat(np.arange(n_c), r), (chunks // qn).ravel()), 1.0)
    return m


def _cmp_attention(q, kcmp, vcmp, gates, B, S, tq=1024):
    H, M, DH = q.shape
    G = N_KV_GROUPS
    tq = min(tq, S)
    nq = S // tq
    nc = S // CMP_STRIDE
    n_s = S // SEL_BLOCK
    assert n_s <= LANES and tq & (tq - 1) == 0
    k_sel = min(N_SEL, n_s)
    ov = jnp.asarray(_overlap_matrix(nc, n_s), BF16)
    return pl.pallas_call(
        functools.partial(_cmp_body, tq=tq, nc=nc, n_s=n_s, k_sel=k_sel),
        grid=(B, G, nq),
        in_specs=[pl.BlockSpec((HEADS_PER_GROUP, tq, DH), lambda b, g, i: (g, b * nq + i, 0)),
                  pl.BlockSpec((1, nc, DH), lambda b, g, i: (g, b, 0)),
                  pl.BlockSpec((1, nc, DH), lambda b, g, i: (g, b, 0)),
                  pl.BlockSpec((tq, LANES), lambda b, g, i: (b * nq + i, 0)),
                  _resident((nc, LANES))],
        out_specs=[pl.BlockSpec((tq, HEADS_PER_GROUP * DH), lambda b, g, i: (b * nq + i, g)),
                   pl.BlockSpec((1, tq, LANES), lambda b, g, i: (g, b * nq + i, 0))],
        out_shape=[jax.ShapeDtypeStruct((M, H * DH), BF16), jax.ShapeDtypeStruct((G, M, LANES), BF16)],
        scratch_shapes=[pltpu.VMEM((tq, LANES), F32)],
        compiler_params=_params("arbitrary", "arbitrary", "arbitrary"),
        name="nsa_cmp_select",
    )(q, kcmp, vcmp, gates, ov)


def _sel_body(q_ref, bias_ref, k_ref, v_ref, gates_ref, o_ref, qcat, m_ref, acc_ref, *, tq):
    g = pl.program_id(1)
    i = pl.program_id(2)
    hp = HEADS_PER_GROUP
    rows = hp * tq
    qcat[:, 0:LANES] = jnp.zeros((rows, LANES), BF16)
    qcat[:, 0:HEAD_DIM] = q_ref[...].reshape(rows, HEAD_DIM)
    bias = bias_ref[0]
    for h in range(hp):
        qcat[h * tq:(h + 1) * tq, LANES:2 * LANES] = bias
    m_ref[...] = jnp.full((rows, LANES), MASK_VALUE, F32)
    acc_ref[...] = jnp.zeros((rows, LANES), F32)

    def tile(j, width, causal):
        start = pl.multiple_of(j * tq, tq)
        kt = k_ref[0, pl.ds(start, width), :]
        vt = v_ref[0, pl.ds(start, width), :]
        for h in range(hp):
            r0 = h * tq
            s = lax.dot_general(qcat[r0:r0 + tq, :], kt, _NT, preferred_element_type=F32)
            if causal:
                r = lax.broadcasted_iota(jnp.int32, (tq, 1), 0)
                c = lax.broadcasted_iota(jnp.int32, (1, width), 1) - (width - tq)
                s = jnp.where(c <= r, s, MASK_VALUE)
            m_prev = m_ref[r0:r0 + tq, :]
            m_new = jnp.maximum(m_prev, jnp.max(s, axis=1, keepdims=True))
            alpha = jnp.exp2(m_prev - m_new)
            p = jnp.exp2(s - jnp.concatenate([m_new] * (width // LANES), axis=1))
            acc_ref[r0:r0 + tq, :] = alpha * acc_ref[r0:r0 + tq, :] + _dot(p.astype(BF16), vt)
            m_ref[r0:r0 + tq, :] = m_new

    def wide_tile(jw, carry):
        tile(SEL_WIDE * jw, SEL_WIDE * tq, False)
        return carry

    n_wide = i // SEL_WIDE
    lax.fori_loop(0, n_wide, wide_tile, 0)
    for rem in range(SEL_WIDE):
        pl.when(i % SEL_WIDE == rem)(functools.partial(tile, SEL_WIDE * n_wide, (rem + 1) * tq, True))

    gates = gates_ref[...]
    for h in range(hp):
        _store_head(o_ref, _normalized_head(acc_ref[h * tq:(h + 1) * tq, :]), gates, g, h, 1)


def _sel_attention(q, bias, kscat, vs, gates, B, S, tq=512):
    H, M, DH = q.shape
    G = N_KV_GROUPS
    tq = min(tq, S)
    nq = S // tq
    assert tq % LANES == 0 and tq % SEL_BLOCK == 0 and S % tq == 0
    rows = HEADS_PER_GROUP * tq
    return pl.pallas_call(
        functools.partial(_sel_body, tq=tq),
        grid=(B, G, nq),
        in_specs=[pl.BlockSpec((HEADS_PER_GROUP, tq, DH), lambda b, g, i: (g, b * nq + i, 0)),
                  pl.BlockSpec((1, tq, LANES), lambda b, g, i: (g, b * nq + i, 0)),
                  pl.BlockSpec((1, S, 2 * LANES), lambda b, g, i: (g, b, 0)),
                  pl.BlockSpec((1, S, LANES), lambda b, g, i: (g, b, 0)),
                  pl.BlockSpec((tq, LANES), lambda b, g, i: (b * nq + i, 0))],
        out_specs=pl.BlockSpec((tq, HEADS_PER_GROUP * DH), lambda b, g, i: (b * nq + i, g)),
        out_shape=jax.ShapeDtypeStruct((M, H * DH), BF16),
        scratch_shapes=[pltpu.VMEM((rows, 2 * LANES), BF16), pltpu.VMEM((rows, LANES), F32),
                        pltpu.VMEM((rows, LANES), F32)],
        compiler_params=_params("arbitrary", "arbitrary", "arbitrary"),
        name="nsa_selected",
    )(q, bias, kscat, vs, gates)


def _win_body(q_ref, k_ref, v_ref, gates_ref, o_ref, *, tq, n_back, n_sub):
    g = pl.program_id(1)
    r = lax.broadcasted_iota(jnp.int32, (tq, 1), 0)
    c = lax.broadcasted_iota(jnp.int32, (1, tq), 1)
    for sub in range(n_sub):
        i = pl.program_id(2) * n_sub + sub
        rows = slice(sub * tq, (sub + 1) * tq)
        gates = gates_ref[rows, :]
        tiles = []
        for back in range(n_back, -1, -1):
            jt = i - back
            start = pl.multiple_of(jnp.maximum(jt, 0) * tq, tq)
            d = r - c + back * tq
            ok = (d >= 0) & (d < WINDOW) & (jt >= 0)
            tiles.append((k_ref[0, pl.ds(start, tq), :], v_ref[0, pl.ds(start, tq), :], ok))
        for h in range(HEADS_PER_GROUP):
            q = q_ref[h, rows, :]
            scores = [jnp.where(ok, lax.dot_general(q, kt, _NT, preferred_element_type=F32), MASK_VALUE)
                      for kt, _, ok in tiles]
            m = functools.reduce(jnp.maximum, scores).max(axis=1, keepdims=True)
            acc = jnp.zeros((tq, LANES), F32)
            for s, (_, vt, _) in zip(scores, tiles):
                acc = acc + _dot(jnp.exp2(s - m).astype(BF16), vt)
            _store_head(o_ref, _normalized_head(acc), gates, g, h, 2, rows)


def _win_attention(q, kw, vw, gates, B, S, tq=256, n_sub=4):
    H, M, DH = q.shape
    G = N_KV_GROUPS
    tq = min(tq, S)
    n_sub = min(n_sub, S // tq)
    tile = tq
    tq = tile * n_sub
    nq = S // tq
    n_back = -(-WINDOW // tile)
    return pl.pallas_call(
        functools.partial(_win_body, tq=tile, n_back=n_back, n_sub=n_sub),
        grid=(B, G, nq),
        in_specs=[pl.BlockSpec((HEADS_PER_GROUP, tq, DH), lambda b, g, i: (g, b * nq + i, 0)),
                  pl.BlockSpec((1, S, DH), lambda b, g, i: (g, b, 0)),
                  pl.BlockSpec((1, S, LANES), lambda b, g, i: (g, b, 0)),
                  pl.BlockSpec((tq, LANES), lambda b, g, i: (b * nq + i, 0))],
        out_specs=pl.BlockSpec((tq, HEADS_PER_GROUP * DH), lambda b, g, i: (b * nq + i, g)),
        out_shape=jax.ShapeDtypeStruct((M, H * DH), BF16),
        compiler_params=_params("arbitrary", "arbitrary", "arbitrary"),
        name="nsa_window",
    )(q, kw, vw, gates)


def _outproj_body(x_ref, a_ref, b_ref, c_ref, w_ref, o_ref):
    o = a_ref[...].astype(F32) + b_ref[...].astype(F32) + c_ref[...].astype(F32)
    o_ref[...] = x_ref[...] + _dot(o.astype(BF16), w_ref[...])


def _outproj(x2, oc, os_, ow, w_out, tm=512):
    M, D = x2.shape
    K = w_out.shape[0]
    tm = min(tm, M)
    spec_o = pl.BlockSpec((tm, K), lambda i: (i, 0))
    return pl.pallas_call(
        _outproj_body,
        grid=(M // tm,),
        in_specs=[pl.BlockSpec((tm, D), lambda i: (i, 0)), spec_o, spec_o, spec_o, _resident((K, D))],
        out_specs=pl.BlockSpec((tm, D), lambda i: (i, 0)),
        out_shape=jax.ShapeDtypeStruct((M, D), F32),
        compiler_params=_params("arbitrary"),
        name="nsa_outproj",
    )(x2, oc, os_, ow, w_out.astype(BF16))


def _nsa_layer(x, gn, w_in, cmp_pe, cmp_w1, cmp_w2, w_out):
    B, S, D = x.shape
    x2 = x.reshape(B * S, D)
    q, kc, vc, kscat, vs, kw, vw, gates = _nsa_proj(x2, gn, w_in, S)
    kcmp, vcmp = _compress(kc, vc, cmp_pe, cmp_w1, cmp_w2, B, S)
    o_cmp, bias = _cmp_attention(q, kcmp, vcmp, gates, B, S)
    o_sel = _sel_attention(q, bias, kscat, vs, gates, B, S)
    o_win = _win_attention(q, kw, vw, gates, B, S)
    return _outproj(x2, o_cmp, o_sel, o_win, w_out).reshape(B, S, D)


def kernel(x, norm_mix, norm_ffn, norm_final, lru_w_in, lru_b_in, lru_conv_w, lru_conv_b, lru_w_a, lru_b_a, lru_w_i, lru_b_i, lru_lambda, lru_w_out, lru_b_out, nsa_w_in, nsa_cmp_pe, nsa_cmp_w1, nsa_cmp_w2, nsa_w_out, ffn_w_gate, ffn_w_up, ffn_w_down, moe_w_router, moe_w_gate, moe_w_up, moe_w_down):
    B, S, D = x.shape
    assert norm_mix.shape[0] == 2 and lru_w_in.shape[0] == 1 and nsa_w_in.shape[0] == 1
    x = _lru_layer(x, norm_mix[0], lru_w_in[0], lru_b_in[0], lru_conv_w[0], lru_conv_b[0], lru_w_a[0],
                   lru_b_a[0], lru_w_i[0], lru_b_i[0], lru_lambda[0], lru_w_out[0], lru_b_out[0])
    x2 = _ffn_layer(x.reshape(B * S, D), norm_ffn[0], ffn_w_gate, ffn_w_up, ffn_w_down)
    x2 = _nsa_layer(x2.reshape(B, S, D), norm_mix[1], nsa_w_in[0], nsa_cmp_pe[0], nsa_cmp_w1[0],
                    nsa_cmp_w2[0], nsa_w_out[0]).reshape(B * S, D)
    out = _moe_layer(x2, norm_ffn[1], moe_w_router[0], moe_w_gate[0], moe_w_up[0], moe_w_down[0], norm_final)
    return out.reshape(B, S, D)
```

```python
import functools

import numpy as np
import jax
import jax.numpy as jnp
from jax import lax
from jax.experimental import pallas as pl
from jax.experimental.pallas import tpu as pltpu

F32 = jnp.float32
BF16 = jnp.bfloat16

RMS_EPS = 1e-6
LRU_C = 8.0
CONV_WIDTH = 4
N_HEADS = 16
HEAD_DIM = 64
N_KV_GROUPS = 4
HEADS_PER_GROUP = N_HEADS // N_KV_GROUPS
CMP_BLOCK = 32
CMP_STRIDE = 16
SEL_BLOCK = 64
N_SEL = 16
WINDOW = 512
FORCE_SCORE = 1.0e4
ROPE_THETA = 10000.0
TOP_K = 2

LANES = 128
SUBLANES = 8
MXU_WIDTH = 256
MASK_VALUE = -1.0e30
VMEM_LIMIT_BYTES = 56 * 1024 * 1024

_NT = (((1,), (1,)), ((), ()))
Q_SCALE = HEAD_DIM ** -0.5 * 1.4426950408889634
SEL_WIDE = 4


def _params(*semantics):
    return pltpu.CompilerParams(dimension_semantics=semantics, vmem_limit_bytes=VMEM_LIMIT_BYTES)


def _resident(shape):
    zeros = (0,) * len(shape)
    return pl.BlockSpec(shape, lambda *_: zeros, pipeline_mode=pl.Buffered(1))


def _rms(x, g):
    return x * lax.rsqrt(jnp.mean(x * x, axis=-1, keepdims=True) + RMS_EPS) * g


def _dot(a, b):
    return jnp.dot(a, b, preferred_element_type=F32)


def _gelu_tanh(x):
    two_z = x * (2.0 * 0.7978845608028654 + (2.0 * 0.7978845608028654 * 0.044715) * (x * x))
    return x * jax.nn.sigmoid(two_z)


def _block_diag_dot(xb, w_ref, col0, n, bw):
    outs = []
    for c0 in range(0, n, MXU_WIDTH):
        w = min(MXU_WIDTH, n - c0)
        lo = (c0 // bw) * bw // LANES * LANES
        hi = min(n, -(-(((c0 + w - 1) // bw + 1) * bw) // LANES) * LANES)
        outs.append(_dot(xb[:, lo:hi], w_ref[lo:hi, col0 + c0:col0 + c0 + w]))
    return jnp.concatenate(outs, axis=1)


def _lane_col(vals, lane, idx):
    return jnp.sum(jnp.where(lane == idx, vals, 0.0), axis=1, keepdims=True)


def _lru_body(x_ref, gn_ref, win_ref, bin_ref, cw_ref, cb_ref, wg_ref, bg_ref, lam_ref,
              wout_ref, bout_ref, o_ref, xbuf, hcar, *, ts, dr, bw):
    t = pl.program_id(1)

    @pl.when(t == 0)
    def _():
        xbuf[0:8, :] = jnp.zeros((8, dr), F32)
        hcar[...] = jnp.zeros_like(hcar)

    x = x_ref[0]
    hn = _rms(x, gn_ref[...]).astype(BF16)
    proj = _dot(hn, win_ref[...]) + bin_ref[...]
    gate = _gelu_tanh(proj[:, :dr])
    xr = proj[:, dr:]

    xbuf[8:8 + ts, :] = xr
    xc = cb_ref[...] + xr * cw_ref[CONV_WIDTH - 1:CONV_WIDTH, :]
    for lag in range(1, CONV_WIDTH):
        k = CONV_WIDTH - 1 - lag
        xc = xc + xbuf[pl.ds(8 - lag, ts), :] * cw_ref[k:k + 1, :]
    xbuf[0:8, :] = xbuf[ts:ts + 8, :]

    xcb = xc.astype(BF16)
    r = jax.nn.sigmoid(_block_diag_dot(xcb, wg_ref, 0, dr, bw) + bg_ref[:, :dr])
    i = jax.nn.sigmoid(_block_diag_dot(xcb, wg_ref, dr, dr, bw) + bg_ref[:, dr:])
    z = -lam_ref[...]
    softplus = jnp.maximum(z, 0.0) + jnp.log(1.0 + jnp.exp(-jnp.abs(z)))
    log_a = (-LRU_C * r) * softplus
    a = jnp.exp(log_a)
    mult = jnp.sqrt(1.0 - a * a)
    row = lax.broadcasted_iota(jnp.int32, (ts, 1), 0)
    mult = jnp.where((row == 0) & (t == 0), 1.0, mult)
    u = mult * (i * xc)

    sub = row & (SUBLANES - 1)
    for shift in (1, 2, 4):
        keep = sub >= shift
        a_prev = jnp.where(keep, pltpu.roll(a, shift, 0), 1.0)
        u_prev = jnp.where(keep, pltpu.roll(u, shift, 0), 0.0)
        u = a * u_prev + u
        a = a * a_prev
    carry = hcar[...]
    groups = []
    for k in range(ts // SUBLANES):
        rows = slice(k * SUBLANES, (k + 1) * SUBLANES)
        hk = u[rows] + a[rows] * carry
        carry = hk[SUBLANES - 1:SUBLANES]
        groups.append(hk)
    h = jnp.concatenate(groups, axis=0)
    hcar[...] = carry

    y = (h * gate).astype(BF16)
    o_ref[0] = x + _dot(y, wout_ref[...]) + bout_ref[...]


def _lru_layer(x, gn, w_in, b_in, conv_w, conv_b, w_a, b_a, w_i, b_i, lam, w_out, b_out):
    B, S, D = x.shape
    dr = w_out.shape[0]
    ts = min(256, S)
    assert S % ts == 0 and ts % 8 == 0
    wg = jnp.concatenate([jax.scipy.linalg.block_diag(*w_a), jax.scipy.linalg.block_diag(*w_i)], axis=1)
    row = lambda v: v.reshape(1, -1)
    body = functools.partial(_lru_body, ts=ts, dr=dr, bw=w_a.shape[1])
    return pl.pallas_call(
        body,
        grid=(B, S // ts),
        in_specs=[
            pl.BlockSpec((1, ts, D), lambda b, t: (b, t, 0)),
            _resident((1, D)),
            _resident((D, 2 * dr)), _resident((1, 2 * dr)),
            _resident((CONV_WIDTH, dr)), _resident((1, dr)),
            _resident((dr, 2 * dr)), _resident((1, 2 * dr)),
            _resident((1, dr)),
            _resident((dr, D)), _resident((1, D)),
        ],
        out_specs=pl.BlockSpec((1, ts, D), lambda b, t: (b, t, 0)),
        out_shape=jax.ShapeDtypeStruct((B, S, D), F32),
        scratch_shapes=[pltpu.VMEM((ts + 8, dr), F32), pltpu.VMEM((1, dr), F32)],
        compiler_params=_params("arbitrary", "arbitrary"),
        name="lru_mixer",
    )(x, row(gn), w_in.astype(BF16), row(b_in), conv_w, row(conv_b), wg.astype(BF16),
      row(jnp.concatenate([b_a, b_i])), row(lam), w_out.astype(BF16), row(b_out))


def _swiglu_partial(hn, wg_ref, wu_ref, wd_ref):
    g = _dot(hn, wg_ref[0].astype(BF16))
    u = _dot(hn, wu_ref[0].astype(BF16))
    return _dot(((g * jax.nn.sigmoid(g)) * u).astype(BF16), wd_ref[0].astype(BF16))


def _ffn_body(x_ref, gn_ref, wg_ref, wu_ref, wd_ref, o_ref, hn_ref, acc_ref, *, n_f):
    f = pl.program_id(1)

    @pl.when(f == 0)
    def _():
        hn_ref[...] = _rms(x_ref[...], gn_ref[...]).astype(BF16)
        acc_ref[...] = jnp.zeros_like(acc_ref)

    acc_ref[...] += _swiglu_partial(hn_ref[...], wg_ref, wu_ref, wd_ref)

    @pl.when(f == n_f - 1)
    def _():
        o_ref[...] = x_ref[...] + acc_ref[...]


def _ffn_layer(x2, gn, w_gate, w_up, w_down, tm=1024, tf=512):
    M, D = x2.shape
    F = w_gate.shape[2]
    tm = min(tm, M)
    tf = min(tf, F)
    assert M % tm == 0 and F % tf == 0
    n_f = F // tf
    return pl.pallas_call(
        functools.partial(_ffn_body, n_f=n_f),
        grid=(M // tm, n_f),
        in_specs=[pl.BlockSpec((tm, D), lambda i, f: (i, 0)), _resident((1, D)),
                  pl.BlockSpec((1, D, tf), lambda i, f: (0, 0, f)),
                  pl.BlockSpec((1, D, tf), lambda i, f: (0, 0, f)),
                  pl.BlockSpec((1, tf, D), lambda i, f: (0, f, 0))],
        out_specs=pl.BlockSpec((tm, D), lambda i, f: (i, 0)),
        out_shape=jax.ShapeDtypeStruct((M, D), F32),
        scratch_shapes=[pltpu.VMEM((tm, D), BF16), pltpu.VMEM((tm, D), F32)],
        compiler_params=_params("arbitrary", "arbitrary"),
        name="dense_swiglu",
    )(x2, gn.reshape(1, D), w_gate, w_up, w_down)


def _router_body(x_ref, gn_ref, wh_ref, wl_ref, tril_ref, o_ref, cnt_ref, *, n_experts):
    @pl.when(pl.program_id(0) == 0)
    def _():
        cnt_ref[...] = jnp.zeros_like(cnt_ref)

    hn = _rms(x_ref[...], gn_ref[...])
    hh = hn.astype(BF16)
    hl = (hn - hh.astype(F32)).astype(BF16)
    wh = wh_ref[...]
    logits = _dot(hh, wh) + _dot(hh, wl_ref[...]) + _dot(hl, wh)
    lane = lax.broadcasted_iota(jnp.int32, logits.shape, 1)
    lg = jnp.where(lane < n_experts, logits, -jnp.inf)
    m0 = jnp.max(lg, axis=1, keepdims=True)
    i0 = jnp.min(jnp.where(lg == m0, lane, LANES), axis=1, keepdims=True)
    lg = jnp.where(lane == i0, -jnp.inf, lg)
    m1 = jnp.max(lg, axis=1, keepdims=True)
    i1 = jnp.min(jnp.where(lg == m1, lane, LANES), axis=1, keepdims=True)
    e1 = jnp.exp(m1 - m0)
    w0 = 1.0 / (1.0 + e1)
    routed = ((lane == i0) | (lane == i1)).astype(F32)
    incl = _dot(tril_ref[...], routed.astype(BF16))
    excl = incl - routed + cnt_ref[0:1, :]

    def put(col, v):
        return jnp.where(lane == col, v, 0.0)

    o_ref[...] = (put(META_I0, i0.astype(F32)) + put(META_I1, i1.astype(F32)) + put(META_W0, w0)
                  + put(META_W1, e1 * w0) + put(META_R0, _lane_col(excl, lane, i0))
                  + put(META_R1, _lane_col(excl, lane, i1)))
    cnt_ref[...] = jnp.broadcast_to(cnt_ref[0:1, :] + incl[incl.shape[0] - 1:, :], cnt_ref.shape)


META_I0, META_I1, META_W0, META_W1, META_R0, META_R1 = range(6)


def _router(x2, gn, w_router, tm=512):
    M, D = x2.shape
    n_experts = w_router.shape[1]
    tm = min(tm, M)
    wpad = jnp.pad(w_router, ((0, 0), (0, LANES - n_experts)))
    wh = wpad.astype(BF16)
    wl = (wpad - wh.astype(F32)).astype(BF16)
    tril = jnp.tril(jnp.ones((tm, tm), BF16))
    return pl.pallas_call(
        functools.partial(_router_body, n_experts=n_experts),
        grid=(M // tm,),
        in_specs=[pl.BlockSpec((tm, D), lambda i: (i, 0)), _resident((1, D)),
                  _resident((D, LANES)), _resident((D, LANES)), _resident((tm, tm))],
        out_specs=[pl.BlockSpec((tm, LANES), lambda i: (i, 0)), pl.BlockSpec((8, LANES), lambda i: (0, 0))],
        out_shape=[jax.ShapeDtypeStruct((M, LANES), F32), jax.ShapeDtypeStruct((8, LANES), F32)],
        compiler_params=_params("arbitrary"),
        name="moe_router",
    )(x2, gn.reshape(1, D), wh, wl, tril)


def _combine_body(pos_ref, pos_next_ref, x_ref, meta_ref, y_ref, gfin_ref, o_ref, ybuf, sem, *, tm, n_tiles):
    i = pl.program_id(0)
    slot = i % 2

    def row_copy(idx_ref, k, r, s):
        return pltpu.make_async_copy(y_ref.at[pl.ds(idx_ref[0, 0, k * tm + r], 1)], ybuf.at[s, k, pl.ds(r, 1)],
                                     sem.at[s])

    @pl.when(i == 0)
    def _():
        def body(r, c):
            row_copy(pos_ref, 0, r, 0).start()
            row_copy(pos_ref, 1, r, 0).start()
            return c
        lax.fori_loop(0, tm, body, 0, unroll=8)

    @pl.when(i + 1 < n_tiles)
    def _():
        for r in range(tm):
            row_copy(pos_next_ref, 0, r, 1 - slot).start(priority=0)
            row_copy(pos_next_ref, 1, r, 1 - slot).start(priority=1)

    pltpu.make_async_copy(ybuf.at[slot], ybuf.at[slot], sem.at[slot]).wait()
    meta = meta_ref[...]
    lane = lax.broadcasted_iota(jnp.int32, meta.shape, 1)
    out = (x_ref[...] + _lane_col(meta, lane, META_W0) * ybuf[slot, 0]
           + _lane_col(meta, lane, META_W1) * ybuf[slot, 1])
    o_ref[...] = _rms(out, gfin_ref[...])


def _combine(x2, meta, y, pos0, pos1, g_final, tm=512):
    M, D = x2.shape
    tm = min(tm, M)
    n_tiles = M // tm
    pos = jnp.concatenate([pos0.reshape(n_tiles, 1, tm), pos1.reshape(n_tiles, 1, tm)], axis=2)
    return pl.pallas_call(
        functools.partial(_combine_body, tm=tm, n_tiles=n_tiles),
        grid=(n_tiles,),
        in_specs=[pl.BlockSpec((1, 1, 2 * tm), lambda i: (i, 0, 0), memory_space=pltpu.SMEM),
                  pl.BlockSpec((1, 1, 2 * tm), lambda i: (jnp.minimum(i + 1, n_tiles - 1), 0, 0),
                               memory_space=pltpu.SMEM),
                  pl.BlockSpec((tm, D), lambda i: (i, 0)),
                  pl.BlockSpec((tm, LANES), lambda i: (i, 0)),
                  pl.BlockSpec(memory_space=pl.ANY),
                  _resident((1, D))],
        out_specs=pl.BlockSpec((tm, D), lambda i: (i, 0)),
        out_shape=jax.ShapeDtypeStruct((M, D), F32),
        scratch_shapes=[pltpu.VMEM((2, 2, tm, D), F32), pltpu.SemaphoreType.DMA((2,))],
        compiler_params=_params("arbitrary"),
        name="moe_combine",
    )(pos, pos, x2, meta, y, g_final.reshape(1, D))


GATHER_AHEAD = 2


def _grouped_ffn_body(te_ref, nused_ref, src_ref, src_next_ref, src_ahead_ref, x_ref, gn_ref, wg_ref, wu_ref,
                      wd_ref, o_ref, xbuf, hn_ref, acc_ref, sem, *, tm, n_f):
    j = pl.program_id(0)
    f = pl.program_id(1)
    n_used = nused_ref[0]
    used = j < n_used
    gathered = j < n_used + GATHER_AHEAD
    slot = j % (GATHER_AHEAD + 1)
    slot_ahead = (j + GATHER_AHEAD) % (GATHER_AHEAD + 1)
    chunk = tm // (n_f + 1)

    def row_copy(idx_ref, r, s):
        return pltpu.make_async_copy(x_ref.at[pl.ds(idx_ref[0, 0, r], 1)], xbuf.at[s, pl.ds(r, 1)], sem.at[s])

    def start_chunk(first):
        for k in range(chunk):
            row_copy(src_ahead_ref, first + k, slot_ahead).start()

    @pl.when((j == 0) & (f == 0))
    def _():
        def body(r, c):
            row_copy(src_ref, r, 0).start()
            row_copy(src_next_ref, r, 1).start()
            return c
        lax.fori_loop(0, tm, body, 0, unroll=8)

    @pl.when(f == 0)
    def _():
        acc_ref[...] = jnp.zeros_like(acc_ref)

        @pl.when(gathered)
        def _():
            pltpu.make_async_copy(xbuf.at[slot], xbuf.at[slot], sem.at[slot]).wait()
            hn_ref[...] = _rms(xbuf[slot], gn_ref[...]).astype(BF16)

        @pl.when(used)
        def _():
            start_chunk(0)

    @pl.when(used)
    def _():
        acc_ref[...] += _swiglu_partial(hn_ref[...], wg_ref, wu_ref, wd_ref)
        start_chunk((f + 1) * chunk)

    @pl.when(f == n_f - 1)
    def _():
        o_ref[...] = acc_ref[...]


def _grouped_ffn(x2, src, gn, w_gate, w_up, w_down, tile_expert, n_used, tm, tf=512):
    P = src.shape[0]
    D = x2.shape[1]
    F = w_gate.shape[2]
    tf = min(tf, F)
    n_f = F // tf
    n_tiles = P // tm
    assert F % tf == 0 and tm % (n_f + 1) == 0
    src3 = src.reshape(n_tiles, 1, tm)

    def src_spec(ahead):
        return pl.BlockSpec((1, 1, tm), lambda j, f, te, nu: (jnp.minimum(j + ahead, n_tiles - 1), 0, 0),
                            memory_space=pltpu.SMEM)

    grid_spec = pltpu.PrefetchScalarGridSpec(
        num_scalar_prefetch=2,
        grid=(n_tiles, n_f),
        in_specs=[src_spec(a) for a in range(GATHER_AHEAD + 1)] + [
                  pl.BlockSpec(memory_space=pl.ANY),
                  pl.BlockSpec((1, D), lambda j, f, te, nu: (0, 0)),
                  pl.BlockSpec((1, D, tf), lambda j, f, te, nu: (te[j], 0, f)),
                  pl.BlockSpec((1, D, tf), lambda j, f, te, nu: (te[j], 0, f)),
                  pl.BlockSpec((1, tf, D), lambda j, f, te, nu: (te[j], f, 0))],
        out_specs=pl.BlockSpec((tm, D), lambda j, f, te, nu: (j, 0)),
        scratch_shapes=[pltpu.VMEM((GATHER_AHEAD + 1, tm, D), F32), pltpu.VMEM((tm, D), BF16),
                        pltpu.VMEM((tm, D), F32), pltpu.SemaphoreType.DMA((GATHER_AHEAD + 1,))])
    return pl.pallas_call(
        functools.partial(_grouped_ffn_body, tm=tm, n_f=n_f),
        grid_spec=grid_spec,
        out_shape=jax.ShapeDtypeStruct((P, D), F32),
        compiler_params=_params("arbitrary", "arbitrary"),
        name="moe_grouped_swiglu",
    )(tile_expert, n_used, *([src3] * (GATHER_AHEAD + 1)), x2, gn.reshape(1, D), w_gate, w_up, w_down)


def _moe_layer(x2, gn, w_router, w_gate, w_up, w_down, g_final, tm=1024):
    M, D = x2.shape
    n_e = w_router.shape[1]
    tm = min(tm, M)
    meta, counts = _router(x2, gn, w_router)
    as_int = lambda col: meta[:, col].astype(jnp.int32)
    i0, i1, r0, r1 = as_int(META_I0), as_int(META_I1), as_int(META_R0), as_int(META_R1)
    padded = (counts[0, :n_e].astype(jnp.int32) + tm - 1) // tm * tm
    ends = jnp.cumsum(padded)
    offsets = ends - padded
    pos0 = offsets[i0] + r0
    pos1 = offsets[i1] + r1
    P = TOP_K * M + (n_e - 1 + GATHER_AHEAD) * tm
    tok = jnp.arange(M, dtype=jnp.int32)
    src = jnp.zeros((P,), jnp.int32).at[jnp.concatenate([pos0, pos1])].set(jnp.concatenate([tok, tok]))
    tile_start = jnp.arange(P // tm, dtype=jnp.int32) * tm
    tile_expert = jnp.minimum(jnp.searchsorted(ends, tile_start, side="right"), n_e - 1).astype(jnp.int32)
    n_used = (ends[n_e - 1:] // tm).astype(jnp.int32)
    y = _grouped_ffn(x2, src, gn, w_gate, w_up, w_down, tile_expert, n_used, tm)
    return _combine(x2, meta, y, pos0, pos1, g_final)


def _nsa_proj_body(x_ref, gn_ref, w_ref, cos_ref, sin_ref,
                   q_ref, kc_ref, vc_ref, ks_ref, vs_ref, kw_ref, vw_ref, g_ref, *, tm, tiles_per_seq):
    i = pl.program_id(0)
    hn = _rms(x_ref[...], gn_ref[...]).astype(BF16)
    proj = _dot(hn, w_ref[...])
    lane = lax.broadcasted_iota(jnp.int32, (tm, LANES), 1)
    first_half = (lane & (HEAD_DIM - 1)) < HEAD_DIM // 2
    cos = cos_ref[...]
    sin = sin_ref[...]

    def chunk(c):
        return proj[:, c * LANES:(c + 1) * LANES]

    def rope(v):
        rot = jnp.where(first_half, pltpu.roll(v, LANES - HEAD_DIM // 2, 1), pltpu.roll(v, HEAD_DIM // 2, 1))
        return v * cos + rot * sin

    def heads(v):
        return v[:, :HEAD_DIM], pltpu.roll(v, HEAD_DIM, 1)[:, :HEAD_DIM]

    c = 0
    for cc in range(N_HEADS // 2):
        lo, hi = heads(rope(chunk(c)) * Q_SCALE); c += 1
        q_ref[2 * cc] = lo.astype(BF16)
        q_ref[2 * cc + 1] = hi.astype(BF16)
    for cc in range(N_KV_GROUPS // 2):
        lo, hi = heads(rope(chunk(c))); c += 1
        kc_ref[2 * cc] = lo
        kc_ref[2 * cc + 1] = hi
    for cc in range(N_KV_GROUPS // 2):
        lo, hi = heads(chunk(c)); c += 1
        vc_ref[2 * cc] = lo
        vc_ref[2 * cc + 1] = hi
    pos = (i % tiles_per_seq) * tm + lax.broadcasted_iota(jnp.int32, (tm, 1), 0)
    onehot = (lane == pos // SEL_BLOCK).astype(BF16)
    for cc in range(N_KV_GROUPS // 2):
        v = rope(chunk(c)); c += 1
        for j, vv in enumerate((v, pltpu.roll(v, HEAD_DIM, 1))):
            ks_ref[2 * cc + j, :, 0:LANES] = jnp.where(lane < HEAD_DIM, vv, 0.0).astype(BF16)
            ks_ref[2 * cc + j, :, LANES:2 * LANES] = onehot
    ones_col = jnp.where(lane == HEAD_DIM, 1.0, 0.0)

    def store_values(ref):
        nonlocal c
        for cc in range(N_KV_GROUPS // 2):
            v = chunk(c); c += 1
            for j, vv in enumerate((v, pltpu.roll(v, HEAD_DIM, 1))):
                ref[2 * cc + j] = jnp.where(lane < HEAD_DIM, vv, ones_col).astype(BF16)

    store_values(vs_ref)
    for cc in range(N_KV_GROUPS // 2):
        lo, hi = heads(rope(chunk(c))); c += 1
        kw_ref[2 * cc] = lo.astype(BF16)
        kw_ref[2 * cc + 1] = hi.astype(BF16)
    store_values(vw_ref)
    g_ref[...] = jax.nn.sigmoid(chunk(c))


def _nsa_proj(x2, gn, w_in, S, tm=256):
    M, D = x2.shape
    H, G, DH = N_HEADS, N_KV_GROUPS, HEAD_DIM
    tm = min(tm, S)
    assert S % tm == 0
    n_in = w_in.shape[1]
    n_pad = -(-n_in // LANES) * LANES
    wp = jnp.pad(w_in, ((0, 0), (0, n_pad - n_in))).astype(BF16)
    half = DH // 2
    freqs = ROPE_THETA ** (-jnp.arange(half, dtype=F32) / half)
    ang = jnp.arange(S, dtype=F32)[:, None] * freqs[None, :]
    cos = jnp.tile(jnp.cos(ang), (1, 2 * LANES // DH))
    sin = jnp.tile(jnp.concatenate([-jnp.sin(ang), jnp.sin(ang)], axis=1), (1, LANES // DH))
    tiles_per_seq = S // tm
    hd = lambda n, dt: jax.ShapeDtypeStruct((n, M, DH), dt)
    hspec = lambda n: pl.BlockSpec((n, tm, DH), lambda i: (0, i, 0))
    vd = jax.ShapeDtypeStruct((G, M, LANES), BF16)
    vspec = pl.BlockSpec((G, tm, LANES), lambda i: (0, i, 0))
    return pl.pallas_call(
        functools.partial(_nsa_proj_body, tm=tm, tiles_per_seq=tiles_per_seq),
        grid=(M // tm,),
        in_specs=[pl.BlockSpec((tm, D), lambda i: (i, 0)), _resident((1, D)), _resident((D, n_pad)),
                  pl.BlockSpec((tm, LANES), lambda i: (i % tiles_per_seq, 0)),
                  pl.BlockSpec((tm, LANES), lambda i: (i % tiles_per_seq, 0))],
        out_specs=[hspec(H), hspec(G), hspec(G),
                   pl.BlockSpec((G, tm, 2 * LANES), lambda i: (0, i, 0)),
                   vspec, hspec(G), vspec,
                   pl.BlockSpec((tm, LANES), lambda i: (i, 0))],
        out_shape=[hd(H, BF16), hd(G, F32), hd(G, F32),
                   jax.ShapeDtypeStruct((G, M, 2 * LANES), BF16),
                   vd, hd(G, BF16), vd,
                   jax.ShapeDtypeStruct((M, LANES), F32)],
        compiler_params=_params("arbitrary"),
        name="nsa_proj",
    )(x2, gn.reshape(1, D), wp, cos, sin)


def _compress_body(kc_ref, vc_ref, pe_ref, w1_ref, w2_ref, ko_ref, vo_ref, *, nc):
    half = CMP_BLOCK // 2
    for kv, (src, dst) in enumerate(((kc_ref, ko_ref), (vc_ref, vo_ref))):
        top = jnp.zeros((nc, w1_ref.shape[2]), F32)
        bot = jnp.zeros((nc, w1_ref.shape[2]), F32)
        for j in range(half):
            xj = src[0, pl.ds(j, nc, stride=CMP_STRIDE), :]
            top = top + _dot((xj + pe_ref[kv, j:j + 1, :]).astype(BF16),
                             w1_ref[kv, j * HEAD_DIM:(j + 1) * HEAD_DIM, :])
            bot = bot + _dot((xj + pe_ref[kv, half + j:half + j + 1, :]).astype(BF16),
                             w1_ref[kv, (half + j) * HEAD_DIM:(half + j + 1) * HEAD_DIM, :])
        hid = top + pltpu.roll(bot, nc - 1, 0)
        dst[0] = _dot(_gelu_tanh(hid).astype(BF16), w2_ref[kv]).astype(BF16)


def _compress(kc, vc, pe, w1, w2, B, S):
    assert CMP_BLOCK == 2 * CMP_STRIDE
    G, M, DH = kc.shape
    nc = S // CMP_STRIDE
    spec_in = pl.BlockSpec((1, S, DH), lambda b, g: (g, b, 0))
    spec_out = pl.BlockSpec((1, nc, DH), lambda b, g: (g, b, 0))
    out = jax.ShapeDtypeStruct((G, B * nc, DH), BF16)
    return pl.pallas_call(
        functools.partial(_compress_body, nc=nc),
        grid=(B, G),
        in_specs=[spec_in, spec_in, _resident(pe.shape), _resident(w1.shape), _resident(w2.shape)],
        out_specs=[spec_out, spec_out],
        out_shape=[out, out],
        compiler_params=_params("arbitrary", "arbitrary"),
        name="nsa_compress",
    )(kc, vc, pe, w1.astype(BF16), w2.astype(BF16))


def _store_head(o_ref, oh, gates, g, h, branch, rows=slice(None)):
    lane = lax.broadcasted_iota(jnp.int32, gates.shape, 1)
    col = 3 * (HEADS_PER_GROUP * g + h) + branch
    o_ref[rows, h * HEAD_DIM:(h + 1) * HEAD_DIM] = (oh * _lane_col(gates, lane, col)).astype(o_ref.dtype)


def _normalized_head(acc):
    return acc[:, :HEAD_DIM] * (1.0 / acc[:, HEAD_DIM:HEAD_DIM + 1])


def _cmp_body(q_ref, kc_ref, vc_ref, gates_ref, ov_ref, o_ref, bias_ref, imp_ref, *, tq, nc, n_s, k_sel):
    g = pl.program_id(1)
    i = pl.program_id(2)
    gates = gates_ref[...]
    t = i * tq + lax.broadcasted_iota(jnp.int32, (tq, 1), 0)

    def attend(w):
        kc = kc_ref[0, 0:w, :]
        vc = vc_ref[0, 0:w, :]
        cend = lax.broadcasted_iota(jnp.int32, (1, w), 1) * CMP_STRIDE + (CMP_BLOCK - 1)
        mask = cend <= t
        ps = None
        for h in range(HEADS_PER_GROUP):
            s = lax.dot_general(q_ref[h], kc, _NT, preferred_element_type=F32)
            s = jnp.where(mask, s, MASK_VALUE)
            m = jnp.max(s, axis=1, keepdims=True)
            m = jnp.where(m > 0.5 * MASK_VALUE, m, 0.0)
            e = jnp.exp2(s - m)
            p = e * (1.0 / jnp.maximum(jnp.sum(e, axis=1, keepdims=True), 1e-30))
            _store_head(o_ref, _dot(p.astype(BF16), vc), gates, g, h, 0)
            ps = p if ps is None else ps + p
        ph = ps.astype(BF16)
        pl_ = (ps - ph.astype(F32)).astype(BF16)
        imp_ref[...] = _dot(ph, ov_ref[0:w, :]) + _dot(pl_, ov_ref[0:w, :])

    n_chunks = nc // LANES
    need = ((i + 1) * (tq // CMP_STRIDE) + LANES - 1) // LANES
    for k in range(1, n_chunks + 1):
        pl.when((need == k) if k < n_chunks else (need >= k))(functools.partial(attend, k * LANES))

    j = lax.broadcasted_iota(jnp.int32, (tq, LANES), 1)
    cur = t // SEL_BLOCK
    forced = (j == 0) | (j == cur) | (j == cur - 1)
    valid = j * SEL_BLOCK <= t
    score = jnp.where(forced, FORCE_SCORE, jnp.where(valid, imp_ref[...], -1.0))
    score = jnp.where(j < n_s, score, -jnp.inf)
    x = score.T
    blk_i = lax.broadcasted_iota(jnp.int32, (LANES, tq), 0)
    blk = blk_i.astype(F32)

    def topk(x, n_iter):
        for _ in range(n_iter):
            mx = jnp.max(x, axis=0, keepdims=True)
            idx = jnp.min(jnp.where(x == mx, blk, float(LANES)), axis=0, keepdims=True)
            x = jnp.where(blk == idx, -jnp.inf, x)
        sel = (x == -jnp.inf) & (blk_i < n_s)
        bias_ref[0] = jnp.where(sel, 0.0, MASK_VALUE).T.astype(BF16)

    n_forced = 3
    direct = (i * tq >= 2 * SEL_BLOCK) & (k_sel > n_forced)

    @pl.when(direct)
    def _():
        cur_t = (i * tq + lax.broadcasted_iota(jnp.int32, (1, tq), 1)) // SEL_BLOCK
        forced_t = (blk_i == 0) | (blk_i == cur_t) | (blk_i == cur_t - 1)
        topk(jnp.where(forced_t, -jnp.inf, x), k_sel - n_forced)

    @pl.when(jnp.logical_not(direct))
    def _():
        topk(x, k_sel)


def _overlap_matrix(nc, n_s):
    r = CMP_BLOCK // CMP_STRIDE
    qn = SEL_BLOCK // CMP_STRIDE
    m = np.zeros((nc, LANES), np.float32)
    n_c = nc - r + 1
    chunks = np.arange(n_c)[:, None] + np.arange(r)[None, :]
    np.add.at(m, (np.repeat(np.arange(n_c), r), (chunks // qn).ravel()), 1.0)
    return m


def _cmp_attention(q, kcmp, vcmp, gates, B, S, tq=1024):
    H, M, DH = q.shape
    G = N_KV_GROUPS
    tq = min(tq, S)
    nq = S // tq
    nc = S // CMP_STRIDE
    n_s = S // SEL_BLOCK
    assert n_s <= LANES and tq & (tq - 1) == 0
    k_sel = min(N_SEL, n_s)
    ov = jnp.asarray(_overlap_matrix(nc, n_s), BF16)
    return pl.pallas_call(
        functools.partial(_cmp_body, tq=tq, nc=nc, n_s=n_s, k_sel=k_sel),
        grid=(B, G, nq),
        in_specs=[pl.BlockSpec((HEADS_PER_GROUP, tq, DH), lambda b, g, i: (g, b * nq + i, 0)),
                  pl.BlockSpec((1, nc, DH), lambda b, g, i: (g, b, 0)),
                  pl.BlockSpec((1, nc, DH), lambda b, g, i: (g, b, 0)),
                  pl.BlockSpec((tq, LANES), lambda b, g, i: (b * nq + i, 0)),
                  _resident((nc, LANES))],
        out_specs=[pl.BlockSpec((tq, HEADS_PER_GROUP * DH), lambda b, g, i: (b * nq + i, g)),
                   pl.BlockSpec((1, tq, LANES), lambda b, g, i: (g, b * nq + i, 0))],
        out_shape=[jax.ShapeDtypeStruct((M, H * DH), BF16), jax.ShapeDtypeStruct((G, M, LANES), BF16)],
        scratch_shapes=[pltpu.VMEM((tq, LANES), F32)],
        compiler_params=_params("arbitrary", "arbitrary", "arbitrary"),
        name="nsa_cmp_select",
    )(q, kcmp, vcmp, gates, ov)


def _sel_body(q_ref, bias_ref, k_ref, v_ref, gates_ref, o_ref, qcat, m_ref, acc_ref, *, tq):
    g = pl.program_id(1)
    i = pl.program_id(2)
    hp = HEADS_PER_GROUP
    rows = hp * tq
    qcat[:, 0:LANES] = jnp.zeros((rows, LANES), BF16)
    qcat[:, 0:HEAD_DIM] = q_ref[...].reshape(rows, HEAD_DIM)
    bias = bias_ref[0]
    for h in range(hp):
        qcat[h * tq:(h + 1) * tq, LANES:2 * LANES] = bias
    m_ref[...] = jnp.full((rows, LANES), MASK_VALUE, F32)
    acc_ref[...] = jnp.zeros((rows, LANES), F32)

    def tile(j, width, causal):
        start = pl.multiple_of(j * tq, tq)
        kt = k_ref[0, pl.ds(start, width), :]
        vt = v_ref[0, pl.ds(start, width), :]
        for h in range(hp):
            r0 = h * tq
            s = lax.dot_general(qcat[r0:r0 + tq, :], kt, _NT, preferred_element_type=F32)
            if causal:
                r = lax.broadcasted_iota(jnp.int32, (tq, 1), 0)
                c = lax.broadcasted_iota(jnp.int32, (1, width), 1) - (width - tq)
                s = jnp.where(c <= r, s, MASK_VALUE)
            m_prev = m_ref[r0:r0 + tq, :]
            m_new = jnp.maximum(m_prev, jnp.max(s, axis=1, keepdims=True))
            alpha = jnp.exp2(m_prev - m_new)
            p = jnp.exp2(s - jnp.concatenate([m_new] * (width // LANES), axis=1))
            acc_ref[r0:r0 + tq, :] = alpha * acc_ref[r0:r0 + tq, :] + _dot(p.astype(BF16), vt)
            m_ref[r0:r0 + tq, :] = m_new

    def wide_tile(jw, carry):
        tile(SEL_WIDE * jw, SEL_WIDE * tq, False)
        return carry

    n_wide = i // SEL_WIDE
    lax.fori_loop(0, n_wide, wide_tile, 0)
    for rem in range(SEL_WIDE):
        pl.when(i % SEL_WIDE == rem)(functools.partial(tile, SEL_WIDE * n_wide, (rem + 1) * tq, True))

    gates = gates_ref[...]
    for h in range(hp):
        _store_head(o_ref, _normalized_head(acc_ref[h * tq:(h + 1) * tq, :]), gates, g, h, 1)


def _sel_attention(q, bias, kscat, vs, gates, B, S, tq=512):
    H, M, DH = q.shape
    G = N_KV_GROUPS
    tq = min(tq, S)
    nq = S // tq
    assert tq % LANES == 0 and tq % SEL_BLOCK == 0 and S % tq == 0
    rows = HEADS_PER_GROUP * tq
    return pl.pallas_call(
        functools.partial(_sel_body, tq=tq),
        grid=(B, G, nq),
        in_specs=[pl.BlockSpec((HEADS_PER_GROUP, tq, DH), lambda b, g, i: (g, b * nq + i, 0)),
                  pl.BlockSpec((1, tq, LANES), lambda b, g, i: (g, b * nq + i, 0)),
                  pl.BlockSpec((1, S, 2 * LANES), lambda b, g, i: (g, b, 0)),
                  pl.BlockSpec((1, S, LANES), lambda b, g, i: (g, b, 0)),
                  pl.BlockSpec((tq, LANES), lambda b, g, i: (b * nq + i, 0))],
        out_specs=pl.BlockSpec((tq, HEADS_PER_GROUP * DH), lambda b, g, i: (b * nq + i, g)),
        out_shape=jax.ShapeDtypeStruct((M, H * DH), BF16),
        scratch_shapes=[pltpu.VMEM((rows, 2 * LANES), BF16), pltpu.VMEM((rows, LANES), F32),
                        pltpu.VMEM((rows, LANES), F32)],
        compiler_params=_params("arbitrary", "arbitrary", "arbitrary"),
        name="nsa_selected",
    )(q, bias, kscat, vs, gates)


def _win_body(q_ref, k_ref, v_ref, gates_ref, o_ref, *, tq, n_back, n_sub):
    g = pl.program_id(1)
    r = lax.broadcasted_iota(jnp.int32, (tq, 1), 0)
    c = lax.broadcasted_iota(jnp.int32, (1, tq), 1)
    for sub in range(n_sub):
        i = pl.program_id(2) * n_sub + sub
        rows = slice(sub * tq, (sub + 1) * tq)
        gates = gates_ref[rows, :]
        tiles = []
        for back in range(n_back, -1, -1):
            jt = i - back
            start = pl.multiple_of(jnp.maximum(jt, 0) * tq, tq)
            d = r - c + back * tq
            ok = (d >= 0) & (d < WINDOW) & (jt >= 0)
            tiles.append((k_ref[0, pl.ds(start, tq), :], v_ref[0, pl.ds(start, tq), :], ok))
        for h in range(HEADS_PER_GROUP):
            q = q_ref[h, rows, :]
            scores = [jnp.where(ok, lax.dot_general(q, kt, _NT, preferred_element_type=F32), MASK_VALUE)
                      for kt, _, ok in tiles]
            m = functools.reduce(jnp.maximum, scores).max(axis=1, keepdims=True)
            acc = jnp.zeros((tq, LANES), F32)
            for s, (_, vt, _) in zip(scores, tiles):
                acc = acc + _dot(jnp.exp2(s - m).astype(BF16), vt)
            _store_head(o_ref, _normalized_head(acc), gates, g, h, 2, rows)


def _win_attention(q, kw, vw, gates, B, S, tq=256, n_sub=4):
    H, M, DH = q.shape
    G = N_KV_GROUPS
    tq = min(tq, S)
    n_sub = min(n_sub, S // tq)
    tile = tq
    tq = tile * n_sub
    nq = S // tq
    n_back = -(-WINDOW // tile)
    return pl.pallas_call(
        functools.partial(_win_body, tq=tile, n_back=n_back, n_sub=n_sub),
        grid=(B, G, nq),
        in_specs=[pl.BlockSpec((HEADS_PER_GROUP, tq, DH), lambda b, g, i: (g, b * nq + i, 0)),
                  pl.BlockSpec((1, S, DH), lambda b, g, i: (g, b, 0)),
                  pl.BlockSpec((1, S, LANES), lambda b, g, i: (g, b, 0)),
                  pl.BlockSpec((tq, LANES), lambda b, g, i: (b * nq + i, 0))],
        out_specs=pl.BlockSpec((tq, HEADS_PER_GROUP * DH), lambda b, g, i: (b * nq + i, g)),
        out_shape=jax.ShapeDtypeStruct((M, H * DH), BF16),
        compiler_params=_params("arbitrary", "arbitrary", "arbitrary"),
        name="nsa_window",
    )(q, kw, vw, gates)


def _outproj_body(x_ref, a_ref, b_ref, c_ref, w_ref, o_ref):
    o = a_ref[...].astype(F32) + b_ref[...].astype(F32) + c_ref[...].astype(F32)
    o_ref[...] = x_ref[...] + _dot(o.astype(BF16), w_ref[...])


def _outproj(x2, oc, os_, ow, w_out, tm=512):
    M, D = x2.shape
    K = w_out.shape[0]
    tm = min(tm, M)
    spec_o = pl.BlockSpec((tm, K), lambda i: (i, 0))
    return pl.pallas_call(
        _outproj_body,
        grid=(M // tm,),
        in_specs=[pl.BlockSpec((tm, D), lambda i: (i, 0)), spec_o, spec_o, spec_o, _resident((K, D))],
        out_specs=pl.BlockSpec((tm, D), lambda i: (i, 0)),
        out_shape=jax.ShapeDtypeStruct((M, D), F32),
        compiler_params=_params("arbitrary"),
        name="nsa_outproj",
    )(x2, oc, os_, ow, w_out.astype(BF16))


def _nsa_layer(x, gn, w_in, cmp_pe, cmp_w1, cmp_w2, w_out):
    B, S, D = x.shape
    x2 = x.reshape(B * S, D)
    q, kc, vc, kscat, vs, kw, vw, gates = _nsa_proj(x2, gn, w_in, S)
    kcmp, vcmp = _compress(kc, vc, cmp_pe, cmp_w1, cmp_w2, B, S)
    o_cmp, bias = _cmp_attention(q, kcmp, vcmp, gates, B, S)
    o_sel = _sel_attention(q, bias, kscat, vs, gates, B, S)
    o_win = _win_attention(q, kw, vw, gates, B, S)
    return _outproj(x2, o_cmp, o_sel, o_win, w_out).reshape(B, S, D)


def kernel(x, norm_mix, norm_ffn, norm_final, lru_w_in, lru_b_in, lru_conv_w, lru_conv_b, lru_w_a, lru_b_a, lru_w_i, lru_b_i, lru_lambda, lru_w_out, lru_b_out, nsa_w_in, nsa_cmp_pe, nsa_cmp_w1, nsa_cmp_w2, nsa_w_out, ffn_w_gate, ffn_w_up, ffn_w_down, moe_w_router, moe_w_gate, moe_w_up, moe_w_down):
    B, S, D = x.shape
    assert norm_mix.shape[0] == 2 and lru_w_in.shape[0] == 1 and nsa_w_in.shape[0] == 1
    x = _lru_layer(x, norm_mix[0], lru_w_in[0], lru_b_in[0], lru_conv_w[0], lru_conv_b[0], lru_w_a[0],
                   lru_b_a[0], lru_w_i[0], lru_b_i[0], lru_lambda[0], lru_w_out[0], lru_b_out[0])
    x2 = _ffn_layer(x.reshape(B * S, D), norm_ffn[0], ffn_w_gate, ffn_w_up, ffn_w_down)
    x2 = _nsa_layer(x2.reshape(B, S, D), norm_mix[1], nsa_w_in[0], nsa_cmp_pe[0], nsa_cmp_w1[0],
                    nsa_cmp_w2[0], nsa_w_out[0]).reshape(B * S, D)
    out = _moe_layer(x2, norm_ffn[1], moe_w_router[0], moe_w_gate[0], moe_w_up[0], moe_w_down[0], norm_final)
    return out.reshape(B, S, D)
```

```python
import functools

import numpy as np
import jax
import jax.numpy as jnp
from jax import lax
from jax.experimental import pallas as pl
from jax.experimental.pallas import tpu as pltpu

F32 = jnp.float32
BF16 = jnp.bfloat16

RMS_EPS = 1e-6
LRU_C = 8.0
CONV_WIDTH = 4
N_HEADS = 16
HEAD_DIM = 64
N_KV_GROUPS = 4
HEADS_PER_GROUP = N_HEADS // N_KV_GROUPS
CMP_BLOCK = 32
CMP_STRIDE = 16
SEL_BLOCK = 64
N_SEL = 16
WINDOW = 512
FORCE_SCORE = 1.0e4
ROPE_THETA = 10000.0
TOP_K = 2

LANES = 128
SUBLANES = 8
MXU_WIDTH = 256
MASK_VALUE = -1.0e30
VMEM_LIMIT_BYTES = 56 * 1024 * 1024

_NT = (((1,), (1,)), ((), ()))
Q_SCALE = HEAD_DIM ** -0.5 * 1.4426950408889634
SEL_WIDE = 4


def _params(*semantics):
    return pltpu.CompilerParams(dimension_semantics=semantics, vmem_limit_bytes=VMEM_LIMIT_BYTES)


def _resident(shape):
    zeros = (0,) * len(shape)
    return pl.BlockSpec(shape, lambda *_: zeros, pipeline_mode=pl.Buffered(1))


def _rms(x, g):
    return x * lax.rsqrt(jnp.mean(x * x, axis=-1, keepdims=True) + RMS_EPS) * g


def _dot(a, b):
    return jnp.dot(a, b, preferred_element_type=F32)


def _gelu_tanh(x):
    two_z = x * (2.0 * 0.7978845608028654 + (2.0 * 0.7978845608028654 * 0.044715) * (x * x))
    return x * jax.nn.sigmoid(two_z)


def _block_diag_dot(xb, w_ref, col0, n, bw):
    outs = []
    for c0 in range(0, n, MXU_WIDTH):
        w = min(MXU_WIDTH, n - c0)
        lo = (c0 // bw) * bw // LANES * LANES
        hi = min(n, -(-(((c0 + w - 1) // bw + 1) * bw) // LANES) * LANES)
        outs.append(_dot(xb[:, lo:hi], w_ref[lo:hi, col0 + c0:col0 + c0 + w]))
    return jnp.concatenate(outs, axis=1)


def _lane_col(vals, lane, idx):
    return jnp.sum(jnp.where(lane == idx, vals, 0.0), axis=1, keepdims=True)


def _lru_body(x_ref, gn_ref, win_ref, bin_ref, cw_ref, cb_ref, wg_ref, bg_ref, lam_ref,
              wout_ref, bout_ref, o_ref, xbuf, hcar, *, ts, dr, bw):
    t = pl.program_id(1)

    @pl.when(t == 0)
    def _():
        xbuf[0:8, :] = jnp.zeros((8, dr), F32)
        hcar[...] = jnp.zeros_like(hcar)

    x = x_ref[0]
    hn = _rms(x, gn_ref[...]).astype(BF16)
    proj = _dot(hn, win_ref[...]) + bin_ref[...]
    gate = _gelu_tanh(proj[:, :dr])
    xr = proj[:, dr:]

    xbuf[8:8 + ts, :] = xr
    xc = cb_ref[...] + xr * cw_ref[CONV_WIDTH - 1:CONV_WIDTH, :]
    for lag in range(1, CONV_WIDTH):
        k = CONV_WIDTH - 1 - lag
        xc = xc + xbuf[pl.ds(8 - lag, ts), :] * cw_ref[k:k + 1, :]
    xbuf[0:8, :] = xbuf[ts:ts + 8, :]

    xcb = xc.astype(BF16)
    r = jax.nn.sigmoid(_block_diag_dot(xcb, wg_ref, 0, dr, bw) + bg_ref[:, :dr])
    i = jax.nn.sigmoid(_block_diag_dot(xcb, wg_ref, dr, dr, bw) + bg_ref[:, dr:])
    z = -lam_ref[...]
    softplus = jnp.maximum(z, 0.0) + jnp.log(1.0 + jnp.exp(-jnp.abs(z)))
    log_a = (-LRU_C * r) * softplus
    a = jnp.exp(log_a)
    mult = jnp.sqrt(1.0 - a * a)
    row = lax.broadcasted_iota(jnp.int32, (ts, 1), 0)
    mult = jnp.where((row == 0) & (t == 0), 1.0, mult)
    u = mult * (i * xc)

    sub = row & (SUBLANES - 1)
    for shift in (1, 2, 4):
        keep = sub >= shift
        a_prev = jnp.where(keep, pltpu.roll(a, shift, 0), 1.0)
        u_prev = jnp.where(keep, pltpu.roll(u, shift, 0), 0.0)
        u = a * u_prev + u
        a = a * a_prev
    carry = hcar[...]
    groups = []
    for k in range(ts // SUBLANES):
        rows = slice(k * SUBLANES, (k + 1) * SUBLANES)
        hk = u[rows] + a[rows] * carry
        carry = hk[SUBLANES - 1:SUBLANES]
        groups.append(hk)
    h = jnp.concatenate(groups, axis=0)
    hcar[...] = carry

    y = (h * gate).astype(BF16)
    o_ref[0] = x + _dot(y, wout_ref[...]) + bout_ref[...]


def _lru_layer(x, gn, w_in, b_in, conv_w, conv_b, w_a, b_a, w_i, b_i, lam, w_out, b_out):
    B, S, D = x.shape
    dr = w_out.shape[0]
    ts = min(256, S)
    assert S % ts == 0 and ts % 8 == 0
    wg = jnp.concatenate([jax.scipy.linalg.block_diag(*w_a), jax.scipy.linalg.block_diag(*w_i)], axis=1)
    row = lambda v: v.reshape(1, -1)
    body = functools.partial(_lru_body, ts=ts, dr=dr, bw=w_a.shape[1])
    return pl.pallas_call(
        body,
        grid=(B, S // ts),
        in_specs=[
            pl.BlockSpec((1, ts, D), lambda b, t: (b, t, 0)),
            _resident((1, D)),
            _resident((D, 2 * dr)), _resident((1, 2 * dr)),
            _resident((CONV_WIDTH, dr)), _resident((1, dr)),
            _resident((dr, 2 * dr)), _resident((1, 2 * dr)),
            _resident((1, dr)),
            _resident((dr, D)), _resident((1, D)),
        ],
        out_specs=pl.BlockSpec((1, ts, D), lambda b, t: (b, t, 0)),
        out_shape=jax.ShapeDtypeStruct((B, S, D), F32),
        scratch_shapes=[pltpu.VMEM((ts + 8, dr), F32), pltpu.VMEM((1, dr), F32)],
        compiler_params=_params("arbitrary", "arbitrary"),
        name="lru_mixer",
    )(x, row(gn), w_in.astype(BF16), row(b_in), conv_w, row(conv_b), wg.astype(BF16),
      row(jnp.concatenate([b_a, b_i])), row(lam), w_out.astype(BF16), row(b_out))


def _swiglu_partial(hn, wg_ref, wu_ref, wd_ref):
    g = _dot(hn, wg_ref[0].astype(BF16))
    u = _dot(hn, wu_ref[0].astype(BF16))
    return _dot(((g * jax.nn.sigmoid(g)) * u).astype(BF16), wd_ref[0].astype(BF16))


def _ffn_body(x_ref, gn_ref, wg_ref, wu_ref, wd_ref, o_ref, hn_ref, acc_ref, *, n_f):
    f = pl.program_id(1)

    @pl.when(f == 0)
    def _():
        hn_ref[...] = _rms(x_ref[...], gn_ref[...]).astype(BF16)
        acc_ref[...] = jnp.zeros_like(acc_ref)

    acc_ref[...] += _swiglu_partial(hn_ref[...], wg_ref, wu_ref, wd_ref)

    @pl.when(f == n_f - 1)
    def _():
        o_ref[...] = x_ref[...] + acc_ref[...]


def _ffn_layer(x2, gn, w_gate, w_up, w_down, tm=1024, tf=512):
    M, D = x2.shape
    F = w_gate.shape[2]
    tm = min(tm, M)
    tf = min(tf, F)
    assert M % tm == 0 and F % tf == 0
    n_f = F // tf
    return pl.pallas_call(
        functools.partial(_ffn_body, n_f=n_f),
        grid=(M // tm, n_f),
        in_specs=[pl.BlockSpec((tm, D), lambda i, f: (i, 0)), _resident((1, D)),
                  pl.BlockSpec((1, D, tf), lambda i, f: (0, 0, f)),
                  pl.BlockSpec((1, D, tf), lambda i, f: (0, 0, f)),
                  pl.BlockSpec((1, tf, D), lambda i, f: (0, f, 0))],
        out_specs=pl.BlockSpec((tm, D), lambda i, f: (i, 0)),
        out_shape=jax.ShapeDtypeStruct((M, D), F32),
        scratch_shapes=[pltpu.VMEM((tm, D), BF16), pltpu.VMEM((tm, D), F32)],
        compiler_params=_params("arbitrary", "arbitrary"),
        name="dense_swiglu",
    )(x2, gn.reshape(1, D), w_gate, w_up, w_down)


_HIGH_HALF = 0xFFFF0000


def _pack_bf16_pair(lo, hi):
    lo_bits = lax.bitcast_convert_type(lo, jnp.uint32)
    hi_bits = lax.bitcast_convert_type(hi, jnp.uint32)
    return (lo_bits >> 16) | (hi_bits & jnp.uint32(_HIGH_HALF))


def _unpack_bf16_pair(w):
    return (lax.bitcast_convert_type(w << 16, F32),
            lax.bitcast_convert_type(w & jnp.uint32(_HIGH_HALF), F32))


def _router_body(x_ref, gn_ref, wh_ref, wl_ref, tril_ref, o_ref, cnt_ref, hp_ref, *, n_experts):
    @pl.when(pl.program_id(0) == 0)
    def _():
        cnt_ref[...] = jnp.zeros_like(cnt_ref)

    hn = _rms(x_ref[...], gn_ref[...])
    hh = hn.astype(BF16)
    hb = hh.astype(F32)
    half = hn.shape[1] // 2
    hp_ref[...] = _pack_bf16_pair(hb[:, :half], hb[:, half:])
    hl = (hn - hb).astype(BF16)
    wh = wh_ref[...]
    logits = _dot(hh, wh) + _dot(hh, wl_ref[...]) + _dot(hl, wh)
    lane = lax.broadcasted_iota(jnp.int32, logits.shape, 1)
    lg = jnp.where(lane < n_experts, logits, -jnp.inf)
    m0 = jnp.max(lg, axis=1, keepdims=True)
    i0 = jnp.min(jnp.where(lg == m0, lane, LANES), axis=1, keepdims=True)
    lg = jnp.where(lane == i0, -jnp.inf, lg)
    m1 = jnp.max(lg, axis=1, keepdims=True)
    i1 = jnp.min(jnp.where(lg == m1, lane, LANES), axis=1, keepdims=True)
    e1 = jnp.exp(m1 - m0)
    w0 = 1.0 / (1.0 + e1)
    routed = ((lane == i0) | (lane == i1)).astype(F32)
    incl = _dot(tril_ref[...], routed.astype(BF16))
    excl = incl - routed + cnt_ref[0:1, :]

    def put(col, v):
        return jnp.where(lane == col, v, 0.0)

    o_ref[...] = (put(META_I0, i0.astype(F32)) + put(META_I1, i1.astype(F32)) + put(META_W0, w0)
                  + put(META_W1, e1 * w0) + put(META_R0, _lane_col(excl, lane, i0))
                  + put(META_R1, _lane_col(excl, lane, i1)))
    cnt_ref[...] = jnp.broadcast_to(cnt_ref[0:1, :] + incl[incl.shape[0] - 1:, :], cnt_ref.shape)


META_I0, META_I1, META_W0, META_W1, META_R0, META_R1 = range(6)


def _router(x2, gn, w_router, tm=512):
    M, D = x2.shape
    n_experts = w_router.shape[1]
    tm = min(tm, M)
    wpad = jnp.pad(w_router, ((0, 0), (0, LANES - n_experts)))
    wh = wpad.astype(BF16)
    wl = (wpad - wh.astype(F32)).astype(BF16)
    tril = jnp.tril(jnp.ones((tm, tm), BF16))
    return pl.pallas_call(
        functools.partial(_router_body, n_experts=n_experts),
        grid=(M // tm,),
        in_specs=[pl.BlockSpec((tm, D), lambda i: (i, 0)), _resident((1, D)),
                  _resident((D, LANES)), _resident((D, LANES)), _resident((tm, tm))],
        out_specs=[pl.BlockSpec((tm, LANES), lambda i: (i, 0)), pl.BlockSpec((8, LANES), lambda i: (0, 0)),
                   pl.BlockSpec((tm, D // 2), lambda i: (i, 0))],
        out_shape=[jax.ShapeDtypeStruct((M, LANES), F32), jax.ShapeDtypeStruct((8, LANES), F32),
                   jax.ShapeDtypeStruct((M, D // 2), jnp.uint32)],
        compiler_params=_params("arbitrary"),
        name="moe_router",
    )(x2, gn.reshape(1, D), wh, wl, tril)


def _combine_body(pos_ref, pos_next_ref, x_ref, meta_ref, y_ref, gfin_ref, o_ref, ybuf, sem, *, tm, n_tiles):
    i = pl.program_id(0)
    slot = i % 2

    def row_copy(idx_ref, k, r, s):
        return pltpu.make_async_copy(y_ref.at[pl.ds(idx_ref[0, 0, k * tm + r], 1)], ybuf.at[s, k, pl.ds(r, 1)],
                                     sem.at[s])

    @pl.when(i == 0)
    def _():
        def body(r, c):
            row_copy(pos_ref, 0, r, 0).start()
            row_copy(pos_ref, 1, r, 0).start()
            return c
        lax.fori_loop(0, tm, body, 0, unroll=8)

    @pl.when(i + 1 < n_tiles)
    def _():
        for r in range(tm):
            row_copy(pos_next_ref, 0, r, 1 - slot).start(priority=0)
            row_copy(pos_next_ref, 1, r, 1 - slot).start(priority=1)

    pltpu.make_async_copy(ybuf.at[slot], ybuf.at[slot], sem.at[slot]).wait()
    meta = meta_ref[...]
    lane = lax.broadcasted_iota(jnp.int32, meta.shape, 1)
    y0 = jnp.concatenate(_unpack_bf16_pair(ybuf[slot, 0]), axis=1)
    y1 = jnp.concatenate(_unpack_bf16_pair(ybuf[slot, 1]), axis=1)
    out = x_ref[...] + _lane_col(meta, lane, META_W0) * y0 + _lane_col(meta, lane, META_W1) * y1
    o_ref[...] = _rms(out, gfin_ref[...])


def _combine(x2, meta, y, pos0, pos1, g_final, tm=512):
    M, D = x2.shape
    tm = min(tm, M)
    n_tiles = M // tm
    pos = jnp.concatenate([pos0.reshape(n_tiles, 1, tm), pos1.reshape(n_tiles, 1, tm)], axis=2)
    return pl.pallas_call(
        functools.partial(_combine_body, tm=tm, n_tiles=n_tiles),
        grid=(n_tiles,),
        in_specs=[pl.BlockSpec((1, 1, 2 * tm), lambda i: (i, 0, 0), memory_space=pltpu.SMEM),
                  pl.BlockSpec((1, 1, 2 * tm), lambda i: (jnp.minimum(i + 1, n_tiles - 1), 0, 0),
                               memory_space=pltpu.SMEM),
                  pl.BlockSpec((tm, D), lambda i: (i, 0)),
                  pl.BlockSpec((tm, LANES), lambda i: (i, 0)),
                  pl.BlockSpec(memory_space=pl.ANY),
                  _resident((1, D))],
        out_specs=pl.BlockSpec((tm, D), lambda i: (i, 0)),
        out_shape=jax.ShapeDtypeStruct((M, D), F32),
        scratch_shapes=[pltpu.VMEM((2, 2, tm, D // 2), jnp.uint32), pltpu.SemaphoreType.DMA((2,))],
        compiler_params=_params("arbitrary"),
        name="moe_combine",
    )(pos, pos, x2, meta, y, g_final.reshape(1, D))


GATHER_AHEAD = 2


def _grouped_ffn_body(te_ref, nused_ref, src_ref, src_next_ref, src_ahead_ref, x_ref, wg_ref, wu_ref,
                      wd_ref, o_ref, xbuf, hn_ref, acc_ref, sem, *, tm, n_f):
    j = pl.program_id(0)
    f = pl.program_id(1)
    n_used = nused_ref[0]
    used = j < n_used
    gathered = j < n_used + GATHER_AHEAD
    slot = j % (GATHER_AHEAD + 1)
    slot_ahead = (j + GATHER_AHEAD) % (GATHER_AHEAD + 1)
    chunk = tm // (n_f + 1)

    def row_copy(idx_ref, r, s):
        return pltpu.make_async_copy(x_ref.at[pl.ds(idx_ref[0, 0, r], 1)], xbuf.at[s, pl.ds(r, 1)], sem.at[s])

    def start_chunk(first):
        for k in range(chunk):
            row_copy(src_ahead_ref, first + k, slot_ahead).start()

    @pl.when((j == 0) & (f == 0))
    def _():
        def body(r, c):
            row_copy(src_ref, r, 0).start()
            row_copy(src_next_ref, r, 1).start()
            return c
        lax.fori_loop(0, tm, body, 0, unroll=8)

    @pl.when(f == 0)
    def _():
        acc_ref[...] = jnp.zeros_like(acc_ref)

        @pl.when(gathered)
        def _():
            pltpu.make_async_copy(xbuf.at[slot], xbuf.at[slot], sem.at[slot]).wait()
            lo, hi = _unpack_bf16_pair(xbuf[slot])
            half = lo.shape[1]
            hn_ref[:, :half] = lo.astype(BF16)
            hn_ref[:, half:] = hi.astype(BF16)

        @pl.when(used)
        def _():
            start_chunk(0)

    @pl.when(used)
    def _():
        acc_ref[...] += _swiglu_partial(hn_ref[...], wg_ref, wu_ref, wd_ref)
        start_chunk((f + 1) * chunk)

    @pl.when(f == n_f - 1)
    def _():
        out = acc_ref[...].astype(BF16).astype(F32)
        half = out.shape[1] // 2
        o_ref[...] = _pack_bf16_pair(out[:, :half], out[:, half:])


def _grouped_ffn(hp, src, w_gate, w_up, w_down, tile_expert, n_used, tm, tf=512):
    P = src.shape[0]
    D = w_gate.shape[1]
    F = w_gate.shape[2]
    tf = min(tf, F)
    n_f = F // tf
    n_tiles = P // tm
    assert F % tf == 0 and tm % (n_f + 1) == 0
    src3 = src.reshape(n_tiles, 1, tm)

    def src_spec(ahead):
        return pl.BlockSpec((1, 1, tm), lambda j, f, te, nu: (jnp.minimum(j + ahead, n_tiles - 1), 0, 0),
                            memory_space=pltpu.SMEM)

    grid_spec = pltpu.PrefetchScalarGridSpec(
        num_scalar_prefetch=2,
        grid=(n_tiles, n_f),
        in_specs=[src_spec(a) for a in range(GATHER_AHEAD + 1)] + [
                  pl.BlockSpec(memory_space=pl.ANY),
                  pl.BlockSpec((1, D, tf), lambda j, f, te, nu: (te[j], 0, f)),
                  pl.BlockSpec((1, D, tf), lambda j, f, te, nu: (te[j], 0, f)),
                  pl.BlockSpec((1, tf, D), lambda j, f, te, nu: (te[j], f, 0))],
        out_specs=pl.BlockSpec((tm, D // 2), lambda j, f, te, nu: (j, 0)),
        scratch_shapes=[pltpu.VMEM((GATHER_AHEAD + 1, tm, D // 2), jnp.uint32), pltpu.VMEM((tm, D), BF16),
                        pltpu.VMEM((tm, D), F32), pltpu.SemaphoreType.DMA((GATHER_AHEAD + 1,))])
    return pl.pallas_call(
        functools.partial(_grouped_ffn_body, tm=tm, n_f=n_f),
        grid_spec=grid_spec,
        out_shape=jax.ShapeDtypeStruct((P, D // 2), jnp.uint32),
        compiler_params=_params("arbitrary", "arbitrary"),
        name="moe_grouped_swiglu",
    )(tile_expert, n_used, *([src3] * (GATHER_AHEAD + 1)), hp, w_gate, w_up, w_down)


def _moe_layer(x2, gn, w_router, w_gate, w_up, w_down, g_final, tm=1024):
    M, D = x2.shape
    n_e = w_router.shape[1]
    tm = min(tm, M)
    meta, counts, hp = _router(x2, gn, w_router)
    as_int = lambda col: meta[:, col].astype(jnp.int32)
    i0, i1, r0, r1 = as_int(META_I0), as_int(META_I1), as_int(META_R0), as_int(META_R1)
    padded = (counts[0, :n_e].astype(jnp.int32) + tm - 1) // tm * tm
    ends = jnp.cumsum(padded)
    offsets = ends - padded
    pos0 = offsets[i0] + r0
    pos1 = offsets[i1] + r1
    P = TOP_K * M + (n_e - 1 + GATHER_AHEAD) * tm
    tok = jnp.arange(M, dtype=jnp.int32)
    src = jnp.zeros((P,), jnp.int32).at[jnp.concatenate([pos0, pos1])].set(jnp.concatenate([tok, tok]))
    tile_start = jnp.arange(P // tm, dtype=jnp.int32) * tm
    tile_expert = jnp.minimum(jnp.searchsorted(ends, tile_start, side="right"), n_e - 1).astype(jnp.int32)
    n_used = (ends[n_e - 1:] // tm).astype(jnp.int32)
    y = _grouped_ffn(hp, src, w_gate, w_up, w_down, tile_expert, n_used, tm)
    return _combine(x2, meta, y, pos0, pos1, g_final)


def _nsa_proj_body(x_ref, gn_ref, w_ref, cos_ref, sin_ref,
                   q_ref, kc_ref, vc_ref, ks_ref, vs_ref, kw_ref, vw_ref, g_ref, *, tm, tiles_per_seq):
    i = pl.program_id(0)
    hn = _rms(x_ref[...], gn_ref[...]).astype(BF16)
    proj = _dot(hn, w_ref[...])
    lane = lax.broadcasted_iota(jnp.int32, (tm, LANES), 1)
    first_half = (lane & (HEAD_DIM - 1)) < HEAD_DIM // 2
    cos = cos_ref[...]
    sin = sin_ref[...]

    def chunk(c):
        return proj[:, c * LANES:(c + 1) * LANES]

    def rope(v):
        rot = jnp.where(first_half, pltpu.roll(v, LANES - HEAD_DIM // 2, 1), pltpu.roll(v, HEAD_DIM // 2, 1))
        return v * cos + rot * sin

    def heads(v):
        return v[:, :HEAD_DIM], pltpu.roll(v, HEAD_DIM, 1)[:, :HEAD_DIM]

    c = 0
    for cc in range(N_HEADS // 2):
        lo, hi = heads(rope(chunk(c)) * Q_SCALE); c += 1
        q_ref[2 * cc] = lo.astype(BF16)
        q_ref[2 * cc + 1] = hi.astype(BF16)
    for cc in range(N_KV_GROUPS // 2):
        lo, hi = heads(rope(chunk(c))); c += 1
        kc_ref[2 * cc] = lo
        kc_ref[2 * cc + 1] = hi
    for cc in range(N_KV_GROUPS // 2):
        lo, hi = heads(chunk(c)); c += 1
        vc_ref[2 * cc] = lo
        vc_ref[2 * cc + 1] = hi
    pos = (i % tiles_per_seq) * tm + lax.broadcasted_iota(jnp.int32, (tm, 1), 0)
    onehot = (lane == pos // SEL_BLOCK).astype(BF16)
    for cc in range(N_KV_GROUPS // 2):
        v = rope(chunk(c)); c += 1
        for j, vv in enumerate((v, pltpu.roll(v, HEAD_DIM, 1))):
            ks_ref[2 * cc + j, :, 0:LANES] = jnp.where(lane < HEAD_DIM, vv, 0.0).astype(BF16)
            ks_ref[2 * cc + j, :, LANES:2 * LANES] = onehot
    ones_col = jnp.where(lane == HEAD_DIM, 1.0, 0.0)

    def store_values(ref):
        nonlocal c
        for cc in range(N_KV_GROUPS // 2):
            v = chunk(c); c += 1
            for j, vv in enumerate((v, pltpu.roll(v, HEAD_DIM, 1))):
                ref[2 * cc + j] = jnp.where(lane < HEAD_DIM, vv, ones_col).astype(BF16)

    store_values(vs_ref)
    for cc in range(N_KV_GROUPS // 2):
        lo, hi = heads(rope(chunk(c))); c += 1
        kw_ref[2 * cc] = lo.astype(BF16)
        kw_ref[2 * cc + 1] = hi.astype(BF16)
    store_values(vw_ref)
    g_ref[...] = jax.nn.sigmoid(chunk(c))


def _nsa_proj(x2, gn, w_in, S, tm=256):
    M, D = x2.shape
    H, G, DH = N_HEADS, N_KV_GROUPS, HEAD_DIM
    tm = min(tm, S)
    assert S % tm == 0
    n_in = w_in.shape[1]
    n_pad = -(-n_in // LANES) * LANES
    wp = jnp.pad(w_in, ((0, 0), (0, n_pad - n_in))).astype(BF16)
    half = DH // 2
    freqs = ROPE_THETA ** (-jnp.arange(half, dtype=F32) / half)
    ang = jnp.arange(S, dtype=F32)[:, None] * freqs[None, :]
    cos = jnp.tile(jnp.cos(ang), (1, 2 * LANES // DH))
    sin = jnp.tile(jnp.concatenate([-jnp.sin(ang), jnp.sin(ang)], axis=1), (1, LANES // DH))
    tiles_per_seq = S // tm
    hd = lambda n, dt: jax.ShapeDtypeStruct((n, M, DH), dt)
    hspec = lambda n: pl.BlockSpec((n, tm, DH), lambda i: (0, i, 0))
    vd = jax.ShapeDtypeStruct((G, M, LANES), BF16)
    vspec = pl.BlockSpec((G, tm, LANES), lambda i: (0, i, 0))
    return pl.pallas_call(
        functools.partial(_nsa_proj_body, tm=tm, tiles_per_seq=tiles_per_seq),
        grid=(M // tm,),
        in_specs=[pl.BlockSpec((tm, D), lambda i: (i, 0)), _resident((1, D)), _resident((D, n_pad)),
                  pl.BlockSpec((tm, LANES), lambda i: (i % tiles_per_seq, 0)),
                  pl.BlockSpec((tm, LANES), lambda i: (i % tiles_per_seq, 0))],
        out_specs=[hspec(H), hspec(G), hspec(G),
                   pl.BlockSpec((G, tm, 2 * LANES), lambda i: (0, i, 0)),
                   vspec, hspec(G), vspec,
                   pl.BlockSpec((tm, LANES), lambda i: (i, 0))],
        out_shape=[hd(H, BF16), hd(G, F32), hd(G, F32),
                   jax.ShapeDtypeStruct((G, M, 2 * LANES), BF16),
                   vd, hd(G, BF16), vd,
                   jax.ShapeDtypeStruct((M, LANES), F32)],
        compiler_params=_params("arbitrary"),
        name="nsa_proj",
    )(x2, gn.reshape(1, D), wp, cos, sin)


def _compress_body(kc_ref, vc_ref, pe_ref, w1_ref, w2_ref, ko_ref, vo_ref, *, nc):
    half = CMP_BLOCK // 2
    for kv, (src, dst) in enumerate(((kc_ref, ko_ref), (vc_ref, vo_ref))):
        top = jnp.zeros((nc, w1_ref.shape[2]), F32)
        bot = jnp.zeros((nc, w1_ref.shape[2]), F32)
        for j in range(half):
            xj = src[0, pl.ds(j, nc, stride=CMP_STRIDE), :]
            top = top + _dot((xj + pe_ref[kv, j:j + 1, :]).astype(BF16),
                             w1_ref[kv, j * HEAD_DIM:(j + 1) * HEAD_DIM, :])
            bot = bot + _dot((xj + pe_ref[kv, half + j:half + j + 1, :]).astype(BF16),
                             w1_ref[kv, (half + j) * HEAD_DIM:(half + j + 1) * HEAD_DIM, :])
        hid = top + pltpu.roll(bot, nc - 1, 0)
        dst[0] = _dot(_gelu_tanh(hid).astype(BF16), w2_ref[kv]).astype(BF16)


def _compress(kc, vc, pe, w1, w2, B, S):
    assert CMP_BLOCK == 2 * CMP_STRIDE
    G, M, DH = kc.shape
    nc = S // CMP_STRIDE
    spec_in = pl.BlockSpec((1, S, DH), lambda b, g: (g, b, 0))
    spec_out = pl.BlockSpec((1, nc, DH), lambda b, g: (g, b, 0))
    out = jax.ShapeDtypeStruct((G, B * nc, DH), BF16)
    return pl.pallas_call(
        functools.partial(_compress_body, nc=nc),
        grid=(B, G),
        in_specs=[spec_in, spec_in, _resident(pe.shape), _resident(w1.shape), _resident(w2.shape)],
        out_specs=[spec_out, spec_out],
        out_shape=[out, out],
        compiler_params=_params("arbitrary", "arbitrary"),
        name="nsa_compress",
    )(kc, vc, pe, w1.astype(BF16), w2.astype(BF16))


def _store_head(o_ref, oh, gates, g, h, branch, rows=slice(None)):
    lane = lax.broadcasted_iota(jnp.int32, gates.shape, 1)
    col = 3 * (HEADS_PER_GROUP * g + h) + branch
    o_ref[rows, h * HEAD_DIM:(h + 1) * HEAD_DIM] = (oh * _lane_col(gates, lane, col)).astype(o_ref.dtype)


def _normalized_head(acc):
    return acc[:, :HEAD_DIM] * (1.0 / acc[:, HEAD_DIM:HEAD_DIM + 1])


def _cmp_body(q_ref, kc_ref, vc_ref, gates_ref, ov_ref, o_ref, bias_ref, imp_ref, *, tq, nc, n_s, k_sel):
    g = pl.program_id(1)
    i = pl.program_id(2)
    gates = gates_ref[...]
    t = i * tq + lax.broadcasted_iota(jnp.int32, (tq, 1), 0)

    def attend(w):
        kc = kc_ref[0, 0:w, :]
        vc = vc_ref[0, 0:w, :]
        cend = lax.broadcasted_iota(jnp.int32, (1, w), 1) * CMP_STRIDE + (CMP_BLOCK - 1)
        mask = cend <= t
        ps = None
        for h in range(HEADS_PER_GROUP):
            s = lax.dot_general(q_ref[h], kc, _NT, preferred_element_type=F32)
            s = jnp.where(mask, s, MASK_VALUE)
            m = jnp.max(s, axis=1, keepdims=True)
            m = jnp.where(m > 0.5 * MASK_VALUE, m, 0.0)
            e = jnp.exp2(s - m)
            p = e * (1.0 / jnp.maximum(jnp.sum(e, axis=1, keepdims=True), 1e-30))
            _store_head(o_ref, _dot(p.astype(BF16), vc), gates, g, h, 0)
            ps = p if ps is None else ps + p
        ph = ps.astype(BF16)
        pl_ = (ps - ph.astype(F32)).astype(BF16)
        imp_ref[...] = _dot(ph, ov_ref[0:w, :]) + _dot(pl_, ov_ref[0:w, :])

    n_chunks = nc // LANES
    need = ((i + 1) * (tq // CMP_STRIDE) + LANES - 1) // LANES
    for k in range(1, n_chunks + 1):
        pl.when((need == k) if k < n_chunks else (need >= k))(functools.partial(attend, k * LANES))

    j = lax.broadcasted_iota(jnp.int32, (tq, LANES), 1)
    cur = t // SEL_BLOCK
    forced = (j == 0) | (j == cur) | (j == cur - 1)
    valid = j * SEL_BLOCK <= t
    score = jnp.where(forced, FORCE_SCORE, jnp.where(valid, imp_ref[...], -1.0))
    score = jnp.where(j < n_s, score, -jnp.inf)
    x = score.T
    blk_i = lax.broadcasted_iota(jnp.int32, (LANES, tq), 0)
    blk = blk_i.astype(F32)

    def topk(x, n_iter):
        for _ in range(n_iter):
            mx = jnp.max(x, axis=0, keepdims=True)
            idx = jnp.min(jnp.where(x == mx, blk, float(LANES)), axis=0, keepdims=True)
            x = jnp.where(blk == idx, -jnp.inf, x)
        sel = (x == -jnp.inf) & (blk_i < n_s)
        bias_ref[0] = jnp.where(sel, 0.0, MASK_VALUE).T.astype(BF16)

    n_forced = 3
    direct = (i * tq >= 2 * SEL_BLOCK) & (k_sel > n_forced)

    @pl.when(direct)
    def _():
        cur_t = (i * tq + lax.broadcasted_iota(jnp.int32, (1, tq), 1)) // SEL_BLOCK
        forced_t = (blk_i == 0) | (blk_i == cur_t) | (blk_i == cur_t - 1)
        topk(jnp.where(forced_t, -jnp.inf, x), k_sel - n_forced)

    @pl.when(jnp.logical_not(direct))
    def _():
        topk(x, k_sel)


def _overlap_matrix(nc, n_s):
    r = CMP_BLOCK // CMP_STRIDE
    qn = SEL_BLOCK // CMP_STRIDE
    m = np.zeros((nc, LANES), np.float32)
    n_c = nc - r + 1
    chunks = np.arange(n_c)[:, None] + np.arange(r)[None, :]
    np.add.at(m, (np.repeat(np.arange(n_c), r), (chunks // qn).ravel()), 1.0)
    return m


def _cmp_attention(q, kcmp, vcmp, gates, B, S, tq=1024):
    H, M, DH = q.shape
    G = N_KV_GROUPS
    tq = min(tq, S)
    nq = S // tq
    nc = S // CMP_STRIDE
    n_s = S // SEL_BLOCK
    assert n_s <= LANES and tq & (tq - 1) == 0
    k_sel = min(N_SEL, n_s)
    ov = jnp.asarray(_overlap_matrix(nc, n_s), BF16)
    return pl.pallas_call(
        functools.partial(_cmp_body, tq=tq, nc=nc, n_s=n_s, k_sel=k_sel),
        grid=(B, G, nq),
        in_specs=[pl.BlockSpec((HEADS_PER_GROUP, tq, DH), lambda b, g, i: (g, b * nq + i, 0)),
                  pl.BlockSpec((1, nc, DH), lambda b, g, i: (g, b, 0)),
                  pl.BlockSpec((1, nc, DH), lambda b, g, i: (g, b, 0)),
                  pl.BlockSpec((tq, LANES), lambda b, g, i: (b * nq + i, 0)),
                  _resident((nc, LANES))],
        out_specs=[pl.BlockSpec((tq, HEADS_PER_GROUP * DH), lambda b, g, i: (b * nq + i, g)),
                   pl.BlockSpec((1, tq, LANES), lambda b, g, i: (g, b * nq + i, 0))],
        out_shape=[jax.ShapeDtypeStruct((M, H * DH), BF16), jax.ShapeDtypeStruct((G, M, LANES), BF16)],
        scratch_shapes=[pltpu.VMEM((tq, LANES), F32)],
        compiler_params=_params("arbitrary", "arbitrary", "arbitrary"),
        name="nsa_cmp_select",
    )(q, kcmp, vcmp, gates, ov)


def _sel_body(q_ref, bias_ref, k_ref, v_ref, gates_ref, o_ref, qcat, m_ref, acc_ref, *, tq):
    g = pl.program_id(1)
    i = pl.program_id(2)
    hp = HEADS_PER_GROUP
    rows = hp * tq
    qcat[:, 0:LANES] = jnp.zeros((rows, LANES), BF16)
    qcat[:, 0:HEAD_DIM] = q_ref[...].reshape(rows, HEAD_DIM)
    bias = bias_ref[0]
    for h in range(hp):
        qcat[h * tq:(h + 1) * tq, LANES:2 * LANES] = bias
    m_ref[...] = jnp.full((rows, LANES), MASK_VALUE, F32)
    acc_ref[...] = jnp.zeros((rows, LANES), F32)

    def tile(j, width, causal):
        start = pl.multiple_of(j * tq, tq)
        kt = k_ref[0, pl.ds(start, width), :]
        vt = v_ref[0, pl.ds(start, width), :]
        for h in range(hp):
            r0 = h * tq
            s = lax.dot_general(qcat[r0:r0 + tq, :], kt, _NT, preferred_element_type=F32)
            if causal:
                r = lax.broadcasted_iota(jnp.int32, (tq, 1), 0)
                c = lax.broadcasted_iota(jnp.int32, (1, width), 1) - (width - tq)
                s = jnp.where(c <= r, s, MASK_VALUE)
            m_prev = m_ref[r0:r0 + tq, :]
            m_new = jnp.maximum(m_prev, jnp.max(s, axis=1, keepdims=True))
            alpha = jnp.exp2(m_prev - m_new)
            p = jnp.exp2(s - jnp.concatenate([m_new] * (width // LANES), axis=1))
            acc_ref[r0:r0 + tq, :] = alpha * acc_ref[r0:r0 + tq, :] + _dot(p.astype(BF16), vt)
            m_ref[r0:r0 + tq, :] = m_new

    def wide_tile(jw, carry):
        tile(SEL_WIDE * jw, SEL_WIDE * tq, False)
        return carry

    n_wide = i // SEL_WIDE
    lax.fori_loop(0, n_wide, wide_tile, 0)
    for rem in range(SEL_WIDE):
        pl.when(i % SEL_WIDE == rem)(functools.partial(tile, SEL_WIDE * n_wide, (rem + 1) * tq, True))

    gates = gates_ref[...]
    for h in range(hp):
        _store_head(o_ref, _normalized_head(acc_ref[h * tq:(h + 1) * tq, :]), gates, g, h, 1)


def _sel_attention(q, bias, kscat, vs, gates, B, S, tq=512):
    H, M, DH = q.shape
    G = N_KV_GROUPS
    tq = min(tq, S)
    nq = S // tq
    assert tq % LANES == 0 and tq % SEL_BLOCK == 0 and S % tq == 0
    rows = HEADS_PER_GROUP * tq
    return pl.pallas_call(
        functools.partial(_sel_body, tq=tq),
        grid=(B, G, nq),
        in_specs=[pl.BlockSpec((HEADS_PER_GROUP, tq, DH), lambda b, g, i: (g, b * nq + i, 0)),
                  pl.BlockSpec((1, tq, LANES), lambda b, g, i: (g, b * nq + i, 0)),
                  pl.BlockSpec((1, S, 2 * LANES), lambda b, g, i: (g, b, 0)),
                  pl.BlockSpec((1, S, LANES), lambda b, g, i: (g, b, 0)),
                  pl.BlockSpec((tq, LANES), lambda b, g, i: (b * nq + i, 0))],
        out_specs=pl.BlockSpec((tq, HEADS_PER_GROUP * DH), lambda b, g, i: (b * nq + i, g)),
        out_shape=jax.ShapeDtypeStruct((M, H * DH), BF16),
        scratch_shapes=[pltpu.VMEM((rows, 2 * LANES), BF16), pltpu.VMEM((rows, LANES), F32),
                        pltpu.VMEM((rows, LANES), F32)],
        compiler_params=_params("arbitrary", "arbitrary", "arbitrary"),
        name="nsa_selected",
    )(q, bias, kscat, vs, gates)


def _win_body(q_ref, k_ref, v_ref, gates_ref, o_ref, *, tq, n_back, n_sub):
    g = pl.program_id(1)
    r = lax.broadcasted_iota(jnp.int32, (tq, 1), 0)
    c = lax.broadcasted_iota(jnp.int32, (1, tq), 1)
    for sub in range(n_sub):
        i = pl.program_id(2) * n_sub + sub
        rows = slice(sub * tq, (sub + 1) * tq)
        gates = gates_ref[rows, :]
        tiles = []
        for back in range(n_back, -1, -1):
            jt = i - back
            start = pl.multiple_of(jnp.maximum(jt, 0) * tq, tq)
            d = r - c + back * tq
            ok = (d >= 0) & (d < WINDOW) & (jt >= 0)
            tiles.append((k_ref[0, pl.ds(start, tq), :], v_ref[0, pl.ds(start, tq), :], ok))
        for h in range(HEADS_PER_GROUP):
            q = q_ref[h, rows, :]
            scores = [jnp.where(ok, lax.dot_general(q, kt, _NT, preferred_element_type=F32), MASK_VALUE)
                      for kt, _, ok in tiles]
            m = functools.reduce(jnp.maximum, scores).max(axis=1, keepdims=True)
            acc = jnp.zeros((tq, LANES), F32)
            for s, (_, vt, _) in zip(scores, tiles):
                acc = acc + _dot(jnp.exp2(s - m).astype(BF16), vt)
            _store_head(o_ref, _normalized_head(acc), gates, g, h, 2, rows)


def _win_attention(q, kw, vw, gates, B, S, tq=256, n_sub=4):
    H, M, DH = q.shape
    G = N_KV_GROUPS
    tq = min(tq, S)
    n_sub = min(n_sub, S // tq)
    tile = tq
    tq = tile * n_sub
    nq = S // tq
    n_back = -(-WINDOW // tile)
    return pl.pallas_call(
        functools.partial(_win_body, tq=tile, n_back=n_back, n_sub=n_sub),
        grid=(B, G, nq),
        in_specs=[pl.BlockSpec((HEADS_PER_GROUP, tq, DH), lambda b, g, i: (g, b * nq + i, 0)),
                  pl.BlockSpec((1, S, DH), lambda b, g, i: (g, b, 0)),
                  pl.BlockSpec((1, S, LANES), lambda b, g, i: (g, b, 0)),
                  pl.BlockSpec((tq, LANES), lambda b, g, i: (b * nq + i, 0))],
        out_specs=pl.BlockSpec((tq, HEADS_PER_GROUP * DH), lambda b, g, i: (b * nq + i, g)),
        out_shape=jax.ShapeDtypeStruct((M, H * DH), BF16),
        compiler_params=_params("arbitrary", "arbitrary", "arbitrary"),
        name="nsa_window",
    )(q, kw, vw, gates)


def _outproj_body(x_ref, a_ref, b_ref, c_ref, w_ref, o_ref):
    o = a_ref[...].astype(F32) + b_ref[...].astype(F32) + c_ref[...].astype(F32)
    o_ref[...] = x_ref[...] + _dot(o.astype(BF16), w_ref[...])


def _outproj(x2, oc, os_, ow, w_out, tm=512):
    M, D = x2.shape
    K = w_out.shape[0]
    tm = min(tm, M)
    spec_o = pl.BlockSpec((tm, K), lambda i: (i, 0))
    return pl.pallas_call(
        _outproj_body,
        grid=(M // tm,),
        in_specs=[pl.BlockSpec((tm, D), lambda i: (i, 0)), spec_o, spec_o, spec_o, _resident((K, D))],
        out_specs=pl.BlockSpec((tm, D), lambda i: (i, 0)),
        out_shape=jax.ShapeDtypeStruct((M, D), F32),
        compiler_params=_params("arbitrary"),
        name="nsa_outproj",
    )(x2, oc, os_, ow, w_out.astype(BF16))


def _nsa_layer(x, gn, w_in, cmp_pe, cmp_w1, cmp_w2, w_out):
    B, S, D = x.shape
    x2 = x.reshape(B * S, D)
    q, kc, vc, kscat, vs, kw, vw, gates = _nsa_proj(x2, gn, w_in, S)
    kcmp, vcmp = _compress(kc, vc, cmp_pe, cmp_w1, cmp_w2, B, S)
    o_cmp, bias = _cmp_attention(q, kcmp, vcmp, gates, B, S)
    o_sel = _sel_attention(q, bias, kscat, vs, gates, B, S)
    o_win = _win_attention(q, kw, vw, gates, B, S)
    return _outproj(x2, o_cmp, o_sel, o_win, w_out).reshape(B, S, D)


def kernel(x, norm_mix, norm_ffn, norm_final, lru_w_in, lru_b_in, lru_conv_w, lru_conv_b, lru_w_a, lru_b_a, lru_w_i, lru_b_i, lru_lambda, lru_w_out, lru_b_out, nsa_w_in, nsa_cmp_pe, nsa_cmp_w1, nsa_cmp_w2, nsa_w_out, ffn_w_gate, ffn_w_up, ffn_w_down, moe_w_router, moe_w_gate, moe_w_up, moe_w_down):
    B, S, D = x.shape
    assert norm_mix.shape[0] == 2 and lru_w_in.shape[0] == 1 and nsa_w_in.shape[0] == 1
    x = _lru_layer(x, norm_mix[0], lru_w_in[0], lru_b_in[0], lru_conv_w[0], lru_conv_b[0], lru_w_a[0],
                   lru_b_a[0], lru_w_i[0], lru_b_i[0], lru_lambda[0], lru_w_out[0], lru_b_out[0])
    x2 = _ffn_layer(x.reshape(B * S, D), norm_ffn[0], ffn_w_gate, ffn_w_up, ffn_w_down)
    x2 = _nsa_layer(x2.reshape(B, S, D), norm_mix[1], nsa_w_in[0], nsa_cmp_pe[0], nsa_cmp_w1[0],
                    nsa_cmp_w2[0], nsa_w_out[0]).reshape(B * S, D)
    out = _moe_layer(x2, norm_ffn[1], moe_w_router[0], moe_w_gate[0], moe_w_up[0], moe_w_down[0], norm_final)
    return out.reshape(B, S, D)
```

```python
import functools

import numpy as np
import jax
import jax.numpy as jnp
from jax import lax
from jax.experimental import pallas as pl
from jax.experimental.pallas import tpu as pltpu

F32 = jnp.float32
BF16 = jnp.bfloat16

RMS_EPS = 1e-6
LRU_C = 8.0
CONV_WIDTH = 4
N_HEADS = 16
HEAD_DIM = 64
N_KV_GROUPS = 4
HEADS_PER_GROUP = N_HEADS // N_KV_GROUPS
CMP_BLOCK = 32
CMP_STRIDE = 16
SEL_BLOCK = 64
N_SEL = 16
N_BRANCHES = 3
WINDOW = 512
FORCE_SCORE = 1.0e4
ROPE_THETA = 10000.0
TOP_K = 2

LANES = 128
SUBLANES = 8
MXU_WIDTH = 256
MASK_VALUE = -1.0e30
VMEM_LIMIT_BYTES = 56 * 1024 * 1024

_NT = (((1,), (1,)), ((), ()))
Q_SCALE = HEAD_DIM ** -0.5 * 1.4426950408889634
SEL_WIDE = 4


def _params(*semantics):
    return pltpu.CompilerParams(dimension_semantics=semantics, vmem_limit_bytes=VMEM_LIMIT_BYTES)


def _resident(shape):
    zeros = (0,) * len(shape)
    return pl.BlockSpec(shape, lambda *_: zeros, pipeline_mode=pl.Buffered(1))


def _rms(x, g):
    return x * lax.rsqrt(jnp.mean(x * x, axis=-1, keepdims=True) + RMS_EPS) * g


def _dot(a, b):
    return jnp.dot(a, b, preferred_element_type=F32)


def _gelu_tanh(x):
    two_z = x * (2.0 * 0.7978845608028654 + (2.0 * 0.7978845608028654 * 0.044715) * (x * x))
    return x * jax.nn.sigmoid(two_z)


def _block_diag_dot(xb, w_ref, col0, n, bw):
    outs = []
    for c0 in range(0, n, MXU_WIDTH):
        w = min(MXU_WIDTH, n - c0)
        lo = (c0 // bw) * bw // LANES * LANES
        hi = min(n, -(-(((c0 + w - 1) // bw + 1) * bw) // LANES) * LANES)
        outs.append(_dot(xb[:, lo:hi], w_ref[lo:hi, col0 + c0:col0 + c0 + w]))
    return jnp.concatenate(outs, axis=1)


def _lane_col(vals, lane, idx):
    return jnp.sum(jnp.where(lane == idx, vals, 0.0), axis=1, keepdims=True)


def _lru_body(x_ref, gn_ref, win_ref, bin_ref, cw_ref, cb_ref, wg_ref, bg_ref, lam_ref,
              wout_ref, bout_ref, o_ref, xtail, hcar, *, ts, dr, bw):
    t = pl.program_id(1)

    @pl.when(t == 0)
    def _():
        xtail[...] = jnp.zeros_like(xtail)
        hcar[...] = jnp.zeros_like(hcar)

    x = x_ref[0]
    hn = _rms(x, gn_ref[...]).astype(BF16)
    proj = _dot(hn, win_ref[...]) + bin_ref[...]
    gate = _gelu_tanh(proj[:, :dr])
    xr = proj[:, dr:]

    tail = xtail[...]
    row = lax.broadcasted_iota(jnp.int32, (ts, 1), 0)
    xc = cb_ref[...] + xr * cw_ref[CONV_WIDTH - 1:CONV_WIDTH, :]
    for lag in range(1, CONV_WIDTH):
        k = CONV_WIDTH - 1 - lag
        shifted = pltpu.roll(xr, lag, 0)
        head = jnp.where(row[:SUBLANES] < lag, pltpu.roll(tail, lag, 0), shifted[:SUBLANES])
        xc = xc + jnp.concatenate([head, shifted[SUBLANES:]], axis=0) * cw_ref[k:k + 1, :]
    xtail[...] = xr[ts - SUBLANES:, :]

    xcb = xc.astype(BF16)
    r = jax.nn.sigmoid(_block_diag_dot(xcb, wg_ref, 0, dr, bw) + bg_ref[:, :dr])
    i = jax.nn.sigmoid(_block_diag_dot(xcb, wg_ref, dr, dr, bw) + bg_ref[:, dr:])
    z = -lam_ref[...]
    softplus = jnp.maximum(z, 0.0) + jnp.log(1.0 + jnp.exp(-jnp.abs(z)))
    log_a = (-LRU_C * r) * softplus
    a = jnp.exp(log_a)
    mult = jnp.sqrt(1.0 - a * a)
    mult =jnp.where((row == 0) & (t == 0), 1.0, mult)
    u = mult * (i * xc)

    sub = row & (SUBLANES - 1)
    for shift in (1, 2, 4):
        keep = sub >= shift
        a_prev = jnp.where(keep, pltpu.roll(a, shift, 0), 1.0)
        u_prev = jnp.where(keep, pltpu.roll(u, shift, 0), 0.0)
        u = a * u_prev + u
        a = a * a_prev
    carry = hcar[...]
    groups = []
    for k in range(ts // SUBLANES):
        rows = slice(k * SUBLANES, (k + 1) * SUBLANES)
        hk = u[rows] + a[rows] * carry
        carry = hk[SUBLANES - 1:SUBLANES]
        groups.append(hk)
    h = jnp.concatenate(groups, axis=0)
    hcar[...] = carry

    y = (h * gate).astype(BF16)
    o_ref[0] = x + _dot(y, wout_ref[...]) + bout_ref[...]


def _lru_layer(x, gn, w_in, b_in, conv_w, conv_b, w_a, b_a, w_i, b_i, lam, w_out, b_out):
    B, S, D = x.shape
    dr = w_out.shape[0]
    ts = min(256, S)
    assert S % ts == 0 and ts % 8 == 0
    wg = jnp.concatenate([jax.scipy.linalg.block_diag(*w_a), jax.scipy.linalg.block_diag(*w_i)], axis=1)
    row = lambda v: v.reshape(1, -1)
    body = functools.partial(_lru_body, ts=ts, dr=dr, bw=w_a.shape[1])
    return pl.pallas_call(
        body,
        grid=(B, S // ts),
        in_specs=[
            pl.BlockSpec((1, ts, D), lambda b, t: (b, t, 0)),
            _resident((1, D)),
            _resident((D, 2 * dr)), _resident((1, 2 * dr)),
            _resident((CONV_WIDTH, dr)), _resident((1, dr)),
            _resident((dr, 2 * dr)), _resident((1, 2 * dr)),
            _resident((1, dr)),
            _resident((dr, D)), _resident((1, D)),
        ],
        out_specs=pl.BlockSpec((1, ts, D), lambda b, t: (b, t, 0)),
        out_shape=jax.ShapeDtypeStruct((B, S, D), F32),
        scratch_shapes=[pltpu.VMEM((SUBLANES, dr), F32), pltpu.VMEM((1, dr), F32)],
        compiler_params=_params("arbitrary", "arbitrary"),
        name="lru_mixer",
    )(x, row(gn), w_in.astype(BF16), row(b_in), conv_w, row(conv_b), wg.astype(BF16),
      row(jnp.concatenate([b_a, b_i])), row(lam), w_out.astype(BF16), row(b_out))


def _swiglu_partial(hn, wg_ref, wu_ref, wd_ref):
    g = _dot(hn, wg_ref[0].astype(BF16))
    u = _dot(hn, wu_ref[0].astype(BF16))
    return _dot(((g * jax.nn.sigmoid(g)) * u).astype(BF16), wd_ref[0].astype(BF16))


def _ffn_body(x_ref, gn_ref, wg_ref, wu_ref, wd_ref, o_ref, hn_ref, acc_ref, *, n_f):
    f = pl.program_id(1)

    @pl.when(f == 0)
    def _():
        hn_ref[...] = _rms(x_ref[...], gn_ref[...]).astype(BF16)
        acc_ref[...] = jnp.zeros_like(acc_ref)

    acc_ref[...] += _swiglu_partial(hn_ref[...], wg_ref, wu_ref, wd_ref)

    @pl.when(f == n_f - 1)
    def _():
        o_ref[...] = x_ref[...] + acc_ref[...]


def _ffn_layer(x2, gn, w_gate, w_up, w_down, tm=1024, tf=512):
    M, D = x2.shape
    F = w_gate.shape[2]
    tm = min(tm, M)
    tf = min(tf, F)
    assert M % tm == 0 and F % tf == 0
    n_f = F // tf
    return pl.pallas_call(
        functools.partial(_ffn_body, n_f=n_f),
        grid=(M // tm, n_f),
        in_specs=[pl.BlockSpec((tm, D), lambda i, f: (i, 0)), _resident((1, D)),
                  pl.BlockSpec((1, D, tf), lambda i, f: (0, 0, f)),
                  pl.BlockSpec((1, D, tf), lambda i, f: (0, 0, f)),
                  pl.BlockSpec((1, tf, D), lambda i, f: (0, f, 0))],
        out_specs=pl.BlockSpec((tm, D), lambda i, f: (i, 0)),
        out_shape=jax.ShapeDtypeStruct((M, D), F32),
        scratch_shapes=[pltpu.VMEM((tm, D), BF16), pltpu.VMEM((tm, D), F32)],
        compiler_params=_params("arbitrary", "arbitrary"),
        name="dense_swiglu",
    )(x2, gn.reshape(1, D), w_gate, w_up, w_down)


_HIGH_HALF = 0xFFFF0000


def _pack_bf16_pair(lo, hi):
    lo_bits = lax.bitcast_convert_type(lo, jnp.uint32)
    hi_bits = lax.bitcast_convert_type(hi, jnp.uint32)
    return (lo_bits >> 16) | (hi_bits & jnp.uint32(_HIGH_HALF))


def _unpack_bf16_pair(w):
    return (lax.bitcast_convert_type(w << 16, F32),
            lax.bitcast_convert_type(w & jnp.uint32(_HIGH_HALF), F32))


def _router_body(x_ref, gn_ref, wh_ref, wl_ref, tril_ref, o_ref, cnt_ref, hp_ref, plan_ref, *, n_experts):
    @pl.when(pl.program_id(0) == 0)
    def _():
        cnt_ref[...] = jnp.zeros_like(cnt_ref)

    hn = _rms(x_ref[...], gn_ref[...])
    hh = hn.astype(BF16)
    hb = hh.astype(F32)
    half = hn.shape[1] // 2
    hp_ref[...] = _pack_bf16_pair(hb[:, :half], hb[:, half:])
    hl = (hn - hb).astype(BF16)
    wh = wh_ref[...]
    logits = _dot(hh, wh) + _dot(hh, wl_ref[...]) + _dot(hl, wh)
    lane = lax.broadcasted_iota(jnp.int32, logits.shape, 1)
    lg = jnp.where(lane < n_experts, logits, -jnp.inf)
    m0 = jnp.max(lg, axis=1, keepdims=True)
    i0 = jnp.min(jnp.where(lg == m0, lane, LANES), axis=1, keepdims=True)
    lg = jnp.where(lane == i0, -jnp.inf, lg)
    m1 = jnp.max(lg, axis=1, keepdims=True)
    i1 = jnp.min(jnp.where(lg == m1, lane, LANES), axis=1, keepdims=True)
    e1 = jnp.exp(m1 - m0)
    w0 = 1.0 / (1.0 + e1)
    routed = ((lane == i0) | (lane == i1)).astype(F32)
    incl = _dot(tril_ref[...], routed.astype(BF16))
    excl = incl - routed + cnt_ref[0:1, :]

    def put(col, v):
        return jnp.where(lane == col, v, 0.0)

    meta = (put(META_I0, i0.astype(F32)) + put(META_I1, i1.astype(F32)) + put(META_W0, w0)
            + put(META_W1, e1 * w0) + put(META_R0, _lane_col(excl, lane, i0))
            + put(META_R1, _lane_col(excl, lane, i1)))
    o_ref[...] = meta
    plan_ref[0] = meta.T[0:SUBLANES, :].astype(jnp.int32)
    cnt_ref[...] = jnp.broadcast_to(cnt_ref[0:1, :] + incl[incl.shape[0] - 1:, :], cnt_ref.shape)


META_I0, META_I1, META_W0, META_W1, META_R0, META_R1 = range(6)


def _router(x2, gn, w_router, tm=512):
    M, D = x2.shape
    n_experts = w_router.shape[1]
    tm = min(tm, M)
    wpad = jnp.pad(w_router, ((0, 0), (0, LANES - n_experts)))
    wh = wpad.astype(BF16)
    wl = (wpad - wh.astype(F32)).astype(BF16)
    tril = jnp.tril(jnp.ones((tm, tm), BF16))
    return pl.pallas_call(
        functools.partial(_router_body, n_experts=n_experts),
        grid=(M // tm,),
        in_specs=[pl.BlockSpec((tm, D), lambda i: (i, 0)), _resident((1, D)),
                  _resident((D, LANES)), _resident((D, LANES)), _resident((tm, tm))],
        out_specs=[pl.BlockSpec((tm, LANES), lambda i: (i, 0)), pl.BlockSpec((SUBLANES, LANES), lambda i: (0, 0)),
                   pl.BlockSpec((tm, D // 2), lambda i: (i, 0)),
                   pl.BlockSpec((1, SUBLANES, tm), lambda i: (i, 0, 0))],
        out_shape=[jax.ShapeDtypeStruct((M, LANES), F32), jax.ShapeDtypeStruct((SUBLANES, LANES), F32),
                   jax.ShapeDtypeStruct((M, D // 2), jnp.uint32),
                   jax.ShapeDtypeStruct((M // tm, SUBLANES, tm), jnp.int32)],
        compiler_params=_params("arbitrary"),
        name="moe_router",
    )(x2, gn.reshape(1, D), wh, wl, tril)


def _combine_body(pos_ref, pos_next_ref, x_ref, meta_ref, y_ref, gfin_ref, o_ref, ybuf, sem, *, tm, n_tiles):
    i = pl.program_id(0)
    slot = i % 2

    def row_copy(idx_ref, k, r, s):
        return pltpu.make_async_copy(y_ref.at[pl.ds(idx_ref[0, 0, k * tm + r], 1)], ybuf.at[s, k, pl.ds(r, 1)],
                                     sem.at[s])

    @pl.when(i == 0)
    def _():
        def body(r, c):
            row_copy(pos_ref, 0, r, 0).start()
            row_copy(pos_ref, 1, r, 0).start()
            return c
        lax.fori_loop(0, tm, body, 0, unroll=DMA_ISSUE_UNROLL)

    @pl.when(i + 1 < n_tiles)
    def _():
        for r in range(tm):
            row_copy(pos_next_ref, 0, r, 1 - slot).start(priority=0)
            row_copy(pos_next_ref, 1, r, 1 - slot).start(priority=1)

    pltpu.make_async_copy(ybuf.at[slot], ybuf.at[slot], sem.at[slot]).wait()
    meta = meta_ref[...]
    lane = lax.broadcasted_iota(jnp.int32, meta.shape, 1)
    y0 = jnp.concatenate(_unpack_bf16_pair(ybuf[slot, 0]), axis=1)
    y1 = jnp.concatenate(_unpack_bf16_pair(ybuf[slot, 1]), axis=1)
    out = x_ref[...] + _lane_col(meta, lane, META_W0) * y0 + _lane_col(meta, lane, META_W1) * y1
    o_ref[...] = _rms(out, gfin_ref[...])


def _combine(x2, meta, y, pos0, pos1, g_final, tm=512):
    M, D = x2.shape
    tm = min(tm, M)
    n_tiles = M // tm
    pos = jnp.concatenate([pos0.reshape(n_tiles, 1, tm), pos1.reshape(n_tiles, 1, tm)], axis=2)
    return pl.pallas_call(
        functools.partial(_combine_body, tm=tm, n_tiles=n_tiles),
        grid=(n_tiles,),
        in_specs=[pl.BlockSpec((1, 1, 2 * tm), lambda i: (i, 0, 0), memory_space=pltpu.SMEM),
                  pl.BlockSpec((1, 1, 2 * tm), lambda i: (jnp.minimum(i + 1, n_tiles - 1), 0, 0),
                               memory_space=pltpu.SMEM),
                  pl.BlockSpec((tm, D), lambda i: (i, 0)),
                  pl.BlockSpec((tm, LANES), lambda i: (i, 0)),
                  pl.BlockSpec(memory_space=pl.ANY),
                  _resident((1, D))],
        out_specs=pl.BlockSpec((tm, D), lambda i: (i, 0)),
        out_shape=jax.ShapeDtypeStruct((M, D), F32),
        scratch_shapes=[pltpu.VMEM((2, 2, tm, D // 2), jnp.uint32), pltpu.SemaphoreType.DMA((2,))],
        compiler_params=_params("arbitrary"),
        name="moe_combine",
    )(pos, pos, x2, meta, y, g_final.reshape(1, D))


GATHER_AHEAD = 2
DMA_ISSUE_UNROLL = 8


def _grouped_ffn_body(te_ref, nused_ref, src_ref, src_next_ref, src_ahead_ref, x_ref, wg_ref, wu_ref,
                      wd_ref, o_ref, xbuf, hn_ref, acc_ref, sem, *, tm, n_f):
    j = pl.program_id(0)
    f = pl.program_id(1)
    n_used = nused_ref[0]
    used = j < n_used
    gathered = j < n_used + GATHER_AHEAD
    slot = j % (GATHER_AHEAD + 1)
    slot_ahead = (j + GATHER_AHEAD) % (GATHER_AHEAD + 1)
    chunk = tm // (n_f + 1)

    def row_copy(idx_ref, r, s):
        return pltpu.make_async_copy(x_ref.at[pl.ds(idx_ref[0, 0, r], 1)], xbuf.at[s, pl.ds(r, 1)], sem.at[s])

    def start_chunk(first):
        for k in range(chunk):
            row_copy(src_ahead_ref, first + k, slot_ahead).start()

    @pl.when((j == 0) & (f == 0))
    def _():
        def body(r, c):
            row_copy(src_ref, r, 0).start()
            row_copy(src_next_ref, r, 1).start()
            return c
        lax.fori_loop(0, tm, body, 0, unroll=DMA_ISSUE_UNROLL)

    @pl.when(f == 0)
    def _():
        acc_ref[...] = jnp.zeros_like(acc_ref)

        @pl.when(gathered)
        def _():
            pltpu.make_async_copy(xbuf.at[slot], xbuf.at[slot], sem.at[slot]).wait()
            lo, hi = _unpack_bf16_pair(xbuf[slot])
            half = lo.shape[1]
            hn_ref[:, :half] = lo.astype(BF16)
            hn_ref[:, half:] = hi.astype(BF16)

        @pl.when(used)
        def _():
            start_chunk(0)

    @pl.when(used)
    def _():
        acc_ref[...] += _swiglu_partial(hn_ref[...], wg_ref, wu_ref, wd_ref)
        start_chunk((f + 1) * chunk)

    @pl.when(f == n_f - 1)
    def _():
        out = acc_ref[...].astype(BF16).astype(F32)
        half = out.shape[1] // 2
        o_ref[...] = _pack_bf16_pair(out[:, :half], out[:, half:])


def _grouped_ffn(hp, src, w_gate, w_up, w_down, tile_expert, n_used, tm, tf=512):
    P = src.shape[0]
    D = w_gate.shape[1]
    F = w_gate.shape[2]
    tf = min(tf, F)
    n_f = F // tf
    n_tiles = P // tm
    assert F % tf == 0 and tm % (n_f + 1) == 0
    src3 = src.reshape(n_tiles, 1, tm)

    def src_spec(ahead):
        return pl.BlockSpec((1, 1, tm), lambda j, f, te, nu: (jnp.minimum(j + ahead, n_tiles - 1), 0, 0),
                            memory_space=pltpu.SMEM)

    grid_spec = pltpu.PrefetchScalarGridSpec(
        num_scalar_prefetch=2,
        grid=(n_tiles, n_f),
        in_specs=[src_spec(a) for a in range(GATHER_AHEAD + 1)] + [
                  pl.BlockSpec(memory_space=pl.ANY),
                  pl.BlockSpec((1, D, tf), lambda j, f, te, nu: (te[j], 0, f)),
                  pl.BlockSpec((1, D, tf), lambda j, f, te, nu: (te[j], 0, f)),
                  pl.BlockSpec((1, tf, D), lambda j, f, te, nu: (te[j], f, 0))],
        out_specs=pl.BlockSpec((tm, D // 2), lambda j, f, te, nu: (j, 0)),
        scratch_shapes=[pltpu.VMEM((GATHER_AHEAD + 1, tm, D // 2), jnp.uint32), pltpu.VMEM((tm, D), BF16),
                        pltpu.VMEM((tm, D), F32), pltpu.SemaphoreType.DMA((GATHER_AHEAD + 1,))])
    return pl.pallas_call(
        functools.partial(_grouped_ffn_body, tm=tm, n_f=n_f),
        grid_spec=grid_spec,
        out_shape=jax.ShapeDtypeStruct((P, D // 2), jnp.uint32),
        compiler_params=_params("arbitrary", "arbitrary"),
        name="moe_grouped_swiglu",
    )(tile_expert, n_used, *([src3] * (GATHER_AHEAD + 1)), hp, w_gate, w_up, w_down)


def _moe_layer(x2, gn, w_router, w_gate, w_up, w_down, g_final, tm=1024):
    M, D = x2.shape
    n_e = w_router.shape[1]
    tm = min(tm, M)
    meta, counts, hp, plan = _router(x2, gn, w_router)
    i0, i1, r0, r1 = (plan[:, col, :].reshape(M) for col in (META_I0, META_I1, META_R0, META_R1))
    padded = (counts[0, :n_e].astype(jnp.int32) + tm - 1) // tm * tm
    ends = jnp.cumsum(padded)
    offsets = ends - padded
    experts = jnp.arange(n_e, dtype=jnp.int32)
    pos0 = jnp.sum(jnp.where(i0[:, None] == experts, offsets, 0), axis=1) + r0
    pos1 = jnp.sum(jnp.where(i1[:, None] == experts, offsets, 0), axis=1) + r1
    P = TOP_K * M + (n_e - 1 + GATHER_AHEAD) * tm
    tok = jnp.arange(M, dtype=jnp.int32)
    src = jnp.zeros((P,), jnp.int32).at[jnp.concatenate([pos0, pos1])].set(
        jnp.concatenate([tok, tok]), unique_indices=True, mode="promise_in_bounds")
    tile_start = jnp.arange(P // tm, dtype=jnp.int32) * tm
    tile_expert = jnp.minimum(jnp.sum(tile_start[:, None] >= ends, axis=1), n_e - 1).astype(jnp.int32)
    n_used = (ends[n_e - 1:] // tm).astype(jnp.int32)
    y = _grouped_ffn(hp, src, w_gate, w_up, w_down, tile_expert, n_used, tm)
    return _combine(x2, meta, y, pos0, pos1, g_final)


def _nsa_proj_body(x_ref, gn_ref, w_ref, cos_ref, sin_ref,
                   q_ref, kc_ref, vc_ref, ks_ref, vs_ref, kw_ref, vw_ref, g_ref, *, tm, tiles_per_seq):
    i = pl.program_id(0)
    hn = _rms(x_ref[...], gn_ref[...]).astype(BF16)
    proj = _dot(hn, w_ref[...])
    lane = lax.broadcasted_iota(jnp.int32, (tm, LANES), 1)
    first_half = (lane & (HEAD_DIM - 1)) < HEAD_DIM // 2
    cos = cos_ref[...]
    sin = sin_ref[...]

    def chunk(c):
        return proj[:, c * LANES:(c + 1) * LANES]

    def rope(v):
        rot = jnp.where(first_half, pltpu.roll(v, LANES - HEAD_DIM // 2, 1), pltpu.roll(v, HEAD_DIM // 2, 1))
        return v * cos + rot * sin

    def heads(v):
        return v[:, :HEAD_DIM], pltpu.roll(v, HEAD_DIM, 1)[:, :HEAD_DIM]

    c = 0
    for cc in range(N_HEADS // 2):
        lo, hi = heads(rope(chunk(c)) * Q_SCALE); c += 1
        q_ref[2 * cc] = lo.astype(BF16)
        q_ref[2 * cc + 1] = hi.astype(BF16)
    for cc in range(N_KV_GROUPS // 2):
        lo, hi = heads(rope(chunk(c))); c += 1
        kc_ref[2 * cc] = lo
        kc_ref[2 * cc + 1] = hi
    for cc in range(N_KV_GROUPS // 2):
        lo, hi = heads(chunk(c)); c += 1
        vc_ref[2 * cc] = lo
        vc_ref[2 * cc + 1] = hi
    pos = (i % tiles_per_seq) * tm + lax.broadcasted_iota(jnp.int32, (tm, 1), 0)
    onehot = (lane == pos // SEL_BLOCK).astype(BF16)
    for cc in range(N_KV_GROUPS // 2):
        v = rope(chunk(c)); c += 1
        for j, vv in enumerate((v, pltpu.roll(v, HEAD_DIM, 1))):
            ks_ref[2 * cc + j, :, 0:LANES] = jnp.where(lane < HEAD_DIM, vv, 0.0).astype(BF16)
            ks_ref[2 * cc + j, :, LANES:2 * LANES] = onehot
    ones_col = jnp.where(lane == HEAD_DIM, 1.0, 0.0)

    def store_values(ref):
        nonlocal c
        for cc in range(N_KV_GROUPS // 2):
            v = chunk(c); c += 1
            for j, vv in enumerate((v, pltpu.roll(v, HEAD_DIM, 1))):
                ref[2 * cc + j] = jnp.where(lane < HEAD_DIM, vv, ones_col).astype(BF16)

    store_values(vs_ref)
    for cc in range(N_KV_GROUPS // 2):
        lo, hi = heads(rope(chunk(c))); c += 1
        kw_ref[2 * cc] = lo.astype(BF16)
        kw_ref[2 * cc + 1] = hi.astype(BF16)
    store_values(vw_ref)
    g_ref[...] = jax.nn.sigmoid(chunk(c))


def _nsa_proj(x2, gn, w_in, S, tm=512):
    M, D = x2.shape
    H, G, DH = N_HEADS, N_KV_GROUPS, HEAD_DIM
    tm = min(tm, S)
    assert S % tm == 0
    n_in = w_in.shape[1]
    n_pad = -(-n_in // LANES) * LANES
    wp = jnp.pad(w_in, ((0, 0), (0, n_pad - n_in))).astype(BF16)
    half = DH // 2
    freqs = ROPE_THETA ** (-jnp.arange(half, dtype=F32) / half)
    ang = jnp.arange(S, dtype=F32)[:, None] * freqs[None, :]
    cos = jnp.tile(jnp.cos(ang), (1, 2 * LANES // DH))
    sin = jnp.tile(jnp.concatenate([-jnp.sin(ang), jnp.sin(ang)], axis=1), (1, LANES // DH))
    tiles_per_seq = S // tm
    hd = lambda n, dt: jax.ShapeDtypeStruct((n, M, DH), dt)
    hspec = lambda n: pl.BlockSpec((n, tm, DH), lambda i: (0, i, 0))
    vd = jax.ShapeDtypeStruct((G, M, LANES), BF16)
    vspec = pl.BlockSpec((G, tm, LANES), lambda i: (0, i, 0))
    return pl.pallas_call(
        functools.partial(_nsa_proj_body, tm=tm, tiles_per_seq=tiles_per_seq),
        grid=(M // tm,),
        in_specs=[pl.BlockSpec((tm, D), lambda i: (i, 0)), _resident((1, D)), _resident((D, n_pad)),
                  pl.BlockSpec((tm, LANES), lambda i: (i % tiles_per_seq, 0)),
                  pl.BlockSpec((tm, LANES), lambda i: (i % tiles_per_seq, 0))],
        out_specs=[hspec(H), hspec(G), hspec(G),
                   pl.BlockSpec((G, tm, 2 * LANES), lambda i: (0, i, 0)),
                   vspec, hspec(G), vspec,
                   pl.BlockSpec((tm, LANES), lambda i: (i, 0))],
        out_shape=[hd(H, BF16), hd(G, F32), hd(G, F32),
                   jax.ShapeDtypeStruct((G, M, 2 * LANES), BF16),
                   vd, hd(G, BF16), vd,
                   jax.ShapeDtypeStruct((M, LANES), F32)],
        compiler_params=_params("arbitrary"),
        name="nsa_proj",
    )(x2, gn.reshape(1, D), wp, cos, sin)


def _compress_body(kc_ref, vc_ref, pe_ref, w1_ref, w2_ref, ko_ref, vo_ref, *, nc):
    half = CMP_BLOCK // 2
    for kv, (src, dst) in enumerate(((kc_ref, ko_ref), (vc_ref, vo_ref))):
        top = jnp.zeros((nc, w1_ref.shape[2]), F32)
        bot = jnp.zeros((nc, w1_ref.shape[2]), F32)
        for j in range(half):
            xj = src[0, pl.ds(j, nc, stride=CMP_STRIDE), :]
            top = top + _dot((xj + pe_ref[kv, j:j + 1, :]).astype(BF16),
                             w1_ref[kv, j * HEAD_DIM:(j + 1) * HEAD_DIM, :])
            bot = bot + _dot((xj + pe_ref[kv, half + j:half + j + 1, :]).astype(BF16),
                             w1_ref[kv, (half + j) * HEAD_DIM:(half + j + 1) * HEAD_DIM, :])
        hid = top + pltpu.roll(bot, nc - 1, 0)
        dst[0] = _dot(_gelu_tanh(hid).astype(BF16), w2_ref[kv]).astype(BF16)


def _compress(kc, vc, pe, w1, w2, B, S):
    assert CMP_BLOCK == 2 * CMP_STRIDE
    G, M, DH = kc.shape
    nc = S // CMP_STRIDE
    spec_in = pl.BlockSpec((1, S, DH), lambda b, g: (g, b, 0))
    spec_out = pl.BlockSpec((1, nc, DH), lambda b, g: (g, b, 0))
    out = jax.ShapeDtypeStruct((G, B * nc, DH), BF16)
    return pl.pallas_call(
        functools.partial(_compress_body, nc=nc),
        grid=(B, G),
        in_specs=[spec_in, spec_in, _resident(pe.shape), _resident(w1.shape), _resident(w2.shape)],
        out_specs=[spec_out, spec_out],
        out_shape=[out, out],
        compiler_params=_params("arbitrary", "arbitrary"),
        name="nsa_compress",
    )(kc, vc, pe, w1.astype(BF16), w2.astype(BF16))


def _store_head(o_ref, oh, gates, g, h, branch, rows=slice(None), denom=None):
    lane = lax.broadcasted_iota(jnp.int32, gates.shape, 1)
    col = N_BRANCHES * (HEADS_PER_GROUP * g + h) + branch
    scale = _lane_col(gates, lane, col)
    if denom is not None:
        scale = scale / denom
    o_ref[rows, h * HEAD_DIM:(h + 1) * HEAD_DIM] = (oh * scale).astype(o_ref.dtype)


def _store_softmax_head(o_ref, acc, gates, g, h, branch, rows=slice(None)):
    _store_head(o_ref, acc[:, :HEAD_DIM], gates, g, h, branch, rows, denom=acc[:, HEAD_DIM:HEAD_DIM + 1])


def _cmp_body(q_ref, kc_ref, vc_ref, gates_ref, ov_ref, o_ref, bias_ref, imp_ref, *, tq, nc, n_s, k_sel):
    g = pl.program_id(1)
    i = pl.program_id(2)
    gates = gates_ref[...]
    t = i * tq + lax.broadcasted_iota(jnp.int32, (tq, 1), 0)

    def attend(w):
        kc = kc_ref[0, 0:w, :]
        vc = vc_ref[0, 0:w, :]
        cend = lax.broadcasted_iota(jnp.int32, (1, w), 1) * CMP_STRIDE + (CMP_BLOCK - 1)
        mask = cend <= t
        ps = None
        for h in range(HEADS_PER_GROUP):
            s = lax.dot_general(q_ref[h], kc, _NT, preferred_element_type=F32)
            s = jnp.where(mask, s, MASK_VALUE)
            m = jnp.max(s, axis=1, keepdims=True)
            m = jnp.where(m > 0.5 * MASK_VALUE, m, 0.0)
            e = jnp.exp2(s - m)
            p = e * (1.0 / jnp.maximum(jnp.sum(e, axis=1, keepdims=True), 1e-30))
            _store_head(o_ref, _dot(p.astype(BF16), vc), gates, g, h, 0)
            ps = p if ps is None else ps + p
        ph = ps.astype(BF16)
        pl_ = (ps - ph.astype(F32)).astype(BF16)
        imp_ref[...] = _dot(ph, ov_ref[0:w, :]) + _dot(pl_, ov_ref[0:w, :])

    n_chunks = nc // LANES
    need = ((i + 1) * (tq // CMP_STRIDE) + LANES - 1) // LANES
    for k in range(1, n_chunks + 1):
        pl.when((need == k) if k < n_chunks else (need >= k))(functools.partial(attend, k * LANES))

    j = lax.broadcasted_iota(jnp.int32, (tq, LANES), 1)
    cur = t // SEL_BLOCK
    forced = (j == 0) | (j == cur) | (j == cur - 1)
    valid = j * SEL_BLOCK <= t
    score = jnp.where(forced, FORCE_SCORE, jnp.where(valid, imp_ref[...], -1.0))
    score = jnp.where(j < n_s, score, -jnp.inf)
    x = score.T
    blk_i = lax.broadcasted_iota(jnp.int32, (LANES, tq), 0)
    blk = blk_i.astype(F32)

    def topk(x, n_iter):
        for _ in range(n_iter):
            mx = jnp.max(x, axis=0, keepdims=True)
            idx = jnp.min(jnp.where(x == mx, blk, float(LANES)), axis=0, keepdims=True)
            x = jnp.where(blk == idx, -jnp.inf, x)
        sel = (x == -jnp.inf) & (blk_i < n_s)
        bias_ref[0] = jnp.where(sel, 0.0, MASK_VALUE).T.astype(BF16)

    n_forced = 3
    direct = (i * tq >= 2 * SEL_BLOCK) & (k_sel > n_forced)

    @pl.when(direct)
    def _():
        cur_t = (i * tq + lax.broadcasted_iota(jnp.int32, (1, tq), 1)) // SEL_BLOCK
        forced_t = (blk_i == 0) | (blk_i == cur_t) | (blk_i == cur_t - 1)
        topk(jnp.where(forced_t, -jnp.inf, x), k_sel - n_forced)

    @pl.when(jnp.logical_not(direct))
    def _():
        topk(x, k_sel)


def _overlap_matrix(nc, n_s):
    r = CMP_BLOCK // CMP_STRIDE
    qn = SEL_BLOCK // CMP_STRIDE
    m = np.zeros((nc, LANES), np.float32)
    n_c = nc - r + 1
    chunks = np.arange(n_c)[:, None] + np.arange(r)[None, :]
    np.add.at(m, (np.repeat(np.arange(n_c), r), (chunks // qn).ravel()), 1.0)
    return m


def _cmp_attention(q, kcmp, vcmp, gates, B, S, tq=1024):
    H, M, DH = q.shape
    G = N_KV_GROUPS
    tq = min(tq, S)
    nq = S // tq
    nc = S // CMP_STRIDE
    n_s = S // SEL_BLOCK
    assert n_s <= LANES and tq & (tq - 1) == 0
    k_sel = min(N_SEL, n_s)
    ov = jnp.asarray(_overlap_matrix(nc, n_s), BF16)
    return pl.pallas_call(
        functools.partial(_cmp_body, tq=tq, nc=nc, n_s=n_s, k_sel=k_sel),
        grid=(B, G, nq),
        in_specs=[pl.BlockSpec((HEADS_PER_GROUP, tq, DH), lambda b, g, i: (g, b * nq + i, 0)),
                  pl.BlockSpec((1, nc, DH), lambda b, g, i: (g, b, 0)),
                  pl.BlockSpec((1, nc, DH), lambda b, g, i: (g, b, 0)),
                  pl.BlockSpec((tq, LANES), lambda b, g, i: (b * nq + i, 0)),
                  _resident((nc, LANES))],
        out_specs=[pl.BlockSpec((tq, HEADS_PER_GROUP * DH), lambda b, g, i: (b * nq + i, g)),
                   pl.BlockSpec((1, tq, LANES), lambda b, g, i: (g, b * nq + i, 0))],
        out_shape=[jax.ShapeDtypeStruct((M, H * DH), BF16), jax.ShapeDtypeStruct((G, M, LANES), BF16)],
        scratch_shapes=[pltpu.VMEM((tq, LANES), F32)],
        compiler_params=_params("arbitrary", "arbitrary", "arbitrary"),
        name="nsa_cmp_select",
    )(q, kcmp, vcmp, gates, ov)


def _sel_body(q_ref, bias_ref, k_ref, v_ref, gates_ref, o_ref, qcat, m_ref, acc_ref, *, tq):
    g = pl.program_id(1)
    i = pl.program_id(2)
    hp = HEADS_PER_GROUP
    rows = hp * tq
    qcat[:, 0:LANES] = jnp.zeros((rows, LANES), BF16)
    qcat[:, 0:HEAD_DIM] = q_ref[...].reshape(rows, HEAD_DIM)
    bias = bias_ref[0]
    for h in range(hp):
        qcat[h * tq:(h + 1) * tq, LANES:2 * LANES] = bias
    m_ref[...] = jnp.full((rows, LANES), MASK_VALUE, F32)
    acc_ref[...] = jnp.zeros((rows, LANES), F32)

    def tile(j, width, causal):
        start = pl.multiple_of(j * tq, tq)
        kt = k_ref[0, pl.ds(start, width), :]
        vt = v_ref[0, pl.ds(start, width), :]
        for h in range(hp):
            r0 = h * tq
            s = lax.dot_general(qcat[r0:r0 + tq, :], kt, _NT, preferred_element_type=F32)
            if causal:
                r = lax.broadcasted_iota(jnp.int32, (tq, 1), 0)
                c = lax.broadcasted_iota(jnp.int32, (1, width), 1) - (width - tq)
                s = jnp.where(c <= r, s, MASK_VALUE)
            m_prev = m_ref[r0:r0 + tq, :]
            m_new = jnp.maximum(m_prev, jnp.max(s, axis=1, keepdims=True))
            alpha = jnp.exp2(m_prev - m_new)
            p = jnp.exp2(s - jnp.concatenate([m_new] * (width // LANES), axis=1))
            acc_ref[r0:r0 + tq, :] = alpha * acc_ref[r0:r0 + tq, :] + _dot(p.astype(BF16), vt)
            m_ref[r0:r0 + tq, :] = m_new

    def wide_tile(jw, carry):
        tile(SEL_WIDE * jw, SEL_WIDE * tq, False)
        return carry

    n_wide = i // SEL_WIDE
    lax.fori_loop(0, n_wide, wide_tile, 0)
    for rem in range(SEL_WIDE):
        pl.when(i % SEL_WIDE == rem)(functools.partial(tile, SEL_WIDE * n_wide, (rem + 1) * tq, True))

    gates = gates_ref[...]
    for h in range(hp):
        _store_softmax_head(o_ref, acc_ref[h * tq:(h + 1) * tq, :], gates, g, h, 1)


def _sel_attention(q, bias, kscat, vs, gates, B, S, tq=512):
    H, M, DH = q.shape
    G = N_KV_GROUPS
    tq = min(tq, S)
    nq = S // tq
    assert tq % LANES == 0 and tq % SEL_BLOCK == 0 and S % tq == 0
    rows = HEADS_PER_GROUP * tq
    return pl.pallas_call(
        functools.partial(_sel_body, tq=tq),
        grid=(B, G, nq),
        in_specs=[pl.BlockSpec((HEADS_PER_GROUP, tq, DH), lambda b, g, i: (g, b * nq + i, 0)),
                  pl.BlockSpec((1, tq, LANES), lambda b, g, i: (g, b * nq + i, 0)),
                  pl.BlockSpec((1, S, 2 * LANES), lambda b, g, i: (g, b, 0)),
                  pl.BlockSpec((1, S, LANES), lambda b, g, i: (g, b, 0)),
                  pl.BlockSpec((tq, LANES), lambda b, g, i: (b * nq + i, 0))],
        out_specs=pl.BlockSpec((tq, HEADS_PER_GROUP * DH), lambda b, g, i: (b * nq + i, g)),
        out_shape=jax.ShapeDtypeStruct((M, H * DH), BF16),
        scratch_shapes=[pltpu.VMEM((rows, 2 * LANES), BF16), pltpu.VMEM((rows, LANES), F32),
                        pltpu.VMEM((rows, LANES), F32)],
        compiler_params=_params("arbitrary", "arbitrary", "arbitrary"),
        name="nsa_selected",
    )(q, bias, kscat, vs, gates)


def _win_body(q_ref, k_ref, v_ref, gates_ref, o_ref, *, tq, n_back, n_sub):
    g = pl.program_id(1)
    r = lax.broadcasted_iota(jnp.int32, (tq, 1), 0)
    c = lax.broadcasted_iota(jnp.int32, (1, tq), 1)
    for sub in range(n_sub):
        i = pl.program_id(2) * n_sub + sub
        rows = slice(sub * tq, (sub + 1) * tq)
        gates = gates_ref[rows, :]
        tiles = []
        for back in range(n_back, -1, -1):
            jt = i - back
            start = pl.multiple_of(jnp.maximum(jt, 0) * tq, tq)
            d = r - c + back * tq
            ok = (d >= 0) & (d < WINDOW) & (jt >= 0)
            tiles.append((k_ref[0, pl.ds(start, tq), :], v_ref[0, pl.ds(start, tq), :], ok))
        for h in range(HEADS_PER_GROUP):
            q = q_ref[h, rows, :]
            scores = [jnp.where(ok, lax.dot_general(q, kt, _NT, preferred_element_type=F32), MASK_VALUE)
                      for kt, _, ok in tiles]
            m = functools.reduce(jnp.maximum, scores).max(axis=1, keepdims=True)
            acc = jnp.zeros((tq, LANES), F32)
            for s, (_, vt, _) in zip(scores, tiles):
                acc = acc + _dot(jnp.exp2(s - m).astype(BF16), vt)
            _store_softmax_head(o_ref, acc, gates, g, h, 2, rows)


def _win_attention(q, kw, vw, gates, B, S, tq=256, n_sub=4):
    H, M, DH = q.shape
    G = N_KV_GROUPS
    tq = min(tq, S)
    n_sub = min(n_sub, S // tq)
    tile = tq
    tq = tile * n_sub
    nq = S // tq
    n_back = -(-WINDOW // tile)
    return pl.pallas_call(
        functools.partial(_win_body, tq=tile, n_back=n_back, n_sub=n_sub),
        grid=(B, G, nq),
        in_specs=[pl.BlockSpec((HEADS_PER_GROUP, tq, DH), lambda b, g, i: (g, b * nq + i, 0)),
                  pl.BlockSpec((1, S, DH), lambda b, g, i: (g, b, 0)),
                  pl.BlockSpec((1, S, LANES), lambda b, g, i: (g, b, 0)),
                  pl.BlockSpec((tq, LANES), lambda b, g, i: (b * nq + i, 0))],
        out_specs=pl.BlockSpec((tq, HEADS_PER_GROUP * DH), lambda b, g, i: (b * nq + i, g)),
        out_shape=jax.ShapeDtypeStruct((M, H * DH), BF16),
        compiler_params=_params("arbitrary", "arbitrary", "arbitrary"),
        name="nsa_window",
    )(q, kw, vw, gates)


def _outproj_body(x_ref, a_ref, b_ref, c_ref, w_ref, o_ref):
    o = a_ref[...].astype(F32) + b_ref[...].astype(F32) + c_ref[...].astype(F32)
    o_ref[...] = x_ref[...] + _dot(o.astype(BF16), w_ref[...])


def _outproj(x2, oc, os_, ow, w_out, tm=512):
    M, D = x2.shape
    K = w_out.shape[0]
    tm = min(tm, M)
    spec_o = pl.BlockSpec((tm, K), lambda i: (i, 0))
    return pl.pallas_call(
        _outproj_body,
        grid=(M // tm,),
        in_specs=[pl.BlockSpec((tm, D), lambda i: (i, 0)), spec_o, spec_o, spec_o, _resident((K, D))],
        out_specs=pl.BlockSpec((tm, D), lambda i: (i, 0)),
        out_shape=jax.ShapeDtypeStruct((M, D), F32),
        compiler_params=_params("arbitrary"),
        name="nsa_outproj",
    )(x2, oc, os_, ow, w_out.astype(BF16))


def _nsa_layer(x, gn, w_in, cmp_pe, cmp_w1, cmp_w2, w_out):
    B, S, D = x.shape
    x2 = x.reshape(B * S, D)
    q, kc, vc, kscat, vs, kw, vw, gates = _nsa_proj(x2, gn, w_in, S)
    kcmp, vcmp = _compress(kc, vc, cmp_pe, cmp_w1, cmp_w2, B, S)
    o_cmp, bias = _cmp_attention(q, kcmp, vcmp, gates, B, S)
    o_sel = _sel_attention(q, bias, kscat, vs, gates, B, S)
    o_win = _win_attention(q, kw, vw, gates, B, S)
    return _outproj(x2, o_cmp, o_sel, o_win, w_out).reshape(B, S, D)


def kernel(x, norm_mix, norm_ffn, norm_final, lru_w_in, lru_b_in, lru_conv_w, lru_conv_b, lru_w_a, lru_b_a, lru_w_i, lru_b_i, lru_lambda, lru_w_out, lru_b_out, nsa_w_in, nsa_cmp_pe, nsa_cmp_w1, nsa_cmp_w2, nsa_w_out, ffn_w_gate, ffn_w_up, ffn_w_down, moe_w_router, moe_w_gate, moe_w_up, moe_w_down):
    B, S, D = x.shape
    assert norm_mix.shape[0] == 2 and lru_w_in.shape[0] == 1 and nsa_w_in.shape[0] == 1
    x = _lru_layer(x, norm_mix[0], lru_w_in[0], lru_b_in[0], lru_conv_w[0], lru_conv_b[0], lru_w_a[0],
                   lru_b_a[0], lru_w_i[0], lru_b_i[0], lru_lambda[0], lru_w_out[0], lru_b_out[0])
    x2 = _ffn_layer(x.reshape(B * S, D), norm_ffn[0], ffn_w_gate, ffn_w_up, ffn_w_down)
    x2 = _nsa_layer(x2.reshape(B, S, D), norm_mix[1], nsa_w_in[0], nsa_cmp_pe[0], nsa_cmp_w1[0],
                    nsa_cmp_w2[0], nsa_w_out[0]).reshape(B * S, D)
    out = _moe_layer(x2, norm_ffn[1], moe_w_router[0], moe_w_gate[0], moe_w_up[0], moe_w_down[0], norm_final)
    return out.reshape(B, S, D)
```

```python
import functools

import numpy as np
import jax
import jax.numpy as jnp
from jax import lax
from jax.experimental import pallas as pl
from jax.experimental.pallas import tpu as pltpu

F32 = jnp.float32
BF16 = jnp.bfloat16

RMS_EPS = 1e-6
LRU_C = 8.0
CONV_WIDTH = 4
N_HEADS = 16
HEAD_DIM = 64
N_KV_GROUPS = 4
HEADS_PER_GROUP = N_HEADS // N_KV_GROUPS
CMP_BLOCK = 32
CMP_STRIDE = 16
SEL_BLOCK = 64
N_SEL = 16
N_BRANCHES = 3
WINDOW = 512
FORCE_SCORE = 1.0e4
ROPE_THETA = 10000.0
TOP_K = 2

LANES = 128
SUBLANES = 8
MXU_WIDTH = 256
MASK_VALUE = -1.0e30
VMEM_LIMIT_BYTES = 56 * 1024 * 1024

_NT = (((1,), (1,)), ((), ()))
Q_SCALE = HEAD_DIM ** -0.5 * 1.4426950408889634
SEL_WIDE = 4


def _params(*semantics):
    return pltpu.CompilerParams(dimension_semantics=semantics, vmem_limit_bytes=VMEM_LIMIT_BYTES)


def _resident(shape):
    zeros = (0,) * len(shape)
    return pl.BlockSpec(shape, lambda *_: zeros, pipeline_mode=pl.Buffered(1))


def _rms(x, g):
    return x * lax.rsqrt(jnp.mean(x * x, axis=-1, keepdims=True) + RMS_EPS) * g


def _dot(a, b):
    return jnp.dot(a, b, preferred_element_type=F32)


def _gelu_tanh(x):
    two_z = x * (2.0 * 0.7978845608028654 + (2.0 * 0.7978845608028654 * 0.044715) * (x * x))
    return x * jax.nn.sigmoid(two_z)


def _block_diag_dot(xb, w_ref, col0, n, bw):
    outs = []
    for c0 in range(0, n, MXU_WIDTH):
        w = min(MXU_WIDTH, n - c0)
        lo = (c0 // bw) * bw // LANES * LANES
        hi = min(n, -(-(((c0 + w - 1) // bw + 1) * bw) // LANES) * LANES)
        outs.append(_dot(xb[:, lo:hi], w_ref[lo:hi, col0 + c0:col0 + c0 + w]))
    return jnp.concatenate(outs, axis=1)


def _lane_col(vals, lane, idx):
    return jnp.sum(jnp.where(lane == idx, vals, 0.0), axis=1, keepdims=True)


def _lru_body(x_ref, gn_ref, win_ref, bin_ref, cw_ref, cb_ref, wg_ref, bg_ref, lam_ref,
              wout_ref, bout_ref, o_ref, xtail, hcar, *, ts, dr, bw):
    t = pl.program_id(1)

    @pl.when(t == 0)
    def _():
        xtail[...] = jnp.zeros_like(xtail)
        hcar[...] = jnp.zeros_like(hcar)

    x = x_ref[0]
    hn = _rms(x, gn_ref[...]).astype(BF16)
    proj = _dot(hn, win_ref[...]) + bin_ref[...]
    gate = _gelu_tanh(proj[:, :dr])
    xr = proj[:, dr:]

    tail = xtail[...]
    row = lax.broadcasted_iota(jnp.int32, (ts, 1), 0)
    xc = cb_ref[...] + xr * cw_ref[CONV_WIDTH - 1:CONV_WIDTH, :]
    for lag in range(1, CONV_WIDTH):
        k = CONV_WIDTH - 1 - lag
        shifted = pltpu.roll(xr, lag, 0)
        head = jnp.where(row[:SUBLANES] < lag, pltpu.roll(tail, lag, 0), shifted[:SUBLANES])
        xc = xc + jnp.concatenate([head, shifted[SUBLANES:]], axis=0) * cw_ref[k:k + 1, :]
    xtail[...] = xr[ts - SUBLANES:, :]

    xcb = xc.astype(BF16)
    r = jax.nn.sigmoid(_block_diag_dot(xcb, wg_ref, 0, dr, bw) + bg_ref[:, :dr])
    i = jax.nn.sigmoid(_block_diag_dot(xcb, wg_ref, dr, dr, bw) + bg_ref[:, dr:])
    z = -lam_ref[...]
    softplus = jnp.maximum(z, 0.0) + jnp.log(1.0 + jnp.exp(-jnp.abs(z)))
    log_a = (-LRU_C * r) * softplus
    a = jnp.exp(log_a)
    mult = jnp.sqrt(1.0 - a * a)
    mult =jnp.where((row == 0) & (t == 0), 1.0, mult)
    u = mult * (i * xc)

    sub = row & (SUBLANES - 1)
    for shift in (1, 2, 4):
        keep = sub >= shift
        a_prev = jnp.where(keep, pltpu.roll(a, shift, 0), 1.0)
        u_prev = jnp.where(keep, pltpu.roll(u, shift, 0), 0.0)
        u = a * u_prev + u
        a = a * a_prev
    carry = hcar[...]
    groups = []
    for k in range(ts // SUBLANES):
        rows = slice(k * SUBLANES, (k + 1) * SUBLANES)
        hk = u[rows] + a[rows] * carry
        carry = hk[SUBLANES - 1:SUBLANES]
        groups.append(hk)
    h = jnp.concatenate(groups, axis=0)
    hcar[...] = carry

    y = (h * gate).astype(BF16)
    o_ref[0] = x + _dot(y, wout_ref[...]) + bout_ref[...]


def _lru_layer(x, gn, w_in, b_in, conv_w, conv_b, w_a, b_a, w_i, b_i, lam, w_out, b_out):
    B, S, D = x.shape
    dr = w_out.shape[0]
    ts = min(256, S)
    assert S % ts == 0 and ts % 8 == 0
    wg = jnp.concatenate([jax.scipy.linalg.block_diag(*w_a), jax.scipy.linalg.block_diag(*w_i)], axis=1)
    row = lambda v: v.reshape(1, -1)
    body = functools.partial(_lru_body, ts=ts, dr=dr, bw=w_a.shape[1])
    return pl.pallas_call(
        body,
        grid=(B, S // ts),
        in_specs=[
            pl.BlockSpec((1, ts, D), lambda b, t: (b, t, 0)),
            _resident((1, D)),
            _resident((D, 2 * dr)), _resident((1, 2 * dr)),
            _resident((CONV_WIDTH, dr)), _resident((1, dr)),
            _resident((dr, 2 * dr)), _resident((1, 2 * dr)),
            _resident((1, dr)),
            _resident((dr, D)), _resident((1, D)),
        ],
        out_specs=pl.BlockSpec((1, ts, D), lambda b, t: (b, t, 0)),
        out_shape=jax.ShapeDtypeStruct((B, S, D), F32),
        scratch_shapes=[pltpu.VMEM((SUBLANES, dr), F32), pltpu.VMEM((1, dr), F32)],
        compiler_params=_params("arbitrary", "arbitrary"),
        name="lru_mixer",
    )(x, row(gn), w_in.astype(BF16), row(b_in), conv_w, row(conv_b), wg.astype(BF16),
      row(jnp.concatenate([b_a, b_i])), row(lam), w_out.astype(BF16), row(b_out))


def _swiglu_partial(hn, wg_ref, wu_ref, wd_ref):
    g = _dot(hn, wg_ref[0].astype(BF16))
    u = _dot(hn, wu_ref[0].astype(BF16))
    return _dot(((g * jax.nn.sigmoid(g)) * u).astype(BF16), wd_ref[0].astype(BF16))


def _ffn_body(x_ref, gn_ref, wg_ref, wu_ref, wd_ref, o_ref, hn_ref, acc_ref, *, n_f):
    f = pl.program_id(1)

    @pl.when(f == 0)
    def _():
        hn_ref[...] = _rms(x_ref[...], gn_ref[...]).astype(BF16)
        acc_ref[...] = jnp.zeros_like(acc_ref)

    acc_ref[...] += _swiglu_partial(hn_ref[...], wg_ref, wu_ref, wd_ref)

    @pl.when(f == n_f - 1)
    def _():
        o_ref[...] = x_ref[...] + acc_ref[...]


def _ffn_layer(x2, gn, w_gate, w_up, w_down, tm=1024, tf=512):
    M, D = x2.shape
    F = w_gate.shape[2]
    tm = min(tm, M)
    tf = min(tf, F)
    assert M % tm == 0 and F % tf == 0
    n_f = F // tf
    return pl.pallas_call(
        functools.partial(_ffn_body, n_f=n_f),
        grid=(M // tm, n_f),
        in_specs=[pl.BlockSpec((tm, D), lambda i, f: (i, 0)), _resident((1, D)),
                  pl.BlockSpec((1, D, tf), lambda i, f: (0, 0, f)),
                  pl.BlockSpec((1, D, tf), lambda i, f: (0, 0, f)),
                  pl.BlockSpec((1, tf, D), lambda i, f: (0, f, 0))],
        out_specs=pl.BlockSpec((tm, D), lambda i, f: (i, 0)),
        out_shape=jax.ShapeDtypeStruct((M, D), F32),
        scratch_shapes=[pltpu.VMEM((tm, D), BF16), pltpu.VMEM((tm, D), F32)],
        compiler_params=_params("arbitrary", "arbitrary"),
        name="dense_swiglu",
    )(x2, gn.reshape(1, D), w_gate, w_up, w_down)


_HIGH_HALF = 0xFFFF0000


def _pack_bf16_pair(lo, hi):
    lo_bits = lax.bitcast_convert_type(lo, jnp.uint32)
    hi_bits = lax.bitcast_convert_type(hi, jnp.uint32)
    return (lo_bits >> 16) | (hi_bits & jnp.uint32(_HIGH_HALF))


def _unpack_bf16_pair(w):
    return (lax.bitcast_convert_type(w << 16, F32),
            lax.bitcast_convert_type(w & jnp.uint32(_HIGH_HALF), F32))


def _outproj_router_body(x_ref, a_ref, b_ref, c_ref, w_ref, gn_ref, wh_ref, wl_ref, tril_ref,
                         x3_ref, o_ref, cnt_ref, hp_ref, plan_ref, *, n_experts):
    @pl.when(pl.program_id(0) == 0)
    def _():
        cnt_ref[...] = jnp.zeros_like(cnt_ref)

    attn = a_ref[...].astype(F32) + b_ref[...].astype(F32) + c_ref[...].astype(F32)
    x3 = x_ref[...] + _dot(attn.astype(BF16), w_ref[...])
    x3_ref[...] = x3

    hn = _rms(x3, gn_ref[...])
    hh = hn.astype(BF16)
    hb = hh.astype(F32)
    half = hn.shape[1] // 2
    hp_ref[...] = _pack_bf16_pair(hb[:, :half], hb[:, half:])
    hl = (hn - hb).astype(BF16)
    wh = wh_ref[...]
    logits = _dot(hh, wh) + _dot(hh, wl_ref[...]) + _dot(hl, wh)
    lane = lax.broadcasted_iota(jnp.int32, logits.shape, 1)
    lg = jnp.where(lane < n_experts, logits, -jnp.inf)
    m0 = jnp.max(lg, axis=1, keepdims=True)
    i0 = jnp.min(jnp.where(lg == m0, lane, LANES), axis=1, keepdims=True)
    lg = jnp.where(lane == i0, -jnp.inf, lg)
    m1 = jnp.max(lg, axis=1, keepdims=True)
    i1 = jnp.min(jnp.where(lg == m1, lane, LANES), axis=1, keepdims=True)
    e1 = jnp.exp(m1 - m0)
    w0 = 1.0 / (1.0 + e1)
    routed = ((lane == i0) | (lane == i1)).astype(F32)
    incl = _dot(tril_ref[...], routed.astype(BF16))
    excl = incl - routed + cnt_ref[0:1, :]

    def put(col, v):
        return jnp.where(lane == col, v, 0.0)

    meta = (put(META_I0, i0.astype(F32)) + put(META_I1, i1.astype(F32)) + put(META_W0, w0)
            + put(META_W1, e1 * w0) + put(META_R0, _lane_col(excl, lane, i0))
            + put(META_R1, _lane_col(excl, lane, i1)))
    o_ref[...] = meta
    plan_ref[0] = meta.T[0:SUBLANES, :].astype(jnp.int32)
    cnt_ref[...] = jnp.broadcast_to(cnt_ref[0:1, :] + incl[incl.shape[0] - 1:, :], cnt_ref.shape)


META_I0, META_I1, META_W0, META_W1, META_R0, META_R1 = range(6)


def _outproj_router(x2, oc, os_, ow, w_out, gn, w_router, tm=512):
    M, D = x2.shape
    K = w_out.shape[0]
    n_experts = w_router.shape[1]
    tm = min(tm, M)
    wpad = jnp.pad(w_router, ((0, 0), (0, LANES - n_experts)))
    wh = wpad.astype(BF16)
    wl = (wpad - wh.astype(F32)).astype(BF16)
    tril = jnp.tril(jnp.ones((tm, tm), BF16))
    row = lambda width: pl.BlockSpec((tm, width), lambda i: (i, 0))
    return pl.pallas_call(
        functools.partial(_outproj_router_body, n_experts=n_experts),
        grid=(M // tm,),
        in_specs=[row(D), row(K), row(K), row(K), _resident((K, D)), _resident((1, D)),
                  _resident((D, LANES)), _resident((D, LANES)), _resident((tm, tm))],
        out_specs=[row(D), row(LANES), pl.BlockSpec((SUBLANES, LANES), lambda i: (0, 0)), row(D // 2),
                   pl.BlockSpec((1, SUBLANES, tm), lambda i: (i, 0, 0))],
        out_shape=[jax.ShapeDtypeStruct((M, D), F32), jax.ShapeDtypeStruct((M, LANES), F32),
                   jax.ShapeDtypeStruct((SUBLANES, LANES), F32), jax.ShapeDtypeStruct((M, D // 2), jnp.uint32),
                   jax.ShapeDtypeStruct((M // tm, SUBLANES, tm), jnp.int32)],
        compiler_params=_params("arbitrary"),
        name="nsa_outproj_moe_router",
    )(x2, oc, os_, ow, w_out.astype(BF16), gn.reshape(1, D), wh, wl, tril)


def _combine_body(pos_ref, pos_next_ref, x_ref, meta_ref, y_ref, gfin_ref, o_ref, ybuf, sem, *, tm, n_tiles):
    i = pl.program_id(0)
    slot = i % 2

    def row_copy(idx_ref, k, r, s):
        return pltpu.make_async_copy(y_ref.at[pl.ds(idx_ref[0, 0, k * tm + r], 1)], ybuf.at[s, k, pl.ds(r, 1)],
                                     sem.at[s])

    @pl.when(i == 0)
    def _():
        def body(r, c):
            row_copy(pos_ref, 0, r, 0).start()
            row_copy(pos_ref, 1, r, 0).start()
            return c
        lax.fori_loop(0, tm, body, 0, unroll=DMA_ISSUE_UNROLL)

    @pl.when(i + 1 < n_tiles)
    def _():
        for r in range(tm):
            row_copy(pos_next_ref, 0, r, 1 - slot).start(priority=0)
            row_copy(pos_next_ref, 1, r, 1 - slot).start(priority=1)

    pltpu.make_async_copy(ybuf.at[slot], ybuf.at[slot], sem.at[slot]).wait()
    meta = meta_ref[...]
    lane = lax.broadcasted_iota(jnp.int32, meta.shape, 1)
    y0 = jnp.concatenate(_unpack_bf16_pair(ybuf[slot, 0]), axis=1)
    y1 = jnp.concatenate(_unpack_bf16_pair(ybuf[slot, 1]), axis=1)
    out = x_ref[...] + _lane_col(meta, lane, META_W0) * y0 + _lane_col(meta, lane, META_W1) * y1
    o_ref[...] = _rms(out, gfin_ref[...])


def _combine(x2, meta, y, pos0, pos1, g_final, tm=512):
    M, D = x2.shape
    tm = min(tm, M)
    n_tiles = M // tm
    pos = jnp.concatenate([pos0.reshape(n_tiles, 1, tm), pos1.reshape(n_tiles, 1, tm)], axis=2)
    return pl.pallas_call(
        functools.partial(_combine_body, tm=tm, n_tiles=n_tiles),
        grid=(n_tiles,),
        in_specs=[pl.BlockSpec((1, 1, 2 * tm), lambda i: (i, 0, 0), memory_space=pltpu.SMEM),
                  pl.BlockSpec((1, 1, 2 * tm), lambda i: (jnp.minimum(i + 1, n_tiles - 1), 0, 0),
                               memory_space=pltpu.SMEM),
                  pl.BlockSpec((tm, D), lambda i: (i, 0)),
                  pl.BlockSpec((tm, LANES), lambda i: (i, 0)),
                  pl.BlockSpec(memory_space=pl.ANY),
                  _resident((1, D))],
        out_specs=pl.BlockSpec((tm, D), lambda i: (i, 0)),
        out_shape=jax.ShapeDtypeStruct((M, D), F32),
        scratch_shapes=[pltpu.VMEM((2, 2, tm, D // 2), jnp.uint32), pltpu.SemaphoreType.DMA((2,))],
        compiler_params=_params("arbitrary"),
        name="moe_combine",
    )(pos, pos, x2, meta, y, g_final.reshape(1, D))


GATHER_AHEAD = 2
DMA_ISSUE_UNROLL = 8


def _grouped_ffn_body(te_ref, nused_ref, src_ref, src_next_ref, src_ahead_ref, x_ref, wg_ref, wu_ref,
                      wd_ref, o_ref, xbuf, hn_ref, acc_ref, sem, *, tm, n_f):
    j = pl.program_id(0)
    f = pl.program_id(1)
    n_used = nused_ref[0]
    used = j < n_used
    gathered = j < n_used + GATHER_AHEAD
    slot = j % (GATHER_AHEAD + 1)
    slot_ahead = (j + GATHER_AHEAD) % (GATHER_AHEAD + 1)
    chunk = tm // (n_f + 1)

    def row_copy(idx_ref, r, s):
        return pltpu.make_async_copy(x_ref.at[pl.ds(idx_ref[0, 0, r], 1)], xbuf.at[s, pl.ds(r, 1)], sem.at[s])

    def start_chunk(first):
        for k in range(chunk):
            row_copy(src_ahead_ref, first + k, slot_ahead).start()

    @pl.when((j == 0) & (f == 0))
    def _():
        def body(r, c):
            row_copy(src_ref, r, 0).start()
            row_copy(src_next_ref, r, 1).start()
            return c
        lax.fori_loop(0, tm, body, 0, unroll=DMA_ISSUE_UNROLL)

    @pl.when(f == 0)
    def _():
        acc_ref[...] = jnp.zeros_like(acc_ref)

        @pl.when(gathered)
        def _():
            pltpu.make_async_copy(xbuf.at[slot], xbuf.at[slot], sem.at[slot]).wait()
            lo, hi = _unpack_bf16_pair(xbuf[slot])
            half = lo.shape[1]
            hn_ref[:, :half] = lo.astype(BF16)
            hn_ref[:, half:] = hi.astype(BF16)

        @pl.when(used)
        def _():
            start_chunk(0)

    @pl.when(used)
    def _():
        acc_ref[...] += _swiglu_partial(hn_ref[...], wg_ref, wu_ref, wd_ref)
        start_chunk((f + 1) * chunk)

    @pl.when(f == n_f - 1)
    def _():
        out = acc_ref[...].astype(BF16).astype(F32)
        half = out.shape[1] // 2
        o_ref[...] = _pack_bf16_pair(out[:, :half], out[:, half:])


def _grouped_ffn(hp, src, w_gate, w_up, w_down, tile_expert, n_used, tm, tf=512):
    P = src.shape[0]
    D = w_gate.shape[1]
    F = w_gate.shape[2]
    tf = min(tf, F)
    n_f = F // tf
    n_tiles = P // tm
    assert F % tf == 0 and tm % (n_f + 1) == 0
    src3 = src.reshape(n_tiles, 1, tm)

    def f_block(j, f, nu):
        return jnp.where(j < nu[0], f, n_f - 1)

    def src_spec(ahead):
        return pl.BlockSpec((1, 1, tm), lambda j, f, te, nu: (jnp.minimum(j + ahead, n_tiles - 1), 0, 0),
                            memory_space=pltpu.SMEM)

    grid_spec = pltpu.PrefetchScalarGridSpec(
        num_scalar_prefetch=2,
        grid=(n_tiles, n_f),
        in_specs=[src_spec(a) for a in range(GATHER_AHEAD + 1)] + [
                  pl.BlockSpec(memory_space=pl.ANY),
                  pl.BlockSpec((1, D, tf), lambda j, f, te, nu: (te[j], 0, f_block(j, f, nu))),
                  pl.BlockSpec((1, D, tf), lambda j, f, te, nu: (te[j], 0, f_block(j, f, nu))),
                  pl.BlockSpec((1, tf, D), lambda j, f, te, nu: (te[j], f_block(j, f, nu), 0))],
        out_specs=pl.BlockSpec((tm, D // 2), lambda j, f, te, nu: (j, 0)),
        scratch_shapes=[pltpu.VMEM((GATHER_AHEAD + 1, tm, D // 2), jnp.uint32), pltpu.VMEM((tm, D), BF16),
                        pltpu.VMEM((tm, D), F32), pltpu.SemaphoreType.DMA((GATHER_AHEAD + 1,))])
    return pl.pallas_call(
        functools.partial(_grouped_ffn_body, tm=tm, n_f=n_f),
        grid_spec=grid_spec,
        out_shape=jax.ShapeDtypeStruct((P, D // 2), jnp.uint32),
        compiler_params=_params("arbitrary", "arbitrary"),
        name="moe_grouped_swiglu",
    )(tile_expert, n_used, *([src3] * (GATHER_AHEAD + 1)), hp, w_gate, w_up, w_down)


def _moe_experts(x2, routing, w_gate, w_up, w_down, g_final, tm=1024):
    M, D = x2.shape
    n_e = w_gate.shape[0]
    tm = min(tm, M)
    meta, counts, hp, plan = routing
    i0, i1, r0, r1 = (plan[:, col, :].reshape(M) for col in (META_I0, META_I1, META_R0, META_R1))
    padded = (counts[0, :n_e].astype(jnp.int32) + tm - 1) // tm * tm
    ends = jnp.cumsum(padded)
    offsets = ends - padded
    experts = jnp.arange(n_e, dtype=jnp.int32)
    pos0 = jnp.sum(jnp.where(i0[:, None] == experts, offsets, 0), axis=1) + r0
    pos1 = jnp.sum(jnp.where(i1[:, None] == experts, offsets, 0), axis=1) + r1
    P = TOP_K * M + (n_e - 1 + GATHER_AHEAD) * tm
    tok = jnp.arange(M, dtype=jnp.int32)
    src = jnp.zeros((P,), jnp.int32).at[jnp.concatenate([pos0, pos1])].set(
        jnp.concatenate([tok, tok]), unique_indices=True, mode="promise_in_bounds")
    tile_start = jnp.arange(P // tm, dtype=jnp.int32) * tm
    tile_expert = jnp.minimum(jnp.sum(tile_start[:, None] >= ends, axis=1), n_e - 1).astype(jnp.int32)
    n_used = (ends[n_e - 1:] // tm).astype(jnp.int32)
    y = _grouped_ffn(hp, src, w_gate, w_up, w_down, tile_expert, n_used, tm)
    return _combine(x2, meta, y, pos0, pos1, g_final)


def _nsa_proj_body(x_ref, gn_ref, w_ref, cos_ref, sin_ref,
                   q_ref, kc_ref, vc_ref, ks_ref, vs_ref, kw_ref, vw_ref, g_ref, *, tm, tiles_per_seq):
    i = pl.program_id(0)
    hn = _rms(x_ref[...], gn_ref[...]).astype(BF16)
    proj = _dot(hn, w_ref[...])
    lane = lax.broadcasted_iota(jnp.int32, (tm, LANES), 1)
    first_half = (lane & (HEAD_DIM - 1)) < HEAD_DIM // 2
    cos = cos_ref[...]
    sin = sin_ref[...]

    def chunk(c):
        return proj[:, c * LANES:(c + 1) * LANES]

    def rope(v):
        rot = jnp.where(first_half, pltpu.roll(v, LANES - HEAD_DIM // 2, 1), pltpu.roll(v, HEAD_DIM // 2, 1))
        return v * cos + rot * sin

    def heads(v):
        return v[:, :HEAD_DIM], pltpu.roll(v, HEAD_DIM, 1)[:, :HEAD_DIM]

    c = 0
    for cc in range(N_HEADS // 2):
        lo, hi = heads(rope(chunk(c)) * Q_SCALE); c += 1
        q_ref[2 * cc] = lo.astype(BF16)
        q_ref[2 * cc + 1] = hi.astype(BF16)
    for cc in range(N_KV_GROUPS // 2):
        lo, hi = heads(rope(chunk(c))); c += 1
        kc_ref[2 * cc] = lo
        kc_ref[2 * cc + 1] = hi
    for cc in range(N_KV_GROUPS // 2):
        lo, hi = heads(chunk(c)); c += 1
        vc_ref[2 * cc] = lo
        vc_ref[2 * cc + 1] = hi
    pos = (i % tiles_per_seq) * tm + lax.broadcasted_iota(jnp.int32, (tm, 1), 0)
    onehot = (lane == pos // SEL_BLOCK).astype(BF16)
    for cc in range(N_KV_GROUPS // 2):
        v = rope(chunk(c)); c += 1
        for j, vv in enumerate((v, pltpu.roll(v, HEAD_DIM, 1))):
            ks_ref[2 * cc + j, :, 0:LANES] = jnp.where(lane < HEAD_DIM, vv, 0.0).astype(BF16)
            ks_ref[2 * cc + j, :, LANES:2 * LANES] = onehot
    ones_col = jnp.where(lane == HEAD_DIM, 1.0, 0.0)

    def store_values(ref):
        nonlocal c
        for cc in range(N_KV_GROUPS // 2):
            v = chunk(c); c += 1
            for j, vv in enumerate((v, pltpu.roll(v, HEAD_DIM, 1))):
                ref[2 * cc + j] = jnp.where(lane < HEAD_DIM, vv, ones_col).astype(BF16)

    store_values(vs_ref)
    for cc in range(N_KV_GROUPS // 2):
        lo, hi = heads(rope(chunk(c))); c += 1
        kw_ref[2 * cc] = lo.astype(BF16)
        kw_ref[2 * cc + 1] = hi.astype(BF16)
    store_values(vw_ref)
    g_ref[...] = jax.nn.sigmoid(chunk(c))


def _nsa_proj(x2, gn, w_in, S, tm=512):
    M, D = x2.shape
    H, G, DH = N_HEADS, N_KV_GROUPS, HEAD_DIM
    tm = min(tm, S)
    assert S % tm == 0
    n_in = w_in.shape[1]
    n_pad = -(-n_in // LANES) * LANES
    wp = jnp.pad(w_in, ((0, 0), (0, n_pad - n_in))).astype(BF16)
    half = DH // 2
    freqs = ROPE_THETA ** (-jnp.arange(half, dtype=F32) / half)
    ang = jnp.arange(S, dtype=F32)[:, None] * freqs[None, :]
    cos = jnp.tile(jnp.cos(ang), (1, 2 * LANES // DH))
    sin = jnp.tile(jnp.concatenate([-jnp.sin(ang), jnp.sin(ang)], axis=1), (1, LANES // DH))
    tiles_per_seq = S // tm
    hd = lambda n, dt: jax.ShapeDtypeStruct((n, M, DH), dt)
    hspec = lambda n: pl.BlockSpec((n, tm, DH), lambda i: (0, i, 0))
    vd = jax.ShapeDtypeStruct((G, M, LANES), BF16)
    vspec = pl.BlockSpec((G, tm, LANES), lambda i: (0, i, 0))
    return pl.pallas_call(
        functools.partial(_nsa_proj_body, tm=tm, tiles_per_seq=tiles_per_seq),
        grid=(M // tm,),
        in_specs=[pl.BlockSpec((tm, D), lambda i: (i, 0)), _resident((1, D)), _resident((D, n_pad)),
                  pl.BlockSpec((tm, LANES), lambda i: (i % tiles_per_seq, 0)),
                  pl.BlockSpec((tm, LANES), lambda i: (i % tiles_per_seq, 0))],
        out_specs=[hspec(H), hspec(G), hspec(G),
                   pl.BlockSpec((G, tm, 2 * LANES), lambda i: (0, i, 0)),
                   vspec, hspec(G), vspec,
                   pl.BlockSpec((tm, LANES), lambda i: (i, 0))],
        out_shape=[hd(H, BF16), hd(G, F32), hd(G, F32),
                   jax.ShapeDtypeStruct((G, M, 2 * LANES), BF16),
                   vd, hd(G, BF16), vd,
                   jax.ShapeDtypeStruct((M, LANES), F32)],
        compiler_params=_params("arbitrary"),
        name="nsa_proj",
    )(x2, gn.reshape(1, D), wp, cos, sin)


def _compress_body(kc_ref, vc_ref, pe_ref, w1_ref, w2_ref, ko_ref, vo_ref, *, nc):
    half = CMP_BLOCK // 2
    for kv, (src, dst) in enumerate(((kc_ref, ko_ref), (vc_ref, vo_ref))):
        top = jnp.zeros((nc, w1_ref.shape[2]), F32)
        bot = jnp.zeros((nc, w1_ref.shape[2]), F32)
        for j in range(half):
            xj = src[0, pl.ds(j, nc, stride=CMP_STRIDE), :]
            top = top + _dot((xj + pe_ref[kv, j:j + 1, :]).astype(BF16),
                             w1_ref[kv, j * HEAD_DIM:(j + 1) * HEAD_DIM, :])
            bot = bot + _dot((xj + pe_ref[kv, half + j:half + j + 1, :]).astype(BF16),
                             w1_ref[kv, (half + j) * HEAD_DIM:(half + j + 1) * HEAD_DIM, :])
        hid = top + pltpu.roll(bot, nc - 1, 0)
        dst[0] = _dot(_gelu_tanh(hid).astype(BF16), w2_ref[kv]).astype(BF16)


def _compress(kc, vc, pe, w1, w2, B, S):
    assert CMP_BLOCK == 2 * CMP_STRIDE
    G, M, DH = kc.shape
    nc = S // CMP_STRIDE
    spec_in = pl.BlockSpec((1, S, DH), lambda b, g: (g, b, 0))
    spec_out = pl.BlockSpec((1, nc, DH), lambda b, g: (g, b, 0))
    out = jax.ShapeDtypeStruct((G, B * nc, DH), BF16)
    return pl.pallas_call(
        functools.partial(_compress_body, nc=nc),
        grid=(B, G),
        in_specs=[spec_in, spec_in, _resident(pe.shape), _resident(w1.shape), _resident(w2.shape)],
        out_specs=[spec_out, spec_out],
        out_shape=[out, out],
        compiler_params=_params("arbitrary", "arbitrary"),
        name="nsa_compress",
    )(kc, vc, pe, w1.astype(BF16), w2.astype(BF16))


def _store_head(o_ref, oh, gates, g, h, branch, rows=slice(None), denom=None):
    lane = lax.broadcasted_iota(jnp.int32, gates.shape, 1)
    col = N_BRANCHES * (HEADS_PER_GROUP * g + h) + branch
    scale = _lane_col(gates, lane, col)
    if denom is not None:
        scale = scale / denom
    o_ref[rows, h * HEAD_DIM:(h + 1) * HEAD_DIM] = (oh * scale).astype(o_ref.dtype)


def _store_softmax_head(o_ref, acc, gates, g, h, branch, rows=slice(None)):
    _store_head(o_ref, acc[:, :HEAD_DIM], gates, g, h, branch, rows, denom=acc[:, HEAD_DIM:HEAD_DIM + 1])


def _cmp_body(q_ref, kc_ref, vc_ref, gates_ref, ov_ref, o_ref, bias_ref, imp_ref, *, tq, nc, n_s, k_sel):
    g = pl.program_id(1)
    i = pl.program_id(2)
    gates = gates_ref[...]
    t = i * tq + lax.broadcasted_iota(jnp.int32, (tq, 1), 0)

    def attend(w):
        kc = kc_ref[0, 0:w, :]
        vc = vc_ref[0, 0:w, :]
        cend = lax.broadcasted_iota(jnp.int32, (1, w), 1) * CMP_STRIDE + (CMP_BLOCK - 1)
        mask = cend <= t
        ps = None
        for h in range(HEADS_PER_GROUP):
            s = lax.dot_general(q_ref[h], kc, _NT, preferred_element_type=F32)
            s = jnp.where(mask, s, MASK_VALUE)
            m = jnp.max(s, axis=1, keepdims=True)
            m = jnp.where(m > 0.5 * MASK_VALUE, m, 0.0)
            e = jnp.exp2(s - m)
            p = e * (1.0 / jnp.maximum(jnp.sum(e, axis=1, keepdims=True), 1e-30))
            _store_head(o_ref, _dot(p.astype(BF16), vc), gates, g, h, 0)
            ps = p if ps is None else ps + p
        ph = ps.astype(BF16)
        pl_ = (ps - ph.astype(F32)).astype(BF16)
        imp_ref[...] = _dot(ph, ov_ref[0:w, :]) + _dot(pl_, ov_ref[0:w, :])

    n_chunks = nc // LANES
    need = ((i + 1) * (tq // CMP_STRIDE) + LANES - 1) // LANES
    for k in range(1, n_chunks + 1):
        pl.when((need == k) if k < n_chunks else (need >= k))(functools.partial(attend, k * LANES))

    j = lax.broadcasted_iota(jnp.int32, (tq, LANES), 1)
    cur = t // SEL_BLOCK
    forced = (j == 0) | (j == cur) | (j == cur - 1)
    valid = j * SEL_BLOCK <= t
    score = jnp.where(forced, FORCE_SCORE, jnp.where(valid, imp_ref[...], -1.0))
    score = jnp.where(j < n_s, score, -jnp.inf)
    x = score.T
    blk_i = lax.broadcasted_iota(jnp.int32, (LANES, tq), 0)
    blk = blk_i.astype(F32)

    def topk(x, n_iter):
        for _ in range(n_iter):
            mx = jnp.max(x, axis=0, keepdims=True)
            idx = jnp.min(jnp.where(x == mx, blk, float(LANES)), axis=0, keepdims=True)
            x = jnp.where(blk == idx, -jnp.inf, x)
        sel = (x == -jnp.inf) & (blk_i < n_s)
        bias_ref[0] = jnp.where(sel, 0.0, MASK_VALUE).T.astype(BF16)

    n_forced = 3
    direct = (i * tq >= 2 * SEL_BLOCK) & (k_sel > n_forced)

    @pl.when(direct)
    def _():
        cur_t = (i * tq + lax.broadcasted_iota(jnp.int32, (1, tq), 1)) // SEL_BLOCK
        forced_t = (blk_i == 0) | (blk_i == cur_t) | (blk_i == cur_t - 1)
        topk(jnp.where(forced_t, -jnp.inf, x), k_sel - n_forced)

    @pl.when(jnp.logical_not(direct))
    def _():
        topk(x, k_sel)


def _overlap_matrix(nc, n_s):
    r = CMP_BLOCK // CMP_STRIDE
    qn = SEL_BLOCK // CMP_STRIDE
    m = np.zeros((nc, LANES), np.float32)
    n_c = nc - r + 1
    chunks = np.arange(n_c)[:, None] + np.arange(r)[None, :]
    np.add.at(m, (np.repeat(np.arange(n_c), r), (chunks // qn).ravel()), 1.0)
    return m


def _cmp_attention(q, kcmp, vcmp, gates, B, S, tq=1024):
    H, M, DH = q.shape
    G = N_KV_GROUPS
    tq = min(tq, S)
    nq = S // tq
    nc = S // CMP_STRIDE
    n_s = S // SEL_BLOCK
    assert n_s <= LANES and tq & (tq - 1) == 0
    k_sel = min(N_SEL, n_s)
    ov = jnp.asarray(_overlap_matrix(nc, n_s), BF16)
    return pl.pallas_call(
        functools.partial(_cmp_body, tq=tq, nc=nc, n_s=n_s, k_sel=k_sel),
        grid=(B, G, nq),
        in_specs=[pl.BlockSpec((HEADS_PER_GROUP, tq, DH), lambda b, g, i: (g, b * nq + i, 0)),
                  pl.BlockSpec((1, nc, DH), lambda b, g, i: (g, b, 0)),
                  pl.BlockSpec((1, nc, DH), lambda b, g, i: (g, b, 0)),
                  pl.BlockSpec((tq, LANES), lambda b, g, i: (b * nq + i, 0)),
                  _resident((nc, LANES))],
        out_specs=[pl.BlockSpec((tq, HEADS_PER_GROUP * DH), lambda b, g, i: (b * nq + i, g)),
                   pl.BlockSpec((1, tq, LANES), lambda b, g, i: (g, b * nq + i, 0))],
        out_shape=[jax.ShapeDtypeStruct((M, H * DH), BF16), jax.ShapeDtypeStruct((G, M, LANES), BF16)],
        scratch_shapes=[pltpu.VMEM((tq, LANES), F32)],
        compiler_params=_params("arbitrary", "arbitrary", "arbitrary"),
        name="nsa_cmp_select",
    )(q, kcmp, vcmp, gates, ov)


def _sel_body(q_ref, bias_ref, k_ref, v_ref, gates_ref, o_ref, qcat, m_ref, acc_ref, *, tq):
    g = pl.program_id(1)
    i = pl.program_id(2)
    hp = HEADS_PER_GROUP
    rows = hp * tq
    qcat[:, 0:LANES] = jnp.zeros((rows, LANES), BF16)
    qcat[:, 0:HEAD_DIM] = q_ref[...].reshape(rows, HEAD_DIM)
    bias = bias_ref[0]
    for h in range(hp):
        qcat[h * tq:(h + 1) * tq, LANES:2 * LANES] = bias
    m_ref[...] = jnp.full((rows, LANES), MASK_VALUE, F32)
    acc_ref[...] = jnp.zeros((rows, LANES), F32)

    def tile(j, width, causal):
        start = pl.multiple_of(j * tq, tq)
        kt = k_ref[0, pl.ds(start, width), :]
        vt = v_ref[0, pl.ds(start, width), :]
        for h in range(hp):
            r0 = h * tq
            s = lax.dot_general(qcat[r0:r0 + tq, :], kt, _NT, preferred_element_type=F32)
            if causal:
                r = lax.broadcasted_iota(jnp.int32, (tq, 1), 0)
                c = lax.broadcasted_iota(jnp.int32, (1, width), 1) - (width - tq)
                s = jnp.where(c <= r, s, MASK_VALUE)
            m_prev = m_ref[r0:r0 + tq, :]
            m_new = jnp.maximum(m_prev, jnp.max(s, axis=1, keepdims=True))
            alpha = jnp.exp2(m_prev - m_new)
            p = jnp.exp2(s - jnp.concatenate([m_new] * (width // LANES), axis=1))
            acc_ref[r0:r0 + tq, :] = alpha * acc_ref[r0:r0 + tq, :] + _dot(p.astype(BF16), vt)
            m_ref[r0:r0 + tq, :] = m_new

    def wide_tile(jw, carry):
        tile(SEL_WIDE * jw, SEL_WIDE * tq, False)
        return carry

    n_wide = i // SEL_WIDE
    lax.fori_loop(0, n_wide, wide_tile, 0)
    for rem in range(SEL_WIDE):
        pl.when(i % SEL_WIDE == rem)(functools.partial(tile, SEL_WIDE * n_wide, (rem + 1) * tq, True))

    gates = gates_ref[...]
    for h in range(hp):
        _store_softmax_head(o_ref, acc_ref[h * tq:(h + 1) * tq, :], gates, g, h, 1)


def _sel_attention(q, bias, kscat, vs, gates, B, S, tq=512):
    H, M, DH = q.shape
    G = N_KV_GROUPS
    tq = min(tq, S)
    nq = S // tq
    assert tq % LANES == 0 and tq % SEL_BLOCK == 0 and S % tq == 0
    rows = HEADS_PER_GROUP * tq
    return pl.pallas_call(
        functools.partial(_sel_body, tq=tq),
        grid=(B, G, nq),
        in_specs=[pl.BlockSpec((HEADS_PER_GROUP, tq, DH), lambda b, g, i: (g, b * nq + i, 0)),
                  pl.BlockSpec((1, tq, LANES), lambda b, g, i: (g, b * nq + i, 0)),
                  pl.BlockSpec((1, S, 2 * LANES), lambda b, g, i: (g, b, 0)),
                  pl.BlockSpec((1, S, LANES), lambda b, g, i: (g, b, 0)),
                  pl.BlockSpec((tq, LANES), lambda b, g, i: (b * nq + i, 0))],
        out_specs=pl.BlockSpec((tq, HEADS_PER_GROUP * DH), lambda b, g, i: (b * nq + i, g)),
        out_shape=jax.ShapeDtypeStruct((M, H * DH), BF16),
        scratch_shapes=[pltpu.VMEM((rows, 2 * LANES), BF16), pltpu.VMEM((rows, LANES), F32),
                        pltpu.VMEM((rows, LANES), F32)],
        compiler_params=_params("arbitrary", "arbitrary", "arbitrary"),
        name="nsa_selected",
    )(q, bias, kscat, vs, gates)


def _win_body(q_ref, k_ref, v_ref, gates_ref, o_ref, *, tq, n_back, n_sub):
    g = pl.program_id(1)
    r = lax.broadcasted_iota(jnp.int32, (tq, 1), 0)
    c = lax.broadcasted_iota(jnp.int32, (1, tq), 1)
    for sub in range(n_sub):
        i = pl.program_id(2) * n_sub + sub
        rows = slice(sub * tq, (sub + 1) * tq)
        gates = gates_ref[rows, :]
        tiles = []
        for back in range(n_back, -1, -1):
            jt = i - back
            start = pl.multiple_of(jnp.maximum(jt, 0) * tq, tq)
            d = r - c + back * tq
            ok = (d >= 0) & (d < WINDOW) & (jt >= 0)
            tiles.append((k_ref[0, pl.ds(start, tq), :], v_ref[0, pl.ds(start, tq), :], ok))
        for h in range(HEADS_PER_GROUP):
            q = q_ref[h, rows, :]
            scores = [jnp.where(ok, lax.dot_general(q, kt, _NT, preferred_element_type=F32), MASK_VALUE)
                      for kt, _, ok in tiles]
            m = functools.reduce(jnp.maximum, scores).max(axis=1, keepdims=True)
            acc = jnp.zeros((tq, LANES), F32)
            for s, (_, vt, _) in zip(scores, tiles):
                acc = acc + _dot(jnp.exp2(s - m).astype(BF16), vt)
            _store_softmax_head(o_ref, acc, gates, g, h, 2, rows)


def _win_attention(q, kw, vw, gates, B, S, tq=256, n_sub=4):
    H, M, DH = q.shape
    G = N_KV_GROUPS
    tq = min(tq, S)
    n_sub = min(n_sub, S // tq)
    tile = tq
    tq = tile * n_sub
    nq = S // tq
    n_back = -(-WINDOW // tile)
    return pl.pallas_call(
        functools.partial(_win_body, tq=tile, n_back=n_back, n_sub=n_sub),
        grid=(B, G, nq),
        in_specs=[pl.BlockSpec((HEADS_PER_GROUP, tq, DH), lambda b, g, i: (g, b * nq + i, 0)),
                  pl.BlockSpec((1, S, DH), lambda b, g, i: (g, b, 0)),
                  pl.BlockSpec((1, S, LANES), lambda b, g, i: (g, b, 0)),
                  pl.BlockSpec((tq, LANES), lambda b, g, i: (b * nq + i, 0))],
        out_specs=pl.BlockSpec((tq, HEADS_PER_GROUP * DH), lambda b, g, i: (b * nq + i, g)),
        out_shape=jax.ShapeDtypeStruct((M, H * DH), BF16),
        compiler_params=_params("arbitrary", "arbitrary", "arbitrary"),
        name="nsa_window",
    )(q, kw, vw, gates)


def _nsa_branches(x, gn, w_in, cmp_pe, cmp_w1, cmp_w2):
    B, S, D = x.shape
    x2 = x.reshape(B * S, D)
    q, kc, vc, kscat, vs, kw, vw, gates = _nsa_proj(x2, gn, w_in, S)
    kcmp, vcmp = _compress(kc, vc, cmp_pe, cmp_w1, cmp_w2, B, S)
    o_cmp, bias = _cmp_attention(q, kcmp, vcmp, gates, B, S)
    o_sel = _sel_attention(q, bias, kscat, vs, gates, B, S)
    o_win = _win_attention(q, kw, vw, gates, B, S)
    return o_cmp, o_sel, o_win


def kernel(x, norm_mix, norm_ffn, norm_final, lru_w_in, lru_b_in, lru_conv_w, lru_conv_b, lru_w_a, lru_b_a, lru_w_i, lru_b_i, lru_lambda, lru_w_out, lru_b_out, nsa_w_in, nsa_cmp_pe, nsa_cmp_w1, nsa_cmp_w2, nsa_w_out, ffn_w_gate, ffn_w_up, ffn_w_down, moe_w_router, moe_w_gate, moe_w_up, moe_w_down):
    B, S, D = x.shape
    assert norm_mix.shape[0] == 2 and lru_w_in.shape[0] == 1 and nsa_w_in.shape[0] == 1
    x = _lru_layer(x, norm_mix[0], lru_w_in[0], lru_b_in[0], lru_conv_w[0], lru_conv_b[0], lru_w_a[0],
                   lru_b_a[0], lru_w_i[0], lru_b_i[0], lru_lambda[0], lru_w_out[0], lru_b_out[0])
    x2 = _ffn_layer(x.reshape(B * S, D), norm_ffn[0], ffn_w_gate, ffn_w_up, ffn_w_down)
    branches = _nsa_branches(x2.reshape(B, S, D), norm_mix[1], nsa_w_in[0], nsa_cmp_pe[0], nsa_cmp_w1[0],
                             nsa_cmp_w2[0])
    x2, *routing = _outproj_router(x2, *branches, nsa_w_out[0], norm_ffn[1], moe_w_router[0])
    out = _moe_experts(x2, routing, moe_w_gate[0], moe_w_up[0], moe_w_down[0], norm_final)
    return out.reshape(B, S, D)
```

```python
import functools

import numpy as np
import jax
import jax.numpy as jnp
from jax import lax
from jax.experimental import pallas as pl
from jax.experimental.pallas import tpu as pltpu

F32 = jnp.float32
BF16 = jnp.bfloat16

RMS_EPS = 1e-6
LRU_C = 8.0
CONV_WIDTH = 4
N_HEADS = 16
HEAD_DIM = 64
N_KV_GROUPS = 4
HEADS_PER_GROUP = N_HEADS // N_KV_GROUPS
CMP_BLOCK = 32
CMP_STRIDE = 16
SEL_BLOCK = 64
N_SEL = 16
N_BRANCHES = 3
WINDOW = 512
FORCE_SCORE = 1.0e4
ROPE_THETA = 10000.0
TOP_K = 2

LANES = 128
SUBLANES = 8
MXU_WIDTH = 256
MASK_VALUE = -1.0e30
VMEM_LIMIT_BYTES = 56 * 1024 * 1024

_NT = (((1,), (1,)), ((), ()))
Q_SCALE = HEAD_DIM ** -0.5 * 1.4426950408889634
SEL_WIDE = 6


def _params(*semantics):
    return pltpu.CompilerParams(dimension_semantics=semantics, vmem_limit_bytes=VMEM_LIMIT_BYTES)


def _resident(shape):
    zeros = (0,) * len(shape)
    return pl.BlockSpec(shape, lambda *_: zeros, pipeline_mode=pl.Buffered(1))


def _rms(x, g):
    return x * lax.rsqrt(jnp.mean(x * x, axis=-1, keepdims=True) + RMS_EPS) * g


def _dot(a, b):
    return jnp.dot(a, b, preferred_element_type=F32)


def _gelu_tanh(x):
    two_z = x * (2.0 * 0.7978845608028654 + (2.0 * 0.7978845608028654 * 0.044715) * (x * x))
    return x * jax.nn.sigmoid(two_z)


def _block_diag_dot(xb, w_ref, col0, n, bw):
    outs = []
    for c0 in range(0, n, MXU_WIDTH):
        w = min(MXU_WIDTH, n - c0)
        lo = (c0 // bw) * bw // LANES * LANES
        hi = min(n, -(-(((c0 + w - 1) // bw + 1) * bw) // LANES) * LANES)
        outs.append(_dot(xb[:, lo:hi], w_ref[lo:hi, col0 + c0:col0 + c0 + w]))
    return jnp.concatenate(outs, axis=1)


def _lane_col(vals, lane, idx):
    return jnp.sum(jnp.where(lane == idx, vals, 0.0), axis=1, keepdims=True)


def _lru_body(x_ref, gn_ref, win_ref, bin_ref, cw_ref, cb_ref, wg_ref, bg_ref, lam_ref,
              wout_ref, bout_ref, o_ref, xtail, hcar, *, ts, dr, bw):
    t = pl.program_id(1)

    @pl.when(t == 0)
    def _():
        xtail[...] = jnp.zeros_like(xtail)
        hcar[...] = jnp.zeros_like(hcar)

    x = x_ref[0]
    hn = _rms(x, gn_ref[...]).astype(BF16)
    proj = _dot(hn, win_ref[...]) + bin_ref[...]
    gate = _gelu_tanh(proj[:, :dr])
    xr = proj[:, dr:]

    tail = xtail[...]
    row = lax.broadcasted_iota(jnp.int32, (ts, 1), 0)
    xc = cb_ref[...] + xr * cw_ref[CONV_WIDTH - 1:CONV_WIDTH, :]
    for lag in range(1, CONV_WIDTH):
        k = CONV_WIDTH - 1 - lag
        shifted = pltpu.roll(xr, lag, 0)
        head = jnp.where(row[:SUBLANES] < lag, pltpu.roll(tail, lag, 0), shifted[:SUBLANES])
        xc = xc + jnp.concatenate([head, shifted[SUBLANES:]], axis=0) * cw_ref[k:k + 1, :]
    xtail[...] = xr[ts - SUBLANES:, :]

    xcb = xc.astype(BF16)
    r = jax.nn.sigmoid(_block_diag_dot(xcb, wg_ref, 0, dr, bw) + bg_ref[:, :dr])
    i = jax.nn.sigmoid(_block_diag_dot(xcb, wg_ref, dr, dr, bw) + bg_ref[:, dr:])
    z = -lam_ref[...]
    softplus = jnp.maximum(z, 0.0) + jnp.log(1.0 + jnp.exp(-jnp.abs(z)))
    log_a = (-LRU_C * r) * softplus
    a = jnp.exp(log_a)
    mult = jnp.sqrt(1.0 - a * a)
    mult =jnp.where((row == 0) & (t == 0), 1.0, mult)
    u = mult * (i * xc)

    sub = row & (SUBLANES - 1)
    for shift in (1, 2, 4):
        keep = sub >= shift
        a_prev = jnp.where(keep, pltpu.roll(a, shift, 0), 1.0)
        u_prev = jnp.where(keep, pltpu.roll(u, shift, 0), 0.0)
        u = a * u_prev + u
        a = a * a_prev
    carry = hcar[...]
    groups = []
    for k in range(ts // SUBLANES):
        rows = slice(k * SUBLANES, (k + 1) * SUBLANES)
        hk = u[rows] + a[rows] * carry
        carry = hk[SUBLANES - 1:SUBLANES]
        groups.append(hk)
    h = jnp.concatenate(groups, axis=0)
    hcar[...] = carry

    y = (h * gate).astype(BF16)
    o_ref[0] = x + _dot(y, wout_ref[...]) + bout_ref[...]


def _lru_layer(x, gn, w_in, b_in, conv_w, conv_b, w_a, b_a, w_i, b_i, lam, w_out, b_out):
    B, S, D = x.shape
    dr = w_out.shape[0]
    ts = min(256, S)
    assert S % ts == 0 and ts % 8 == 0
    wg = jnp.concatenate([jax.scipy.linalg.block_diag(*w_a), jax.scipy.linalg.block_diag(*w_i)], axis=1)
    row = lambda v: v.reshape(1, -1)
    body = functools.partial(_lru_body, ts=ts, dr=dr, bw=w_a.shape[1])
    return pl.pallas_call(
        body,
        grid=(B, S // ts),
        in_specs=[
            pl.BlockSpec((1, ts, D), lambda b, t: (b, t, 0)),
            _resident((1, D)),
            _resident((D, 2 * dr)), _resident((1, 2 * dr)),
            _resident((CONV_WIDTH, dr)), _resident((1, dr)),
            _resident((dr, 2 * dr)), _resident((1, 2 * dr)),
            _resident((1, dr)),
            _resident((dr, D)), _resident((1, D)),
        ],
        out_specs=pl.BlockSpec((1, ts, D), lambda b, t: (b, t, 0)),
        out_shape=jax.ShapeDtypeStruct((B, S, D), F32),
        scratch_shapes=[pltpu.VMEM((SUBLANES, dr), F32), pltpu.VMEM((1, dr), F32)],
        compiler_params=_params("arbitrary", "arbitrary"),
        name="lru_mixer",
    )(x, row(gn), w_in.astype(BF16), row(b_in), conv_w, row(conv_b), wg.astype(BF16),
      row(jnp.concatenate([b_a, b_i])), row(lam), w_out.astype(BF16), row(b_out))


def _swiglu_partial(hn, wg_ref, wu_ref, wd_ref):
    g = _dot(hn, wg_ref[0].astype(BF16))
    u = _dot(hn, wu_ref[0].astype(BF16))
    return _dot(((g * jax.nn.sigmoid(g)) * u).astype(BF16), wd_ref[0].astype(BF16))


def _ffn_body(x_ref, gn_ref, wg_ref, wu_ref, wd_ref, o_ref, hn_ref, acc_ref, *, n_f):
    f = pl.program_id(1)

    @pl.when(f == 0)
    def _():
        hn_ref[...] = _rms(x_ref[...], gn_ref[...]).astype(BF16)
        acc_ref[...] = jnp.zeros_like(acc_ref)

    acc_ref[...] += _swiglu_partial(hn_ref[...], wg_ref, wu_ref, wd_ref)

    @pl.when(f == n_f - 1)
    def _():
        o_ref[...] = x_ref[...] + acc_ref[...]


def _ffn_layer(x2, gn, w_gate, w_up, w_down, tm=1024, tf=512):
    M, D = x2.shape
    F = w_gate.shape[2]
    tm = min(tm, M)
    tf = min(tf, F)
    assert M % tm == 0 and F % tf == 0
    n_f = F // tf
    return pl.pallas_call(
        functools.partial(_ffn_body, n_f=n_f),
        grid=(M // tm, n_f),
        in_specs=[pl.BlockSpec((tm, D), lambda i, f: (i, 0)), _resident((1, D)),
                  pl.BlockSpec((1, D, tf), lambda i, f: (0, 0, f)),
                  pl.BlockSpec((1, D, tf), lambda i, f: (0, 0, f)),
                  pl.BlockSpec((1, tf, D), lambda i, f: (0, f, 0))],
        out_specs=pl.BlockSpec((tm, D), lambda i, f: (i, 0)),
        out_shape=jax.ShapeDtypeStruct((M, D), F32),
        scratch_shapes=[pltpu.VMEM((tm, D), BF16), pltpu.VMEM((tm, D), F32)],
        compiler_params=_params("arbitrary", "arbitrary"),
        name="dense_swiglu",
    )(x2, gn.reshape(1, D), w_gate, w_up, w_down)


def _outproj_router_body(x_ref, a_ref, b_ref, c_ref, w_ref, gn_ref, wh_ref, wl_ref, tril_ref,
                         x3_ref, o_ref, cnt_ref, hn_ref, plan_ref, *, n_experts):
    @pl.when(pl.program_id(0) == 0)
    def _():
        cnt_ref[...] = jnp.zeros_like(cnt_ref)

    attn = a_ref[...].astype(F32) + b_ref[...].astype(F32) + c_ref[...].astype(F32)
    x3 = x_ref[...] + _dot(attn.astype(BF16), w_ref[...])
    x3_ref[...] = x3

    hn = _rms(x3, gn_ref[...])
    hh = hn.astype(BF16)
    hb = hh.astype(F32)
    hn_ref[...] = hb
    hl = (hn - hb).astype(BF16)
    wh = wh_ref[...]
    logits = _dot(hh, wh) + _dot(hh, wl_ref[...]) + _dot(hl, wh)
    lane = lax.broadcasted_iota(jnp.int32, logits.shape, 1)
    lg = jnp.where(lane < n_experts, logits, -jnp.inf)
    m0 = jnp.max(lg, axis=1, keepdims=True)
    i0 = jnp.min(jnp.where(lg == m0, lane, LANES), axis=1, keepdims=True)
    lg = jnp.where(lane == i0, -jnp.inf, lg)
    m1 = jnp.max(lg, axis=1, keepdims=True)
    i1 = jnp.min(jnp.where(lg == m1, lane, LANES), axis=1, keepdims=True)
    e1 = jnp.exp(m1 - m0)
    w0 = 1.0 / (1.0 + e1)
    routed = ((lane == i0) | (lane == i1)).astype(F32)
    incl = _dot(tril_ref[...], routed.astype(BF16))
    excl = incl - routed + cnt_ref[0:1, :]

    def put(col, v):
        return jnp.where(lane == col, v, 0.0)

    meta = (put(META_I0, i0.astype(F32)) + put(META_I1, i1.astype(F32)) + put(META_W0, w0)
            + put(META_W1, e1 * w0) + put(META_R0, _lane_col(excl, lane, i0))
            + put(META_R1, _lane_col(excl, lane, i1)))
    o_ref[...] = meta
    plan_ref[0] = meta.T[0:SUBLANES, :].astype(jnp.int32)
    cnt_ref[...] = jnp.broadcast_to(cnt_ref[0:1, :] + incl[incl.shape[0] - 1:, :], cnt_ref.shape)


META_I0, META_I1, META_W0, META_W1, META_R0, META_R1 = range(6)


def _outproj_router(x2, oc, os_, ow, w_out, gn, w_router, tm=512):
    M, D = x2.shape
    K = w_out.shape[0]
    n_experts = w_router.shape[1]
    tm = min(tm, M)
    wpad = jnp.pad(w_router, ((0, 0), (0, LANES - n_experts)))
    wh = wpad.astype(BF16)
    wl = (wpad - wh.astype(F32)).astype(BF16)
    tril = jnp.tril(jnp.ones((tm, tm), BF16))
    row = lambda width: pl.BlockSpec((tm, width), lambda i: (i, 0))
    return pl.pallas_call(
        functools.partial(_outproj_router_body, n_experts=n_experts),
        grid=(M // tm,),
        in_specs=[row(D), row(K), row(K), row(K), _resident((K, D)), _resident((1, D)),
                  _resident((D, LANES)), _resident((D, LANES)), _resident((tm, tm))],
        out_specs=[row(D), row(LANES), pl.BlockSpec((SUBLANES, LANES), lambda i: (0, 0)), row(D),
                   pl.BlockSpec((1, SUBLANES, tm), lambda i: (i, 0, 0))],
        out_shape=[jax.ShapeDtypeStruct((M, D), F32), jax.ShapeDtypeStruct((M, LANES), F32),
                   jax.ShapeDtypeStruct((SUBLANES, LANES), F32), jax.ShapeDtypeStruct((M, D), F32),
                   jax.ShapeDtypeStruct((M // tm, SUBLANES, tm), jnp.int32)],
        compiler_params=_params("arbitrary"),
        name="nsa_outproj_moe_router",
    )(x2, oc, os_, ow, w_out.astype(BF16), gn.reshape(1, D), wh, wl, tril)


def _combine_body(pos_ref, pos_next_ref, x_ref, meta_ref, y_ref, gfin_ref, o_ref, ybuf, sem, *, tm, n_tiles):
    i = pl.program_id(0)
    slot = i % 2

    def row_copy(idx_ref, k, r, s):
        return pltpu.make_async_copy(y_ref.at[pl.ds(idx_ref[0, 0, k * tm + r], 1)], ybuf.at[s, k, pl.ds(r, 1)],
                                     sem.at[s])

    @pl.when(i == 0)
    def _():
        def body(r, c):
            row_copy(pos_ref, 0, r, 0).start()
            row_copy(pos_ref, 1, r, 0).start()
            return c
        lax.fori_loop(0, tm, body, 0, unroll=DMA_ISSUE_UNROLL)

    @pl.when(i + 1 < n_tiles)
    def _():
        for r in range(tm):
            row_copy(pos_next_ref, 0, r, 1 - slot).start(priority=0)
            row_copy(pos_next_ref, 1, r, 1 - slot).start(priority=1)

    pltpu.make_async_copy(ybuf.at[slot], ybuf.at[slot], sem.at[slot]).wait()
    meta = meta_ref[...]
    lane = lax.broadcasted_iota(jnp.int32, meta.shape, 1)
    out = (x_ref[...] + _lane_col(meta, lane, META_W0) * ybuf[slot, 0]
           + _lane_col(meta, lane, META_W1) * ybuf[slot, 1])
    o_ref[...] = _rms(out, gfin_ref[...])


def _combine(x2, meta, y, pos0, pos1, g_final, tm=512):
    M, D = x2.shape
    tm = min(tm, M)
    n_tiles = M // tm
    pos = jnp.concatenate([pos0.reshape(n_tiles, 1, tm), pos1.reshape(n_tiles, 1, tm)], axis=2)
    return pl.pallas_call(
        functools.partial(_combine_body, tm=tm, n_tiles=n_tiles),
        grid=(n_tiles,),
        in_specs=[pl.BlockSpec((1, 1, 2 * tm), lambda i: (i, 0, 0), memory_space=pltpu.SMEM),
                  pl.BlockSpec((1, 1, 2 * tm), lambda i: (jnp.minimum(i + 1, n_tiles - 1), 0, 0),
                               memory_space=pltpu.SMEM),
                  pl.BlockSpec((tm, D), lambda i: (i, 0)),
                  pl.BlockSpec((tm, LANES), lambda i: (i, 0)),
                  pl.BlockSpec(memory_space=pl.ANY),
                  _resident((1, D))],
        out_specs=pl.BlockSpec((tm, D), lambda i: (i, 0)),
        out_shape=jax.ShapeDtypeStruct((M, D), F32),
        scratch_shapes=[pltpu.VMEM((2, 2, tm, D), F32), pltpu.SemaphoreType.DMA((2,))],
        compiler_params=_params("arbitrary"),
        name="moe_combine",
    )(pos, pos, x2, meta, y, g_final.reshape(1, D))


GATHER_AHEAD = 2
DMA_ISSUE_UNROLL = 8


def _grouped_ffn_body(te_ref, nused_ref, src_ref, src_next_ref, src_ahead_ref, x_ref, wg_ref, wu_ref,
                      wd_ref, o_ref, xbuf, hn_ref, acc_ref, sem, *, tm, n_f):
    j = pl.program_id(0)
    f = pl.program_id(1)
    n_used = nused_ref[0]
    used = j < n_used
    gathered = j < n_used + GATHER_AHEAD
    slot = j % (GATHER_AHEAD + 1)
    slot_ahead = (j + GATHER_AHEAD) % (GATHER_AHEAD + 1)
    chunk = tm // (n_f + 1)

    def row_copy(idx_ref, r, s):
        return pltpu.make_async_copy(x_ref.at[pl.ds(idx_ref[0, 0, r], 1)], xbuf.at[s, pl.ds(r, 1)], sem.at[s])

    def start_chunk(first):
        for k in range(chunk):
            row_copy(src_ahead_ref, first + k, slot_ahead).start()

    @pl.when((j == 0) & (f == 0))
    def _():
        def body(r, c):
            row_copy(src_ref, r, 0).start()
            row_copy(src_next_ref, r, 1).start()
            return c
        lax.fori_loop(0, tm, body, 0, unroll=DMA_ISSUE_UNROLL)

    @pl.when(f == 0)
    def _():
        acc_ref[...] = jnp.zeros_like(acc_ref)

        @pl.when(gathered)
        def _():
            pltpu.make_async_copy(xbuf.at[slot], xbuf.at[slot], sem.at[slot]).wait()
            hn_ref[...] = xbuf[slot].astype(BF16)

        @pl.when(used)
        def _():
            start_chunk(0)

    @pl.when(used)
    def _():
        acc_ref[...] += _swiglu_partial(hn_ref[...], wg_ref, wu_ref, wd_ref)
        start_chunk((f + 1) * chunk)

    @pl.when(f == n_f - 1)
    def _():
        o_ref[...] = acc_ref[...]


def _grouped_ffn(hp, src, w_gate, w_up, w_down, tile_expert, n_used, tm, tf=512):
    P = src.shape[0]
    D = w_gate.shape[1]
    F = w_gate.shape[2]
    tf = min(tf, F)
    n_f = F // tf
    n_tiles = P // tm
    assert F % tf == 0 and tm % (n_f + 1) == 0
    src3 = src.reshape(n_tiles, 1, tm)

    def f_block(j, f, nu):
        return jnp.where(j < nu[0], f, n_f - 1)

    def src_spec(ahead):
        return pl.BlockSpec((1, 1, tm), lambda j, f, te, nu: (jnp.minimum(j + ahead, n_tiles - 1), 0, 0),
                            memory_space=pltpu.SMEM)

    grid_spec = pltpu.PrefetchScalarGridSpec(
        num_scalar_prefetch=2,
        grid=(n_tiles, n_f),
        in_specs=[src_spec(a) for a in range(GATHER_AHEAD + 1)] + [
                  pl.BlockSpec(memory_space=pl.ANY),
                  pl.BlockSpec((1, D, tf), lambda j, f, te, nu: (te[j], 0, f_block(j, f, nu))),
                  pl.BlockSpec((1, D, tf), lambda j, f, te, nu: (te[j], 0, f_block(j, f, nu))),
                  pl.BlockSpec((1, tf, D), lambda j, f, te, nu: (te[j], f_block(j, f, nu), 0))],
        out_specs=pl.BlockSpec((tm, D), lambda j, f, te, nu: (j, 0)),
        scratch_shapes=[pltpu.VMEM((GATHER_AHEAD + 1, tm, D), F32), pltpu.VMEM((tm, D), BF16),
                        pltpu.VMEM((tm, D), F32), pltpu.SemaphoreType.DMA((GATHER_AHEAD + 1,))])
    return pl.pallas_call(
        functools.partial(_grouped_ffn_body, tm=tm, n_f=n_f),
        grid_spec=grid_spec,
        out_shape=jax.ShapeDtypeStruct((P, D), F32),
        compiler_params=_params("arbitrary", "arbitrary"),
        name="moe_grouped_swiglu",
    )(tile_expert, n_used, *([src3] * (GATHER_AHEAD + 1)), hp, w_gate, w_up, w_down)


def _moe_experts(x2, routing, w_gate, w_up, w_down, g_final, tm=1024):
    M, D = x2.shape
    n_e = w_gate.shape[0]
    tm = min(tm, M)
    meta, counts, hp, plan = routing
    i0, i1, r0, r1 = (plan[:, col, :].reshape(M) for col in (META_I0, META_I1, META_R0, META_R1))
    padded = (counts[0, :n_e].astype(jnp.int32) + tm - 1) // tm * tm
    ends = jnp.cumsum(padded)
    offsets = ends - padded
    experts = jnp.arange(n_e, dtype=jnp.int32)
    pos0 = jnp.sum(jnp.where(i0[:, None] == experts, offsets, 0), axis=1) + r0
    pos1 = jnp.sum(jnp.where(i1[:, None] == experts, offsets, 0), axis=1) + r1
    P = TOP_K * M + (n_e - 1 + GATHER_AHEAD) * tm
    tok = jnp.arange(M, dtype=jnp.int32)
    src = jnp.zeros((P,), jnp.int32).at[jnp.concatenate([pos0, pos1])].set(
        jnp.concatenate([tok, tok]), unique_indices=True, mode="promise_in_bounds")
    tile_start = jnp.arange(P // tm, dtype=jnp.int32) * tm
    tile_expert = jnp.minimum(jnp.sum(tile_start[:, None] >= ends, axis=1), n_e - 1).astype(jnp.int32)
    n_used = (ends[n_e - 1:] // tm).astype(jnp.int32)
    y = _grouped_ffn(hp, src, w_gate, w_up, w_down, tile_expert, n_used, tm)
    return _combine(x2, meta, y, pos0, pos1, g_final)


def _nsa_proj_body(x_ref, gn_ref, w_ref, cos_ref, sin_ref,
                   q_ref, kc_ref, vc_ref, ks_ref, vs_ref, kw_ref, vw_ref, g_ref, *, tm, tiles_per_seq):
    i = pl.program_id(0)
    hn = _rms(x_ref[...], gn_ref[...]).astype(BF16)
    proj = _dot(hn, w_ref[...])
    lane = lax.broadcasted_iota(jnp.int32, (tm, LANES), 1)
    first_half = (lane & (HEAD_DIM - 1)) < HEAD_DIM // 2
    cos = cos_ref[...]
    sin = sin_ref[...]

    def chunk(c):
        return proj[:, c * LANES:(c + 1) * LANES]

    def rope(v):
        rot = jnp.where(first_half, pltpu.roll(v, LANES - HEAD_DIM // 2, 1), pltpu.roll(v, HEAD_DIM // 2, 1))
        return v * cos + rot * sin

    def heads(v):
        return v[:, :HEAD_DIM], pltpu.roll(v, HEAD_DIM, 1)[:, :HEAD_DIM]

    c = 0
    for cc in range(N_HEADS // 2):
        lo, hi = heads(rope(chunk(c)) * Q_SCALE); c += 1
        q_ref[2 * cc] = lo.astype(BF16)
        q_ref[2 * cc + 1] = hi.astype(BF16)
    for cc in range(N_KV_GROUPS // 2):
        lo, hi = heads(rope(chunk(c))); c += 1
        kc_ref[2 * cc] = lo
        kc_ref[2 * cc + 1] = hi
    for cc in range(N_KV_GROUPS // 2):
        lo, hi = heads(chunk(c)); c += 1
        vc_ref[2 * cc] = lo
        vc_ref[2 * cc + 1] = hi
    pos = (i % tiles_per_seq) * tm + lax.broadcasted_iota(jnp.int32, (tm, 1), 0)
    onehot = (lane == pos // SEL_BLOCK).astype(BF16)
    for cc in range(N_KV_GROUPS // 2):
        v = rope(chunk(c)); c += 1
        for j, vv in enumerate((v, pltpu.roll(v, HEAD_DIM, 1))):
            ks_ref[2 * cc + j, :, 0:LANES] = jnp.where(lane < HEAD_DIM, vv, 0.0).astype(BF16)
            ks_ref[2 * cc + j, :, LANES:2 * LANES] = onehot
    ones_col = jnp.where(lane == HEAD_DIM, 1.0, 0.0)

    def store_values(ref):
        nonlocal c
        for cc in range(N_KV_GROUPS // 2):
            v = chunk(c); c += 1
            for j, vv in enumerate((v, pltpu.roll(v, HEAD_DIM, 1))):
                ref[2 * cc + j] = jnp.where(lane < HEAD_DIM, vv, ones_col).astype(BF16)

    store_values(vs_ref)
    for cc in range(N_KV_GROUPS // 2):
        lo, hi = heads(rope(chunk(c))); c += 1
        kw_ref[2 * cc] = lo.astype(BF16)
        kw_ref[2 * cc + 1] = hi.astype(BF16)
    store_values(vw_ref)
    g_ref[...] = jax.nn.sigmoid(chunk(c))


def _nsa_proj(x2, gn, w_in, S, tm=512):
    M, D = x2.shape
    H, G, DH = N_HEADS, N_KV_GROUPS, HEAD_DIM
    tm = min(tm, S)
    assert S % tm == 0
    n_in = w_in.shape[1]
    n_pad = -(-n_in // LANES) * LANES
    wp = jnp.pad(w_in, ((0, 0), (0, n_pad - n_in))).astype(BF16)
    half = DH // 2
    freqs = ROPE_THETA ** (-jnp.arange(half, dtype=F32) / half)
    ang = jnp.arange(S, dtype=F32)[:, None] * freqs[None, :]
    cos = jnp.tile(jnp.cos(ang), (1, 2 * LANES // DH))
    sin = jnp.tile(jnp.concatenate([-jnp.sin(ang), jnp.sin(ang)], axis=1), (1, LANES // DH))
    tiles_per_seq = S // tm
    hd = lambda n, dt: jax.ShapeDtypeStruct((n, M, DH), dt)
    hspec = lambda n: pl.BlockSpec((n, tm, DH), lambda i: (0, i, 0))
    vd = jax.ShapeDtypeStruct((G, M, LANES), BF16)
    vspec = pl.BlockSpec((G, tm, LANES), lambda i: (0, i, 0))
    return pl.pallas_call(
        functools.partial(_nsa_proj_body, tm=tm, tiles_per_seq=tiles_per_seq),
        grid=(M // tm,),
        in_specs=[pl.BlockSpec((tm, D), lambda i: (i, 0)), _resident((1, D)), _resident((D, n_pad)),
                  pl.BlockSpec((tm, LANES), lambda i: (i % tiles_per_seq, 0)),
                  pl.BlockSpec((tm, LANES), lambda i: (i % tiles_per_seq, 0))],
        out_specs=[hspec(H), hspec(G), hspec(G),
                   pl.BlockSpec((G, tm, 2 * LANES), lambda i: (0, i, 0)),
                   vspec, hspec(G), vspec,
                   pl.BlockSpec((tm, LANES), lambda i: (i, 0))],
        out_shape=[hd(H, BF16), hd(G, F32), hd(G, F32),
                   jax.ShapeDtypeStruct((G, M, 2 * LANES), BF16),
                   vd, hd(G, BF16), vd,
                   jax.ShapeDtypeStruct((M, LANES), F32)],
        compiler_params=_params("arbitrary"),
        name="nsa_proj",
    )(x2, gn.reshape(1, D), wp, cos, sin)


def _compress_body(kc_ref, vc_ref, pe_ref, w1_ref, w2_ref, ko_ref, vo_ref, *, nc):
    half = CMP_BLOCK // 2
    for kv, (src, dst) in enumerate(((kc_ref, ko_ref), (vc_ref, vo_ref))):
        top = jnp.zeros((nc, w1_ref.shape[2]), F32)
        bot = jnp.zeros((nc, w1_ref.shape[2]), F32)
        for j in range(half):
            xj = src[0, pl.ds(j, nc, stride=CMP_STRIDE), :]
            top = top + _dot((xj + pe_ref[kv, j:j + 1, :]).astype(BF16),
                             w1_ref[kv, j * HEAD_DIM:(j + 1) * HEAD_DIM, :])
            bot = bot + _dot((xj + pe_ref[kv, half + j:half + j + 1, :]).astype(BF16),
                             w1_ref[kv, (half + j) * HEAD_DIM:(half + j + 1) * HEAD_DIM, :])
        hid = top + pltpu.roll(bot, nc - 1, 0)
        dst[0] = _dot(_gelu_tanh(hid).astype(BF16), w2_ref[kv]).astype(BF16)


def _compress(kc, vc, pe, w1, w2, B, S):
    assert CMP_BLOCK == 2 * CMP_STRIDE
    G, M, DH = kc.shape
    nc = S // CMP_STRIDE
    spec_in = pl.BlockSpec((1, S, DH), lambda b, g: (g, b, 0))
    spec_out = pl.BlockSpec((1, nc, DH), lambda b, g: (g, b, 0))
    out = jax.ShapeDtypeStruct((G, B * nc, DH), BF16)
    return pl.pallas_call(
        functools.partial(_compress_body, nc=nc),
        grid=(B, G),
        in_specs=[spec_in, spec_in, _resident(pe.shape), _resident(w1.shape), _resident(w2.shape)],
        out_specs=[spec_out, spec_out],
        out_shape=[out, out],
        compiler_params=_params("arbitrary", "arbitrary"),
        name="nsa_compress",
    )(kc, vc, pe, w1.astype(BF16), w2.astype(BF16))


def _store_head(o_ref, oh, gates, g, h, branch, rows=slice(None), denom=None):
    lane = lax.broadcasted_iota(jnp.int32, gates.shape, 1)
    col = N_BRANCHES * (HEADS_PER_GROUP * g + h) + branch
    scale = _lane_col(gates, lane, col)
    if denom is not None:
        scale = scale / denom
    o_ref[rows, h * HEAD_DIM:(h + 1) * HEAD_DIM] = (oh * scale).astype(o_ref.dtype)


def _store_softmax_head(o_ref, acc, gates, g, h, branch, rows=slice(None)):
    _store_head(o_ref, acc[:, :HEAD_DIM], gates, g, h, branch, rows, denom=acc[:, HEAD_DIM:HEAD_DIM + 1])


def _cmp_body(q_ref, kc_ref, vc_ref, gates_ref, ov_ref, o_ref, bias_ref, imp_ref, *, tq, nc, n_s, k_sel):
    g = pl.program_id(1)
    i = pl.program_id(2)
    gates = gates_ref[...]
    t = i * tq + lax.broadcasted_iota(jnp.int32, (tq, 1), 0)

    def attend(w):
        kc = kc_ref[0, 0:w, :]
        vc = vc_ref[0, 0:w, :]
        cend = lax.broadcasted_iota(jnp.int32, (1, w), 1) * CMP_STRIDE + (CMP_BLOCK - 1)
        mask = cend <= t
        ps = None
        for h in range(HEADS_PER_GROUP):
            s = lax.dot_general(q_ref[h], kc, _NT, preferred_element_type=F32)
            s = jnp.where(mask, s, MASK_VALUE)
            m = jnp.max(s, axis=1, keepdims=True)
            m = jnp.where(m > 0.5 * MASK_VALUE, m, 0.0)
            e = jnp.exp2(s - m)
            p = e * (1.0 / jnp.maximum(jnp.sum(e, axis=1, keepdims=True), 1e-30))
            _store_head(o_ref, _dot(p.astype(BF16), vc), gates, g, h, 0)
            ps = p if ps is None else ps + p
        ph = ps.astype(BF16)
        pl_ = (ps - ph.astype(F32)).astype(BF16)
        imp_ref[...] = _dot(ph, ov_ref[0:w, :]) + _dot(pl_, ov_ref[0:w, :])

    n_chunks = nc // LANES
    need = ((i + 1) * (tq // CMP_STRIDE) + LANES - 1) // LANES
    for k in range(1, n_chunks + 1):
        pl.when((need == k) if k < n_chunks else (need >= k))(functools.partial(attend, k * LANES))

    j = lax.broadcasted_iota(jnp.int32, (tq, LANES), 1)
    cur = t // SEL_BLOCK
    forced = (j == 0) | (j == cur) | (j == cur - 1)
    valid = j * SEL_BLOCK <= t
    score = jnp.where(forced, FORCE_SCORE, jnp.where(valid, imp_ref[...], -1.0))
    score = jnp.where(j < n_s, score, -jnp.inf)
    x = score.T
    blk_i = lax.broadcasted_iota(jnp.int32, (LANES, tq), 0)
    blk = blk_i.astype(F32)

    def topk(x, n_iter):
        for _ in range(n_iter):
            mx = jnp.max(x, axis=0, keepdims=True)
            idx = jnp.min(jnp.where(x == mx, blk, float(LANES)), axis=0, keepdims=True)
            x = jnp.where(blk == idx, -jnp.inf, x)
        sel = (x == -jnp.inf) & (blk_i < n_s)
        bias_ref[0] = jnp.where(sel, 0.0, MASK_VALUE).T.astype(BF16)

    n_forced = 3
    direct = (i * tq >= 2 * SEL_BLOCK) & (k_sel > n_forced)

    @pl.when(direct)
    def _():
        cur_t = (i * tq + lax.broadcasted_iota(jnp.int32, (1, tq), 1)) // SEL_BLOCK
        forced_t = (blk_i == 0) | (blk_i == cur_t) | (blk_i == cur_t - 1)
        topk(jnp.where(forced_t, -jnp.inf, x), k_sel - n_forced)

    @pl.when(jnp.logical_not(direct))
    def _():
        topk(x, k_sel)


def _overlap_matrix(nc, n_s):
    r = CMP_BLOCK // CMP_STRIDE
    qn = SEL_BLOCK // CMP_STRIDE
    m = np.zeros((nc, LANES), np.float32)
    n_c = nc - r + 1
    chunks = np.arange(n_c)[:, None] + np.arange(r)[None, :]
    np.add.at(m, (np.repeat(np.arange(n_c), r), (chunks // qn).ravel()), 1.0)
    return m


def _cmp_attention(q, kcmp, vcmp, gates, B, S, tq=1024):
    H, M, DH = q.shape
    G = N_KV_GROUPS
    tq = min(tq, S)
    nq = S // tq
    nc = S // CMP_STRIDE
    n_s = S // SEL_BLOCK
    assert n_s <= LANES and tq & (tq - 1) == 0
    k_sel = min(N_SEL, n_s)
    ov = jnp.asarray(_overlap_matrix(nc, n_s), BF16)
    return pl.pallas_call(
        functools.partial(_cmp_body, tq=tq, nc=nc, n_s=n_s, k_sel=k_sel),
        grid=(B, G, nq),
        in_specs=[pl.BlockSpec((HEADS_PER_GROUP, tq, DH), lambda b, g, i: (g, b * nq + i, 0)),
                  pl.BlockSpec((1, nc, DH), lambda b, g, i: (g, b, 0)),
                  pl.BlockSpec((1, nc, DH), lambda b, g, i: (g, b, 0)),
                  pl.BlockSpec((tq, LANES), lambda b, g, i: (b * nq + i, 0)),
                  _resident((nc, LANES))],
        out_specs=[pl.BlockSpec((tq, HEADS_PER_GROUP * DH), lambda b, g, i: (b * nq + i, g)),
                   pl.BlockSpec((1, tq, LANES), lambda b, g, i: (g, b * nq + i, 0))],
        out_shape=[jax.ShapeDtypeStruct((M, H * DH), BF16), jax.ShapeDtypeStruct((G, M, LANES), BF16)],
        scratch_shapes=[pltpu.VMEM((tq, LANES), F32)],
        compiler_params=_params("arbitrary", "arbitrary", "arbitrary"),
        name="nsa_cmp_select",
    )(q, kcmp, vcmp, gates, ov)


def _sel_body(q_ref, bias_ref, k_ref, v_ref, gates_ref, o_ref, qcat, m_ref, acc_ref, *, tq):
    g = pl.program_id(1)
    i = pl.program_id(2)
    hp = HEADS_PER_GROUP
    rows = hp * tq
    qcat[:, 0:LANES] = jnp.zeros((rows, LANES), BF16)
    qcat[:, 0:HEAD_DIM] = q_ref[...].reshape(rows, HEAD_DIM)
    bias = bias_ref[0]
    for h in range(hp):
        qcat[h * tq:(h + 1) * tq, LANES:2 * LANES] = bias
    m_ref[...] = jnp.full((rows, LANES), MASK_VALUE, F32)
    acc_ref[...] = jnp.zeros((rows, LANES), F32)

    def tile(j, width, causal):
        start = pl.multiple_of(j * tq, tq)
        kt = k_ref[0, pl.ds(start, width), :]
        vt = v_ref[0, pl.ds(start, width), :]
        for h in range(hp):
            r0 = h * tq
            s = lax.dot_general(qcat[r0:r0 + tq, :], kt, _NT, preferred_element_type=F32)
            if causal:
                r = lax.broadcasted_iota(jnp.int32, (tq, 1), 0)
                c = lax.broadcasted_iota(jnp.int32, (1, width), 1) - (width - tq)
                s = jnp.where(c <= r, s, MASK_VALUE)
            m_prev = m_ref[r0:r0 + tq, :]
            m_new = jnp.maximum(m_prev, jnp.max(s, axis=1, keepdims=True))
            alpha = jnp.exp2(m_prev - m_new)
            p = jnp.exp2(s - jnp.concatenate([m_new] * (width // LANES), axis=1))
            acc_ref[r0:r0 + tq, :] = alpha * acc_ref[r0:r0 + tq, :] + _dot(p.astype(BF16), vt)
            m_ref[r0:r0 + tq, :] = m_new

    def wide_tile(jw, carry):
        tile(SEL_WIDE * jw, SEL_WIDE * tq, False)
        return carry

    n_wide = i // SEL_WIDE
    lax.fori_loop(0, n_wide, wide_tile, 0)
    for rem in range(SEL_WIDE):
        pl.when(i % SEL_WIDE == rem)(functools.partial(tile, SEL_WIDE * n_wide, (rem + 1) * tq, True))

    gates = gates_ref[...]
    for h in range(hp):
        _store_softmax_head(o_ref, acc_ref[h * tq:(h + 1) * tq, :], gates, g, h, 1)


def _sel_attention(q, bias, kscat, vs, gates, B, S, tq=512):
    H, M, DH = q.shape
    G = N_KV_GROUPS
    tq = min(tq, S)
    nq = S // tq
    assert tq % LANES == 0 and tq % SEL_BLOCK == 0 and S % tq == 0
    rows = HEADS_PER_GROUP * tq
    return pl.pallas_call(
        functools.partial(_sel_body, tq=tq),
        grid=(B, G, nq),
        in_specs=[pl.BlockSpec((HEADS_PER_GROUP, tq, DH), lambda b, g, i: (g, b * nq + i, 0)),
                  pl.BlockSpec((1, tq, LANES), lambda b, g, i: (g, b * nq + i, 0)),
                  pl.BlockSpec((1, S, 2 * LANES), lambda b, g, i: (g, b, 0)),
                  pl.BlockSpec((1, S, LANES), lambda b, g, i: (g, b, 0)),
                  pl.BlockSpec((tq, LANES), lambda b, g, i: (b * nq + i, 0))],
        out_specs=pl.BlockSpec((tq, HEADS_PER_GROUP * DH), lambda b, g, i: (b * nq + i, g)),
        out_shape=jax.ShapeDtypeStruct((M, H * DH), BF16),
        scratch_shapes=[pltpu.VMEM((rows, 2 * LANES), BF16), pltpu.VMEM((rows, LANES), F32),
                        pltpu.VMEM((rows, LANES), F32)],
        compiler_params=_params("arbitrary", "arbitrary", "arbitrary"),
        name="nsa_selected",
    )(q, bias, kscat, vs, gates)


def _win_body(q_ref, k_ref, v_ref, gates_ref, o_ref, *, tq, n_back, n_sub):
    g = pl.program_id(1)
    r = lax.broadcasted_iota(jnp.int32, (tq, 1), 0)
    c = lax.broadcasted_iota(jnp.int32, (1, tq), 1)
    for sub in range(n_sub):
        i = pl.program_id(2) * n_sub + sub
        rows = slice(sub * tq, (sub + 1) * tq)
        gates = gates_ref[rows, :]
        tiles = []
        for back in range(n_back, -1, -1):
            jt = i - back
            start = pl.multiple_of(jnp.maximum(jt, 0) * tq, tq)
            d = r - c + back * tq
            ok = (d >= 0) & (d < WINDOW) & (jt >= 0)
            tiles.append((k_ref[0, pl.ds(start, tq), :], v_ref[0, pl.ds(start, tq), :], ok))
        for h in range(HEADS_PER_GROUP):
            q = q_ref[h, rows, :]
            scores = [jnp.where(ok, lax.dot_general(q, kt, _NT, preferred_element_type=F32), MASK_VALUE)
                      for kt, _, ok in tiles]
            m = functools.reduce(jnp.maximum, scores).max(axis=1, keepdims=True)
            acc = jnp.zeros((tq, LANES), F32)
            for s, (_, vt, _) in zip(scores, tiles):
                acc = acc + _dot(jnp.exp2(s - m).astype(BF16), vt)
            _store_softmax_head(o_ref, acc, gates, g, h, 2, rows)


def _win_attention(q, kw, vw, gates, B, S, tq=256, n_sub=4):
    H, M, DH = q.shape
    G = N_KV_GROUPS
    tq = min(tq, S)
    n_sub = min(n_sub, S // tq)
    tile = tq
    tq = tile * n_sub
    nq = S // tq
    n_back = -(-WINDOW // tile)
    return pl.pallas_call(
        functools.partial(_win_body, tq=tile, n_back=n_back, n_sub=n_sub),
        grid=(B, G, nq),
        in_specs=[pl.BlockSpec((HEADS_PER_GROUP, tq, DH), lambda b, g, i: (g, b * nq + i, 0)),
                  pl.BlockSpec((1, S, DH), lambda b, g, i: (g, b, 0)),
                  pl.BlockSpec((1, S, LANES), lambda b, g, i: (g, b, 0)),
                  pl.BlockSpec((tq, LANES), lambda b, g, i: (b * nq + i, 0))],
        out_specs=pl.BlockSpec((tq, HEADS_PER_GROUP * DH), lambda b, g, i: (b * nq + i, g)),
        out_shape=jax.ShapeDtypeStruct((M, H * DH), BF16),
        compiler_params=_params("arbitrary", "arbitrary", "arbitrary"),
        name="nsa_window",
    )(q, kw, vw, gates)


def _nsa_branches(x, gn, w_in, cmp_pe, cmp_w1, cmp_w2):
    B, S, D = x.shape
    x2 = x.reshape(B * S, D)
    q, kc, vc, kscat, vs, kw, vw, gates = _nsa_proj(x2, gn, w_in, S)
    kcmp, vcmp = _compress(kc, vc, cmp_pe, cmp_w1, cmp_w2, B, S)
    o_cmp, bias = _cmp_attention(q, kcmp, vcmp, gates, B, S)
    o_sel = _sel_attention(q, bias, kscat, vs, gates, B, S)
    o_win = _win_attention(q, kw, vw, gates, B, S)
    return o_cmp, o_sel, o_win


def kernel(x, norm_mix, norm_ffn, norm_final, lru_w_in, lru_b_in, lru_conv_w, lru_conv_b, lru_w_a, lru_b_a, lru_w_i, lru_b_i, lru_lambda, lru_w_out, lru_b_out, nsa_w_in, nsa_cmp_pe, nsa_cmp_w1, nsa_cmp_w2, nsa_w_out, ffn_w_gate, ffn_w_up, ffn_w_down, moe_w_router, moe_w_gate, moe_w_up, moe_w_down):
    B, S, D = x.shape
    assert norm_mix.shape[0] == 2 and lru_w_in.shape[0] == 1 and nsa_w_in.shape[0] == 1
    x = _lru_layer(x, norm_mix[0], lru_w_in[0], lru_b_in[0], lru_conv_w[0], lru_conv_b[0], lru_w_a[0],
                   lru_b_a[0], lru_w_i[0], lru_b_i[0], lru_lambda[0], lru_w_out[0], lru_b_out[0])
    x2 = _ffn_layer(x.reshape(B * S, D), norm_ffn[0], ffn_w_gate, ffn_w_up, ffn_w_down)
    branches = _nsa_branches(x2.reshape(B, S, D), norm_mix[1], nsa_w_in[0], nsa_cmp_pe[0], nsa_cmp_w1[0],
                             nsa_cmp_w2[0])
    x2, *routing = _outproj_router(x2, *branches, nsa_w_out[0], norm_ffn[1], moe_w_router[0])
    out = _moe_experts(x2, routing, moe_w_gate[0], moe_w_up[0], moe_w_down[0], norm_final)
    return out.reshape(B, S, D)
```

```python
import functools

import numpy as np
import jax
import jax.numpy as jnp
from jax import lax
from jax.experimental import pallas as pl
from jax.experimental.pallas import tpu as pltpu

F32 = jnp.float32
BF16 = jnp.bfloat16

RMS_EPS = 1e-6
LRU_C = 8.0
CONV_WIDTH = 4
N_HEADS = 16
HEAD_DIM = 64
N_KV_GROUPS = 4
HEADS_PER_GROUP = N_HEADS // N_KV_GROUPS
CMP_BLOCK = 32
CMP_STRIDE = 16
SEL_BLOCK = 64
N_SEL = 16
N_BRANCHES = 3
WINDOW = 512
FORCE_SCORE = 1.0e4
ROPE_THETA = 10000.0
TOP_K = 2

LANES = 128
SUBLANES = 8
MXU_WIDTH = 256
MASK_VALUE = -1.0e30
VMEM_LIMIT_BYTES = 56 * 1024 * 1024

_NT = (((1,), (1,)), ((), ()))
Q_SCALE = HEAD_DIM ** -0.5 * 1.4426950408889634
SEL_WIDE = 8


def _params(*semantics):
    return pltpu.CompilerParams(dimension_semantics=semantics, vmem_limit_bytes=VMEM_LIMIT_BYTES)


def _resident(shape):
    zeros = (0,) * len(shape)
    return pl.BlockSpec(shape, lambda *_: zeros, pipeline_mode=pl.Buffered(1))


def _rms(x, g):
    return x * lax.rsqrt(jnp.mean(x * x, axis=-1, keepdims=True) + RMS_EPS) * g


def _dot(a, b):
    return jnp.dot(a, b, preferred_element_type=F32)


def _gelu_tanh(x):
    two_z = x * (2.0 * 0.7978845608028654 + (2.0 * 0.7978845608028654 * 0.044715) * (x * x))
    return x * jax.nn.sigmoid(two_z)


def _block_diag_dot(xb, w_ref, col0, n, bw):
    outs = []
    for c0 in range(0, n, MXU_WIDTH):
        w = min(MXU_WIDTH, n - c0)
        lo = (c0 // bw) * bw // LANES * LANES
        hi = min(n, -(-(((c0 + w - 1) // bw + 1) * bw) // LANES) * LANES)
        outs.append(_dot(xb[:, lo:hi], w_ref[lo:hi, col0 + c0:col0 + c0 + w]))
    return jnp.concatenate(outs, axis=1)


def _lane_col(vals, lane, idx):
    return jnp.sum(jnp.where(lane == idx, vals, 0.0), axis=1, keepdims=True)


def _lru_body(x_ref, gn_ref, win_ref, bin_ref, cw_ref, cb_ref, wg_ref, bg_ref, lam_ref,
              wout_ref, bout_ref, o_ref, xtail, hcar, *, ts, dr, bw):
    t = pl.program_id(1)

    @pl.when(t == 0)
    def _():
        xtail[...] = jnp.zeros_like(xtail)
        hcar[...] = jnp.zeros_like(hcar)

    x = x_ref[0]
    hn = _rms(x, gn_ref[...]).astype(BF16)
    proj = _dot(hn, win_ref[...]) + bin_ref[...]
    gate = _gelu_tanh(proj[:, :dr])
    xr = proj[:, dr:]

    tail = xtail[...]
    row = lax.broadcasted_iota(jnp.int32, (ts, 1), 0)
    xc = cb_ref[...] + xr * cw_ref[CONV_WIDTH - 1:CONV_WIDTH, :]
    for lag in range(1, CONV_WIDTH):
        k = CONV_WIDTH - 1 - lag
        shifted = pltpu.roll(xr, lag, 0)
        head = jnp.where(row[:SUBLANES] < lag, pltpu.roll(tail, lag, 0), shifted[:SUBLANES])
        xc = xc + jnp.concatenate([head, shifted[SUBLANES:]], axis=0) * cw_ref[k:k + 1, :]
    xtail[...] = xr[ts - SUBLANES:, :]

    xcb = xc.astype(BF16)
    r = jax.nn.sigmoid(_block_diag_dot(xcb, wg_ref, 0, dr, bw) + bg_ref[:, :dr])
    i = jax.nn.sigmoid(_block_diag_dot(xcb, wg_ref, dr, dr, bw) + bg_ref[:, dr:])
    z = -lam_ref[...]
    softplus = jnp.maximum(z, 0.0) + jnp.log(1.0 + jnp.exp(-jnp.abs(z)))
    log_a = (-LRU_C * r) * softplus
    a = jnp.exp(log_a)
    mult = jnp.sqrt(1.0 - a * a)
    mult =jnp.where((row == 0) & (t == 0), 1.0, mult)
    u = mult * (i * xc)

    sub = row & (SUBLANES - 1)
    for shift in (1, 2, 4):
        keep = sub >= shift
        a_prev = jnp.where(keep, pltpu.roll(a, shift, 0), 1.0)
        u_prev = jnp.where(keep, pltpu.roll(u, shift, 0), 0.0)
        u = a * u_prev + u
        a = a * a_prev
    carry = hcar[...]
    groups = []
    for k in range(ts // SUBLANES):
        rows = slice(k * SUBLANES, (k + 1) * SUBLANES)
        hk = u[rows] + a[rows] * carry
        carry = hk[SUBLANES - 1:SUBLANES]
        groups.append(hk)
    h = jnp.concatenate(groups, axis=0)
    hcar[...] = carry

    y = (h * gate).astype(BF16)
    o_ref[0] = x + _dot(y, wout_ref[...]) + bout_ref[...]


def _lru_layer(x, gn, w_in, b_in, conv_w, conv_b, w_a, b_a, w_i, b_i, lam, w_out, b_out):
    B, S, D = x.shape
    dr = w_out.shape[0]
    ts = min(256, S)
    assert S % ts == 0 and ts % 8 == 0
    wg = jnp.concatenate([jax.scipy.linalg.block_diag(*w_a), jax.scipy.linalg.block_diag(*w_i)], axis=1)
    row = lambda v: v.reshape(1, -1)
    body = functools.partial(_lru_body, ts=ts, dr=dr, bw=w_a.shape[1])
    return pl.pallas_call(
        body,
        grid=(B, S // ts),
        in_specs=[
            pl.BlockSpec((1, ts, D), lambda b, t: (b, t, 0)),
            _resident((1, D)),
            _resident((D, 2 * dr)), _resident((1, 2 * dr)),
            _resident((CONV_WIDTH, dr)), _resident((1, dr)),
            _resident((dr, 2 * dr)), _resident((1, 2 * dr)),
            _resident((1, dr)),
            _resident((dr, D)), _resident((1, D)),
        ],
        out_specs=pl.BlockSpec((1, ts, D), lambda b, t: (b, t, 0)),
        out_shape=jax.ShapeDtypeStruct((B, S, D), F32),
        scratch_shapes=[pltpu.VMEM((SUBLANES, dr), F32), pltpu.VMEM((1, dr), F32)],
        compiler_params=_params("arbitrary", "arbitrary"),
        name="lru_mixer",
    )(x, row(gn), w_in.astype(BF16), row(b_in), conv_w, row(conv_b), wg.astype(BF16),
      row(jnp.concatenate([b_a, b_i])), row(lam), w_out.astype(BF16), row(b_out))


def _swiglu_partial(hn, wg_ref, wu_ref, wd_ref):
    g = _dot(hn, wg_ref[0].astype(BF16))
    u = _dot(hn, wu_ref[0].astype(BF16))
    return _dot(((g * jax.nn.sigmoid(g)) * u).astype(BF16), wd_ref[0].astype(BF16))


def _ffn_body(x_ref, gn_ref, wg_ref, wu_ref, wd_ref, o_ref, hn_ref, acc_ref, *, n_f):
    f = pl.program_id(1)

    @pl.when(f == 0)
    def _():
        hn_ref[...] = _rms(x_ref[...], gn_ref[...]).astype(BF16)
        acc_ref[...] = jnp.zeros_like(acc_ref)

    acc_ref[...] += _swiglu_partial(hn_ref[...], wg_ref, wu_ref, wd_ref)

    @pl.when(f == n_f - 1)
    def _():
        o_ref[...] = x_ref[...] + acc_ref[...]


def _ffn_layer(x2, gn, w_gate, w_up, w_down, tm=1024, tf=512):
    M, D = x2.shape
    F = w_gate.shape[2]
    tm = min(tm, M)
    tf = min(tf, F)
    assert M % tm == 0 and F % tf == 0
    n_f = F // tf
    return pl.pallas_call(
        functools.partial(_ffn_body, n_f=n_f),
        grid=(M // tm, n_f),
        in_specs=[pl.BlockSpec((tm, D), lambda i, f: (i, 0)), _resident((1, D)),
                  pl.BlockSpec((1, D, tf), lambda i, f: (0, 0, f)),
                  pl.BlockSpec((1, D, tf), lambda i, f: (0, 0, f)),
                  pl.BlockSpec((1, tf, D), lambda i, f: (0, f, 0))],
        out_specs=pl.BlockSpec((tm, D), lambda i, f: (i, 0)),
        out_shape=jax.ShapeDtypeStruct((M, D), F32),
        scratch_shapes=[pltpu.VMEM((tm, D), BF16), pltpu.VMEM((tm, D), F32)],
        compiler_params=_params("arbitrary", "arbitrary"),
        name="dense_swiglu",
    )(x2, gn.reshape(1, D), w_gate, w_up, w_down)


def _outproj_router_body(x_ref, a_ref, b_ref, c_ref, w_ref, gn_ref, wh_ref, wl_ref, tril_ref,
                         x3_ref, o_ref, cnt_ref, hn_ref, plan_ref, *, n_experts):
    @pl.when(pl.program_id(0) == 0)
    def _():
        cnt_ref[...] = jnp.zeros_like(cnt_ref)

    attn = a_ref[...].astype(F32) + b_ref[...].astype(F32) + c_ref[...].astype(F32)
    x3 = x_ref[...] + _dot(attn.astype(BF16), w_ref[...])
    x3_ref[...] = x3

    hn = _rms(x3, gn_ref[...])
    hh = hn.astype(BF16)
    hb = hh.astype(F32)
    hn_ref[...] = hb
    hl = (hn - hb).astype(BF16)
    wh = wh_ref[...]
    logits = _dot(hh, wh) + _dot(hh, wl_ref[...]) + _dot(hl, wh)
    lane = lax.broadcasted_iota(jnp.int32, logits.shape, 1)
    lg = jnp.where(lane < n_experts, logits, -jnp.inf)
    m0 = jnp.max(lg, axis=1, keepdims=True)
    i0 = jnp.min(jnp.where(lg == m0, lane, LANES), axis=1, keepdims=True)
    lg = jnp.where(lane == i0, -jnp.inf, lg)
    m1 = jnp.max(lg, axis=1, keepdims=True)
    i1 = jnp.min(jnp.where(lg == m1, lane, LANES), axis=1, keepdims=True)
    e1 = jnp.exp(m1 - m0)
    w0 = 1.0 / (1.0 + e1)
    routed = ((lane == i0) | (lane == i1)).astype(F32)
    incl = _dot(tril_ref[...], routed.astype(BF16))
    excl = incl - routed + cnt_ref[0:1, :]

    def put(col, v):
        return jnp.where(lane == col, v, 0.0)

    meta = (put(META_I0, i0.astype(F32)) + put(META_I1, i1.astype(F32)) + put(META_W0, w0)
            + put(META_W1, e1 * w0) + put(META_R0, _lane_col(excl, lane, i0))
            + put(META_R1, _lane_col(excl, lane, i1)))
    o_ref[...] = meta
    plan_ref[0] = meta.T[0:SUBLANES, :].astype(jnp.int32)
    cnt_ref[...] = jnp.broadcast_to(cnt_ref[0:1, :] + incl[incl.shape[0] - 1:, :], cnt_ref.shape)


META_I0, META_I1, META_W0, META_W1, META_R0, META_R1 = range(6)


def _outproj_router(x2, oc, os_, ow, w_out, gn, w_router, tm=512):
    M, D = x2.shape
    K = w_out.shape[0]
    n_experts = w_router.shape[1]
    tm = min(tm, M)
    wpad = jnp.pad(w_router, ((0, 0), (0, LANES - n_experts)))
    wh = wpad.astype(BF16)
    wl = (wpad - wh.astype(F32)).astype(BF16)
    tril = jnp.tril(jnp.ones((tm, tm), BF16))
    row = lambda width: pl.BlockSpec((tm, width), lambda i: (i, 0))
    return pl.pallas_call(
        functools.partial(_outproj_router_body, n_experts=n_experts),
        grid=(M // tm,),
        in_specs=[row(D), row(K), row(K), row(K), _resident((K, D)), _resident((1, D)),
                  _resident((D, LANES)), _resident((D, LANES)), _resident((tm, tm))],
        out_specs=[row(D), row(LANES), pl.BlockSpec((SUBLANES, LANES), lambda i: (0, 0)), row(D),
                   pl.BlockSpec((1, SUBLANES, tm), lambda i: (i, 0, 0))],
        out_shape=[jax.ShapeDtypeStruct((M, D), F32), jax.ShapeDtypeStruct((M, LANES), F32),
                   jax.ShapeDtypeStruct((SUBLANES, LANES), F32), jax.ShapeDtypeStruct((M, D), F32),
                   jax.ShapeDtypeStruct((M // tm, SUBLANES, tm), jnp.int32)],
        compiler_params=_params("arbitrary"),
        name="nsa_outproj_moe_router",
    )(x2, oc, os_, ow, w_out.astype(BF16), gn.reshape(1, D), wh, wl, tril)


def _combine_body(pos_ref, pos_next_ref, x_ref, meta_ref, y_ref, gfin_ref, o_ref, ybuf, sem, *, tm, n_tiles):
    i = pl.program_id(0)
    slot = i % 2

    def row_copy(idx_ref, k, r, s):
        return pltpu.make_async_copy(y_ref.at[pl.ds(idx_ref[0, 0, k * tm + r], 1)], ybuf.at[s, k, pl.ds(r, 1)],
                                     sem.at[s])

    @pl.when(i == 0)
    def _():
        def body(r, c):
            row_copy(pos_ref, 0, r, 0).start()
            row_copy(pos_ref, 1, r, 0).start()
            return c
        lax.fori_loop(0, tm, body, 0, unroll=DMA_ISSUE_UNROLL)

    @pl.when(i + 1 < n_tiles)
    def _():
        for r in range(tm):
            row_copy(pos_next_ref, 0, r, 1 - slot).start(priority=0)
            row_copy(pos_next_ref, 1, r, 1 - slot).start(priority=1)

    pltpu.make_async_copy(ybuf.at[slot], ybuf.at[slot], sem.at[slot]).wait()
    meta = meta_ref[...]
    lane = lax.broadcasted_iota(jnp.int32, meta.shape, 1)
    out = (x_ref[...] + _lane_col(meta, lane, META_W0) * ybuf[slot, 0]
           + _lane_col(meta, lane, META_W1) * ybuf[slot, 1])
    o_ref[...] = _rms(out, gfin_ref[...])


def _combine(x2, meta, y, pos0, pos1, g_final, tm=512):
    M, D = x2.shape
    tm = min(tm, M)
    n_tiles = M // tm
    pos = jnp.concatenate([pos0.reshape(n_tiles, 1, tm), pos1.reshape(n_tiles, 1, tm)], axis=2)
    return pl.pallas_call(
        functools.partial(_combine_body, tm=tm, n_tiles=n_tiles),
        grid=(n_tiles,),
        in_specs=[pl.BlockSpec((1, 1, 2 * tm), lambda i: (i, 0, 0), memory_space=pltpu.SMEM),
                  pl.BlockSpec((1, 1, 2 * tm), lambda i: (jnp.minimum(i + 1, n_tiles - 1), 0, 0),
                               memory_space=pltpu.SMEM),
                  pl.BlockSpec((tm, D), lambda i: (i, 0)),
                  pl.BlockSpec((tm, LANES), lambda i: (i, 0)),
                  pl.BlockSpec(memory_space=pl.ANY),
                  _resident((1, D))],
        out_specs=pl.BlockSpec((tm, D), lambda i: (i, 0)),
        out_shape=jax.ShapeDtypeStruct((M, D), F32),
        scratch_shapes=[pltpu.VMEM((2, 2, tm, D), F32), pltpu.SemaphoreType.DMA((2,))],
        compiler_params=_params("arbitrary"),
        name="moe_combine",
    )(pos, pos, x2, meta, y, g_final.reshape(1, D))


GATHER_AHEAD = 2
DMA_ISSUE_UNROLL = 8


def _grouped_ffn_body(te_ref, nused_ref, src_ref, src_next_ref, src_ahead_ref, x_ref, wg_ref, wu_ref,
                      wd_ref, o_ref, xbuf, hn_ref, acc_ref, sem, *, tm, n_f):
    j = pl.program_id(0)
    f = pl.program_id(1)
    n_used = nused_ref[0]
    used = j < n_used
    gathered = j < n_used + GATHER_AHEAD
    slot = j % (GATHER_AHEAD + 1)
    slot_ahead = (j + GATHER_AHEAD) % (GATHER_AHEAD + 1)
    chunk = tm // (n_f + 1)

    def row_copy(idx_ref, r, s):
        return pltpu.make_async_copy(x_ref.at[pl.ds(idx_ref[0, 0, r], 1)], xbuf.at[s, pl.ds(r, 1)], sem.at[s])

    def start_chunk(first):
        for k in range(chunk):
            row_copy(src_ahead_ref, first + k, slot_ahead).start()

    @pl.when((j == 0) & (f == 0))
    def _():
        def body(r, c):
            row_copy(src_ref, r, 0).start()
            row_copy(src_next_ref, r, 1).start()
            return c
        lax.fori_loop(0, tm, body, 0, unroll=DMA_ISSUE_UNROLL)

    @pl.when(f == 0)
    def _():
        acc_ref[...] = jnp.zeros_like(acc_ref)

        @pl.when(gathered)
        def _():
            pltpu.make_async_copy(xbuf.at[slot], xbuf.at[slot], sem.at[slot]).wait()
            hn_ref[...] = xbuf[slot].astype(BF16)

        @pl.when(used)
        def _():
            start_chunk(0)

    @pl.when(used)
    def _():
        acc_ref[...] += _swiglu_partial(hn_ref[...], wg_ref, wu_ref, wd_ref)
        start_chunk((f + 1) * chunk)

    @pl.when(f == n_f - 1)
    def _():
        o_ref[...] = acc_ref[...]


def _grouped_ffn(hp, src, w_gate, w_up, w_down, tile_expert, n_used, tm, tf=512):
    P = src.shape[0]
    D = w_gate.shape[1]
    F = w_gate.shape[2]
    tf = min(tf, F)
    n_f = F // tf
    n_tiles = P // tm
    assert F % tf == 0 and tm % (n_f + 1) == 0
    src3 = src.reshape(n_tiles, 1, tm)

    def f_block(j, f, nu):
        return jnp.where(j < nu[0], f, n_f - 1)

    def src_spec(ahead):
        return pl.BlockSpec((1, 1, tm), lambda j, f, te, nu: (jnp.minimum(j + ahead, n_tiles - 1), 0, 0),
                            memory_space=pltpu.SMEM)

    grid_spec = pltpu.PrefetchScalarGridSpec(
        num_scalar_prefetch=2,
        grid=(n_tiles, n_f),
        in_specs=[src_spec(a) for a in range(GATHER_AHEAD + 1)] + [
                  pl.BlockSpec(memory_space=pl.ANY),
                  pl.BlockSpec((1, D, tf), lambda j, f, te, nu: (te[j], 0, f_block(j, f, nu))),
                  pl.BlockSpec((1, D, tf), lambda j, f, te, nu: (te[j], 0, f_block(j, f, nu))),
                  pl.BlockSpec((1, tf, D), lambda j, f, te, nu: (te[j], f_block(j, f, nu), 0))],
        out_specs=pl.BlockSpec((tm, D), lambda j, f, te, nu: (j, 0)),
        scratch_shapes=[pltpu.VMEM((GATHER_AHEAD + 1, tm, D), F32), pltpu.VMEM((tm, D), BF16),
                        pltpu.VMEM((tm, D), F32), pltpu.SemaphoreType.DMA((GATHER_AHEAD + 1,))])
    return pl.pallas_call(
        functools.partial(_grouped_ffn_body, tm=tm, n_f=n_f),
        grid_spec=grid_spec,
        out_shape=jax.ShapeDtypeStruct((P, D), F32),
        compiler_params=_params("arbitrary", "arbitrary"),
        name="moe_grouped_swiglu",
    )(tile_expert, n_used, *([src3] * (GATHER_AHEAD + 1)), hp, w_gate, w_up, w_down)


def _moe_experts(x2, routing, w_gate, w_up, w_down, g_final, tm=1024):
    M, D = x2.shape
    n_e = w_gate.shape[0]
    tm = min(tm, M)
    meta, counts, hp, plan = routing
    i0, i1, r0, r1 = (plan[:, col, :].reshape(M) for col in (META_I0, META_I1, META_R0, META_R1))
    padded = (counts[0, :n_e].astype(jnp.int32) + tm - 1) // tm * tm
    ends = jnp.cumsum(padded)
    offsets = ends - padded
    experts = jnp.arange(n_e, dtype=jnp.int32)
    pos0 = jnp.sum(jnp.where(i0[:, None] == experts, offsets, 0), axis=1) + r0
    pos1 = jnp.sum(jnp.where(i1[:, None] == experts, offsets, 0), axis=1) + r1
    P = TOP_K * M + (n_e - 1 + GATHER_AHEAD) * tm
    tok = jnp.arange(M, dtype=jnp.int32)
    src = jnp.zeros((P,), jnp.int32).at[jnp.concatenate([pos0, pos1])].set(
        jnp.concatenate([tok, tok]), unique_indices=True, mode="promise_in_bounds")
    tile_start = jnp.arange(P // tm, dtype=jnp.int32) * tm
    tile_expert = jnp.minimum(jnp.sum(tile_start[:, None] >= ends, axis=1), n_e - 1).astype(jnp.int32)
    n_used = (ends[n_e - 1:] // tm).astype(jnp.int32)
    y = _grouped_ffn(hp, src, w_gate, w_up, w_down, tile_expert, n_used, tm)
    return _combine(x2, meta, y, pos0, pos1, g_final)


def _nsa_proj_body(x_ref, gn_ref, w_ref, cos_ref, sin_ref,
                   q_ref, kc_ref, vc_ref, ks_ref, vs_ref, kw_ref, vw_ref, g_ref, *, tm, tiles_per_seq):
    i = pl.program_id(0)
    hn = _rms(x_ref[...], gn_ref[...]).astype(BF16)
    proj = _dot(hn, w_ref[...])
    lane = lax.broadcasted_iota(jnp.int32, (tm, LANES), 1)
    first_half = (lane & (HEAD_DIM - 1)) < HEAD_DIM // 2
    cos = cos_ref[...]
    sin = sin_ref[...]

    def chunk(c):
        return proj[:, c * LANES:(c + 1) * LANES]

    def rope(v):
        rot = jnp.where(first_half, pltpu.roll(v, LANES - HEAD_DIM // 2, 1), pltpu.roll(v, HEAD_DIM // 2, 1))
        return v * cos + rot * sin

    def heads(v):
        return v[:, :HEAD_DIM], pltpu.roll(v, HEAD_DIM, 1)[:, :HEAD_DIM]

    c = 0
    for cc in range(N_HEADS // 2):
        lo, hi = heads(rope(chunk(c)) * Q_SCALE); c += 1
        q_ref[2 * cc] = lo.astype(BF16)
        q_ref[2 * cc + 1] = hi.astype(BF16)
    for cc in range(N_KV_GROUPS // 2):
        lo, hi = heads(rope(chunk(c))); c += 1
        kc_ref[2 * cc] = lo
        kc_ref[2 * cc + 1] = hi
    for cc in range(N_KV_GROUPS // 2):
        lo, hi = heads(chunk(c)); c += 1
        vc_ref[2 * cc] = lo
        vc_ref[2 * cc + 1] = hi
    pos = (i % tiles_per_seq) * tm + lax.broadcasted_iota(jnp.int32, (tm, 1), 0)
    onehot = (lane == pos // SEL_BLOCK).astype(BF16)
    for cc in range(N_KV_GROUPS // 2):
        v = rope(chunk(c)); c += 1
        for j, vv in enumerate((v, pltpu.roll(v, HEAD_DIM, 1))):
            ks_ref[2 * cc + j, :, 0:LANES] = jnp.where(lane < HEAD_DIM, vv, 0.0).astype(BF16)
            ks_ref[2 * cc + j, :, LANES:2 * LANES] = onehot
    ones_col = jnp.where(lane == HEAD_DIM, 1.0, 0.0)

    def store_values(ref):
        nonlocal c
        for cc in range(N_KV_GROUPS // 2):
            v = chunk(c); c += 1
            for j, vv in enumerate((v, pltpu.roll(v, HEAD_DIM, 1))):
                ref[2 * cc + j] = jnp.where(lane < HEAD_DIM, vv, ones_col).astype(BF16)

    store_values(vs_ref)
    for cc in range(N_KV_GROUPS // 2):
        lo, hi = heads(rope(chunk(c))); c += 1
        kw_ref[2 * cc] = lo.astype(BF16)
        kw_ref[2 * cc + 1] = hi.astype(BF16)
    store_values(vw_ref)
    g_ref[...] = jax.nn.sigmoid(chunk(c))


def _nsa_proj(x2, gn, w_in, S, tm=512):
    M, D = x2.shape
    H, G, DH = N_HEADS, N_KV_GROUPS, HEAD_DIM
    tm = min(tm, S)
    assert S % tm == 0
    n_in = w_in.shape[1]
    n_pad = -(-n_in // LANES) * LANES
    wp = jnp.pad(w_in, ((0, 0), (0, n_pad - n_in))).astype(BF16)
    half = DH // 2
    freqs = ROPE_THETA ** (-jnp.arange(half, dtype=F32) / half)
    ang = jnp.arange(S, dtype=F32)[:, None] * freqs[None, :]
    cos = jnp.tile(jnp.cos(ang), (1, 2 * LANES // DH))
    sin = jnp.tile(jnp.concatenate([-jnp.sin(ang), jnp.sin(ang)], axis=1), (1, LANES // DH))
    tiles_per_seq = S // tm
    hd = lambda n, dt: jax.ShapeDtypeStruct((n, M, DH), dt)
    hspec = lambda n: pl.BlockSpec((n, tm, DH), lambda i: (0, i, 0))
    vd = jax.ShapeDtypeStruct((G, M, LANES), BF16)
    vspec = pl.BlockSpec((G, tm, LANES), lambda i: (0, i, 0))
    return pl.pallas_call(
        functools.partial(_nsa_proj_body, tm=tm, tiles_per_seq=tiles_per_seq),
        grid=(M // tm,),
        in_specs=[pl.BlockSpec((tm, D), lambda i: (i, 0)), _resident((1, D)), _resident((D, n_pad)),
                  pl.BlockSpec((tm, LANES), lambda i: (i % tiles_per_seq, 0)),
                  pl.BlockSpec((tm, LANES), lambda i: (i % tiles_per_seq, 0))],
        out_specs=[hspec(H), hspec(G), hspec(G),
                   pl.BlockSpec((G, tm, 2 * LANES), lambda i: (0, i, 0)),
                   vspec, hspec(G), vspec,
                   pl.BlockSpec((tm, LANES), lambda i: (i, 0))],
        out_shape=[hd(H, BF16), hd(G, F32), hd(G, F32),
                   jax.ShapeDtypeStruct((G, M, 2 * LANES), BF16),
                   vd, hd(G, BF16), vd,
                   jax.ShapeDtypeStruct((M, LANES), F32)],
        compiler_params=_params("arbitrary"),
        name="nsa_proj",
    )(x2, gn.reshape(1, D), wp, cos, sin)


def _compress_body(kc_ref, vc_ref, pe_ref, w1_ref, w2_ref, ko_ref, vo_ref, *, nc):
    half = CMP_BLOCK // 2
    for kv, (src, dst) in enumerate(((kc_ref, ko_ref), (vc_ref, vo_ref))):
        top = jnp.zeros((nc, w1_ref.shape[2]), F32)
        bot = jnp.zeros((nc, w1_ref.shape[2]), F32)
        for j in range(half):
            xj = src[0, pl.ds(j, nc, stride=CMP_STRIDE), :]
            top = top + _dot((xj + pe_ref[kv, j:j + 1, :]).astype(BF16),
                             w1_ref[kv, j * HEAD_DIM:(j + 1) * HEAD_DIM, :])
            bot = bot + _dot((xj + pe_ref[kv, half + j:half + j + 1, :]).astype(BF16),
                             w1_ref[kv, (half + j) * HEAD_DIM:(half + j + 1) * HEAD_DIM, :])
        hid = top + pltpu.roll(bot, nc - 1, 0)
        dst[0] = _dot(_gelu_tanh(hid).astype(BF16), w2_ref[kv]).astype(BF16)


def _compress(kc, vc, pe, w1, w2, B, S):
    assert CMP_BLOCK == 2 * CMP_STRIDE
    G, M, DH = kc.shape
    nc = S // CMP_STRIDE
    spec_in = pl.BlockSpec((1, S, DH), lambda b, g: (g, b, 0))
    spec_out = pl.BlockSpec((1, nc, DH), lambda b, g: (g, b, 0))
    out = jax.ShapeDtypeStruct((G, B * nc, DH), BF16)
    return pl.pallas_call(
        functools.partial(_compress_body, nc=nc),
        grid=(B, G),
        in_specs=[spec_in, spec_in, _resident(pe.shape), _resident(w1.shape), _resident(w2.shape)],
        out_specs=[spec_out, spec_out],
        out_shape=[out, out],
        compiler_params=_params("arbitrary", "arbitrary"),
        name="nsa_compress",
    )(kc, vc, pe, w1.astype(BF16), w2.astype(BF16))


def _store_head(o_ref, oh, gates, g, h, branch, rows=slice(None), denom=None):
    lane = lax.broadcasted_iota(jnp.int32, gates.shape, 1)
    col = N_BRANCHES * (HEADS_PER_GROUP * g + h) + branch
    scale = _lane_col(gates, lane, col)
    if denom is not None:
        scale = scale / denom
    o_ref[rows, h * HEAD_DIM:(h + 1) * HEAD_DIM] = (oh * scale).astype(o_ref.dtype)


def _store_softmax_head(o_ref, acc, gates, g, h, branch, rows=slice(None)):
    _store_head(o_ref, acc[:, :HEAD_DIM], gates, g, h, branch, rows, denom=acc[:, HEAD_DIM:HEAD_DIM + 1])


def _cmp_body(q_ref, kc_ref, vc_ref, gates_ref, ov_ref, o_ref, bias_ref, imp_ref, *, tq, nc, n_s, k_sel):
    g = pl.program_id(1)
    i = pl.program_id(2)
    gates = gates_ref[...]
    t = i * tq + lax.broadcasted_iota(jnp.int32, (tq, 1), 0)

    def attend(w):
        kc = kc_ref[0, 0:w, :]
        vc = vc_ref[0, 0:w, :]
        cend = lax.broadcasted_iota(jnp.int32, (1, w), 1) * CMP_STRIDE + (CMP_BLOCK - 1)
        mask = cend <= t
        ps = None
        for h in range(HEADS_PER_GROUP):
            s = lax.dot_general(q_ref[h], kc, _NT, preferred_element_type=F32)
            s = jnp.where(mask, s, MASK_VALUE)
            m = jnp.max(s, axis=1, keepdims=True)
            m = jnp.where(m > 0.5 * MASK_VALUE, m, 0.0)
            e = jnp.exp2(s - m)
            p = e * (1.0 / jnp.maximum(jnp.sum(e, axis=1, keepdims=True), 1e-30))
            _store_head(o_ref, _dot(p.astype(BF16), vc), gates, g, h, 0)
            ps = p if ps is None else ps + p
        ph = ps.astype(BF16)
        pl_ = (ps - ph.astype(F32)).astype(BF16)
        imp_ref[...] = _dot(ph, ov_ref[0:w, :]) + _dot(pl_, ov_ref[0:w, :])

    n_chunks = nc // LANES
    need = ((i + 1) * (tq // CMP_STRIDE) + LANES - 1) // LANES
    for k in range(1, n_chunks + 1):
        pl.when((need == k) if k < n_chunks else (need >= k))(functools.partial(attend, k * LANES))

    j = lax.broadcasted_iota(jnp.int32, (tq, LANES), 1)
    cur = t // SEL_BLOCK
    forced = (j == 0) | (j == cur) | (j == cur - 1)
    valid = j * SEL_BLOCK <= t
    score = jnp.where(forced, FORCE_SCORE, jnp.where(valid, imp_ref[...], -1.0))
    score = jnp.where(j < n_s, score, -jnp.inf)
    x = score.T
    blk_i = lax.broadcasted_iota(jnp.int32, (LANES, tq), 0)
    blk = blk_i.astype(F32)

    def topk(x, n_iter):
        for _ in range(n_iter):
            mx = jnp.max(x, axis=0, keepdims=True)
            idx = jnp.min(jnp.where(x == mx, blk, float(LANES)), axis=0, keepdims=True)
            x = jnp.where(blk == idx, -jnp.inf, x)
        sel = (x == -jnp.inf) & (blk_i < n_s)
        bias_ref[0] = jnp.where(sel, 0.0, MASK_VALUE).T.astype(BF16)

    n_forced = 3
    direct = (i * tq >= 2 * SEL_BLOCK) & (k_sel > n_forced)

    @pl.when(direct)
    def _():
        cur_t = (i * tq + lax.broadcasted_iota(jnp.int32, (1, tq), 1)) // SEL_BLOCK
        forced_t = (blk_i == 0) | (blk_i == cur_t) | (blk_i == cur_t - 1)
        topk(jnp.where(forced_t, -jnp.inf, x), k_sel - n_forced)

    @pl.when(jnp.logical_not(direct))
    def _():
        topk(x, k_sel)


def _overlap_matrix(nc, n_s):
    r = CMP_BLOCK // CMP_STRIDE
    qn = SEL_BLOCK // CMP_STRIDE
    m = np.zeros((nc, LANES), np.float32)
    n_c = nc - r + 1
    chunks = np.arange(n_c)[:, None] + np.arange(r)[None, :]
    np.add.at(m, (np.repeat(np.arange(n_c), r), (chunks // qn).ravel()), 1.0)
    return m


def _cmp_attention(q, kcmp, vcmp, gates, B, S, tq=1024):
    H, M, DH = q.shape
    G = N_KV_GROUPS
    tq = min(tq, S)
    nq = S // tq
    nc = S // CMP_STRIDE
    n_s = S // SEL_BLOCK
    assert n_s <= LANES and tq & (tq - 1) == 0
    k_sel = min(N_SEL, n_s)
    ov = jnp.asarray(_overlap_matrix(nc, n_s), BF16)
    return pl.pallas_call(
        functools.partial(_cmp_body, tq=tq, nc=nc, n_s=n_s, k_sel=k_sel),
        grid=(B, G, nq),
        in_specs=[pl.BlockSpec((HEADS_PER_GROUP, tq, DH), lambda b, g, i: (g, b * nq + i, 0)),
                  pl.BlockSpec((1, nc, DH), lambda b, g, i: (g, b, 0)),
                  pl.BlockSpec((1, nc, DH), lambda b, g, i: (g, b, 0)),
                  pl.BlockSpec((tq, LANES), lambda b, g, i: (b * nq + i, 0)),
                  _resident((nc, LANES))],
        out_specs=[pl.BlockSpec((tq, HEADS_PER_GROUP * DH), lambda b, g, i: (b * nq + i, g)),
                   pl.BlockSpec((1, tq, LANES), lambda b, g, i: (g, b * nq + i, 0))],
        out_shape=[jax.ShapeDtypeStruct((M, H * DH), BF16), jax.ShapeDtypeStruct((G, M, LANES), BF16)],
        scratch_shapes=[pltpu.VMEM((tq, LANES), F32)],
        compiler_params=_params("arbitrary", "arbitrary", "arbitrary"),
        name="nsa_cmp_select",
    )(q, kcmp, vcmp, gates, ov)


def _sel_body(q_ref, bias_ref, k_ref, v_ref, gates_ref, o_ref, qcat, m_ref, acc_ref, *, tq):
    g = pl.program_id(1)
    i = pl.program_id(2)
    hp = HEADS_PER_GROUP
    rows = hp * tq
    qcat[:, 0:LANES] = jnp.zeros((rows, LANES), BF16)
    qcat[:, 0:HEAD_DIM] = q_ref[...].reshape(rows, HEAD_DIM)
    bias = bias_ref[0]
    for h in range(hp):
        qcat[h * tq:(h + 1) * tq, LANES:2 * LANES] = bias
    m_ref[...] = jnp.full((rows, LANES), MASK_VALUE, F32)
    acc_ref[...] = jnp.zeros((rows, LANES), F32)

    def tile(j, width, causal):
        start = pl.multiple_of(j * tq, tq)
        kt = k_ref[0, pl.ds(start, width), :]
        vt = v_ref[0, pl.ds(start, width), :]
        for h in range(hp):
            r0 = h * tq
            s = lax.dot_general(qcat[r0:r0 + tq, :], kt, _NT, preferred_element_type=F32)
            if causal:
                r = lax.broadcasted_iota(jnp.int32, (tq, 1), 0)
                c = lax.broadcasted_iota(jnp.int32, (1, width), 1) - (width - tq)
                s = jnp.where(c <= r, s, MASK_VALUE)
            m_prev = m_ref[r0:r0 + tq, :]
            m_new = jnp.maximum(m_prev, jnp.max(s, axis=1, keepdims=True))
            alpha = jnp.exp2(m_prev - m_new)
            p = jnp.exp2(s - jnp.concatenate([m_new] * (width // LANES), axis=1))
            acc_ref[r0:r0 + tq, :] = alpha * acc_ref[r0:r0 + tq, :] + _dot(p.astype(BF16), vt)
            m_ref[r0:r0 + tq, :] = m_new

    def wide_tile(jw, carry):
        tile(SEL_WIDE * jw, SEL_WIDE * tq, False)
        return carry

    n_wide = i // SEL_WIDE
    lax.fori_loop(0, n_wide, wide_tile, 0)
    for rem in range(SEL_WIDE):
        pl.when(i % SEL_WIDE == rem)(functools.partial(tile, SEL_WIDE * n_wide, (rem + 1) * tq, True))

    gates = gates_ref[...]
    for h in range(hp):
        _store_softmax_head(o_ref, acc_ref[h * tq:(h + 1) * tq, :], gates, g, h, 1)


def _sel_attention(q, bias, kscat, vs, gates, B, S, tq=512):
    H, M, DH = q.shape
    G = N_KV_GROUPS
    tq = min(tq, S)
    nq = S // tq
    assert tq % LANES == 0 and tq % SEL_BLOCK == 0 and S % tq == 0
    rows = HEADS_PER_GROUP * tq
    return pl.pallas_call(
        functools.partial(_sel_body, tq=tq),
        grid=(B, G, nq),
        in_specs=[pl.BlockSpec((HEADS_PER_GROUP, tq, DH), lambda b, g, i: (g, b * nq + i, 0)),
                  pl.BlockSpec((1, tq, LANES), lambda b, g, i: (g, b * nq + i, 0)),
                  pl.BlockSpec((1, S, 2 * LANES), lambda b, g, i: (g, b, 0)),
                  pl.BlockSpec((1, S, LANES), lambda b, g, i: (g, b, 0)),
                  pl.BlockSpec((tq, LANES), lambda b, g, i: (b * nq + i, 0))],
        out_specs=pl.BlockSpec((tq, HEADS_PER_GROUP * DH), lambda b, g, i: (b * nq + i, g)),
        out_shape=jax.ShapeDtypeStruct((M, H * DH), BF16),
        scratch_shapes=[pltpu.VMEM((rows, 2 * LANES), BF16), pltpu.VMEM((rows, LANES), F32),
                        pltpu.VMEM((rows, LANES), F32)],
        compiler_params=_params("arbitrary", "arbitrary", "arbitrary"),
        name="nsa_selected",
    )(q, bias, kscat, vs, gates)


def _win_body(q_ref, k_ref, v_ref, gates_ref, o_ref, *, tq, n_back, n_sub):
    g = pl.program_id(1)
    r = lax.broadcasted_iota(jnp.int32, (tq, 1), 0)
    c = lax.broadcasted_iota(jnp.int32, (1, tq), 1)
    for sub in range(n_sub):
        i = pl.program_id(2) * n_sub + sub
        rows = slice(sub * tq, (sub + 1) * tq)
        gates = gates_ref[rows, :]
        tiles = []
        for back in range(n_back, -1, -1):
            jt = i - back
            start = pl.multiple_of(jnp.maximum(jt, 0) * tq, tq)
            d = r - c + back * tq
            ok = (d >= 0) & (d < WINDOW) & (jt >= 0)
            tiles.append((k_ref[0, pl.ds(start, tq), :], v_ref[0, pl.ds(start, tq), :], ok))
        for h in range(HEADS_PER_GROUP):
            q = q_ref[h, rows, :]
            scores = [jnp.where(ok, lax.dot_general(q, kt, _NT, preferred_element_type=F32), MASK_VALUE)
                      for kt, _, ok in tiles]
            m = functools.reduce(jnp.maximum, scores).max(axis=1, keepdims=True)
            acc = jnp.zeros((tq, LANES), F32)
            for s, (_, vt, _) in zip(scores, tiles):
                acc = acc + _dot(jnp.exp2(s - m).astype(BF16), vt)
            _store_softmax_head(o_ref, acc, gates, g, h, 2, rows)


def _win_attention(q, kw, vw, gates, B, S, tq=256, n_sub=4):
    H, M, DH = q.shape
    G = N_KV_GROUPS
    tq = min(tq, S)
    n_sub = min(n_sub, S // tq)
    tile = tq
    tq = tile * n_sub
    nq = S // tq
    n_back = -(-WINDOW // tile)
    return pl.pallas_call(
        functools.partial(_win_body, tq=tile, n_back=n_back, n_sub=n_sub),
        grid=(B, G, nq),
        in_specs=[pl.BlockSpec((HEADS_PER_GROUP, tq, DH), lambda b, g, i: (g, b * nq + i, 0)),
                  pl.BlockSpec((1, S, DH), lambda b, g, i: (g, b, 0)),
                  pl.BlockSpec((1, S, LANES), lambda b, g, i: (g, b, 0)),
                  pl.BlockSpec((tq, LANES), lambda b, g, i: (b * nq + i, 0))],
        out_specs=pl.BlockSpec((tq, HEADS_PER_GROUP * DH), lambda b, g, i: (b * nq + i, g)),
        out_shape=jax.ShapeDtypeStruct((M, H * DH), BF16),
        compiler_params=_params("arbitrary", "arbitrary", "arbitrary"),
        name="nsa_window",
    )(q, kw, vw, gates)


def _nsa_branches(x, gn, w_in, cmp_pe, cmp_w1, cmp_w2):
    B, S, D = x.shape
    x2 = x.reshape(B * S, D)
    q, kc, vc, kscat, vs, kw, vw, gates = _nsa_proj(x2, gn, w_in, S)
    kcmp, vcmp = _compress(kc, vc, cmp_pe, cmp_w1, cmp_w2, B, S)
    o_cmp, bias = _cmp_attention(q, kcmp, vcmp, gates, B, S)
    o_sel = _sel_attention(q, bias, kscat, vs, gates, B, S)
    o_win = _win_attention(q, kw, vw, gates, B, S)
    return o_cmp, o_sel, o_win


def kernel(x, norm_mix, norm_ffn, norm_final, lru_w_in, lru_b_in, lru_conv_w, lru_conv_b, lru_w_a, lru_b_a, lru_w_i, lru_b_i, lru_lambda, lru_w_out, lru_b_out, nsa_w_in, nsa_cmp_pe, nsa_cmp_w1, nsa_cmp_w2, nsa_w_out, ffn_w_gate, ffn_w_up, ffn_w_down, moe_w_router, moe_w_gate, moe_w_up, moe_w_down):
    B, S, D = x.shape
    assert norm_mix.shape[0] == 2 and lru_w_in.shape[0] == 1 and nsa_w_in.shape[0] == 1
    x = _lru_layer(x, norm_mix[0], lru_w_in[0], lru_b_in[0], lru_conv_w[0], lru_conv_b[0], lru_w_a[0],
                   lru_b_a[0], lru_w_i[0], lru_b_i[0], lru_lambda[0], lru_w_out[0], lru_b_out[0])
    x2 = _ffn_layer(x.reshape(B * S, D), norm_ffn[0], ffn_w_gate, ffn_w_up, ffn_w_down)
    branches = _nsa_branches(x2.reshape(B, S, D), norm_mix[1], nsa_w_in[0], nsa_cmp_pe[0], nsa_cmp_w1[0],
                             nsa_cmp_w2[0])
    x2, *routing = _outproj_router(x2, *branches, nsa_w_out[0], norm_ffn[1], moe_w_router[0])
    out = _moe_experts(x2, routing, moe_w_gate[0], moe_w_up[0], moe_w_down[0], norm_final)
    return out.reshape(B, S, D)
```
